```python
import math
import jax
import jax.numpy as jnp
from jax import lax
import numpy as np

D_MODEL = 1024
BATCH = 16
SEQ = 256
DEPTH = 2
DEC_BATCH = 2
DEC_SEQ = 1024
PAST_LEN = 256

GRID_W = 64
N_EVEN = (DEPTH + 1) // 2
N_ODD = DEPTH // 2

ATTN_HEADS = 8
ATTN_KV_HEADS = 2
HEAD_DIM = 64
ROPE_THETA = 10000.0
Q_BLOCK = 128
ATTN_Q_DIM = ATTN_HEADS * HEAD_DIM
ATTN_KV_DIM = ATTN_KV_HEADS * HEAD_DIM

SSD_HEADS = 8
SSD_HEAD_DIM = 64
SSD_D_INNER = SSD_HEADS * SSD_HEAD_DIM
SSD_GROUPS = 2
SSD_STATE = 64
SSD_CONV_K = 5
SSD_CHUNK = 128
SSD_CONV_DIM = SSD_D_INNER + 2 * SSD_GROUPS * SSD_STATE

AB_IN_DIM = ATTN_Q_DIM + 2 * ATTN_KV_DIM + SSD_D_INNER + SSD_CONV_DIM + 2 * SSD_HEADS
AB_SPLITS = (ATTN_Q_DIM, ATTN_Q_DIM + ATTN_KV_DIM, ATTN_Q_DIM + 2 * ATTN_KV_DIM,
             ATTN_Q_DIM + 2 * ATTN_KV_DIM + SSD_D_INNER,
             ATTN_Q_DIM + 2 * ATTN_KV_DIM + SSD_D_INNER + SSD_CONV_DIM)
AB_MIX_DIM = ATTN_Q_DIM + SSD_D_INNER

RWKV_HEAD_DIM = 64
RWKV_HEADS = D_MODEL // RWKV_HEAD_DIM
R_DECAY = 64
R_AAA = 64
R_GATE = 128

FFN_DIM = (((8 * D_MODEL + 2) // 3 + 255) // 256) * 256

RMS_EPS = 1e-6
GN_EPS = 64e-5
L2_EPS = 1e-12
F32 = jnp.float32

kernel_name = 'hybrid_diffusion_gqa_ssd_rwkv7_step'


def rmsnorm(x, g, eps=RMS_EPS):
    xf = x.astype(F32)
    y = xf * lax.rsqrt(jnp.mean(xf * xf, axis=-1, keepdims=True) + eps)
    return (y * g.astype(F32)).astype(x.dtype)


def ada_params(cvec, w, b):
    m = jnp.einsum('...d,de->...e', jax.nn.silu(cvec), w) + b
    return jnp.split(m[..., None, :], 6, axis=-1)


def modulate(h, shift, scale):
    return h * (1 + scale) + shift


def swiglu(h, w_gate, w_up, w_down):
    hid = jax.nn.silu(jnp.einsum('btd,df->btf', h, w_gate)) * jnp.einsum('btd,df->btf', h, w_up)
    return jnp.einsum('btf,fd->btd', hid, w_down)


def grid_rope(n_tokens):
    rows = n_tokens // GRID_W
    row = jnp.repeat(jnp.arange(rows, dtype=F32), GRID_W)
    col = jnp.tile(jnp.arange(GRID_W, dtype=F32), rows)
    n_freq = HEAD_DIM // 4
    inv_freq = ROPE_THETA ** (-jnp.arange(n_freq, dtype=F32) / n_freq)
    ang = jnp.concatenate([row[:, None] * inv_freq, col[:, None] * inv_freq], axis=-1)
    return jnp.cos(ang), jnp.sin(ang)


def apply_rope(x, cos, sin):
    xf = x.astype(F32)
    x1, x2 = jnp.split(xf, 2, axis=-1)
    c = cos[None, :, None, :]
    s = sin[None, :, None, :]
    return jnp.concatenate([x1 * c - x2 * s, x1 * s + x2 * c], axis=-1).astype(x.dtype)


def block_attention(q, k, v):
    b, n, nh, d = q.shape
    kvh = k.shape[2]
    grp = nh // kvh
    nb = n // Q_BLOCK
    qb = jnp.moveaxis(q.reshape(b, nb, Q_BLOCK, kvh, grp, d), 1, 0)
    scale = d ** -0.5

    def one_block(qblk):
        s = jnp.einsum('bqkgd,bskd->bkgqs', qblk, k).astype(F32) * scale
        p = jax.nn.softmax(s, axis=-1).astype(v.dtype)
        return jnp.einsum('bkgqs,bskd->bqkgd', p, v)

    o = lax.map(one_block, qb)
    return jnp.moveaxis(o, 0, 1).reshape(b, n, nh * d)


def centred_dwconv(x, w, b):
    pad = SSD_CONV_K // 2
    n = x.shape[1]
    xp = jnp.pad(x, ((0, 0), (pad, pad), (0, 0)))
    y = b
    for i in range(SSD_CONV_K):
        y = y + xp[:, i:i + n] * w[i]
    return y


def ssd_chunked(x, dt, A, bm, cm, h0):
    b, n, nh, hp = x.shape
    ng, ns = bm.shape[2], bm.shape[3]
    L = SSD_CHUNK
    nc = n // L
    bh = jnp.repeat(bm.astype(F32), nh // ng, axis=2).reshape(b, nc, L, nh, ns)
    ch = jnp.repeat(cm.astype(F32), nh // ng, axis=2).reshape(b, nc, L, nh, ns)
    xdt = (x.astype(F32) * dt[..., None]).reshape(b, nc, L, nh, hp)
    a_cs = jnp.cumsum((dt * A).reshape(b, nc, L, nh), axis=2)
    lower = jnp.tril(jnp.ones((L, L), dtype=bool))[None, None, :, :, None]
    seg = jnp.exp(jnp.where(lower, a_cs[:, :, :, None, :] - a_cs[:, :, None, :, :], -jnp.inf))
    scores = jnp.einsum('bclhn,bcshn->bclsh', ch, bh) * seg
    y_diag = jnp.einsum('bclsh,bcshp->bclhp', scores, xdt)
    decay_end = jnp.exp(a_cs[:, :, -1:, :] - a_cs)
    chunk_states = jnp.einsum('bclhn,bclhp->bchpn', bh * decay_end[..., None], xdt)
    chunk_decay = jnp.exp(a_cs[:, :, -1, :])

    def carry_step(h, inp):
        st, dec = inp
        return h * dec[:, :, None, None] + st, h

    h_final, h_in = lax.scan(carry_step, h0.astype(F32),
                             (jnp.moveaxis(chunk_states, 1, 0), jnp.moveaxis(chunk_decay, 1, 0)))
    h_in = jnp.moveaxis(h_in, 0, 1)
    y_off = jnp.einsum('bclhn,bchpn->bclhp', ch * jnp.exp(a_cs)[..., None], h_in)
    return (y_diag + y_off).reshape(b, n, nh, hp), h_final


def mixer_ab(h, w_in, w_out, q_g, k_g, conv_w, conv_b, dt_bias, a_log, d_skip, ssd_g,
             rope, ctx_k, ctx_v, h0_f, h0_b):
    b, n, _ = h.shape
    proj = jnp.einsum('btd,de->bte', h, w_in)
    q, k, v, z, xbc, dt = jnp.split(proj, AB_SPLITS, axis=-1)
    q = rmsnorm(q.reshape(b, n, ATTN_HEADS, HEAD_DIM), q_g)
    k = rmsnorm(k.reshape(b, n, ATTN_KV_HEADS, HEAD_DIM), k_g)
    v = v.reshape(b, n, ATTN_KV_HEADS, HEAD_DIM)
    if rope is not None:
        q = apply_rope(q, rope[0], rope[1])
        k = apply_rope(k, rope[0], rope[1])
    if ctx_k is None:
        keys, vals = k, v
    else:
        keys = jnp.concatenate([ctx_k.astype(k.dtype), k], axis=1)
        vals = jnp.concatenate([ctx_v.astype(v.dtype), v], axis=1)
    attn = block_attention(q, keys, vals)
    xbc = jax.nn.silu(centred_dwconv(xbc, conv_w, conv_b))
    xs, bm, cm = jnp.split(xbc, [SSD_D_INNER, SSD_D_INNER + SSD_GROUPS * SSD_STATE], axis=-1)
    xs = xs.reshape(b, n, SSD_HEADS, SSD_HEAD_DIM)
    bm = bm.reshape(b, n, SSD_GROUPS, SSD_STATE)
    cm = cm.reshape(b, n, SSD_GROUPS, SSD_STATE)
    dt = jax.nn.softplus(dt.astype(F32).reshape(b, n, 2, SSD_HEADS) + dt_bias.astype(F32))
    A = -jnp.exp(a_log.astype(F32))
    y_f, hf = ssd_chunked(xs, dt[:, :, 0], A[0], bm, cm, h0_f)
    y_b, hb = ssd_chunked(xs[:, ::-1], dt[:, ::-1, 1], A[1], bm[:, ::-1], cm[:, ::-1], h0_b)
    y = y_f + y_b[:, ::-1] + d_skip.astype(F32)[:, None] * xs.astype(F32)
    y = rmsnorm(y.reshape(b, n, SSD_D_INNER) * jax.nn.silu(z.astype(F32)), ssd_g).astype(h.dtype)
    out = jnp.einsum('bte,ed->btd', jnp.concatenate([attn.astype(h.dtype), y], axis=-1), w_out)
    return out, k, v, hf, hb


def centred_shift(x):
    prev = jnp.pad(x, ((0, 0), (1, 0), (0, 0)))[:, :-1]
    nxt = jnp.pad(x, ((0, 0), (0, 1), (0, 0)))[:, 1:]
    return prev - x, nxt - x


def split_heads(t):
    return t.reshape(t.shape[:-1] + (RWKV_HEADS, RWKV_HEAD_DIM))


def rwkv7_scan(r, w, k, v, kk, a, s0):
    def step(S, inp):
        r_t, w_t, k_t, v_t, kk_t, a_t = inp
        sa = jnp.einsum('bhvk,bhk->bhv', S, kk_t)
        S = (S * w_t[:, :, None, :] - sa[..., None] * (kk_t * a_t)[:, :, None, :]
             + v_t[..., None] * k_t[:, :, None, :])
        return S, jnp.einsum('bhvk,bhk->bhv', S, r_t)

    xs = tuple(jnp.moveaxis(t, 1, 0) for t in (r, w, k, v, kk, a))
    s_final, y = lax.scan(step, s0.astype(F32), xs)
    return jnp.moveaxis(y, 0, 1), s_final


def mixer_rwkv(h, mu, w_r, w_k, w_v, w0, w1, w2, a0, a1, a2, g1, g2, k_k, k_a, r_k, ln_g, ln_b, w_o,
               s0_f, s0_b):
    b, n, _ = h.shape
    dp, dn = centred_shift(h)
    xmix = h[:, :, None, :] + dp[:, :, None, :] * mu[0] + dn[:, :, None, :] * mu[1]
    xr, xw, xk, xv, xa, xg = (xmix[:, :, i] for i in range(6))
    r = jnp.einsum('btd,de->bte', xr, w_r).astype(F32)
    k = jnp.einsum('btd,de->bte', xk, w_k).astype(F32)
    v = jnp.einsum('btd,de->bte', xv, w_v).astype(F32)
    wl = w0[:, None, None, :] + jnp.einsum('jbtr,jrd->jbtd', jnp.tanh(jnp.einsum('btd,jdr->jbtr', xw, w1)), w2)
    decay = jnp.exp(-jnp.exp(-jax.nn.softplus(-wl.astype(F32)) - 0.5))
    a = jax.nn.sigmoid((a0[:, None, None, :]
                        + jnp.einsum('jbtr,jrd->jbtd', jnp.einsum('btd,jdr->jbtr', xa, a1), a2)).astype(F32))
    g = jnp.einsum('btr,rd->btd', jax.nn.sigmoid(jnp.einsum('btd,dr->btr', xg, g1)), g2).astype(F32)
    kk = split_heads(k * k_k.astype(F32))
    kk = kk * lax.rsqrt(jnp.sum(kk * kk, axis=-1, keepdims=True) + L2_EPS)
    k_dir = split_heads(k[None] * (1 + (a - 1) * k_a.astype(F32)))
    rh, vh, a_h, w_h = split_heads(r), split_heads(v), split_heads(a), split_heads(decay)
    y_f, s_f = rwkv7_scan(rh, w_h[0], k_dir[0], vh, kk, a_h[0], s0_f)
    y_b, s_b = rwkv7_scan(rh[:, ::-1], w_h[1][:, ::-1], k_dir[1][:, ::-1], vh[:, ::-1], kk[:, ::-1],
                          a_h[1][:, ::-1], s0_b)
    y = y_f + y_b[:, ::-1]
    mean = jnp.mean(y, axis=-1, keepdims=True)
    var = jnp.mean(jnp.square(y - mean), axis=-1, keepdims=True)
    y = ((y - mean) * lax.rsqrt(var + GN_EPS)).reshape(b, n, D_MODEL) * ln_g.astype(F32) + ln_b.astype(F32)
    bonus = jnp.sum(rh[None] * k_dir * r_k.astype(F32), axis=(0, -1))[..., None] * vh
    out = ((y + bonus.reshape(b, n, D_MODEL)) * g).astype(h.dtype)
    return jnp.einsum('btd,de->bte', out, w_o), s_f, s_b


def trunk(x, cvec, rope, ctx, P):
    collected = {'k': [], 'v': [], 'ssd_f': [], 'ssd_b': [], 'rwkv_f': [], 'rwkv_b': []}
    b = x.shape[0]
    for layer in range(DEPTH):
        j = layer // 2
        sh1, sc1, gt1, sh2, sc2, gt2 = ada_params(cvec, P['mod_w'][layer], P['mod_b'][layer])
        h = modulate(rmsnorm(x, P['norm_mix_g'][layer]), sh1, sc1)
        if layer % 2 == 0:
            if ctx is None:
                ck = cv = None
                h0f = h0b = jnp.zeros((b, SSD_HEADS, SSD_HEAD_DIM, SSD_STATE), F32)
            else:
                ck, cv = ctx['k'][:, j], ctx['v'][:, j]
                h0f, h0b = ctx['ssd_f'][:, j], ctx['ssd_b'][:, j]
            out, k, v, hf, hb = mixer_ab(h, P['ab_w_in'][j], P['ab_w_out'][j], P['attn_q_g'][j], P['attn_k_g'][j],
                                         P['ssd_conv_w'][j], P['ssd_conv_b'][j], P['ssd_dt_bias'][j],
                                         P['ssd_a_log'][j], P['ssd_d'][j], P['ssd_norm_g'][j],
                                         rope, ck, cv, h0f, h0b)
            produced = {'k': k, 'v': v, 'ssd_f': hf, 'ssd_b': hb}
        else:
            if ctx is None:
                s0f = s0b = jnp.zeros((b, RWKV_HEADS, RWKV_HEAD_DIM, RWKV_HEAD_DIM), F32)
            else:
                s0f, s0b = ctx['rwkv_f'][:, j], ctx['rwkv_b'][:, j]
            out, sf, sb = mixer_rwkv(h, P['rwkv_mu'][j], P['rwkv_w_r'][j], P['rwkv_w_k'][j], P['rwkv_w_v'][j],
                                     P['rwkv_w0'][j], P['rwkv_w1'][j], P['rwkv_w2'][j],
                                     P['rwkv_a0'][j], P['rwkv_a1'][j], P['rwkv_a2'][j],
                                     P['rwkv_g1'][j], P['rwkv_g2'][j], P['rwkv_k_k'][j], P['rwkv_k_a'][j],
                                     P['rwkv_r_k'][j], P['rwkv_ln_g'][j], P['rwkv_ln_b'][j], P['rwkv_w_o'][j],
                                     s0f, s0b)
            produced = {'rwkv_f': sf, 'rwkv_b': sb}
        if ctx is None:
            for name, t in produced.items():
                collected[name].append(t.astype(x.dtype))
        x = x + gt1 * out
        h = modulate(rmsnorm(x, P['norm_ffn_g'][layer]), sh2, sc2)
        x = x + gt2 * swiglu(h, P['ffn_w_gate'][layer], P['ffn_w_up'][layer], P['ffn_w_down'][layer])
    return rmsnorm(x, P['final_norm_g']), collected


def setup_inputs(seed: int = 0) -> dict:
    key = jax.random.key(seed)
    keys = iter(jax.random.split(key, 64))

    def nrm(shape, scale):
        return jax.random.normal(next(keys), shape, F32) * scale

    def unif(shape, lo, hi):
        return jax.random.uniform(next(keys), shape, F32, lo, hi)

    D = D_MODEL
    dt0 = jnp.exp(unif((N_EVEN, 2, SSD_HEADS), math.log(1e-3), math.log(1e-1)))
    return {
        'x_prompt': nrm((BATCH, SEQ, D), 1.0),
        'x_sample': nrm((DEC_BATCH, DEC_SEQ, D), 1.0),
        'cache_attn_k': nrm((DEC_BATCH, N_EVEN, PAST_LEN, ATTN_KV_HEADS, HEAD_DIM), 1.0),
        'cache_attn_v': nrm((DEC_BATCH, N_EVEN, PAST_LEN, ATTN_KV_HEADS, HEAD_DIM), 1.0),
        'state_ssd_fwd': nrm((DEC_BATCH, N_EVEN, SSD_HEADS, SSD_HEAD_DIM, SSD_STATE), 0.1),
        'state_ssd_bwd': nrm((DEC_BATCH, N_EVEN, SSD_HEADS, SSD_HEAD_DIM, SSD_STATE), 0.1),
        'state_rwkv_fwd': nrm((DEC_BATCH, N_ODD, RWKV_HEADS, RWKV_HEAD_DIM, RWKV_HEAD_DIM), 0.3),
        'state_rwkv_bwd': nrm((DEC_BATCH, N_ODD, RWKV_HEADS, RWKV_HEAD_DIM, RWKV_HEAD_DIM), 0.3),
        'c': nrm((DEC_BATCH, D), 1.0),
        'c_ctx': nrm((D,), 1.0),
        'mod_w': nrm((DEPTH, D, 6 * D), 0.3 * D ** -0.5),
        'mod_b': nrm((DEPTH, 6 * D), 0.02),
        'norm_mix_g': 1.0 + nrm((DEPTH, D), 0.05),
        'norm_ffn_g': 1.0 + nrm((DEPTH, D), 0.05),
        'ffn_w_gate': nrm((DEPTH, D, FFN_DIM), D ** -0.5),
        'ffn_w_up': nrm((DEPTH, D, FFN_DIM), D ** -0.5),
        'ffn_w_down': nrm((DEPTH, FFN_DIM, D), FFN_DIM ** -0.5),
        'ab_w_in': nrm((N_EVEN, D, AB_IN_DIM), D ** -0.5),
        'ab_w_out': nrm((N_EVEN, AB_MIX_DIM, D), AB_MIX_DIM ** -0.5),
        'attn_q_g': 1.0 + nrm((N_EVEN, HEAD_DIM), 0.05),
        'attn_k_g': 1.0 + nrm((N_EVEN, HEAD_DIM), 0.05),
        'ssd_conv_w': nrm((N_EVEN, SSD_CONV_K, SSD_CONV_DIM), SSD_CONV_K ** -0.5),
        'ssd_conv_b': nrm((N_EVEN, SSD_CONV_DIM), 0.02),
        'ssd_dt_bias': dt0 + jnp.log(-jnp.expm1(-dt0)),
        'ssd_a_log': jnp.log(unif((N_EVEN, 2, SSD_HEADS), 1.0, 16.0)),
        'ssd_d': 1.0 + nrm((N_EVEN, SSD_HEADS), 0.1),
        'ssd_norm_g': 1.0 + nrm((N_EVEN, SSD_D_INNER), 0.05),
        'rwkv_mu': unif((N_ODD, 2, 6, D), 0.0, 0.5),
        'rwkv_w_r': nrm((N_ODD, D, D), D ** -0.5),
        'rwkv_w_k': nrm((N_ODD, D, D), D ** -0.5),
        'rwkv_w_v': nrm((N_ODD, D, D), D ** -0.5),
        'rwkv_w0': unif((N_ODD, 2, D), -6.0, 1.0),
        'rwkv_w1': nrm((N_ODD, 2, D, R_DECAY), D ** -0.5),
        'rwkv_w2': nrm((N_ODD, 2, R_DECAY, D), 0.3 * R_DECAY ** -0.5),
        'rwkv_a0': nrm((N_ODD, 2, D), 0.3),
        'rwkv_a1': nrm((N_ODD, 2, D, R_AAA), D ** -0.5),
        'rwkv_a2': nrm((N_ODD, 2, R_AAA, D), 0.3 * R_AAA ** -0.5),
        'rwkv_g1': nrm((N_ODD, D, R_GATE), D ** -0.5),
        'rwkv_g2': nrm((N_ODD, R_GATE, D), R_GATE ** -0.5),
        'rwkv_k_k': 0.85 + nrm((N_ODD, D), 0.05),
        'rwkv_k_a': 1.0 + nrm((N_ODD, D), 0.05),
        'rwkv_r_k': nrm((N_ODD, RWKV_HEADS, RWKV_HEAD_DIM), 0.1),
        'rwkv_ln_g': 1.0 + nrm((N_ODD, D), 0.05),
        'rwkv_ln_b': nrm((N_ODD, D), 0.02),
        'rwkv_w_o': nrm((N_ODD, D, D), D ** -0.5),
        'final_norm_g': 1.0 + nrm((D,), 0.05),
    }


def reference(x_prompt, x_sample, cache_attn_k, cache_attn_v, state_ssd_fwd, state_ssd_bwd,
              state_rwkv_fwd, state_rwkv_bwd, c, c_ctx, mod_w, mod_b, norm_mix_g, norm_ffn_g,
              ffn_w_gate, ffn_w_up, ffn_w_down, ab_w_in, ab_w_out, attn_q_g, attn_k_g,
              ssd_conv_w, ssd_conv_b, ssd_dt_bias, ssd_a_log, ssd_d, ssd_norm_g,
              rwkv_mu, rwkv_w_r, rwkv_w_k, rwkv_w_v, rwkv_w0, rwkv_w1, rwkv_w2,
              rwkv_a0, rwkv_a1, rwkv_a2, rwkv_g1, rwkv_g2, rwkv_k_k, rwkv_k_a, rwkv_r_k,
              rwkv_ln_g, rwkv_ln_b, rwkv_w_o, final_norm_g):
    P = dict(mod_w=mod_w, mod_b=mod_b, norm_mix_g=norm_mix_g, norm_ffn_g=norm_ffn_g,
             ffn_w_gate=ffn_w_gate, ffn_w_up=ffn_w_up, ffn_w_down=ffn_w_down,
             ab_w_in=ab_w_in, ab_w_out=ab_w_out, attn_q_g=attn_q_g, attn_k_g=attn_k_g,
             ssd_conv_w=ssd_conv_w, ssd_conv_b=ssd_conv_b, ssd_dt_bias=ssd_dt_bias,
             ssd_a_log=ssd_a_log, ssd_d=ssd_d, ssd_norm_g=ssd_norm_g,
             rwkv_mu=rwkv_mu, rwkv_w_r=rwkv_w_r, rwkv_w_k=rwkv_w_k, rwkv_w_v=rwkv_w_v,
             rwkv_w0=rwkv_w0, rwkv_w1=rwkv_w1, rwkv_w2=rwkv_w2,
             rwkv_a0=rwkv_a0, rwkv_a1=rwkv_a1, rwkv_a2=rwkv_a2, rwkv_g1=rwkv_g1, rwkv_g2=rwkv_g2,
             rwkv_k_k=rwkv_k_k, rwkv_k_a=rwkv_k_a, rwkv_r_k=rwkv_r_k,
             rwkv_ln_g=rwkv_ln_g, rwkv_ln_b=rwkv_ln_b, rwkv_w_o=rwkv_w_o, final_norm_g=final_norm_g)
    y_prompt, st = trunk(x_prompt, c_ctx, None, None, P)
    ctx = dict(k=cache_attn_k, v=cache_attn_v, ssd_f=state_ssd_fwd, ssd_b=state_ssd_bwd,
               rwkv_f=state_rwkv_fwd, rwkv_b=state_rwkv_bwd)
    y_sample, _ = trunk(x_sample, c, grid_rope(x_sample.shape[1]), ctx, P)
    new_attn_k = jnp.stack(st['k'], axis=1)
    new_attn_v = jnp.stack(st['v'], axis=1)
    new_ssd_fwd = jnp.stack(st['ssd_f'], axis=1)
    new_ssd_bwd = jnp.stack(st['ssd_b'], axis=1)
    new_rwkv_fwd = jnp.stack(st['rwkv_f'], axis=1)
    new_rwkv_bwd = jnp.stack(st['rwkv_b'], axis=1)
    return (y_prompt, y_sample, new_attn_k, new_attn_v, new_ssd_fwd, new_ssd_bwd, new_rwkv_fwd, new_rwkv_bwd)
```

```python
import functools

import jax
import jax.numpy as jnp
from jax import lax
from jax.experimental import pallas as pl
from jax.experimental.pallas import tpu as pltpu

F32 = jnp.float32
BF16 = jnp.bfloat16

D_MODEL = 1024
BATCH = 16
SEQ = 256
DEC_BATCH = 2
DEC_SEQ = 1024
PAST_LEN = 256
GRID_W = 64
ATTN_HEADS = 8
ATTN_KV_HEADS = 2
HEAD_DIM = 64
ROPE_THETA = 10000.0
ATTN_Q_DIM = ATTN_HEADS * HEAD_DIM
ATTN_KV_DIM = ATTN_KV_HEADS * HEAD_DIM
SSD_HEADS = 8
SSD_HEAD_DIM = 64
SSD_D_INNER = SSD_HEADS * SSD_HEAD_DIM
SSD_GROUPS = 2
SSD_STATE = 64
SSD_CONV_K = 5
SSD_CHUNK = 128
SSD_CONV_DIM = SSD_D_INNER + 2 * SSD_GROUPS * SSD_STATE
AB_IN_DIM = ATTN_Q_DIM + 2 * ATTN_KV_DIM + SSD_D_INNER + SSD_CONV_DIM + 2 * SSD_HEADS
RWKV_HEAD_DIM = 64
RWKV_HEADS = D_MODEL // RWKV_HEAD_DIM
FFN_DIM = (((8 * D_MODEL + 2) // 3 + 255) // 256) * 256
RMS_EPS = 1e-6
GN_EPS = 64e-5
L2_EPS = 1e-12

R_CTX = BATCH * SEQ
R_SMP = DEC_BATCH * DEC_SEQ
R_ALL = R_CTX + R_SMP
LANES = 128
QKV_DIM = ATTN_Q_DIM + 2 * ATTN_KV_DIM
DT_PAD = LANES
AB_PAD = QKV_DIM + SSD_D_INNER + SSD_CONV_DIM + DT_PAD
VMEM_LIMIT = 56 * 1024 * 1024


def _cparams(sem):
    return pltpu.CompilerParams(dimension_semantics=sem, vmem_limit_bytes=VMEM_LIMIT)


def _mod_index(i, tm):
    n_ctx = R_CTX // tm
    per = DEC_SEQ // tm
    return jnp.where(i < n_ctx, 0, 1 + jnp.maximum(i - n_ctx, 0) // per)


def _sigmoid(x):
    return 1.0 / (1.0 + jnp.exp(-x))


def _silu(x):
    return x * _sigmoid(x)


def _softplus(x):
    return jnp.maximum(x, 0.0) + jnp.log1p(jnp.exp(-jnp.abs(x)))


def _rms(x, g):
    return x * lax.rsqrt(jnp.mean(x * x, axis=-1, keepdims=True) + RMS_EPS) * g


def _group_allsum(x, group):
    n = x.shape[-1]
    ax = x.ndim - 1
    lane = lax.broadcasted_iota(jnp.int32, x.shape, ax)
    s = 1
    while s < group:
        up = pltpu.roll(x, n - s, axis=ax)
        dn = pltpu.roll(x, s, axis=ax)
        x = x + jnp.where((lane & s) == 0, up, dn)
        s *= 2
    return x


def _dot(a, b):
    return jnp.dot(a, b, preferred_element_type=F32)


ADA_TN = 1536


def _ada_kernel(c_ref, w_ref, b_ref, o_ref):
    s = _silu(c_ref[...]).astype(BF16)
    o_ref[0] = _dot(s, w_ref[0].astype(BF16)) + b_ref[0]


def _ada(cv, mod_w, mod_b):
    depth = mod_w.shape[0]
    n = mod_w.shape[2]
    return pl.pallas_call(
        _ada_kernel,
        out_shape=jax.ShapeDtypeStruct((depth, 8, n), F32),
        grid=(depth, n // ADA_TN),
        in_specs=[
            pl.BlockSpec((8, D_MODEL), lambda l, j: (0, 0)),
            pl.BlockSpec((1, D_MODEL, ADA_TN), lambda l, j: (l, 0, j)),
            pl.BlockSpec((1, 1, ADA_TN), lambda l, j: (l, 0, j)),
        ],
        out_specs=pl.BlockSpec((1, 8, ADA_TN), lambda l, j: (l, 0, j)),
        compiler_params=_cparams(("arbitrary", "arbitrary")),
        name="ada",
    )(cv, mod_w, mod_b.reshape(depth, 1, n))


INPROJ_TM = 512


def _inproj_kernel(x_ref, g_ref, mod_ref, w_ref, qkv_ref, z_ref, xbc_ref, dt_ref):
    m = mod_ref[0]
    h = _rms(x_ref[...], g_ref[...]) * (1 + m[1:2]) + m[0:1]
    p = _dot(h.astype(BF16), w_ref[...])
    qkv_ref[...] = p[:, 0:QKV_DIM]
    z_ref[...] = p[:, QKV_DIM:QKV_DIM + SSD_D_INNER]
    xbc_ref[...] = p[:, QKV_DIM + SSD_D_INNER:QKV_DIM + SSD_D_INNER + SSD_CONV_DIM]
    dt_ref[...] = p[:, QKV_DIM + SSD_D_INNER + SSD_CONV_DIM:AB_PAD]


def _inproj(x, g, mod, w_pad):
    tm = INPROJ_TM
    row = lambda i: (i, 0)
    return pl.pallas_call(
        _inproj_kernel,
        out_shape=(
            jax.ShapeDtypeStruct((R_ALL, QKV_DIM), F32),
            jax.ShapeDtypeStruct((R_ALL, SSD_D_INNER), F32),
            jax.ShapeDtypeStruct((R_ALL, SSD_CONV_DIM), F32),
            jax.ShapeDtypeStruct((R_ALL, DT_PAD), F32),
        ),
        grid=(R_ALL // tm,),
        in_specs=[
            pl.BlockSpec((tm, D_MODEL), row),
            pl.BlockSpec((1, D_MODEL), lambda i: (0, 0)),
            pl.BlockSpec((1, 8, D_MODEL), lambda i: (_mod_index(i, tm), 0, 0)),
            pl.BlockSpec((D_MODEL, AB_PAD), lambda i: (0, 0)),
        ],
        out_specs=(
            pl.BlockSpec((tm, QKV_DIM), row),
            pl.BlockSpec((tm, SSD_D_INNER), row),
            pl.BlockSpec((tm, SSD_CONV_DIM), row),
            pl.BlockSpec((tm, DT_PAD), row),
        ),
        compiler_params=_cparams(("arbitrary",)),
        name="inproj",
    )(x, g, mod, w_pad)


QK_TM = 256


def _qkprep_kernel(qkv_ref, qg_ref, kg_ref, cos_ref, sin_ref, qn_ref, kn_ref):
    cos = cos_ref[...]
    sin = sin_ref[...]

    def norm_rope(x, g, reps):
        ms = _group_allsum(x * x, HEAD_DIM) * (1.0 / HEAD_DIM)
        y = x * lax.rsqrt(ms + RMS_EPS) * g
        n = y.shape[-1]
        lane = lax.broadcasted_iota(jnp.int32, y.shape, 1)
        half = HEAD_DIM // 2
        swapped = jnp.where((lane & half) == 0, pltpu.roll(y, n - half, axis=1), pltpu.roll(y, half, axis=1))
        c = jnp.concatenate([cos] * reps, axis=1) if reps > 1 else cos
        s = jnp.concatenate([sin] * reps, axis=1) if reps > 1 else sin
        return y * c + swapped * s

    q = qkv_ref[:, 0:ATTN_Q_DIM]
    k = qkv_ref[:, ATTN_Q_DIM:ATTN_Q_DIM + ATTN_KV_DIM]
    qn_ref[...] = norm_rope(q, qg_ref[...], ATTN_Q_DIM // LANES)
    kn_ref[...] = norm_rope(k, kg_ref[...], 1)


def _qkprep(qkv, qg, kg, cos_t, sin_t):
    tm = QK_TM
    n_ctx = R_CTX // tm
    per = DEC_SEQ // tm
    tab = lambda i: (jnp.where(i < n_ctx, 0, 1 + jnp.maximum(i - n_ctx, 0) % per), 0)
    return pl.pallas_call(
        _qkprep_kernel,
        out_shape=(
            jax.ShapeDtypeStruct((R_ALL, ATTN_Q_DIM), F32),
            jax.ShapeDtypeStruct((R_ALL, ATTN_KV_DIM), F32),
        ),
        grid=(R_ALL // tm,),
        in_specs=[
            pl.BlockSpec((tm, QKV_DIM), lambda i: (i, 0)),
            pl.BlockSpec((1, ATTN_Q_DIM), lambda i: (0, 0)),
            pl.BlockSpec((1, ATTN_KV_DIM), lambda i: (0, 0)),
            pl.BlockSpec((tm, LANES), tab),
            pl.BlockSpec((tm, LANES), tab),
        ],
        out_specs=(
            pl.BlockSpec((tm, ATTN_Q_DIM), lambda i: (i, 0)),
            pl.BlockSpec((tm, ATTN_KV_DIM), lambda i: (i, 0)),
        ),
        compiler_params=_cparams(("arbitrary",)),
        name="qkprep",
    )(qkv, qg, kg, cos_t, sin_t)


def _attn_core(q, ks, vs):
    tq = q.shape[0]
    grp = ATTN_HEADS // ATTN_KV_HEADS
    scale = HEAD_DIM ** -0.5
    outs = []
    for g in range(ATTN_KV_HEADS):
        sl = slice(g * HEAD_DIM, (g + 1) * HEAD_DIM)
        qs = jnp.concatenate(
            [q[:, (g * grp + j) * HEAD_DIM:(g * grp + j + 1) * HEAD_DIM] for j in range(grp)], axis=0).astype(BF16)
        ss = [lax.dot_general(qs, k[:, sl].astype(BF16), (((1,), (1,)), ((), ())),
                              preferred_element_type=F32) * scale for k in ks]
        m = ss[0].max(axis=-1, keepdims=True)
        for s in ss[1:]:
            m = jnp.maximum(m, s.max(axis=-1, keepdims=True))
        ps = [jnp.exp(s - m) for s in ss]
        l = ps[0].sum(axis=-1, keepdims=True)
        for p in ps[1:]:
            l = l + p.sum(axis=-1, keepdims=True)
        inv = 1.0 / l
        o = None
        for p, v in zip(ps, vs):
            t = _dot((p * inv).astype(BF16), v[:, sl].astype(BF16))
            o = t if o is None else o + t
        outs += [o[j * tq:(j + 1) * tq] for j in range(grp)]
    return jnp.concatenate(outs, axis=1)


def _attn_ctx_kernel(q_ref, k_ref, v_ref, o_ref):
    o_ref[...] = _attn_core(q_ref[...], [k_ref[...]], [v_ref[...]]).astype(BF16)


def _attn_smp_kernel(q_ref, k_ref, v_ref, ck_ref, cv_ref, o_ref):
    ks = [ck_ref[0], k_ref[...]]
    vs = [cv_ref[0], v_ref[...]]
    o_ref[...] = _attn_core(q_ref[...], ks, vs).astype(BF16)


ATTN_TQ = 128


def _attention(qn, kn, qkv, cache_k, cache_v):
    v_col = (ATTN_Q_DIM + ATTN_KV_DIM) // ATTN_KV_DIM
    ctx = pl.pallas_call(
        _attn_ctx_kernel,
        out_shape=jax.ShapeDtypeStruct((R_CTX, ATTN_Q_DIM), BF16),
        grid=(BATCH,),
        in_specs=[
            pl.BlockSpec((SEQ, ATTN_Q_DIM), lambda b: (b, 0)),
            pl.BlockSpec((SEQ, ATTN_KV_DIM), lambda b: (b, 0)),
            pl.BlockSpec((SEQ, ATTN_KV_DIM), lambda b: (b, v_col)),
        ],
        out_specs=pl.BlockSpec((SEQ, ATTN_Q_DIM), lambda b: (b, 0)),
        compiler_params=_cparams(("arbitrary",)),
        name="attn_ctx",
    )(qn, kn, qkv)
    nq = DEC_SEQ // ATTN_TQ
    q0 = R_CTX // ATTN_TQ
    s0 = R_CTX // DEC_SEQ
    smp = pl.pallas_call(
        _attn_smp_kernel,
        out_shape=jax.ShapeDtypeStruct((R_SMP, ATTN_Q_DIM), BF16),
        grid=(DEC_BATCH, nq),
        in_specs=[
            pl.BlockSpec((ATTN_TQ, ATTN_Q_DIM), lambda b, i: (q0 + b * nq + i, 0)),
            pl.BlockSpec((DEC_SEQ, ATTN_KV_DIM), lambda b, i: (s0 + b, 0)),
            pl.BlockSpec((DEC_SEQ, ATTN_KV_DIM), lambda b, i: (s0 + b, v_col)),
            pl.BlockSpec((1, PAST_LEN, ATTN_KV_DIM), lambda b, i: (b, 0, 0)),
            pl.BlockSpec((1, PAST_LEN, ATTN_KV_DIM), lambda b, i: (b, 0, 0)),
        ],
        out_specs=pl.BlockSpec((ATTN_TQ, ATTN_Q_DIM), lambda b, i: (b * nq + i, 0)),
        compiler_params=_cparams(("arbitrary", "arbitrary")),
        name="attn_smp",
    )(qn, kn, qkv, cache_k, cache_v)
    return ctx, smp


CONV_HALO = 8


def _cumsum_rows(a, reverse):
    n = a.shape[0]
    row = lax.broadcasted_iota(jnp.int32, a.shape, 0)
    s = 1
    while s < n:
        if reverse:
            a = a + jnp.where(row < n - s, pltpu.roll(a, n - s, axis=0), 0.0)
        else:
            a = a + jnp.where(row >= s, pltpu.roll(a, s, axis=0), 0.0)
        s *= 2
    return a


def _ssd_kernel(z_ref, xbc_ref, dt_ref, cw_ref, cb_ref, dtb_ref, a_ref, d_ref, g_ref, h0f_ref, h0b_ref,
                y_ref, hf_ref, hb_ref, pad_sc, xc_sc, dt_sc, y_sc, h_sc, *, seq):
    L = SSD_CHUNK
    nc = seq // L
    pad = SSD_CONV_K // 2
    zeros = jnp.zeros((CONV_HALO, SSD_CONV_DIM), F32)
    pad_sc[0:CONV_HALO, :] = zeros
    pad_sc[seq + CONV_HALO:seq + 2 * CONV_HALO, :] = zeros
    pad_sc[CONV_HALO:seq + CONV_HALO, :] = xbc_ref[...]
    h_sc[0] = h0f_ref[0]
    h_sc[1] = h0b_ref[0]

    def conv_chunk(c, carry):
        r0 = pl.multiple_of(c * L, L)
        win = pad_sc[pl.ds(r0, L + 2 * CONV_HALO), :]
        acc = cb_ref[...]
        for i in range(SSD_CONV_K):
            acc = acc + win[CONV_HALO - pad + i:CONV_HALO - pad + i + L, :] * cw_ref[i:i + 1, :]
        xc_sc[pl.ds(r0, L), :] = _silu(acc)
        dt_sc[pl.ds(r0, L), :] = _softplus(dt_ref[pl.ds(r0, L), :] + dtb_ref[...])
        return carry

    lax.fori_loop(0, nc, conv_chunk, 0)

    rr = lax.broadcasted_iota(jnp.int32, (L, L), 0)
    cc = lax.broadcasted_iota(jnp.int32, (L, L), 1)
    grp = SSD_HEADS // SSD_GROUPS
    P = SSD_HEAD_DIM
    N = SSD_STATE

    def make_chunk(dirn):
        mask = (rr >= cc) if dirn == 0 else (rr <= cc)

        def chunk(ci, carry):
            c = ci if dirn == 0 else nc - 1 - ci
            r0 = pl.multiple_of(c * L, L)
            xs = xc_sc[pl.ds(r0, L), 0:SSD_D_INNER]
            bm = xc_sc[pl.ds(r0, L), SSD_D_INNER:SSD_D_INNER + SSD_GROUPS * N]
            cm = xc_sc[pl.ds(r0, L), SSD_D_INNER + SSD_GROUPS * N:SSD_CONV_DIM]
            dtc = dt_sc[pl.ds(r0, L), :]
            acs = _cumsum_rows(dtc * a_ref[...], reverse=(dirn == 1))
            acs_t = acs.T
            tot = acs[L - 1:L, :] if dirn == 0 else acs[0:1, :]
            dec_end = jnp.exp(tot - acs)
            eacs = jnp.exp(acs)
            cdec = jnp.exp(tot)
            cb = [lax.dot_general(cm[:, g * N:(g + 1) * N].astype(BF16), bm[:, g * N:(g + 1) * N].astype(BF16),
                                  (((1,), (1,)), ((), ())), preferred_element_type=F32)
                  for g in range(SSD_GROUPS)]
            if dirn == 0:
                y_sc[pl.ds(r0, L), :] = xs * d_ref[...]
            for h in range(SSD_HEADS):
                g = h // grp
                ln = h + SSD_HEADS * dirn
                seg = jnp.exp(jnp.where(mask, acs[:, ln:ln + 1] - acs_t[ln:ln + 1, :], -jnp.inf))
                sc = (cb[g] * seg).astype(BF16)
                xdt = (xs[:, h * P:(h + 1) * P] * dtc[:, ln:ln + 1]).astype(BF16)
                hin = h_sc[dirn, h]
                ce = (cm[:, g * N:(g + 1) * N] * eacs[:, ln:ln + 1]).astype(BF16)
                yh = _dot(sc, xdt) + lax.dot_general(ce, hin.astype(BF16), (((1,), (1,)), ((), ())),
                                                     preferred_element_type=F32)
                bd = (bm[:, g * N:(g + 1) * N] * dec_end[:, ln:ln + 1]).astype(BF16)
                st = lax.dot_general(xdt, bd, (((0,), (0,)), ((), ())), preferred_element_type=F32)
                h_sc[dirn, h] = hin * cdec[:, ln:ln + 1] + st
                y_sc[pl.ds(r0, L), h * P:(h + 1) * P] += yh
            return carry

        return chunk

    lax.fori_loop(0, nc, make_chunk(0), 0)
    lax.fori_loop(0, nc, make_chunk(1), 0)

    def out_chunk(c, carry):
        r0 = pl.multiple_of(c * L, L)
        y = y_sc[pl.ds(r0, L), :] * _silu(z_ref[pl.ds(r0, L), :])
        y_ref[pl.ds(r0, L), :] = _rms(y, g_ref[...]).astype(BF16)
        return carry

    lax.fori_loop(0, nc, out_chunk, 0)
    hf_ref[0] = h_sc[0]
    hb_ref[0] = h_sc[1]


def _ssd(z, xbc, dt, cw, cb, dtb, a_row, d_row, g, h0f, h0b, *, seq, nb, row0):
    blk0 = row0 // seq
    row = lambda b: (blk0 + b, 0)
    fixed = lambda b: (0, 0)
    st = lambda b: (b, 0, 0, 0)
    st_shape = (nb, SSD_HEADS, SSD_HEAD_DIM, SSD_STATE)
    st_blk = (1, SSD_HEADS, SSD_HEAD_DIM, SSD_STATE)
    return pl.pallas_call(
        functools.partial(_ssd_kernel, seq=seq),
        out_shape=(
            jax.ShapeDtypeStruct((nb * seq, SSD_D_INNER), BF16),
            jax.ShapeDtypeStruct(st_shape, F32),
            jax.ShapeDtypeStruct(st_shape, F32),
        ),
        grid=(nb,),
        in_specs=[
            pl.BlockSpec((seq, SSD_D_INNER), row),
            pl.BlockSpec((seq, SSD_CONV_DIM), row),
            pl.BlockSpec((seq, DT_PAD), row),
            pl.BlockSpec((SSD_CONV_K, SSD_CONV_DIM), fixed),
            pl.BlockSpec((1, SSD_CONV_DIM), fixed),
            pl.BlockSpec((1, DT_PAD), fixed),
            pl.BlockSpec((1, DT_PAD), fixed),
            pl.BlockSpec((1, SSD_D_INNER), fixed),
            pl.BlockSpec((1, SSD_D_INNER), fixed),
            pl.BlockSpec(st_blk, st),
            pl.BlockSpec(st_blk, st),
        ],
        out_specs=(
            pl.BlockSpec((seq, SSD_D_INNER), lambda b: (b, 0)),
            pl.BlockSpec(st_blk, st),
            pl.BlockSpec(st_blk, st),
        ),
        scratch_shapes=[
            pltpu.VMEM((seq + 2 * CONV_HALO, SSD_CONV_DIM), F32),
            pltpu.VMEM((seq, SSD_CONV_DIM), F32),
            pltpu.VMEM((seq, DT_PAD), F32),
            pltpu.VMEM((seq, SSD_D_INNER), F32),
            pltpu.VMEM((2, SSD_HEADS, SSD_HEAD_DIM, SSD_STATE), F32),
        ],
        compiler_params=_cparams(("arbitrary",)),
        name=f"ssd_{seq}",
    )(z, xbc, dt, cw, cb, dtb, a_row, d_row, g, h0f, h0b)


RES_TM = 512


def _mixres_kernel(x_ref, mod_ref, a1_ref, a2_ref, w_ref, o_ref):
    k1 = a1_ref.shape[1]
    out = _dot(a1_ref[...], w_ref[0:k1, :]) + _dot(a2_ref[...], w_ref[k1:, :])
    o_ref[...] = x_ref[...] + mod_ref[0][2:3] * out


def _mixres(x, mod, a1, a2, w):
    tm = RES_TM
    row = lambda i: (i, 0)
    return pl.pallas_call(
        _mixres_kernel,
        out_shape=jax.ShapeDtypeStruct((R_ALL, D_MODEL), F32),
        grid=(R_ALL // tm,),
        in_specs=[
            pl.BlockSpec((tm, D_MODEL), row),
            pl.BlockSpec((1, 8, D_MODEL), lambda i: (_mod_index(i, tm), 0, 0)),
            pl.BlockSpec((tm, a1.shape[1]), row),
            pl.BlockSpec((tm, a2.shape[1]), row),
            pl.BlockSpec(w.shape, lambda i: (0, 0)),
        ],
        out_specs=pl.BlockSpec((tm, D_MODEL), row),
        compiler_params=_cparams(("arbitrary",)),
        name="mixres",
    )(x, mod, a1, a2, w)


FFN_TM = 1024
FFN_TF = 256


def _ffn_kernel(x_ref, g_ref, mod_ref, wg_ref, wu_ref, wd_ref, fg_ref, o_ref, h_sc, acc_sc, *, final):
    j = pl.program_id(1)

    @pl.when(j == 0)
    def _():
        m = mod_ref[0]
        h = _rms(x_ref[...], g_ref[...]) * (1 + m[4:5]) + m[3:4]
        h_sc[...] = h.astype(BF16)
        acc_sc[...] = jnp.zeros_like(acc_sc)

    h = h_sc[...]
    hid = _silu(_dot(h, wg_ref[...])) * _dot(h, wu_ref[...])
    acc_sc[...] += _dot(hid.astype(BF16), wd_ref[...])

    @pl.when(j == pl.num_programs(1) - 1)
    def _():
        y = x_ref[...] + mod_ref[0][5:6] * acc_sc[...]
        if final:
            y = _rms(y, fg_ref[...])
        o_ref[...] = y


def _ffn(x, g, mod, wg, wu, wd, fg, *, final):
    tm, tf = FFN_TM, FFN_TF
    row = lambda i, j: (i, 0)
    return pl.pallas_call(
        functools.partial(_ffn_kernel, final=final),
        out_shape=jax.ShapeDtypeStruct((R_ALL, D_MODEL), F32),
        grid=(R_ALL // tm, FFN_DIM // tf),
        in_specs=[
            pl.BlockSpec((tm, D_MODEL), row),
            pl.BlockSpec((1, D_MODEL), lambda i, j: (0, 0)),
            pl.BlockSpec((1, 8, D_MODEL), lambda i, j: (_mod_index(i, tm), 0, 0)),
            pl.BlockSpec((D_MODEL, tf), lambda i, j: (0, j)),
            pl.BlockSpec((D_MODEL, tf), lambda i, j: (0, j)),
            pl.BlockSpec((tf, D_MODEL), lambda i, j: (j, 0)),
            pl.BlockSpec((1, D_MODEL), lambda i, j: (0, 0)),
        ],
        out_specs=pl.BlockSpec((tm, D_MODEL), row),
        scratch_shapes=[pltpu.VMEM((tm, D_MODEL), BF16), pltpu.VMEM((tm, D_MODEL), F32)],
        compiler_params=_cparams(("arbitrary", "arbitrary")),
        name="ffn_final" if final else "ffn",
    )(x, g, mod, wg, wu, wd, fg)


RW_TM = 128
HALO = 8


def _rwkv_pre_kernel(x_ref, xp_ref, xn_ref, g_ref, mod_ref, mu_ref, wr_ref, wk_ref, wv_ref, w1_ref, w2_ref,
                     a1_ref, a2_ref, g1_ref, g2_ref, w0_ref, a0_ref, kk_ref, ka_ref, rk_ref,
                     r_o, v_o, kkn_o, g_o, bon_o, wf_o, wb_o, kdf_o, kdb_o, kaf_o, kab_o):
    i = pl.program_id(0)
    tm = RW_TM
    n_ctx = R_CTX // tm
    per_c = SEQ // tm
    per_s = DEC_SEQ // tm
    rel = jnp.where(i < n_ctx, i % per_c, jnp.maximum(i - n_ctx, 0) % per_s)
    last = jnp.where(i < n_ctx, per_c - 1, per_s - 1)
    m = mod_ref[0]

    def nm(x):
        return _rms(x, g_ref[...]) * (1 + m[1:2]) + m[0:1]

    h = nm(x_ref[...])
    prev_row = jnp.where(rel == 0, 0.0, nm(xp_ref[...])[HALO - 1:HALO, :])
    next_row = jnp.where(rel == last, 0.0, nm(xn_ref[...])[0:1, :])
    row = lax.broadcasted_iota(jnp.int32, h.shape, 0)
    hp = jnp.where(row == 0, prev_row, pltpu.roll(h, 1, axis=0))
    hn = jnp.where(row == tm - 1, next_row, pltpu.roll(h, tm - 1, axis=0))
    dp = hp - h
    dn = hn - h

    def mix(idx):
        return (h + dp * mu_ref[0, idx:idx + 1, :] + dn * mu_ref[1, idx:idx + 1, :]).astype(BF16)

    r = _dot(mix(0), wr_ref[...])
    k = _dot(mix(2), wk_ref[...])
    v = _dot(mix(3), wv_ref[...])
    lw = jnp.tanh(_dot(mix(1), w1_ref[...])).astype(BF16)
    la = _dot(mix(4), a1_ref[...]).astype(BF16)
    gg = _dot(_sigmoid(_dot(mix(5), g1_ref[...])).astype(BF16), g2_ref[...])

    kk = k * kk_ref[...]
    kk = kk * lax.rsqrt(_group_allsum(kk * kk, RWKV_HEAD_DIM) + L2_EPS)
    r_o[...] = r
    v_o[...] = v
    kkn_o[...] = kk
    g_o[...] = gg
    bsum = None
    for j, (w_o, kd_o, ka_o) in enumerate(((wf_o, kdf_o, kaf_o), (wb_o, kdb_o, kab_o))):
        wl = w0_ref[j:j + 1, :] + _dot(lw, w2_ref[j])
        w_o[...] = jnp.exp(-jnp.exp(-_softplus(-wl) - 0.5))
        a = _sigmoid(a0_ref[j:j + 1, :] + _dot(la, a2_ref[j]))
        kd = k * (1 + (a - 1) * ka_ref[...])
        kd_o[...] = kd
        ka_o[...] = kk * a
        t = r * kd * rk_ref[...]
        bsum = t if bsum is None else bsum + t
    bon_o[...] = _group_allsum(bsum, RWKV_HEAD_DIM) * v


def _rwkv_pre(x, g, mod, mu, wr, wk, wv, w1, w2, a1, a2, g1, g2, w0, a0, k_k, k_a, r_k):
    tm = RW_TM
    nblk = R_ALL // HALO
    per = tm // HALO
    row = lambda i: (i, 0)
    fixed2 = lambda i: (0, 0)
    fixed3 = lambda i: (0, 0, 0)
    full = lambda a: pl.BlockSpec(a.shape, fixed2 if a.ndim == 2 else fixed3)
    out = jax.ShapeDtypeStruct((R_ALL, D_MODEL), F32)
    return pl.pallas_call(
        _rwkv_pre_kernel,
        out_shape=(out,) * 11,
        grid=(R_ALL // tm,),
        in_specs=[
            pl.BlockSpec((tm, D_MODEL), row),
            pl.BlockSpec((HALO, D_MODEL), lambda i: (jnp.maximum(i * per - 1, 0), 0)),
            pl.BlockSpec((HALO, D_MODEL), lambda i: (jnp.minimum((i + 1) * per, nblk - 1), 0)),
            pl.BlockSpec((1, D_MODEL), fixed2),
            pl.BlockSpec((1, 8, D_MODEL), lambda i: (_mod_index(i, tm), 0, 0)),
            full(mu), full(wr), full(wk), full(wv), full(w1), full(w2), full(a1), full(a2), full(g1), full(g2),
            full(w0), full(a0), full(k_k), full(k_a), full(r_k),
        ],
        out_specs=(pl.BlockSpec((tm, D_MODEL), row),) * 11,
        compiler_params=_cparams(("arbitrary",)),
        name="rwkv_pre",
    )(x, x, x, g, mod, mu, wr, wk, wv, w1, w2, a1, a2, g1, g2, w0, a0, k_k, k_a, r_k)


SCAN_TT = 32
NK = RWKV_HEAD_DIM


def _scan_kernel(r_ref, w_ref, kd_ref, v_ref, kk_ref, ka_ref, s0_ref, y_ref, st_ref, s_sc):
    j = pl.program_id(1)

    @pl.when(j == 0)
    def _():
        s_sc[...] = s0_ref[...]

    def step(t, carry):
        vv = v_ref[t]
        sa = jnp.zeros_like(vv)
        for k in range(NK):
            sa = sa + s_sc[k] * kk_ref[t, pl.ds(k, 1), :]
        y = jnp.zeros_like(vv)
        for k in range(NK):
            sn = s_sc[k] * w_ref[t, pl.ds(k, 1), :] - sa * ka_ref[t, pl.ds(k, 1), :] + vv * kd_ref[t, pl.ds(k, 1), :]
            s_sc[k] = sn
            y = y + sn * r_ref[t, pl.ds(k, 1), :]
        y_ref[t] = y
        return carry

    lax.fori_loop(0, SCAN_TT, step, 0)

    @pl.when(j == pl.num_programs(1) - 1)
    def _():
        st_ref[...] = s_sc[...]


def _scan(r, w, kd, v, kk, ka, s0):
    seq, _, chains = r.shape
    tt = SCAN_TT
    blk = pl.BlockSpec((tt, NK, LANES), lambda c, j: (j, 0, c))
    sblk = pl.BlockSpec((NK, NK, LANES), lambda c, j: (0, 0, c))
    return pl.pallas_call(
        _scan_kernel,
        out_shape=(jax.ShapeDtypeStruct((seq, NK, chains), F32), jax.ShapeDtypeStruct((NK, NK, chains), F32)),
        grid=(chains // LANES, seq // tt),
        in_specs=[blk] * 6 + [sblk],
        out_specs=(blk, sblk),
        scratch_shapes=[pltpu.VMEM((NK, NK, LANES), F32)],
        compiler_params=_cparams(("arbitrary", "arbitrary")),
        name=f"rwkv_scan_{seq}",
    )(r, w, kd, v, kk, ka, s0)


POST_TM = 256


def _rwkv_post_kernel(x_ref, mod_ref, yf_ref, yb_ref, bon_ref, g_ref, lng_ref, lnb_ref, wo_ref, o_ref):
    y = yf_ref[...] + yb_ref[...]
    inv = 1.0 / RWKV_HEAD_DIM
    mean = _group_allsum(y, RWKV_HEAD_DIM) * inv
    d = y - mean
    var = _group_allsum(d * d, RWKV_HEAD_DIM) * inv
    yn = d * lax.rsqrt(var + GN_EPS) * lng_ref[...] + lnb_ref[...]
    out = ((yn + bon_ref[...]) * g_ref[...]).astype(BF16)
    o_ref[...] = x_ref[...] + mod_ref[0][2:3] * _dot(out, wo_ref[...])


def _rwkv_post(x, mod, yf, yb, bon, g, lng, lnb, wo):
    tm = POST_TM
    row = lambda i: (i, 0)
    fixed = lambda i: (0, 0)
    big = pl.BlockSpec((tm, D_MODEL), row)
    return pl.pallas_call(
        _rwkv_post_kernel,
        out_shape=jax.ShapeDtypeStruct((R_ALL, D_MODEL), F32),
        grid=(R_ALL // tm,),
        in_specs=[
            big,
            pl.BlockSpec((1, 8, D_MODEL), lambda i: (_mod_index(i, tm), 0, 0)),
            big, big, big, big,
            pl.BlockSpec((1, D_MODEL), fixed),
            pl.BlockSpec((1, D_MODEL), fixed),
            pl.BlockSpec((D_MODEL, D_MODEL), fixed),
        ],
        out_specs=big,
        compiler_params=_cparams(("arbitrary",)),
        name="rwkv_post",
    )(x, mod, yf, yb, bon, g, lng, lnb, wo)


def _to_chains(a, nb, seq):
    return a.reshape(nb, seq, RWKV_HEADS, NK).transpose(1, 3, 0, 2).reshape(seq, NK, nb * RWKV_HEADS)


def _from_chains(y, nb, seq):
    return y.reshape(seq, NK, nb, RWKV_HEADS).transpose(2, 0, 3, 1).reshape(nb * seq, D_MODEL)


def _scan_pass(arrs_f, arrs_b, s0, nb, seq, row0):
    nch = nb * RWKV_HEADS
    ins = []
    for af, ab in zip(arrs_f, arrs_b):
        f = _to_chains(af[row0:row0 + nb * seq], nb, seq)
        b = _to_chains(ab[row0:row0 + nb * seq], nb, seq)[::-1]
        ins.append(jnp.concatenate([f, b], axis=2))
    chains = 2 * nch
    padc = (-chains) % LANES
    if s0 is None:
        s0 = jnp.zeros((NK, NK, chains), F32)
    if padc:
        ins = [jnp.pad(a, ((0, 0), (0, 0), (0, padc))) for a in ins]
        s0 = jnp.pad(s0, ((0, 0), (0, 0), (0, padc)))
    y, st = _scan(*ins, s0)
    yf = _from_chains(y[:, :, :nch], nb, seq)
    yb = _from_chains(y[::-1, :, nch:chains], nb, seq)
    return yf, yb, st


def _state_to_chains(s):
    b = s.shape[0]
    return s.transpose(3, 2, 0, 1).reshape(NK, NK, b * RWKV_HEADS)


def _state_from_chains(s, nb):
    return s.reshape(NK, NK, nb, RWKV_HEADS).transpose(2, 3, 1, 0)


def _rope_tables():
    rows = DEC_SEQ // GRID_W
    row = jnp.repeat(jnp.arange(rows, dtype=F32), GRID_W)
    col = jnp.tile(jnp.arange(GRID_W, dtype=F32), rows)
    n_freq = HEAD_DIM // 4
    inv_freq = ROPE_THETA ** (-jnp.arange(n_freq, dtype=F32) / n_freq)
    ang = jnp.concatenate([row[:, None] * inv_freq, col[:, None] * inv_freq], axis=-1)
    cos, sin = jnp.cos(ang), jnp.sin(ang)
    reps = LANES // HEAD_DIM
    cos_t = jnp.tile(jnp.concatenate([cos, cos], axis=-1), (1, reps))
    sin_t = jnp.tile(jnp.concatenate([-sin, sin], axis=-1), (1, reps))
    ident_c = jnp.ones((QK_TM, LANES), F32)
    ident_s = jnp.zeros((QK_TM, LANES), F32)
    return jnp.concatenate([ident_c, cos_t], axis=0), jnp.concatenate([ident_s, sin_t], axis=0)


def kernel(x_prompt, x_sample, cache_attn_k, cache_attn_v, state_ssd_fwd, state_ssd_bwd, state_rwkv_fwd, state_rwkv_bwd, c, c_ctx, mod_w, mod_b, norm_mix_g, norm_ffn_g, ffn_w_gate, ffn_w_up, ffn_w_down, ab_w_in, ab_w_out, attn_q_g, attn_k_g, ssd_conv_w, ssd_conv_b, ssd_dt_bias, ssd_a_log, ssd_d, ssd_norm_g, rwkv_mu, rwkv_w_r, rwkv_w_k, rwkv_w_v, rwkv_w0, rwkv_w1, rwkv_w2, rwkv_a0, rwkv_a1, rwkv_a2, rwkv_g1, rwkv_g2, rwkv_k_k, rwkv_k_a, rwkv_r_k, rwkv_ln_g, rwkv_ln_b, rwkv_w_o, final_norm_g):
    bf = lambda a: a.astype(BF16)
    x = jnp.concatenate([x_prompt.reshape(R_CTX, D_MODEL), x_sample.reshape(R_SMP, D_MODEL)], axis=0)

    cv = jnp.concatenate([c_ctx[None], c, jnp.zeros((8 - 1 - DEC_BATCH, D_MODEL), F32)], axis=0)
    m = _ada(cv, mod_w, mod_b)
    m = m[:, :1 + DEC_BATCH].reshape(2, 1 + DEC_BATCH, 6, D_MODEL)
    m = jnp.pad(m, ((0, 0), (0, 0), (0, 2), (0, 0)))
    row2 = lambda a: a.reshape(1, -1)

    w_in = jnp.pad(bf(ab_w_in[0]), ((0, 0), (0, AB_PAD - AB_IN_DIM)))
    qkv, z, xbc, dt = _inproj(x, row2(norm_mix_g[0]), m[0], w_in)
    cos_t, sin_t = _rope_tables()
    qg = jnp.tile(attn_q_g[0], ATTN_HEADS).reshape(1, -1)
    kg = jnp.tile(attn_k_g[0], ATTN_KV_HEADS).reshape(1, -1)
    qn, kn = _qkprep(qkv, qg, kg, cos_t, sin_t)
    ck = cache_attn_k[:, 0].reshape(DEC_BATCH, PAST_LEN, ATTN_KV_DIM)
    cvv = cache_attn_v[:, 0].reshape(DEC_BATCH, PAST_LEN, ATTN_KV_DIM)
    attn_c, attn_s = _attention(qn, kn, qkv, ck, cvv)
    attn = jnp.concatenate([attn_c, attn_s], axis=0)

    dtb = jnp.pad(ssd_dt_bias[0].reshape(1, -1), ((0, 0), (0, DT_PAD - 2 * SSD_HEADS)))
    a_row = jnp.pad((-jnp.exp(ssd_a_log[0])).reshape(1, -1), ((0, 0), (0, DT_PAD - 2 * SSD_HEADS)))
    d_row = jnp.repeat(ssd_d[0], SSD_HEAD_DIM).reshape(1, -1)
    ssd_args = (ssd_conv_w[0], row2(ssd_conv_b[0]), dtb, a_row, d_row, row2(ssd_norm_g[0]))
    zero_st = jnp.zeros((BATCH, SSD_HEADS, SSD_HEAD_DIM, SSD_STATE), F32)
    y_c, hf_c, hb_c = _ssd(z, xbc, dt, *ssd_args, zero_st, zero_st, seq=SEQ, nb=BATCH, row0=0)
    y_s, _, _ = _ssd(z, xbc, dt, *ssd_args, state_ssd_fwd[:, 0], state_ssd_bwd[:, 0],
                     seq=DEC_SEQ, nb=DEC_BATCH, row0=R_CTX)
    y_ssd = jnp.concatenate([y_c, y_s], axis=0)
    x = _mixres(x, m[0], attn, y_ssd, bf(ab_w_out[0]))
    x = _ffn(x, row2(norm_ffn_g[0]), m[0], bf(ffn_w_gate[0]), bf(ffn_w_up[0]), bf(ffn_w_down[0]),
             row2(final_norm_g), final=False)

    w1 = bf(jnp.concatenate([rwkv_w1[0, 0], rwkv_w1[0, 1]], axis=1))
    a1 = bf(jnp.concatenate([rwkv_a1[0, 0], rwkv_a1[0, 1]], axis=1))
    zpad = lambda w: bf(jnp.stack([jnp.concatenate([w[0], jnp.zeros_like(w[1])], axis=0),
                                   jnp.concatenate([jnp.zeros_like(w[0]), w[1]], axis=0)]))
    pre = _rwkv_pre(x, row2(norm_mix_g[1]), m[1], rwkv_mu[0], bf(rwkv_w_r[0]), bf(rwkv_w_k[0]), bf(rwkv_w_v[0]),
                    w1, zpad(rwkv_w2[0]), a1, zpad(rwkv_a2[0]), bf(rwkv_g1[0]), bf(rwkv_g2[0]),
                    rwkv_w0[0], rwkv_a0[0], row2(rwkv_k_k[0]), row2(rwkv_k_a[0]), rwkv_r_k[0].reshape(1, -1))
    r, v, kk, gg, bon, wf, wb, kdf, kdb, kaf, kab = pre
    arrs_f = (r, wf, kdf, v, kk, kaf)
    arrs_b = (r, wb, kdb, v, kk, kab)
    yf_c, yb_c, st_c = _scan_pass(arrs_f, arrs_b, None, BATCH, SEQ, 0)
    s0 = jnp.concatenate([_state_to_chains(state_rwkv_fwd[:, 0]), _state_to_chains(state_rwkv_bwd[:, 0])], axis=2)
    yf_s, yb_s, _ = _scan_pass(arrs_f, arrs_b, s0, DEC_BATCH, DEC_SEQ, R_CTX)
    yf = jnp.concatenate([yf_c, yf_s], axis=0)
    yb = jnp.concatenate([yb_c, yb_s], axis=0)
    x = _rwkv_post(x, m[1], yf, yb, bon, gg, row2(rwkv_ln_g[0]), row2(rwkv_ln_b[0]), bf(rwkv_w_o[0]))
    x = _ffn(x, row2(norm_ffn_g[1]), m[1], bf(ffn_w_gate[1]), bf(ffn_w_up[1]), bf(ffn_w_down[1]),
             row2(final_norm_g), final=True)

    y_prompt = x[:R_CTX].reshape(BATCH, SEQ, D_MODEL)
    y_sample = x[R_CTX:].reshape(DEC_BATCH, DEC_SEQ, D_MODEL)
    new_k = kn[:R_CTX].reshape(BATCH, 1, SEQ, ATTN_KV_HEADS, HEAD_DIM)
    new_v = qkv[:R_CTX, ATTN_Q_DIM + ATTN_KV_DIM:].reshape(BATCH, 1, SEQ, ATTN_KV_HEADS, HEAD_DIM)
    nch = BATCH * RWKV_HEADS
    new_rf = _state_from_chains(st_c[:, :, :nch], BATCH)[:, None]
    new_rb = _state_from_chains(st_c[:, :, nch:2 * nch], BATCH)[:, None]
    return (y_prompt, y_sample, new_k, new_v, hf_c[:, None], hb_c[:, None], new_rf, new_rb)
```

```python
import functools

import jax
import jax.numpy as jnp
from jax import lax
from jax.experimental import pallas as pl
from jax.experimental.pallas import tpu as pltpu

F32 = jnp.float32
BF16 = jnp.bfloat16

D_MODEL = 1024
BATCH = 16
SEQ = 256
DEC_BATCH = 2
DEC_SEQ = 1024
PAST_LEN = 256
GRID_W = 64
ATTN_HEADS = 8
ATTN_KV_HEADS = 2
HEAD_DIM = 64
ROPE_THETA = 10000.0
ATTN_Q_DIM = ATTN_HEADS * HEAD_DIM
ATTN_KV_DIM = ATTN_KV_HEADS * HEAD_DIM
SSD_HEADS = 8
SSD_HEAD_DIM = 64
SSD_D_INNER = SSD_HEADS * SSD_HEAD_DIM
SSD_GROUPS = 2
SSD_STATE = 64
SSD_CONV_K = 5
SSD_CHUNK = 128
SSD_CONV_DIM = SSD_D_INNER + 2 * SSD_GROUPS * SSD_STATE
AB_IN_DIM = ATTN_Q_DIM + 2 * ATTN_KV_DIM + SSD_D_INNER + SSD_CONV_DIM + 2 * SSD_HEADS
RWKV_HEAD_DIM = 64
RWKV_HEADS = D_MODEL // RWKV_HEAD_DIM
FFN_DIM = (((8 * D_MODEL + 2) // 3 + 255) // 256) * 256
RMS_EPS = 1e-6
GN_EPS = 64e-5
L2_EPS = 1e-12

R_CTX = BATCH * SEQ
R_SMP = DEC_BATCH * DEC_SEQ
R_ALL = R_CTX + R_SMP
LANES = 128
QKV_DIM = ATTN_Q_DIM + 2 * ATTN_KV_DIM
DT_PAD = LANES
AB_PAD = QKV_DIM + SSD_D_INNER + SSD_CONV_DIM + DT_PAD
VMEM_LIMIT = 56 * 1024 * 1024


def _cparams(sem):
    return pltpu.CompilerParams(dimension_semantics=sem, vmem_limit_bytes=VMEM_LIMIT)


def _mod_index(i, tm):
    n_ctx = R_CTX // tm
    per = DEC_SEQ // tm
    return jnp.where(i < n_ctx, 0, 1 + jnp.maximum(i - n_ctx, 0) // per)


def _sigmoid(x):
    return 1.0 / (1.0 + jnp.exp(-x))


def _silu(x):
    return x * _sigmoid(x)


def _softplus(x):
    return jnp.maximum(x, 0.0) + jnp.log1p(jnp.exp(-jnp.abs(x)))


def _rms(x, g):
    return x * lax.rsqrt(jnp.mean(x * x, axis=-1, keepdims=True) + RMS_EPS) * g


def _group_allsum(x, group):
    n = x.shape[-1]
    ax = x.ndim - 1
    lane = lax.broadcasted_iota(jnp.int32, x.shape, ax)
    s = 1
    while s < group:
        up = pltpu.roll(x, n - s, axis=ax)
        dn = pltpu.roll(x, s, axis=ax)
        x = x + jnp.where((lane & s) == 0, up, dn)
        s *= 2
    return x


def _dot(a, b):
    return jnp.dot(a, b, preferred_element_type=F32)


ADA_TN = 1536


def _ada_kernel(c_ref, w_ref, b_ref, o_ref):
    s = _silu(c_ref[...]).astype(BF16)
    o_ref[0] = _dot(s, w_ref[0].astype(BF16)) + b_ref[0]


def _ada(cv, mod_w, mod_b):
    depth = mod_w.shape[0]
    n = mod_w.shape[2]
    return pl.pallas_call(
        _ada_kernel,
        out_shape=jax.ShapeDtypeStruct((depth, 8, n), F32),
        grid=(depth, n // ADA_TN),
        in_specs=[
            pl.BlockSpec((8, D_MODEL), lambda l, j: (0, 0)),
            pl.BlockSpec((1, D_MODEL, ADA_TN), lambda l, j: (l, 0, j)),
            pl.BlockSpec((1, 1, ADA_TN), lambda l, j: (l, 0, j)),
        ],
        out_specs=pl.BlockSpec((1, 8, ADA_TN), lambda l, j: (l, 0, j)),
        compiler_params=_cparams(("arbitrary", "arbitrary")),
        name="ada",
    )(cv, mod_w, mod_b.reshape(depth, 1, n))


INPROJ_TM = 512


def _inproj_kernel(x_ref, g_ref, mod_ref, w_ref, qkv_ref, z_ref, xbc_ref, dt_ref):
    m = mod_ref[0]
    h = _rms(x_ref[...], g_ref[...]) * (1 + m[1:2]) + m[0:1]
    p = _dot(h.astype(BF16), w_ref[...])
    qkv_ref[...] = p[:, 0:QKV_DIM]
    z_ref[...] = p[:, QKV_DIM:QKV_DIM + SSD_D_INNER]
    xbc_ref[...] = p[:, QKV_DIM + SSD_D_INNER:QKV_DIM + SSD_D_INNER + SSD_CONV_DIM]
    dt_ref[...] = p[:, QKV_DIM + SSD_D_INNER + SSD_CONV_DIM:AB_PAD]


def _inproj(x, g, mod, w_pad):
    tm = INPROJ_TM
    row = lambda i: (i, 0)
    return pl.pallas_call(
        _inproj_kernel,
        out_shape=(
            jax.ShapeDtypeStruct((R_ALL, QKV_DIM), F32),
            jax.ShapeDtypeStruct((R_ALL, SSD_D_INNER), F32),
            jax.ShapeDtypeStruct((R_ALL, SSD_CONV_DIM), F32),
            jax.ShapeDtypeStruct((R_ALL, DT_PAD), F32),
        ),
        grid=(R_ALL // tm,),
        in_specs=[
            pl.BlockSpec((tm, D_MODEL), row),
            pl.BlockSpec((1, D_MODEL), lambda i: (0, 0)),
            pl.BlockSpec((1, 8, D_MODEL), lambda i: (_mod_index(i, tm), 0, 0)),
            pl.BlockSpec((D_MODEL, AB_PAD), lambda i: (0, 0)),
        ],
        out_specs=(
            pl.BlockSpec((tm, QKV_DIM), row),
            pl.BlockSpec((tm, SSD_D_INNER), row),
            pl.BlockSpec((tm, SSD_CONV_DIM), row),
            pl.BlockSpec((tm, DT_PAD), row),
        ),
        compiler_params=_cparams(("arbitrary",)),
        name="inproj",
    )(x, g, mod, w_pad)


QK_TM = 256


def _qkprep_kernel(qkv_ref, qg_ref, kg_ref, cos_ref, sin_ref, qn_ref, kn_ref):
    cos = cos_ref[...]
    sin = sin_ref[...]

    def norm_rope(x, g, reps):
        ms = _group_allsum(x * x, HEAD_DIM) * (1.0 / HEAD_DIM)
        y = x * lax.rsqrt(ms + RMS_EPS) * g
        n = y.shape[-1]
        lane = lax.broadcasted_iota(jnp.int32, y.shape, 1)
        half = HEAD_DIM // 2
        swapped = jnp.where((lane & half) == 0, pltpu.roll(y, n - half, axis=1), pltpu.roll(y, half, axis=1))
        c = jnp.concatenate([cos] * reps, axis=1) if reps > 1 else cos
        s = jnp.concatenate([sin] * reps, axis=1) if reps > 1 else sin
        return y * c + swapped * s

    q = qkv_ref[:, 0:ATTN_Q_DIM]
    k = qkv_ref[:, ATTN_Q_DIM:ATTN_Q_DIM + ATTN_KV_DIM]
    qn_ref[...] = norm_rope(q, qg_ref[...], ATTN_Q_DIM // LANES)
    kn_ref[...] = norm_rope(k, kg_ref[...], 1)


def _qkprep(qkv, qg, kg, cos_t, sin_t):
    tm = QK_TM
    n_ctx = R_CTX // tm
    per = DEC_SEQ // tm
    tab = lambda i: (jnp.where(i < n_ctx, 0, 1 + jnp.maximum(i - n_ctx, 0) % per), 0)
    return pl.pallas_call(
        _qkprep_kernel,
        out_shape=(
            jax.ShapeDtypeStruct((R_ALL, ATTN_Q_DIM), F32),
            jax.ShapeDtypeStruct((R_ALL, ATTN_KV_DIM), F32),
        ),
        grid=(R_ALL // tm,),
        in_specs=[
            pl.BlockSpec((tm, QKV_DIM), lambda i: (i, 0)),
            pl.BlockSpec((1, ATTN_Q_DIM), lambda i: (0, 0)),
            pl.BlockSpec((1, ATTN_KV_DIM), lambda i: (0, 0)),
            pl.BlockSpec((tm, LANES), tab),
            pl.BlockSpec((tm, LANES), tab),
        ],
        out_specs=(
            pl.BlockSpec((tm, ATTN_Q_DIM), lambda i: (i, 0)),
            pl.BlockSpec((tm, ATTN_KV_DIM), lambda i: (i, 0)),
        ),
        compiler_params=_cparams(("arbitrary",)),
        name="qkprep",
    )(qkv, qg, kg, cos_t, sin_t)


def _attn_core(q, ks, vs):
    tq = q.shape[0]
    grp = ATTN_HEADS // ATTN_KV_HEADS
    scale = HEAD_DIM ** -0.5
    outs = []
    for g in range(ATTN_KV_HEADS):
        sl = slice(g * HEAD_DIM, (g + 1) * HEAD_DIM)
        qs = jnp.concatenate(
            [q[:, (g * grp + j) * HEAD_DIM:(g * grp + j + 1) * HEAD_DIM] for j in range(grp)], axis=0).astype(BF16)
        ss = [lax.dot_general(qs, k[:, sl].astype(BF16), (((1,), (1,)), ((), ())),
                              preferred_element_type=F32) * scale for k in ks]
        m = ss[0].max(axis=-1, keepdims=True)
        for s in ss[1:]:
            m = jnp.maximum(m, s.max(axis=-1, keepdims=True))
        ps = [jnp.exp(s - m) for s in ss]
        l = ps[0].sum(axis=-1, keepdims=True)
        for p in ps[1:]:
            l = l + p.sum(axis=-1, keepdims=True)
        inv = 1.0 / l
        o = None
        for p, v in zip(ps, vs):
            t = _dot((p * inv).astype(BF16), v[:, sl].astype(BF16))
            o = t if o is None else o + t
        outs += [o[j * tq:(j + 1) * tq] for j in range(grp)]
    return jnp.concatenate(outs, axis=1)


def _attn_ctx_kernel(q_ref, k_ref, v_ref, o_ref):
    o_ref[...] = _attn_core(q_ref[...], [k_ref[...]], [v_ref[...]]).astype(BF16)


def _attn_smp_kernel(q_ref, k_ref, v_ref, ck_ref, cv_ref, o_ref):
    ks = [ck_ref[0], k_ref[...]]
    vs = [cv_ref[0], v_ref[...]]
    o_ref[...] = _attn_core(q_ref[...], ks, vs).astype(BF16)


ATTN_TQ = 128


def _attention(qn, kn, qkv, cache_k, cache_v):
    v_col = (ATTN_Q_DIM + ATTN_KV_DIM) // ATTN_KV_DIM
    ctx = pl.pallas_call(
        _attn_ctx_kernel,
        out_shape=jax.ShapeDtypeStruct((R_CTX, ATTN_Q_DIM), BF16),
        grid=(BATCH,),
        in_specs=[
            pl.BlockSpec((SEQ, ATTN_Q_DIM), lambda b: (b, 0)),
            pl.BlockSpec((SEQ, ATTN_KV_DIM), lambda b: (b, 0)),
            pl.BlockSpec((SEQ, ATTN_KV_DIM), lambda b: (b, v_col)),
        ],
        out_specs=pl.BlockSpec((SEQ, ATTN_Q_DIM), lambda b: (b, 0)),
        compiler_params=_cparams(("arbitrary",)),
        name="attn_ctx",
    )(qn, kn, qkv)
    nq = DEC_SEQ // ATTN_TQ
    q0 = R_CTX // ATTN_TQ
    s0 = R_CTX // DEC_SEQ
    smp = pl.pallas_call(
        _attn_smp_kernel,
        out_shape=jax.ShapeDtypeStruct((R_SMP, ATTN_Q_DIM), BF16),
        grid=(DEC_BATCH, nq),
        in_specs=[
            pl.BlockSpec((ATTN_TQ, ATTN_Q_DIM), lambda b, i: (q0 + b * nq + i, 0)),
            pl.BlockSpec((DEC_SEQ, ATTN_KV_DIM), lambda b, i: (s0 + b, 0)),
            pl.BlockSpec((DEC_SEQ, ATTN_KV_DIM), lambda b, i: (s0 + b, v_col)),
            pl.BlockSpec((1, PAST_LEN, ATTN_KV_DIM), lambda b, i: (b, 0, 0)),
            pl.BlockSpec((1, PAST_LEN, ATTN_KV_DIM), lambda b, i: (b, 0, 0)),
        ],
        out_specs=pl.BlockSpec((ATTN_TQ, ATTN_Q_DIM), lambda b, i: (b * nq + i, 0)),
        compiler_params=_cparams(("arbitrary", "arbitrary")),
        name="attn_smp",
    )(qn, kn, qkv, cache_k, cache_v)
    return ctx, smp


CONV_HALO = 8


def _cumsum_rows(a, reverse):
    n = a.shape[0]
    row = lax.broadcasted_iota(jnp.int32, a.shape, 0)
    s = 1
    while s < n:
        if reverse:
            a = a + jnp.where(row < n - s, pltpu.roll(a, n - s, axis=0), 0.0)
        else:
            a = a + jnp.where(row >= s, pltpu.roll(a, s, axis=0), 0.0)
        s *= 2
    return a


def _ssd_kernel(z_ref, xbc_ref, dt_ref, cw_ref, cb_ref, dtb_ref, a_ref, d_ref, g_ref, h0f_ref, h0b_ref,
                y_ref, hf_ref, hb_ref, pad_sc, xc_sc, dt_sc, y_sc, h_sc, *, seq):
    L = SSD_CHUNK
    nc = seq // L
    pad = SSD_CONV_K // 2
    zeros = jnp.zeros((CONV_HALO, SSD_CONV_DIM), F32)
    pad_sc[0:CONV_HALO, :] = zeros
    pad_sc[seq + CONV_HALO:seq + 2 * CONV_HALO, :] = zeros
    pad_sc[CONV_HALO:seq + CONV_HALO, :] = xbc_ref[...]
    h_sc[0] = h0f_ref[0]
    h_sc[1] = h0b_ref[0]

    def conv_chunk(c, carry):
        r0 = pl.multiple_of(c * L, L)
        win = pad_sc[pl.ds(r0, L + 2 * CONV_HALO), :]
        acc = cb_ref[...]
        for i in range(SSD_CONV_K):
            acc = acc + win[CONV_HALO - pad + i:CONV_HALO - pad + i + L, :] * cw_ref[i:i + 1, :]
        xc_sc[pl.ds(r0, L), :] = _silu(acc)
        dt_sc[pl.ds(r0, L), :] = _softplus(dt_ref[pl.ds(r0, L), :] + dtb_ref[...])
        return carry

    lax.fori_loop(0, nc, conv_chunk, 0)

    rr = lax.broadcasted_iota(jnp.int32, (L, L), 0)
    cc = lax.broadcasted_iota(jnp.int32, (L, L), 1)
    grp = SSD_HEADS // SSD_GROUPS
    P = SSD_HEAD_DIM
    N = SSD_STATE

    def make_chunk(dirn):
        mask = (rr >= cc) if dirn == 0 else (rr <= cc)

        def chunk(ci, carry):
            c = ci if dirn == 0 else nc - 1 - ci
            r0 = pl.multiple_of(c * L, L)
            xs = xc_sc[pl.ds(r0, L), 0:SSD_D_INNER]
            bm = xc_sc[pl.ds(r0, L), SSD_D_INNER:SSD_D_INNER + SSD_GROUPS * N]
            cm = xc_sc[pl.ds(r0, L), SSD_D_INNER + SSD_GROUPS * N:SSD_CONV_DIM]
            dtc = dt_sc[pl.ds(r0, L), :]
            acs = _cumsum_rows(dtc * a_ref[...], reverse=(dirn == 1))
            acs_t = acs.T
            tot = acs[L - 1:L, :] if dirn == 0 else acs[0:1, :]
            dec_end = jnp.exp(tot - acs)
            eacs = jnp.exp(acs)
            cdec = jnp.exp(tot)
            cb = [lax.dot_general(cm[:, g * N:(g + 1) * N].astype(BF16), bm[:, g * N:(g + 1) * N].astype(BF16),
                                  (((1,), (1,)), ((), ())), preferred_element_type=F32)
                  for g in range(SSD_GROUPS)]
            if dirn == 0:
                y_sc[pl.ds(r0, L), :] = xs * d_ref[...]
            for h in range(SSD_HEADS):
                g = h // grp
                ln = h + SSD_HEADS * dirn
                seg = jnp.exp(jnp.where(mask, acs[:, ln:ln + 1] - acs_t[ln:ln + 1, :], -jnp.inf))
                sc = (cb[g] * seg).astype(BF16)
                xdt = (xs[:, h * P:(h + 1) * P] * dtc[:, ln:ln + 1]).astype(BF16)
                hin = h_sc[dirn, h]
                ce = (cm[:, g * N:(g + 1) * N] * eacs[:, ln:ln + 1]).astype(BF16)
                yh = _dot(sc, xdt) + lax.dot_general(ce, hin.astype(BF16), (((1,), (1,)), ((), ())),
                                                     preferred_element_type=F32)
                bd = (bm[:, g * N:(g + 1) * N] * dec_end[:, ln:ln + 1]).astype(BF16)
                st = lax.dot_general(xdt, bd, (((0,), (0,)), ((), ())), preferred_element_type=F32)
                h_sc[dirn, h] = hin * cdec[:, ln:ln + 1] + st
                y_sc[pl.ds(r0, L), h * P:(h + 1) * P] += yh
            return carry

        return chunk

    lax.fori_loop(0, nc, make_chunk(0), 0)
    lax.fori_loop(0, nc, make_chunk(1), 0)

    def out_chunk(c, carry):
        r0 = pl.multiple_of(c * L, L)
        y = y_sc[pl.ds(r0, L), :] * _silu(z_ref[pl.ds(r0, L), :])
        y_ref[pl.ds(r0, L), :] = _rms(y, g_ref[...]).astype(BF16)
        return carry

    lax.fori_loop(0, nc, out_chunk, 0)
    hf_ref[0] = h_sc[0]
    hb_ref[0] = h_sc[1]


def _ssd(z, xbc, dt, cw, cb, dtb, a_row, d_row, g, h0f, h0b, *, seq, nb, row0):
    blk0 = row0 // seq
    row = lambda b: (blk0 + b, 0)
    fixed = lambda b: (0, 0)
    st = lambda b: (b, 0, 0, 0)
    st_shape = (nb, SSD_HEADS, SSD_HEAD_DIM, SSD_STATE)
    st_blk = (1, SSD_HEADS, SSD_HEAD_DIM, SSD_STATE)
    return pl.pallas_call(
        functools.partial(_ssd_kernel, seq=seq),
        out_shape=(
            jax.ShapeDtypeStruct((nb * seq, SSD_D_INNER), BF16),
            jax.ShapeDtypeStruct(st_shape, F32),
            jax.ShapeDtypeStruct(st_shape, F32),
        ),
        grid=(nb,),
        in_specs=[
            pl.BlockSpec((seq, SSD_D_INNER), row),
            pl.BlockSpec((seq, SSD_CONV_DIM), row),
            pl.BlockSpec((seq, DT_PAD), row),
            pl.BlockSpec((SSD_CONV_K, SSD_CONV_DIM), fixed),
            pl.BlockSpec((1, SSD_CONV_DIM), fixed),
            pl.BlockSpec((1, DT_PAD), fixed),
            pl.BlockSpec((1, DT_PAD), fixed),
            pl.BlockSpec((1, SSD_D_INNER), fixed),
            pl.BlockSpec((1, SSD_D_INNER), fixed),
            pl.BlockSpec(st_blk, st),
            pl.BlockSpec(st_blk, st),
        ],
        out_specs=(
            pl.BlockSpec((seq, SSD_D_INNER), lambda b: (b, 0)),
            pl.BlockSpec(st_blk, st),
            pl.BlockSpec(st_blk, st),
        ),
        scratch_shapes=[
            pltpu.VMEM((seq + 2 * CONV_HALO, SSD_CONV_DIM), F32),
            pltpu.VMEM((seq, SSD_CONV_DIM), F32),
            pltpu.VMEM((seq, DT_PAD), F32),
            pltpu.VMEM((seq, SSD_D_INNER), F32),
            pltpu.VMEM((2, SSD_HEADS, SSD_HEAD_DIM, SSD_STATE), F32),
        ],
        compiler_params=_cparams(("arbitrary",)),
        name=f"ssd_{seq}",
    )(z, xbc, dt, cw, cb, dtb, a_row, d_row, g, h0f, h0b)


RES_TM = 512


def _mixres_kernel(x_ref, mod_ref, a1_ref, a2_ref, w_ref, o_ref):
    k1 = a1_ref.shape[1]
    out = _dot(a1_ref[...], w_ref[0:k1, :]) + _dot(a2_ref[...], w_ref[k1:, :])
    o_ref[...] = x_ref[...] + mod_ref[0][2:3] * out


def _mixres(x, mod, a1, a2, w):
    tm = RES_TM
    row = lambda i: (i, 0)
    return pl.pallas_call(
        _mixres_kernel,
        out_shape=jax.ShapeDtypeStruct((R_ALL, D_MODEL), F32),
        grid=(R_ALL // tm,),
        in_specs=[
            pl.BlockSpec((tm, D_MODEL), row),
            pl.BlockSpec((1, 8, D_MODEL), lambda i: (_mod_index(i, tm), 0, 0)),
            pl.BlockSpec((tm, a1.shape[1]), row),
            pl.BlockSpec((tm, a2.shape[1]), row),
            pl.BlockSpec(w.shape, lambda i: (0, 0)),
        ],
        out_specs=pl.BlockSpec((tm, D_MODEL), row),
        compiler_params=_cparams(("arbitrary",)),
        name="mixres",
    )(x, mod, a1, a2, w)


FFN_TM = 1024
FFN_TF = 256


def _ffn_kernel(x_ref, g_ref, mod_ref, wg_ref, wu_ref, wd_ref, fg_ref, o_ref, h_sc, acc_sc, *, final):
    j = pl.program_id(1)

    @pl.when(j == 0)
    def _():
        m = mod_ref[0]
        h = _rms(x_ref[...], g_ref[...]) * (1 + m[4:5]) + m[3:4]
        h_sc[...] = h.astype(BF16)
        acc_sc[...] = jnp.zeros_like(acc_sc)

    h = h_sc[...]
    hid = _silu(_dot(h, wg_ref[...])) * _dot(h, wu_ref[...])
    acc_sc[...] += _dot(hid.astype(BF16), wd_ref[...])

    @pl.when(j == pl.num_programs(1) - 1)
    def _():
        y = x_ref[...] + mod_ref[0][5:6] * acc_sc[...]
        if final:
            y = _rms(y, fg_ref[...])
        o_ref[...] = y


def _ffn(x, g, mod, wg, wu, wd, fg, *, final):
    tm, tf = FFN_TM, FFN_TF
    row = lambda i, j: (i, 0)
    return pl.pallas_call(
        functools.partial(_ffn_kernel, final=final),
        out_shape=jax.ShapeDtypeStruct((R_ALL, D_MODEL), F32),
        grid=(R_ALL // tm, FFN_DIM // tf),
        in_specs=[
            pl.BlockSpec((tm, D_MODEL), row),
            pl.BlockSpec((1, D_MODEL), lambda i, j: (0, 0)),
            pl.BlockSpec((1, 8, D_MODEL), lambda i, j: (_mod_index(i, tm), 0, 0)),
            pl.BlockSpec((D_MODEL, tf), lambda i, j: (0, j)),
            pl.BlockSpec((D_MODEL, tf), lambda i, j: (0, j)),
            pl.BlockSpec((tf, D_MODEL), lambda i, j: (j, 0)),
            pl.BlockSpec((1, D_MODEL), lambda i, j: (0, 0)),
        ],
        out_specs=pl.BlockSpec((tm, D_MODEL), row),
        scratch_shapes=[pltpu.VMEM((tm, D_MODEL), BF16), pltpu.VMEM((tm, D_MODEL), F32)],
        compiler_params=_cparams(("arbitrary", "arbitrary")),
        name="ffn_final" if final else "ffn",
    )(x, g, mod, wg, wu, wd, fg)


RW_TM = 128
HALO = 8


def _rwkv_pre_kernel(x_ref, xp_ref, xn_ref, g_ref, mod_ref, mu_ref, wr_ref, wk_ref, wv_ref, w1_ref, w2_ref,
                     a1_ref, a2_ref, g1_ref, g2_ref, w0_ref, a0_ref, kk_ref, ka_ref, rk_ref,
                     rk_o, v_o, wka_o, g_o, bon_o):
    i = pl.program_id(0)
    tm = RW_TM
    n_ctx = R_CTX // tm
    per_c = SEQ // tm
    per_s = DEC_SEQ // tm
    rel = jnp.where(i < n_ctx, i % per_c, jnp.maximum(i - n_ctx, 0) % per_s)
    last = jnp.where(i < n_ctx, per_c - 1, per_s - 1)
    m = mod_ref[0]

    def nm(x):
        return _rms(x, g_ref[...]) * (1 + m[1:2]) + m[0:1]

    h = nm(x_ref[...])
    prev_row = jnp.where(rel == 0, 0.0, nm(xp_ref[...])[HALO - 1:HALO, :])
    next_row = jnp.where(rel == last, 0.0, nm(xn_ref[...])[0:1, :])
    row = lax.broadcasted_iota(jnp.int32, h.shape, 0)
    hp = jnp.where(row == 0, prev_row, pltpu.roll(h, 1, axis=0))
    hn = jnp.where(row == tm - 1, next_row, pltpu.roll(h, tm - 1, axis=0))
    dp = hp - h
    dn = hn - h

    def mix(idx):
        return (h + dp * mu_ref[0, idx:idx + 1, :] + dn * mu_ref[1, idx:idx + 1, :]).astype(BF16)

    r = _dot(mix(0), wr_ref[...])
    k = _dot(mix(2), wk_ref[...])
    v = _dot(mix(3), wv_ref[...])
    lw = jnp.tanh(_dot(mix(1), w1_ref[...])).astype(BF16)
    la = _dot(mix(4), a1_ref[...]).astype(BF16)
    gg = _dot(_sigmoid(_dot(mix(5), g1_ref[...])).astype(BF16), g2_ref[...])

    kk = k * kk_ref[...]
    kk = kk * lax.rsqrt(_group_allsum(kk * kk, RWKV_HEAD_DIM) + L2_EPS)
    rk_o[0] = r
    rk_o[1] = kk
    v_o[...] = v
    g_o[...] = gg
    bsum = None
    for j in range(2):
        wl = w0_ref[j:j + 1, :] + _dot(lw, w2_ref[j])
        wka_o[j, 0] = jnp.exp(-jnp.exp(-_softplus(-wl) - 0.5))
        a = _sigmoid(a0_ref[j:j + 1, :] + _dot(la, a2_ref[j]))
        kd = k * (1 + (a - 1) * ka_ref[...])
        wka_o[j, 1] = kd
        wka_o[j, 2] = kk * a
        t = r * kd * rk_ref[...]
        bsum = t if bsum is None else bsum + t
    bon_o[...] = _group_allsum(bsum, RWKV_HEAD_DIM) * v


def _rwkv_pre(x, g, mod, mu, wr, wk, wv, w1, w2, a1, a2, g1, g2, w0, a0, k_k, k_a, r_k):
    tm = RW_TM
    nblk = R_ALL // HALO
    per = tm // HALO
    row = lambda i: (i, 0)
    fixed2 = lambda i: (0, 0)
    fixed3 = lambda i: (0, 0, 0)
    full = lambda a: pl.BlockSpec(a.shape, fixed2 if a.ndim == 2 else fixed3)
    out = jax.ShapeDtypeStruct((R_ALL, D_MODEL), F32)
    orow = pl.BlockSpec((tm, D_MODEL), row)
    return pl.pallas_call(
        _rwkv_pre_kernel,
        out_shape=(jax.ShapeDtypeStruct((2, R_ALL, D_MODEL), F32), out,
                   jax.ShapeDtypeStruct((2, 3, R_ALL, D_MODEL), F32), out, out),
        grid=(R_ALL // tm,),
        in_specs=[
            pl.BlockSpec((tm, D_MODEL), row),
            pl.BlockSpec((HALO, D_MODEL), lambda i: (jnp.maximum(i * per - 1, 0), 0)),
            pl.BlockSpec((HALO, D_MODEL), lambda i: (jnp.minimum((i + 1) * per, nblk - 1), 0)),
            pl.BlockSpec((1, D_MODEL), fixed2),
            pl.BlockSpec((1, 8, D_MODEL), lambda i: (_mod_index(i, tm), 0, 0)),
            full(mu), full(wr), full(wk), full(wv), full(w1), full(w2), full(a1), full(a2), full(g1), full(g2),
            full(w0), full(a0), full(k_k), full(k_a), full(r_k),
        ],
        out_specs=(pl.BlockSpec((2, tm, D_MODEL), lambda i: (0, i, 0)), orow,
                   pl.BlockSpec((2, 3, tm, D_MODEL), lambda i: (0, 0, i, 0)), orow, orow),
        compiler_params=_cparams(("arbitrary",)),
        name="rwkv_pre",
    )(x, x, x, g, mod, mu, wr, wk, wv, w1, w2, a1, a2, g1, g2, w0, a0, k_k, k_a, r_k)


SCAN_TT = 32
NK = RWKV_HEAD_DIM


def _scan_kernel(kt_ref, vt_ref, dk_ref, s0_ref, y_ref, st_ref, s_sc):
    d = pl.program_id(0)
    j = pl.program_id(2)

    @pl.when(j == 0)
    def _():
        s_sc[...] = s0_ref[0]

    def step(i, carry):
        t = jnp.where(d == 0, i, SCAN_TT - 1 - i)
        vv = vt_ref[t]
        sa = jnp.zeros_like(vv)
        for k in range(NK):
            sa = sa + s_sc[k] * kt_ref[1, t, k:k + 1, :]
        y = jnp.zeros_like(vv)
        for k in range(NK):
            sn = (s_sc[k] * dk_ref[0, 0, t, k:k + 1, :] - sa * dk_ref[0, 2, t, k:k + 1, :]
                  + vv * dk_ref[0, 1, t, k:k + 1, :])
            s_sc[k] = sn
            y = y + sn * kt_ref[0, t, k:k + 1, :]
        y_ref[0, t] = y
        return carry

    lax.fori_loop(0, SCAN_TT, step, 0)

    @pl.when(j == pl.num_programs(2) - 1)
    def _():
        st_ref[0] = s_sc[...]


def _scan(kt, vt, dk, s0):
    seq, nv, chains = vt.shape
    tt = SCAN_TT
    nt = seq // tt
    tb = lambda d, j: jnp.where(d == 0, j, nt - 1 - j)
    sblk = pl.BlockSpec((1, NK, nv, LANES), lambda d, c, j: (d, 0, 0, c))
    yblk = pl.BlockSpec((1, tt, nv, LANES), lambda d, c, j: (d, tb(d, j), 0, c))
    return pl.pallas_call(
        _scan_kernel,
        out_shape=(jax.ShapeDtypeStruct((2, seq, nv, chains), F32), jax.ShapeDtypeStruct((2, NK, nv, chains), F32)),
        grid=(2, chains // LANES, nt),
        in_specs=[
            pl.BlockSpec((2, tt, NK, LANES), lambda d, c, j: (0, tb(d, j), 0, c)),
            pl.BlockSpec((tt, nv, LANES), lambda d, c, j: (tb(d, j), 0, c)),
            pl.BlockSpec((1, 3, tt, NK, LANES), lambda d, c, j: (d, 0, tb(d, j), 0, c)),
            sblk,
        ],
        out_specs=(yblk, sblk),
        scratch_shapes=[pltpu.VMEM((NK, nv, LANES), F32)],
        compiler_params=_cparams(("arbitrary", "arbitrary", "arbitrary")),
        name=f"rwkv_scan_{seq}",
    )(kt, vt, dk, s0)


POST_TM = 256


def _rwkv_post_kernel(x_ref, mod_ref, yc_ref, ys_ref, bon_ref, g_ref, lng_ref, lnb_ref, wo_ref, o_ref):
    is_ctx = pl.program_id(0) < R_CTX // POST_TM
    y = jnp.where(is_ctx, yc_ref[0] + yc_ref[1], ys_ref[0] + ys_ref[1])
    inv = 1.0 / RWKV_HEAD_DIM
    mean = _group_allsum(y, RWKV_HEAD_DIM) * inv
    d = y - mean
    var = _group_allsum(d * d, RWKV_HEAD_DIM) * inv
    yn = d * lax.rsqrt(var + GN_EPS) * lng_ref[...] + lnb_ref[...]
    out = ((yn + bon_ref[...]) * g_ref[...]).astype(BF16)
    o_ref[...] = x_ref[...] + mod_ref[0][2:3] * _dot(out, wo_ref[...])


def _rwkv_post(x, mod, yc, ys, bon, g, lng, lnb, wo):
    tm = POST_TM
    n_ctx = R_CTX // tm
    row = lambda i: (i, 0)
    fixed = lambda i: (0, 0)
    big = pl.BlockSpec((tm, D_MODEL), row)
    ycb = pl.BlockSpec((2, tm, D_MODEL), lambda i: (0, jnp.minimum(i, n_ctx - 1), 0))
    ysb = pl.BlockSpec((2, tm, D_MODEL), lambda i: (0, jnp.maximum(i - n_ctx, 0), 0))
    return pl.pallas_call(
        _rwkv_post_kernel,
        out_shape=jax.ShapeDtypeStruct((R_ALL, D_MODEL), F32),
        grid=(R_ALL // tm,),
        in_specs=[
            big,
            pl.BlockSpec((1, 8, D_MODEL), lambda i: (_mod_index(i, tm), 0, 0)),
            ycb, ysb, big, big,
            pl.BlockSpec((1, D_MODEL), fixed),
            pl.BlockSpec((1, D_MODEL), fixed),
            pl.BlockSpec((D_MODEL, D_MODEL), fixed),
        ],
        out_specs=big,
        compiler_params=_cparams(("arbitrary",)),
        name="rwkv_post",
    )(x, mod, yc, ys, bon, g, lng, lnb, wo)


def _scan_pass(rk, v, wka, s0, nb, seq, row0):
    H = RWKV_HEADS
    nch = nb * H
    vq = max(1, LANES // nch)
    nv = NK // vq
    rows = slice(row0, row0 + nb * seq)

    def ktype(a):
        lead = a.shape[:-2]
        n = len(lead)
        a = a.reshape(lead + (nb, seq, H, NK))
        a = a.transpose(tuple(range(n)) + (n + 1, n + 3, n, n + 2)).reshape(lead + (seq, NK, 1, nch))
        return jnp.broadcast_to(a, lead + (seq, NK, vq, nch)).reshape(lead + (seq, NK, vq * nch))

    kt = ktype(rk[:, rows])
    dk = ktype(wka[:, :, rows])
    vt = v[rows].reshape(nb, seq, H, vq, nv).transpose(1, 4, 3, 0, 2).reshape(seq, nv, vq * nch)
    if s0 is None:
        s0 = jnp.zeros((2, NK, nv, vq * nch), F32)
    else:
        s0 = s0.reshape(2, nb, H, vq, nv, NK).transpose(0, 5, 4, 3, 1, 2).reshape(2, NK, nv, vq * nch)
    y, st = _scan(kt, vt, dk, s0)
    y = y.reshape(2, seq, nv, vq, nb, H).transpose(0, 4, 1, 5, 3, 2).reshape(2, nb * seq, D_MODEL)
    st = st.reshape(2, NK, nv, vq, nb, H).transpose(0, 4, 5, 3, 2, 1).reshape(2, nb, H, NK, NK)
    return y, st


def _rope_tables():
    rows = DEC_SEQ // GRID_W
    row = jnp.repeat(jnp.arange(rows, dtype=F32), GRID_W)
    col = jnp.tile(jnp.arange(GRID_W, dtype=F32), rows)
    n_freq = HEAD_DIM // 4
    inv_freq = ROPE_THETA ** (-jnp.arange(n_freq, dtype=F32) / n_freq)
    ang = jnp.concatenate([row[:, None] * inv_freq, col[:, None] * inv_freq], axis=-1)
    cos, sin = jnp.cos(ang), jnp.sin(ang)
    reps = LANES // HEAD_DIM
    cos_t = jnp.tile(jnp.concatenate([cos, cos], axis=-1), (1, reps))
    sin_t = jnp.tile(jnp.concatenate([-sin, sin], axis=-1), (1, reps))
    ident_c = jnp.ones((QK_TM, LANES), F32)
    ident_s = jnp.zeros((QK_TM, LANES), F32)
    return jnp.concatenate([ident_c, cos_t], axis=0), jnp.concatenate([ident_s, sin_t], axis=0)


def kernel(x_prompt, x_sample, cache_attn_k, cache_attn_v, state_ssd_fwd, state_ssd_bwd, state_rwkv_fwd, state_rwkv_bwd, c, c_ctx, mod_w, mod_b, norm_mix_g, norm_ffn_g, ffn_w_gate, ffn_w_up, ffn_w_down, ab_w_in, ab_w_out, attn_q_g, attn_k_g, ssd_conv_w, ssd_conv_b, ssd_dt_bias, ssd_a_log, ssd_d, ssd_norm_g, rwkv_mu, rwkv_w_r, rwkv_w_k, rwkv_w_v, rwkv_w0, rwkv_w1, rwkv_w2, rwkv_a0, rwkv_a1, rwkv_a2, rwkv_g1, rwkv_g2, rwkv_k_k, rwkv_k_a, rwkv_r_k, rwkv_ln_g, rwkv_ln_b, rwkv_w_o, final_norm_g):
    bf = lambda a: a.astype(BF16)
    x = jnp.concatenate([x_prompt.reshape(R_CTX, D_MODEL), x_sample.reshape(R_SMP, D_MODEL)], axis=0)

    cv = jnp.concatenate([c_ctx[None], c, jnp.zeros((8 - 1 - DEC_BATCH, D_MODEL), F32)], axis=0)
    m = _ada(cv, mod_w, mod_b)
    m = m[:, :1 + DEC_BATCH].reshape(2, 1 + DEC_BATCH, 6, D_MODEL)
    m = jnp.pad(m, ((0, 0), (0, 0), (0, 2), (0, 0)))
    row2 = lambda a: a.reshape(1, -1)

    w_in = jnp.pad(bf(ab_w_in[0]), ((0, 0), (0, AB_PAD - AB_IN_DIM)))
    qkv, z, xbc, dt = _inproj(x, row2(norm_mix_g[0]), m[0], w_in)
    cos_t, sin_t = _rope_tables()
    qg = jnp.tile(attn_q_g[0], ATTN_HEADS).reshape(1, -1)
    kg = jnp.tile(attn_k_g[0], ATTN_KV_HEADS).reshape(1, -1)
    qn, kn = _qkprep(qkv, qg, kg, cos_t, sin_t)
    ck = cache_attn_k[:, 0].reshape(DEC_BATCH, PAST_LEN, ATTN_KV_DIM)
    cvv = cache_attn_v[:, 0].reshape(DEC_BATCH, PAST_LEN, ATTN_KV_DIM)
    attn_c, attn_s = _attention(qn, kn, qkv, ck, cvv)
    attn = jnp.concatenate([attn_c, attn_s], axis=0)

    dtb = jnp.pad(ssd_dt_bias[0].reshape(1, -1), ((0, 0), (0, DT_PAD - 2 * SSD_HEADS)))
    a_row = jnp.pad((-jnp.exp(ssd_a_log[0])).reshape(1, -1), ((0, 0), (0, DT_PAD - 2 * SSD_HEADS)))
    d_row = jnp.repeat(ssd_d[0], SSD_HEAD_DIM).reshape(1, -1)
    ssd_args = (ssd_conv_w[0], row2(ssd_conv_b[0]), dtb, a_row, d_row, row2(ssd_norm_g[0]))
    zero_st = jnp.zeros((BATCH, SSD_HEADS, SSD_HEAD_DIM, SSD_STATE), F32)
    y_c, hf_c, hb_c = _ssd(z, xbc, dt, *ssd_args, zero_st, zero_st, seq=SEQ, nb=BATCH, row0=0)
    y_s, _, _ = _ssd(z, xbc, dt, *ssd_args, state_ssd_fwd[:, 0], state_ssd_bwd[:, 0],
                     seq=DEC_SEQ, nb=DEC_BATCH, row0=R_CTX)
    y_ssd = jnp.concatenate([y_c, y_s], axis=0)
    x = _mixres(x, m[0], attn, y_ssd, bf(ab_w_out[0]))
    x = _ffn(x, row2(norm_ffn_g[0]), m[0], bf(ffn_w_gate[0]), bf(ffn_w_up[0]), bf(ffn_w_down[0]),
             row2(final_norm_g), final=False)

    w1 = bf(jnp.concatenate([rwkv_w1[0, 0], rwkv_w1[0, 1]], axis=1))
    a1 = bf(jnp.concatenate([rwkv_a1[0, 0], rwkv_a1[0, 1]], axis=1))
    zpad = lambda w: bf(jnp.stack([jnp.concatenate([w[0], jnp.zeros_like(w[1])], axis=0),
                                   jnp.concatenate([jnp.zeros_like(w[0]), w[1]], axis=0)]))
    pre = _rwkv_pre(x, row2(norm_mix_g[1]), m[1], rwkv_mu[0], bf(rwkv_w_r[0]), bf(rwkv_w_k[0]), bf(rwkv_w_v[0]),
                    w1, zpad(rwkv_w2[0]), a1, zpad(rwkv_a2[0]), bf(rwkv_g1[0]), bf(rwkv_g2[0]),
                    rwkv_w0[0], rwkv_a0[0], row2(rwkv_k_k[0]), row2(rwkv_k_a[0]), rwkv_r_k[0].reshape(1, -1))
    rk, v, wka, gg, bon = pre
    y_c, st_c = _scan_pass(rk, v, wka, None, BATCH, SEQ, 0)
    s0 = jnp.stack([state_rwkv_fwd[:, 0], state_rwkv_bwd[:, 0]])
    y_s, _ = _scan_pass(rk, v, wka, s0, DEC_BATCH, DEC_SEQ, R_CTX)
    x = _rwkv_post(x, m[1], y_c, y_s, bon, gg, row2(rwkv_ln_g[0]), row2(rwkv_ln_b[0]), bf(rwkv_w_o[0]))
    x = _ffn(x, row2(norm_ffn_g[1]), m[1], bf(ffn_w_gate[1]), bf(ffn_w_up[1]), bf(ffn_w_down[1]),
             row2(final_norm_g), final=True)

    y_prompt = x[:R_CTX].reshape(BATCH, SEQ, D_MODEL)
    y_sample = x[R_CTX:].reshape(DEC_BATCH, DEC_SEQ, D_MODEL)
    new_k = kn[:R_CTX].reshape(BATCH, 1, SEQ, ATTN_KV_HEADS, HEAD_DIM)
    new_v = qkv[:R_CTX, ATTN_Q_DIM + ATTN_KV_DIM:].reshape(BATCH, 1, SEQ, ATTN_KV_HEADS, HEAD_DIM)
    return (y_prompt, y_sample, new_k, new_v, hf_c[:, None], hb_c[:, None], st_c[0][:, None], st_c[1][:, None])
```

```python
import functools

import jax
import jax.numpy as jnp
from jax import lax
from jax.experimental import pallas as pl
from jax.experimental.pallas import tpu as pltpu

F32 = jnp.float32
BF16 = jnp.bfloat16

D_MODEL = 1024
BATCH = 16
SEQ = 256
DEC_BATCH = 2
DEC_SEQ = 1024
PAST_LEN = 256
GRID_W = 64
ATTN_HEADS = 8
ATTN_KV_HEADS = 2
HEAD_DIM = 64
ROPE_THETA = 10000.0
ATTN_Q_DIM = ATTN_HEADS * HEAD_DIM
ATTN_KV_DIM = ATTN_KV_HEADS * HEAD_DIM
SSD_HEADS = 8
SSD_HEAD_DIM = 64
SSD_D_INNER = SSD_HEADS * SSD_HEAD_DIM
SSD_GROUPS = 2
SSD_STATE = 64
SSD_CONV_K = 5
SSD_CHUNK = 128
SSD_CONV_DIM = SSD_D_INNER + 2 * SSD_GROUPS * SSD_STATE
AB_IN_DIM = ATTN_Q_DIM + 2 * ATTN_KV_DIM + SSD_D_INNER + SSD_CONV_DIM + 2 * SSD_HEADS
RWKV_HEAD_DIM = 64
RWKV_HEADS = D_MODEL // RWKV_HEAD_DIM
FFN_DIM = (((8 * D_MODEL + 2) // 3 + 255) // 256) * 256
RMS_EPS = 1e-6
GN_EPS = 64e-5
L2_EPS = 1e-12

R_CTX = BATCH * SEQ
R_SMP = DEC_BATCH * DEC_SEQ
R_ALL = R_CTX + R_SMP
LANES = 128
QKV_DIM = ATTN_Q_DIM + 2 * ATTN_KV_DIM
DT_PAD = LANES
AB_PAD = QKV_DIM + SSD_D_INNER + SSD_CONV_DIM + DT_PAD
VMEM_LIMIT = 56 * 1024 * 1024


def _cparams(sem):
    return pltpu.CompilerParams(dimension_semantics=sem, vmem_limit_bytes=VMEM_LIMIT)


def _mod_index(i, tm):
    n_ctx = R_CTX // tm
    per = DEC_SEQ // tm
    return jnp.where(i < n_ctx, 0, 1 + jnp.maximum(i - n_ctx, 0) // per)


def _sigmoid(x):
    return 1.0 / (1.0 + jnp.exp(-x))


def _silu(x):
    return x * _sigmoid(x)


def _softplus(x):
    return jnp.maximum(x, 0.0) + jnp.log1p(jnp.exp(-jnp.abs(x)))


def _rms(x, g):
    return x * lax.rsqrt(jnp.mean(x * x, axis=-1, keepdims=True) + RMS_EPS) * g


def _group_allsum(x, group):
    n = x.shape[-1]
    ax = x.ndim - 1
    lane = lax.broadcasted_iota(jnp.int32, x.shape, ax)
    s = 1
    while s < group:
        up = pltpu.roll(x, n - s, axis=ax)
        dn = pltpu.roll(x, s, axis=ax)
        x = x + jnp.where((lane & s) == 0, up, dn)
        s *= 2
    return x


def _lane_group_allsum(s, flat=False):
    ax = s.ndim - 1
    if flat:
        parts = [s] + [pltpu.roll(s, g * RWKV_HEADS, axis=ax) for g in range(1, LANES // RWKV_HEADS)]
        while len(parts) > 1:
            parts = [parts[i] + parts[i + 1] for i in range(0, len(parts), 2)]
        return parts[0]
    for sh in (LANES // 2, LANES // 4, LANES // 8):
        s = s + pltpu.roll(s, sh, axis=ax)
    return s


def _tile_lanes(s, n):
    return jnp.concatenate([s] * n, axis=s.ndim - 1)


def _head_allsum(x):
    s = x[:, 0:LANES]
    for j in range(1, x.shape[-1] // LANES):
        s = s + x[:, j * LANES:(j + 1) * LANES]
    return _lane_group_allsum(s)


def _dot(a, b):
    return jnp.dot(a, b, preferred_element_type=F32)


ADA_TN = 1536


def _ada_kernel(c_ref, w_ref, b_ref, o_ref):
    s = _silu(c_ref[...]).astype(BF16)
    o_ref[0] = _dot(s, w_ref[0].astype(BF16)) + b_ref[0]


def _ada(cv, mod_w, mod_b):
    depth = mod_w.shape[0]
    n = mod_w.shape[2]
    return pl.pallas_call(
        _ada_kernel,
        out_shape=jax.ShapeDtypeStruct((depth, 8, n), F32),
        grid=(depth, n // ADA_TN),
        in_specs=[
            pl.BlockSpec((8, D_MODEL), lambda l, j: (0, 0)),
            pl.BlockSpec((1, D_MODEL, ADA_TN), lambda l, j: (l, 0, j)),
            pl.BlockSpec((1, 1, ADA_TN), lambda l, j: (l, 0, j)),
        ],
        out_specs=pl.BlockSpec((1, 8, ADA_TN), lambda l, j: (l, 0, j)),
        compiler_params=_cparams(("arbitrary", "arbitrary")),
        name="ada",
    )(cv, mod_w, mod_b.reshape(depth, 1, n))


INPROJ_TM = 512


def _inproj_kernel(x_ref, g_ref, mod_ref, w_ref, qkv_ref, z_ref, xbc_ref, dt_ref):
    m = mod_ref[0]
    h = _rms(x_ref[...], g_ref[...]) * (1 + m[1:2]) + m[0:1]
    p = _dot(h.astype(BF16), w_ref[...])
    qkv_ref[...] = p[:, 0:QKV_DIM]
    z_ref[...] = p[:, QKV_DIM:QKV_DIM + SSD_D_INNER]
    xbc_ref[...] = p[:, QKV_DIM + SSD_D_INNER:QKV_DIM + SSD_D_INNER + SSD_CONV_DIM]
    dt_ref[...] = p[:, QKV_DIM + SSD_D_INNER + SSD_CONV_DIM:AB_PAD]


def _inproj(x, g, mod, w_pad):
    tm = INPROJ_TM
    row = lambda i: (i, 0)
    return pl.pallas_call(
        _inproj_kernel,
        out_shape=(
            jax.ShapeDtypeStruct((R_ALL, QKV_DIM), F32),
            jax.ShapeDtypeStruct((R_ALL, SSD_D_INNER), F32),
            jax.ShapeDtypeStruct((R_ALL, SSD_CONV_DIM), F32),
            jax.ShapeDtypeStruct((R_ALL, DT_PAD), F32),
        ),
        grid=(R_ALL // tm,),
        in_specs=[
            pl.BlockSpec((tm, D_MODEL), row),
            pl.BlockSpec((1, D_MODEL), lambda i: (0, 0)),
            pl.BlockSpec((1, 8, D_MODEL), lambda i: (_mod_index(i, tm), 0, 0)),
            pl.BlockSpec((D_MODEL, AB_PAD), lambda i: (0, 0)),
        ],
        out_specs=(
            pl.BlockSpec((tm, QKV_DIM), row),
            pl.BlockSpec((tm, SSD_D_INNER), row),
            pl.BlockSpec((tm, SSD_CONV_DIM), row),
            pl.BlockSpec((tm, DT_PAD), row),
        ),
        compiler_params=_cparams(("arbitrary",)),
        name="inproj",
    )(x, g, mod, w_pad)


QK_TM = 256


def _qkprep_kernel(qkv_ref, qg_ref, kg_ref, cos_ref, sin_ref, qn_ref, kn_ref):
    cos = cos_ref[...]
    sin = sin_ref[...]

    def norm_rope(x, g, reps):
        ms = _group_allsum(x * x, HEAD_DIM) * (1.0 / HEAD_DIM)
        y = x * lax.rsqrt(ms + RMS_EPS) * g
        n = y.shape[-1]
        lane = lax.broadcasted_iota(jnp.int32, y.shape, 1)
        half = HEAD_DIM // 2
        swapped = jnp.where((lane & half) == 0, pltpu.roll(y, n - half, axis=1), pltpu.roll(y, half, axis=1))
        c = jnp.concatenate([cos] * reps, axis=1) if reps > 1 else cos
        s = jnp.concatenate([sin] * reps, axis=1) if reps > 1 else sin
        return y * c + swapped * s

    q = qkv_ref[:, 0:ATTN_Q_DIM]
    k = qkv_ref[:, ATTN_Q_DIM:ATTN_Q_DIM + ATTN_KV_DIM]
    qn_ref[...] = norm_rope(q, qg_ref[...], ATTN_Q_DIM // LANES)
    kn_ref[...] = norm_rope(k, kg_ref[...], 1)


def _qkprep(qkv, qg, kg, cos_t, sin_t):
    tm = QK_TM
    n_ctx = R_CTX // tm
    per = DEC_SEQ // tm
    tab = lambda i: (jnp.where(i < n_ctx, 0, 1 + jnp.maximum(i - n_ctx, 0) % per), 0)
    return pl.pallas_call(
        _qkprep_kernel,
        out_shape=(
            jax.ShapeDtypeStruct((R_ALL, ATTN_Q_DIM), F32),
            jax.ShapeDtypeStruct((R_ALL, ATTN_KV_DIM), F32),
        ),
        grid=(R_ALL // tm,),
        in_specs=[
            pl.BlockSpec((tm, QKV_DIM), lambda i: (i, 0)),
            pl.BlockSpec((1, ATTN_Q_DIM), lambda i: (0, 0)),
            pl.BlockSpec((1, ATTN_KV_DIM), lambda i: (0, 0)),
            pl.BlockSpec((tm, LANES), tab),
            pl.BlockSpec((tm, LANES), tab),
        ],
        out_specs=(
            pl.BlockSpec((tm, ATTN_Q_DIM), lambda i: (i, 0)),
            pl.BlockSpec((tm, ATTN_KV_DIM), lambda i: (i, 0)),
        ),
        compiler_params=_cparams(("arbitrary",)),
        name="qkprep",
    )(qkv, qg, kg, cos_t, sin_t)


def _attn_core(q, ks, vs):
    tq = q.shape[0]
    grp = ATTN_HEADS // ATTN_KV_HEADS
    scale = HEAD_DIM ** -0.5
    outs = []
    for g in range(ATTN_KV_HEADS):
        sl = slice(g * HEAD_DIM, (g + 1) * HEAD_DIM)
        qs = jnp.concatenate(
            [q[:, (g * grp + j) * HEAD_DIM:(g * grp + j + 1) * HEAD_DIM] for j in range(grp)], axis=0).astype(BF16)
        ss = [lax.dot_general(qs, k[:, sl].astype(BF16), (((1,), (1,)), ((), ())),
                              preferred_element_type=F32) * scale for k in ks]
        m = ss[0].max(axis=-1, keepdims=True)
        for s in ss[1:]:
            m = jnp.maximum(m, s.max(axis=-1, keepdims=True))
        ps = [jnp.exp(s - m) for s in ss]
        l = ps[0].sum(axis=-1, keepdims=True)
        for p in ps[1:]:
            l = l + p.sum(axis=-1, keepdims=True)
        inv = 1.0 / l
        o = None
        for p, v in zip(ps, vs):
            t = _dot((p * inv).astype(BF16), v[:, sl].astype(BF16))
            o = t if o is None else o + t
        outs += [o[j * tq:(j + 1) * tq] for j in range(grp)]
    return jnp.concatenate(outs, axis=1)


def _attn_ctx_kernel(q_ref, k_ref, v_ref, o_ref):
    o_ref[...] = _attn_core(q_ref[...], [k_ref[...]], [v_ref[...]]).astype(BF16)


def _attn_smp_kernel(q_ref, k_ref, v_ref, ck_ref, cv_ref, o_ref):
    ks = [ck_ref[0], k_ref[...]]
    vs = [cv_ref[0], v_ref[...]]
    o_ref[...] = _attn_core(q_ref[...], ks, vs).astype(BF16)


ATTN_TQ = 128


def _attention(qn, kn, qkv, cache_k, cache_v):
    v_col = (ATTN_Q_DIM + ATTN_KV_DIM) // ATTN_KV_DIM
    ctx = pl.pallas_call(
        _attn_ctx_kernel,
        out_shape=jax.ShapeDtypeStruct((R_CTX, ATTN_Q_DIM), BF16),
        grid=(BATCH,),
        in_specs=[
            pl.BlockSpec((SEQ, ATTN_Q_DIM), lambda b: (b, 0)),
            pl.BlockSpec((SEQ, ATTN_KV_DIM), lambda b: (b, 0)),
            pl.BlockSpec((SEQ, ATTN_KV_DIM), lambda b: (b, v_col)),
        ],
        out_specs=pl.BlockSpec((SEQ, ATTN_Q_DIM), lambda b: (b, 0)),
        compiler_params=_cparams(("arbitrary",)),
        name="attn_ctx",
    )(qn, kn, qkv)
    nq = DEC_SEQ // ATTN_TQ
    q0 = R_CTX // ATTN_TQ
    s0 = R_CTX // DEC_SEQ
    smp = pl.pallas_call(
        _attn_smp_kernel,
        out_shape=jax.ShapeDtypeStruct((R_SMP, ATTN_Q_DIM), BF16),
        grid=(DEC_BATCH, nq),
        in_specs=[
            pl.BlockSpec((ATTN_TQ, ATTN_Q_DIM), lambda b, i: (q0 + b * nq + i, 0)),
            pl.BlockSpec((DEC_SEQ, ATTN_KV_DIM), lambda b, i: (s0 + b, 0)),
            pl.BlockSpec((DEC_SEQ, ATTN_KV_DIM), lambda b, i: (s0 + b, v_col)),
            pl.BlockSpec((1, PAST_LEN, ATTN_KV_DIM), lambda b, i: (b, 0, 0)),
            pl.BlockSpec((1, PAST_LEN, ATTN_KV_DIM), lambda b, i: (b, 0, 0)),
        ],
        out_specs=pl.BlockSpec((ATTN_TQ, ATTN_Q_DIM), lambda b, i: (b * nq + i, 0)),
        compiler_params=_cparams(("arbitrary", "arbitrary")),
        name="attn_smp",
    )(qn, kn, qkv, cache_k, cache_v)
    return ctx, smp


CONV_HALO = 8


def _cumsum_rows(a, reverse):
    n = a.shape[0]
    row = lax.broadcasted_iota(jnp.int32, a.shape, 0)
    s = 1
    while s < n:
        if reverse:
            a = a + jnp.where(row < n - s, pltpu.roll(a, n - s, axis=0), 0.0)
        else:
            a = a + jnp.where(row >= s, pltpu.roll(a, s, axis=0), 0.0)
        s *= 2
    return a


def _ssd_kernel(z_ref, xbc_ref, dt_ref, cw_ref, cb_ref, dtb_ref, a_ref, d_ref, g_ref, h0f_ref, h0b_ref,
                y_ref, hf_ref, hb_ref, pad_sc, xc_sc, dt_sc, y_sc, h_sc, *, seq):
    L = SSD_CHUNK
    nc = seq // L
    pad = SSD_CONV_K // 2
    zeros = jnp.zeros((CONV_HALO, SSD_CONV_DIM), F32)
    pad_sc[0:CONV_HALO, :] = zeros
    pad_sc[seq + CONV_HALO:seq + 2 * CONV_HALO, :] = zeros
    pad_sc[CONV_HALO:seq + CONV_HALO, :] = xbc_ref[...]
    h_sc[0] = h0f_ref[0]
    h_sc[1] = h0b_ref[0]

    def conv_chunk(c, carry):
        r0 = pl.multiple_of(c * L, L)
        win = pad_sc[pl.ds(r0, L + 2 * CONV_HALO), :]
        acc = cb_ref[...]
        for i in range(SSD_CONV_K):
            acc = acc + win[CONV_HALO - pad + i:CONV_HALO - pad + i + L, :] * cw_ref[i:i + 1, :]
        xc_sc[pl.ds(r0, L), :] = _silu(acc)
        dt_sc[pl.ds(r0, L), :] = _softplus(dt_ref[pl.ds(r0, L), :] + dtb_ref[...])
        return carry

    lax.fori_loop(0, nc, conv_chunk, 0)

    rr = lax.broadcasted_iota(jnp.int32, (L, L), 0)
    cc = lax.broadcasted_iota(jnp.int32, (L, L), 1)
    grp = SSD_HEADS // SSD_GROUPS
    P = SSD_HEAD_DIM
    N = SSD_STATE

    def make_chunk(dirn):
        mask = (rr >= cc) if dirn == 0 else (rr <= cc)

        def chunk(ci, carry):
            c = ci if dirn == 0 else nc - 1 - ci
            r0 = pl.multiple_of(c * L, L)
            xs = xc_sc[pl.ds(r0, L), 0:SSD_D_INNER]
            bm = xc_sc[pl.ds(r0, L), SSD_D_INNER:SSD_D_INNER + SSD_GROUPS * N]
            cm = xc_sc[pl.ds(r0, L), SSD_D_INNER + SSD_GROUPS * N:SSD_CONV_DIM]
            dtc = dt_sc[pl.ds(r0, L), :]
            acs = _cumsum_rows(dtc * a_ref[...], reverse=(dirn == 1))
            acs_t = acs.T
            tot = acs[L - 1:L, :] if dirn == 0 else acs[0:1, :]
            dec_end = jnp.exp(tot - acs)
            eacs = jnp.exp(acs)
            cdec = jnp.exp(tot)
            cb = [lax.dot_general(cm[:, g * N:(g + 1) * N].astype(BF16), bm[:, g * N:(g + 1) * N].astype(BF16),
                                  (((1,), (1,)), ((), ())), preferred_element_type=F32)
                  for g in range(SSD_GROUPS)]
            if dirn == 0:
                y_sc[pl.ds(r0, L), :] = xs * d_ref[...]
            for h in range(SSD_HEADS):
                g = h // grp
                ln = h + SSD_HEADS * dirn
                seg = jnp.exp(jnp.where(mask, acs[:, ln:ln + 1] - acs_t[ln:ln + 1, :], -jnp.inf))
                sc = (cb[g] * seg).astype(BF16)
                xdt = (xs[:, h * P:(h + 1) * P] * dtc[:, ln:ln + 1]).astype(BF16)
                hin = h_sc[dirn, h]
                ce = (cm[:, g * N:(g + 1) * N] * eacs[:, ln:ln + 1]).astype(BF16)
                yh = _dot(sc, xdt) + lax.dot_general(ce, hin.astype(BF16), (((1,), (1,)), ((), ())),
                                                     preferred_element_type=F32)
                bd = (bm[:, g * N:(g + 1) * N] * dec_end[:, ln:ln + 1]).astype(BF16)
                st = lax.dot_general(xdt, bd, (((0,), (0,)), ((), ())), preferred_element_type=F32)
                h_sc[dirn, h] = hin * cdec[:, ln:ln + 1] + st
                y_sc[pl.ds(r0, L), h * P:(h + 1) * P] += yh
            return carry

        return chunk

    lax.fori_loop(0, nc, make_chunk(0), 0)
    lax.fori_loop(0, nc, make_chunk(1), 0)

    def out_chunk(c, carry):
        r0 = pl.multiple_of(c * L, L)
        y = y_sc[pl.ds(r0, L), :] * _silu(z_ref[pl.ds(r0, L), :])
        y_ref[pl.ds(r0, L), :] = _rms(y, g_ref[...]).astype(BF16)
        return carry

    lax.fori_loop(0, nc, out_chunk, 0)
    hf_ref[0] = h_sc[0]
    hb_ref[0] = h_sc[1]


def _ssd(z, xbc, dt, cw, cb, dtb, a_row, d_row, g, h0f, h0b, *, seq, nb, row0):
    blk0 = row0 // seq
    row = lambda b: (blk0 + b, 0)
    fixed = lambda b: (0, 0)
    st = lambda b: (b, 0, 0, 0)
    st_shape = (nb, SSD_HEADS, SSD_HEAD_DIM, SSD_STATE)
    st_blk = (1, SSD_HEADS, SSD_HEAD_DIM, SSD_STATE)
    return pl.pallas_call(
        functools.partial(_ssd_kernel, seq=seq),
        out_shape=(
            jax.ShapeDtypeStruct((nb * seq, SSD_D_INNER), BF16),
            jax.ShapeDtypeStruct(st_shape, F32),
            jax.ShapeDtypeStruct(st_shape, F32),
        ),
        grid=(nb,),
        in_specs=[
            pl.BlockSpec((seq, SSD_D_INNER), row),
            pl.BlockSpec((seq, SSD_CONV_DIM), row),
            pl.BlockSpec((seq, DT_PAD), row),
            pl.BlockSpec((SSD_CONV_K, SSD_CONV_DIM), fixed),
            pl.BlockSpec((1, SSD_CONV_DIM), fixed),
            pl.BlockSpec((1, DT_PAD), fixed),
            pl.BlockSpec((1, DT_PAD), fixed),
            pl.BlockSpec((1, SSD_D_INNER), fixed),
            pl.BlockSpec((1, SSD_D_INNER), fixed),
            pl.BlockSpec(st_blk, st),
            pl.BlockSpec(st_blk, st),
        ],
        out_specs=(
            pl.BlockSpec((seq, SSD_D_INNER), lambda b: (b, 0)),
            pl.BlockSpec(st_blk, st),
            pl.BlockSpec(st_blk, st),
        ),
        scratch_shapes=[
            pltpu.VMEM((seq + 2 * CONV_HALO, SSD_CONV_DIM), F32),
            pltpu.VMEM((seq, SSD_CONV_DIM), F32),
            pltpu.VMEM((seq, DT_PAD), F32),
            pltpu.VMEM((seq, SSD_D_INNER), F32),
            pltpu.VMEM((2, SSD_HEADS, SSD_HEAD_DIM, SSD_STATE), F32),
        ],
        compiler_params=_cparams(("arbitrary",)),
        name=f"ssd_{seq}",
    )(z, xbc, dt, cw, cb, dtb, a_row, d_row, g, h0f, h0b)


RES_TM = 512


def _mixres_kernel(x_ref, mod_ref, a1_ref, a2_ref, w_ref, o_ref):
    k1 = a1_ref.shape[1]
    out = _dot(a1_ref[...], w_ref[0:k1, :]) + _dot(a2_ref[...], w_ref[k1:, :])
    o_ref[...] = x_ref[...] + mod_ref[0][2:3] * out


def _mixres(x, mod, a1, a2, w):
    tm = RES_TM
    row = lambda i: (i, 0)
    return pl.pallas_call(
        _mixres_kernel,
        out_shape=jax.ShapeDtypeStruct((R_ALL, D_MODEL), F32),
        grid=(R_ALL // tm,),
        in_specs=[
            pl.BlockSpec((tm, D_MODEL), row),
            pl.BlockSpec((1, 8, D_MODEL), lambda i: (_mod_index(i, tm), 0, 0)),
            pl.BlockSpec((tm, a1.shape[1]), row),
            pl.BlockSpec((tm, a2.shape[1]), row),
            pl.BlockSpec(w.shape, lambda i: (0, 0)),
        ],
        out_specs=pl.BlockSpec((tm, D_MODEL), row),
        compiler_params=_cparams(("arbitrary",)),
        name="mixres",
    )(x, mod, a1, a2, w)


FFN_TM = 1024
FFN_TF = 256


def _ffn_kernel(x_ref, g_ref, mod_ref, wg_ref, wu_ref, wd_ref, fg_ref, o_ref, h_sc, acc_sc, *, final):
    j = pl.program_id(1)

    @pl.when(j == 0)
    def _():
        m = mod_ref[0]
        h = _rms(x_ref[...], g_ref[...]) * (1 + m[4:5]) + m[3:4]
        h_sc[...] = h.astype(BF16)
        acc_sc[...] = jnp.zeros_like(acc_sc)

    h = h_sc[...]
    hid = _silu(_dot(h, wg_ref[...])) * _dot(h, wu_ref[...])
    acc_sc[...] += _dot(hid.astype(BF16), wd_ref[...])

    @pl.when(j == pl.num_programs(1) - 1)
    def _():
        y = x_ref[...] + mod_ref[0][5:6] * acc_sc[...]
        if final:
            y = _rms(y, fg_ref[...])
        o_ref[...] = y


def _ffn(x, g, mod, wg, wu, wd, fg, *, final):
    tm, tf = FFN_TM, FFN_TF
    row = lambda i, j: (i, 0)
    return pl.pallas_call(
        functools.partial(_ffn_kernel, final=final),
        out_shape=jax.ShapeDtypeStruct((R_ALL, D_MODEL), F32),
        grid=(R_ALL // tm, FFN_DIM // tf),
        in_specs=[
            pl.BlockSpec((tm, D_MODEL), row),
            pl.BlockSpec((1, D_MODEL), lambda i, j: (0, 0)),
            pl.BlockSpec((1, 8, D_MODEL), lambda i, j: (_mod_index(i, tm), 0, 0)),
            pl.BlockSpec((D_MODEL, tf), lambda i, j: (0, j)),
            pl.BlockSpec((D_MODEL, tf), lambda i, j: (0, j)),
            pl.BlockSpec((tf, D_MODEL), lambda i, j: (j, 0)),
            pl.BlockSpec((1, D_MODEL), lambda i, j: (0, 0)),
        ],
        out_specs=pl.BlockSpec((tm, D_MODEL), row),
        scratch_shapes=[pltpu.VMEM((tm, D_MODEL), BF16), pltpu.VMEM((tm, D_MODEL), F32)],
        compiler_params=_cparams(("arbitrary", "arbitrary")),
        name="ffn_final" if final else "ffn",
    )(x, g, mod, wg, wu, wd, fg)


RW_TM = 128
HALO = 8


def _rwkv_pre_kernel(x_ref, xp_ref, xn_ref, g_ref, mod_ref, mu_ref, wr_ref, wk_ref, wv_ref, w1_ref, w2_ref,
                     a1_ref, a2_ref, g1_ref, g2_ref, w0_ref, a0_ref, kk_ref, ka_ref, rk_ref,
                     rk_o, v_o, wka_o, g_o, bon_o):
    i = pl.program_id(0)
    tm = RW_TM
    n_ctx = R_CTX // tm
    per_c = SEQ // tm
    per_s = DEC_SEQ // tm
    rel = jnp.where(i < n_ctx, i % per_c, jnp.maximum(i - n_ctx, 0) % per_s)
    last = jnp.where(i < n_ctx, per_c - 1, per_s - 1)
    m = mod_ref[0]

    def nm(x):
        return _rms(x, g_ref[...]) * (1 + m[1:2]) + m[0:1]

    h = nm(x_ref[...])
    prev_row = jnp.where(rel == 0, 0.0, nm(xp_ref[...])[HALO - 1:HALO, :])
    next_row = jnp.where(rel == last, 0.0, nm(xn_ref[...])[0:1, :])
    row = lax.broadcasted_iota(jnp.int32, h.shape, 0)
    hp = jnp.where(row == 0, prev_row, pltpu.roll(h, 1, axis=0))
    hn = jnp.where(row == tm - 1, next_row, pltpu.roll(h, tm - 1, axis=0))
    dp = hp - h
    dn = hn - h

    def mix(idx):
        return (h + dp * mu_ref[0, idx:idx + 1, :] + dn * mu_ref[1, idx:idx + 1, :]).astype(BF16)

    r = _dot(mix(0), wr_ref[...])
    k = _dot(mix(2), wk_ref[...])
    v = _dot(mix(3), wv_ref[...])
    lw = jnp.tanh(_dot(mix(1), w1_ref[...])).astype(BF16)
    la = _dot(mix(4), a1_ref[...]).astype(BF16)
    gg = _dot(_sigmoid(_dot(mix(5), g1_ref[...])).astype(BF16), g2_ref[...])

    kk = k * kk_ref[...]
    nrep = D_MODEL // LANES
    kk = kk * _tile_lanes(lax.rsqrt(_head_allsum(kk * kk) + L2_EPS), nrep)
    rk_o[0] = r
    rk_o[1] = kk
    v_o[...] = v
    g_o[...] = gg
    bsum = None
    for j in range(2):
        wl = w0_ref[j:j + 1, :] + _dot(lw, w2_ref[j])
        wka_o[j, 0] = jnp.exp(-jnp.exp(-_softplus(-wl) - 0.5))
        a = _sigmoid(a0_ref[j:j + 1, :] + _dot(la, a2_ref[j]))
        kd = k * (1 + (a - 1) * ka_ref[...])
        wka_o[j, 1] = kd
        wka_o[j, 2] = kk * a
        t = r * kd * rk_ref[...]
        bsum = t if bsum is None else bsum + t
    bon_o[...] = _tile_lanes(_head_allsum(bsum), nrep) * v


def _rwkv_pre(x, g, mod, mu, wr, wk, wv, w1, w2, a1, a2, g1, g2, w0, a0, k_k, k_a, r_k):
    tm = RW_TM
    nblk = R_ALL // HALO
    per = tm // HALO
    row = lambda i: (i, 0)
    fixed2 = lambda i: (0, 0)
    fixed3 = lambda i: (0, 0, 0)
    full = lambda a: pl.BlockSpec(a.shape, fixed2 if a.ndim == 2 else fixed3)
    out = jax.ShapeDtypeStruct((R_ALL, D_MODEL), F32)
    orow = pl.BlockSpec((tm, D_MODEL), row)
    return pl.pallas_call(
        _rwkv_pre_kernel,
        out_shape=(jax.ShapeDtypeStruct((2, R_ALL, D_MODEL), F32), out,
                   jax.ShapeDtypeStruct((2, 3, R_ALL, D_MODEL), F32), out, out),
        grid=(R_ALL // tm,),
        in_specs=[
            pl.BlockSpec((tm, D_MODEL), row),
            pl.BlockSpec((HALO, D_MODEL), lambda i: (jnp.maximum(i * per - 1, 0), 0)),
            pl.BlockSpec((HALO, D_MODEL), lambda i: (jnp.minimum((i + 1) * per, nblk - 1), 0)),
            pl.BlockSpec((1, D_MODEL), fixed2),
            pl.BlockSpec((1, 8, D_MODEL), lambda i: (_mod_index(i, tm), 0, 0)),
            full(mu), full(wr), full(wk), full(wv), full(w1), full(w2), full(a1), full(a2), full(g1), full(g2),
            full(w0), full(a0), full(k_k), full(k_a), full(r_k),
        ],
        out_specs=(pl.BlockSpec((2, tm, D_MODEL), lambda i: (0, i, 0)), orow,
                   pl.BlockSpec((2, 3, tm, D_MODEL), lambda i: (0, 0, i, 0)), orow, orow),
        compiler_params=_cparams(("arbitrary",)),
        name="rwkv_pre",
    )(x, x, x, g, mod, mu, wr, wk, wv, w1, w2, a1, a2, g1, g2, w0, a0, k_k, k_a, r_k)


SCAN_TT = 16
SCAN_NB_CTX = 2
SCAN_NB_SMP = 2
NK = RWKV_HEAD_DIM
KSUB = NK * RWKV_HEADS // LANES
NGRP = LANES // RWKV_HEADS


def _lane_group_allsum2(s):
    ax = s.ndim - 1
    s = s + pltpu.roll(s, LANES // 2, axis=ax)
    r = [pltpu.roll(s, g * RWKV_HEADS, axis=ax) for g in range(1, NGRP // 2)]
    return (s + r[0]) + (r[1] + r[2])


def _scan_kernel(rkf, rkb, vf, vb, dkf, dkb, s0_ref, gs_ref, yf_ref, yb_ref, st_ref, s_sc, yp_sc, *, tt, nbb):
    j = pl.program_id(1)

    @pl.when(j == 0)
    def _():
        s_sc[...] = s0_ref[...]

    refs = ((rkf, vf, dkf, yf_ref), (rkb, vb, dkb, yb_ref))
    seqs = range(nbb)

    def tix(d, i):
        return i if d == 0 else tt - 1 - i

    def project(d, b, i):
        rk = refs[d][0]
        t = tix(d, i)
        sa = s_sc[d, b, 0] * rk[1, b, t, 0:1, :]
        for k in range(1, KSUB):
            sa = sa + s_sc[d, b, k] * rk[1, b, t, k:k + 1, :]
        return _lane_group_allsum2(sa)

    def update(d, b, i, sa):
        rk, vr, dk, _ = refs[d]
        t = tix(d, i)
        vv = vr[b, t]
        y = None
        for k in range(KSUB):
            sn = (s_sc[d, b, k] * dk[0, 0, b, t, k:k + 1, :] - sa * dk[0, 2, b, t, k:k + 1, :]
                  + vv * dk[0, 1, b, t, k:k + 1, :])
            s_sc[d, b, k] = sn
            yk = sn * rk[0, b, t, k:k + 1, :]
            y = yk if y is None else y + yk
        yp_sc[i % 2, d, b] = y

    gs = gs_ref[...]

    def emit(i):
        for d in range(2):
            for b in seqs:
                y = jnp.dot(yp_sc[i % 2, d, b], gs, precision=lax.Precision.HIGHEST, preferred_element_type=F32)
                refs[d][3][b, tix(d, i)] = y[:, 0:RWKV_HEADS]

    def body(i, sa_b, with_emit):
        for b in seqs:
            update(1, b, i, sa_b[b])
        sa_f = [project(0, b, i) for b in seqs]
        nxt = tuple(project(1, b, i + 1) for b in seqs)
        if with_emit:
            emit(i - 1)
        for b in seqs:
            update(0, b, i, sa_f[b])
        return nxt

    carry = body(0, tuple(project(1, b, 0) for b in seqs), False)
    carry = lax.fori_loop(1, tt - 1, lambda i, c: body(i, c, True), carry)
    last = tt - 1
    emit(last - 1)
    for b in seqs:
        update(1, b, last, carry[b])
    sa_f = [project(0, b, last) for b in seqs]
    for b in seqs:
        update(0, b, last, sa_f[b])
    emit(last)

    @pl.when(j == pl.num_programs(1) - 1)
    def _():
        st_ref[...] = s_sc[...]


def _scan(rk, vrep, wka, s0, *, seq, nb, row0, nbb, tt):
    nt = seq // tt
    nseq = R_ALL // seq
    g0 = row0 // seq // nbb
    rk5 = rk.reshape(2, nseq, seq, KSUB, LANES)
    dk6 = wka.reshape(2, 3, nseq, seq, KSUB, LANES)
    v4 = vrep.reshape(nseq, seq, NK, LANES)
    sblk = pl.BlockSpec((2, nbb, KSUB, NK, LANES), lambda g, j: (0, g, 0, 0, 0))
    lane = jnp.arange(LANES)
    gs = (lane[:, None] % RWKV_HEADS == lane[None, :] % RWKV_HEADS).astype(F32)
    yshape = jax.ShapeDtypeStruct((nb, seq, NK, RWKV_HEADS), F32)
    yf, yb, st = pl.pallas_call(
        functools.partial(_scan_kernel, tt=tt, nbb=nbb),
        out_shape=(yshape, yshape, jax.ShapeDtypeStruct((2, nb, KSUB, NK, LANES), F32)),
        grid=(nb // nbb, nt),
        in_specs=[
            pl.BlockSpec((2, nbb, tt, KSUB, LANES), lambda g, j: (0, g0 + g, j, 0, 0)),
            pl.BlockSpec((2, nbb, tt, KSUB, LANES), lambda g, j: (0, g0 + g, nt - 1 - j, 0, 0)),
            pl.BlockSpec((nbb, tt, NK, LANES), lambda g, j: (g0 + g, j, 0, 0)),
            pl.BlockSpec((nbb, tt, NK, LANES), lambda g, j: (g0 + g, nt - 1 - j, 0, 0)),
            pl.BlockSpec((1, 3, nbb, tt, KSUB, LANES), lambda g, j: (0, 0, g0 + g, j, 0, 0)),
            pl.BlockSpec((1, 3, nbb, tt, KSUB, LANES), lambda g, j: (1, 0, g0 + g, nt - 1 - j, 0, 0)),
            sblk,
            pl.BlockSpec((LANES, LANES), lambda g, j: (0, 0)),
        ],
        out_specs=(
            pl.BlockSpec((nbb, tt, NK, RWKV_HEADS), lambda g, j: (g, j, 0, 0)),
            pl.BlockSpec((nbb, tt, NK, RWKV_HEADS), lambda g, j: (g, nt - 1 - j, 0, 0)),
            sblk,
        ),
        scratch_shapes=[pltpu.VMEM((2, nbb, KSUB, NK, LANES), F32), pltpu.VMEM((2, 2, nbb, NK, LANES), F32)],
        compiler_params=_cparams(("arbitrary", "arbitrary")),
        name=f"rwkv_scan_{seq}",
    )(rk5, rk5, v4, v4, dk6, dk6, s0, gs)
    return yf.reshape(nb * seq, D_MODEL), yb.reshape(nb * seq, D_MODEL), st


POST_TM = 256


def _rwkv_post_kernel(x_ref, mod_ref, ycf_ref, ycb_ref, ysf_ref, ysb_ref, bon_ref, g_ref, lng_ref, lnb_ref, wo_ref,
                      o_ref):
    is_ctx = pl.program_id(0) < R_CTX // POST_TM
    y = jnp.where(is_ctx, ycf_ref[...] + ycb_ref[...], ysf_ref[...] + ysb_ref[...])
    inv = 1.0 / RWKV_HEAD_DIM
    nrep = D_MODEL // LANES
    d = y - _tile_lanes(_head_allsum(y) * inv, nrep)
    var = _head_allsum(d * d) * inv
    yn = d * _tile_lanes(lax.rsqrt(var + GN_EPS), nrep) * lng_ref[...] + lnb_ref[...]
    out = ((yn + bon_ref[...]) * g_ref[...]).astype(BF16)
    o_ref[...] = x_ref[...] + mod_ref[0][2:3] * _dot(out, wo_ref[...])


def _rwkv_post(x, mod, ycf, ycb_, ysf, ysb_, bon, g, lng, lnb, wo):
    tm = POST_TM
    n_ctx = R_CTX // tm
    row = lambda i: (i, 0)
    fixed = lambda i: (0, 0)
    big = pl.BlockSpec((tm, D_MODEL), row)
    ycb = pl.BlockSpec((tm, D_MODEL), lambda i: (jnp.minimum(i, n_ctx - 1), 0))
    ysb = pl.BlockSpec((tm, D_MODEL), lambda i: (jnp.maximum(i - n_ctx, 0), 0))
    return pl.pallas_call(
        _rwkv_post_kernel,
        out_shape=jax.ShapeDtypeStruct((R_ALL, D_MODEL), F32),
        grid=(R_ALL // tm,),
        in_specs=[
            big,
            pl.BlockSpec((1, 8, D_MODEL), lambda i: (_mod_index(i, tm), 0, 0)),
            ycb, ycb, ysb, ysb, big, big,
            pl.BlockSpec((1, D_MODEL), fixed),
            pl.BlockSpec((1, D_MODEL), fixed),
            pl.BlockSpec((D_MODEL, D_MODEL), fixed),
        ],
        out_specs=big,
        compiler_params=_cparams(("arbitrary",)),
        name="rwkv_post",
    )(x, mod, ycf, ycb_, ysf, ysb_, bon, g, lng, lnb, wo)


def _head_minor_perm():
    c = jnp.arange(D_MODEL)
    return (c % RWKV_HEADS) * NK + c // RWKV_HEADS


def _state_to_scan(s):
    nd, nb = s.shape[:2]
    s = s.reshape(nd, nb, RWKV_HEADS, NK, KSUB, NK // KSUB).transpose(0, 1, 4, 3, 5, 2)
    return s.reshape(nd, nb, KSUB, NK, LANES)


def _state_from_scan(s):
    nd, nb = s.shape[:2]
    s = s.reshape(nd, nb, KSUB, NK, NK // KSUB, RWKV_HEADS).transpose(0, 1, 5, 3, 2, 4)
    return s.reshape(nd, nb, RWKV_HEADS, NK, NK)


def _rope_tables():
    rows = DEC_SEQ // GRID_W
    row = jnp.repeat(jnp.arange(rows, dtype=F32), GRID_W)
    col = jnp.tile(jnp.arange(GRID_W, dtype=F32), rows)
    n_freq = HEAD_DIM // 4
    inv_freq = ROPE_THETA ** (-jnp.arange(n_freq, dtype=F32) / n_freq)
    ang = jnp.concatenate([row[:, None] * inv_freq, col[:, None] * inv_freq], axis=-1)
    cos, sin = jnp.cos(ang), jnp.sin(ang)
    reps = LANES // HEAD_DIM
    cos_t = jnp.tile(jnp.concatenate([cos, cos], axis=-1), (1, reps))
    sin_t = jnp.tile(jnp.concatenate([-sin, sin], axis=-1), (1, reps))
    ident_c = jnp.ones((QK_TM, LANES), F32)
    ident_s = jnp.zeros((QK_TM, LANES), F32)
    return jnp.concatenate([ident_c, cos_t], axis=0), jnp.concatenate([ident_s, sin_t], axis=0)


def kernel(x_prompt, x_sample, cache_attn_k, cache_attn_v, state_ssd_fwd, state_ssd_bwd, state_rwkv_fwd, state_rwkv_bwd, c, c_ctx, mod_w, mod_b, norm_mix_g, norm_ffn_g, ffn_w_gate, ffn_w_up, ffn_w_down, ab_w_in, ab_w_out, attn_q_g, attn_k_g, ssd_conv_w, ssd_conv_b, ssd_dt_bias, ssd_a_log, ssd_d, ssd_norm_g, rwkv_mu, rwkv_w_r, rwkv_w_k, rwkv_w_v, rwkv_w0, rwkv_w1, rwkv_w2, rwkv_a0, rwkv_a1, rwkv_a2, rwkv_g1, rwkv_g2, rwkv_k_k, rwkv_k_a, rwkv_r_k, rwkv_ln_g, rwkv_ln_b, rwkv_w_o, final_norm_g):
    bf = lambda a: a.astype(BF16)
    x = jnp.concatenate([x_prompt.reshape(R_CTX, D_MODEL), x_sample.reshape(R_SMP, D_MODEL)], axis=0)

    cv = jnp.concatenate([c_ctx[None], c, jnp.zeros((8 - 1 - DEC_BATCH, D_MODEL), F32)], axis=0)
    m = _ada(cv, mod_w, mod_b)
    m = m[:, :1 + DEC_BATCH].reshape(2, 1 + DEC_BATCH, 6, D_MODEL)
    m = jnp.pad(m, ((0, 0), (0, 0), (0, 2), (0, 0)))
    row2 = lambda a: a.reshape(1, -1)

    w_in = jnp.pad(bf(ab_w_in[0]), ((0, 0), (0, AB_PAD - AB_IN_DIM)))
    qkv, z, xbc, dt = _inproj(x, row2(norm_mix_g[0]), m[0], w_in)
    cos_t, sin_t = _rope_tables()
    qg = jnp.tile(attn_q_g[0], ATTN_HEADS).reshape(1, -1)
    kg = jnp.tile(attn_k_g[0], ATTN_KV_HEADS).reshape(1, -1)
    qn, kn = _qkprep(qkv, qg, kg, cos_t, sin_t)
    ck = cache_attn_k[:, 0].reshape(DEC_BATCH, PAST_LEN, ATTN_KV_DIM)
    cvv = cache_attn_v[:, 0].reshape(DEC_BATCH, PAST_LEN, ATTN_KV_DIM)
    attn_c, attn_s = _attention(qn, kn, qkv, ck, cvv)
    attn = jnp.concatenate([attn_c, attn_s], axis=0)

    dtb = jnp.pad(ssd_dt_bias[0].reshape(1, -1), ((0, 0), (0, DT_PAD - 2 * SSD_HEADS)))
    a_row = jnp.pad((-jnp.exp(ssd_a_log[0])).reshape(1, -1), ((0, 0), (0, DT_PAD - 2 * SSD_HEADS)))
    d_row = jnp.repeat(ssd_d[0], SSD_HEAD_DIM).reshape(1, -1)
    ssd_args = (ssd_conv_w[0], row2(ssd_conv_b[0]), dtb, a_row, d_row, row2(ssd_norm_g[0]))
    zero_st = jnp.zeros((BATCH, SSD_HEADS, SSD_HEAD_DIM, SSD_STATE), F32)
    y_c, hf_c, hb_c = _ssd(z, xbc, dt, *ssd_args, zero_st, zero_st, seq=SEQ, nb=BATCH, row0=0)
    y_s, _, _ = _ssd(z, xbc, dt, *ssd_args, state_ssd_fwd[:, 0], state_ssd_bwd[:, 0],
                     seq=DEC_SEQ, nb=DEC_BATCH, row0=R_CTX)
    y_ssd = jnp.concatenate([y_c, y_s], axis=0)
    x = _mixres(x, m[0], attn, y_ssd, bf(ab_w_out[0]))
    x = _ffn(x, row2(norm_ffn_g[0]), m[0], bf(ffn_w_gate[0]), bf(ffn_w_up[0]), bf(ffn_w_down[0]),
             row2(final_norm_g), final=False)

    perm = _head_minor_perm()
    pc = lambda w: w[..., perm]
    w1 = bf(jnp.concatenate([rwkv_w1[0, 0], rwkv_w1[0, 1]], axis=1))
    a1 = bf(jnp.concatenate([rwkv_a1[0, 0], rwkv_a1[0, 1]], axis=1))
    zpad = lambda w: bf(jnp.stack([jnp.concatenate([w[0], jnp.zeros_like(w[1])], axis=0),
                                   jnp.concatenate([jnp.zeros_like(w[0]), w[1]], axis=0)]))
    pre = _rwkv_pre(x, row2(norm_mix_g[1]), m[1], rwkv_mu[0], bf(pc(rwkv_w_r[0])), bf(pc(rwkv_w_k[0])),
                    bf(pc(rwkv_w_v[0])), w1, zpad(pc(rwkv_w2[0])), a1, zpad(pc(rwkv_a2[0])), bf(rwkv_g1[0]),
                    bf(pc(rwkv_g2[0])), pc(rwkv_w0[0]), pc(rwkv_a0[0]), row2(pc(rwkv_k_k[0])),
                    row2(pc(rwkv_k_a[0])), row2(pc(rwkv_r_k[0].reshape(-1))))
    rk, v, wka, gg, bon = pre
    vrep = jnp.broadcast_to(v.reshape(R_ALL, NK, 1, RWKV_HEADS), (R_ALL, NK, LANES // RWKV_HEADS, RWKV_HEADS))
    vrep = vrep.reshape(R_ALL, NK, LANES)
    zero_s = jnp.zeros((2, BATCH, KSUB, NK, LANES), F32)
    ycf, ycb, st_c = _scan(rk, vrep, wka, zero_s, seq=SEQ, nb=BATCH, row0=0, nbb=SCAN_NB_CTX, tt=SCAN_TT)
    s0 = _state_to_scan(jnp.stack([state_rwkv_fwd[:, 0], state_rwkv_bwd[:, 0]]))
    ysf, ysb, _ = _scan(rk, vrep, wka, s0, seq=DEC_SEQ, nb=DEC_BATCH, row0=R_CTX, nbb=SCAN_NB_SMP, tt=SCAN_TT)
    st_c = _state_from_scan(st_c)
    x = _rwkv_post(x, m[1], ycf, ycb, ysf, ysb, bon, gg, row2(pc(rwkv_ln_g[0])), row2(pc(rwkv_ln_b[0])),
                   bf(rwkv_w_o[0][perm, :]))
    x = _ffn(x, row2(norm_ffn_g[1]), m[1], bf(ffn_w_gate[1]), bf(ffn_w_up[1]), bf(ffn_w_down[1]),
             row2(final_norm_g), final=True)

    y_prompt = x[:R_CTX].reshape(BATCH, SEQ, D_MODEL)
    y_sample = x[R_CTX:].reshape(DEC_BATCH, DEC_SEQ, D_MODEL)
    new_k = kn[:R_CTX].reshape(BATCH, 1, SEQ, ATTN_KV_HEADS, HEAD_DIM)
    new_v = qkv[:R_CTX, ATTN_Q_DIM + ATTN_KV_DIM:].reshape(BATCH, 1, SEQ, ATTN_KV_HEADS, HEAD_DIM)
    return (y_prompt, y_sample, new_k, new_v, hf_c[:, None], hb_c[:, None], st_c[0][:, None], st_c[1][:, None])
```

```python
import functools

import jax
import jax.numpy as jnp
from jax import lax
from jax.experimental import pallas as pl
from jax.experimental.pallas import tpu as pltpu

F32 = jnp.float32
BF16 = jnp.bfloat16

D_MODEL = 1024
BATCH = 16
SEQ = 256
DEC_BATCH = 2
DEC_SEQ = 1024
PAST_LEN = 256
GRID_W = 64
ATTN_HEADS = 8
ATTN_KV_HEADS = 2
HEAD_DIM = 64
ROPE_THETA = 10000.0
ATTN_Q_DIM = ATTN_HEADS * HEAD_DIM
ATTN_KV_DIM = ATTN_KV_HEADS * HEAD_DIM
SSD_HEADS = 8
SSD_HEAD_DIM = 64
SSD_D_INNER = SSD_HEADS * SSD_HEAD_DIM
SSD_GROUPS = 2
SSD_STATE = 64
SSD_CONV_K = 5
SSD_CHUNK = 128
SSD_CONV_DIM = SSD_D_INNER + 2 * SSD_GROUPS * SSD_STATE
AB_IN_DIM = ATTN_Q_DIM + 2 * ATTN_KV_DIM + SSD_D_INNER + SSD_CONV_DIM + 2 * SSD_HEADS
RWKV_HEAD_DIM = 64
RWKV_HEADS = D_MODEL // RWKV_HEAD_DIM
FFN_DIM = (((8 * D_MODEL + 2) // 3 + 255) // 256) * 256
RMS_EPS = 1e-6
GN_EPS = 64e-5
L2_EPS = 1e-12

R_CTX = BATCH * SEQ
R_SMP = DEC_BATCH * DEC_SEQ
R_ALL = R_CTX + R_SMP
LANES = 128
QKV_DIM = ATTN_Q_DIM + 2 * ATTN_KV_DIM
DT_PAD = LANES
AB_PAD = QKV_DIM + SSD_D_INNER + SSD_CONV_DIM + DT_PAD
VMEM_LIMIT = 56 * 1024 * 1024


def _cparams(sem):
    return pltpu.CompilerParams(dimension_semantics=sem, vmem_limit_bytes=VMEM_LIMIT)


def _mod_index(i, tm):
    n_ctx = R_CTX // tm
    per = DEC_SEQ // tm
    return jnp.where(i < n_ctx, 0, 1 + jnp.maximum(i - n_ctx, 0) // per)


def _sigmoid(x):
    return 1.0 / (1.0 + jnp.exp(-x))


def _silu(x):
    return x * _sigmoid(x)


def _softplus(x):
    return jnp.maximum(x, 0.0) + jnp.log1p(jnp.exp(-jnp.abs(x)))


def _rms(x, g):
    return x * lax.rsqrt(jnp.mean(x * x, axis=-1, keepdims=True) + RMS_EPS) * g


def _group_allsum(x, group):
    n = x.shape[-1]
    ax = x.ndim - 1
    lane = lax.broadcasted_iota(jnp.int32, x.shape, ax)
    s = 1
    while s < group:
        up = pltpu.roll(x, n - s, axis=ax)
        dn = pltpu.roll(x, s, axis=ax)
        x = x + jnp.where((lane & s) == 0, up, dn)
        s *= 2
    return x


def _lane_group_allsum(s, flat=False):
    ax = s.ndim - 1
    if flat:
        parts = [s] + [pltpu.roll(s, g * RWKV_HEADS, axis=ax) for g in range(1, LANES // RWKV_HEADS)]
        while len(parts) > 1:
            parts = [parts[i] + parts[i + 1] for i in range(0, len(parts), 2)]
        return parts[0]
    for sh in (LANES // 2, LANES // 4, LANES // 8):
        s = s + pltpu.roll(s, sh, axis=ax)
    return s


def _tile_lanes(s, n):
    return jnp.concatenate([s] * n, axis=s.ndim - 1)


def _head_allsum(x):
    s = x[:, 0:LANES]
    for j in range(1, x.shape[-1] // LANES):
        s = s + x[:, j * LANES:(j + 1) * LANES]
    return _lane_group_allsum(s)


def _dot(a, b):
    return jnp.dot(a, b, preferred_element_type=F32)


ADA_TN = 1536


def _ada_kernel(c_ref, w_ref, b_ref, o_ref):
    s = _silu(c_ref[...]).astype(BF16)
    o_ref[0] = _dot(s, w_ref[0].astype(BF16)) + b_ref[0]


def _ada(cv, mod_w, mod_b):
    depth = mod_w.shape[0]
    n = mod_w.shape[2]
    return pl.pallas_call(
        _ada_kernel,
        out_shape=jax.ShapeDtypeStruct((depth, 8, n), F32),
        grid=(depth, n // ADA_TN),
        in_specs=[
            pl.BlockSpec((8, D_MODEL), lambda l, j: (0, 0)),
            pl.BlockSpec((1, D_MODEL, ADA_TN), lambda l, j: (l, 0, j)),
            pl.BlockSpec((1, 1, ADA_TN), lambda l, j: (l, 0, j)),
        ],
        out_specs=pl.BlockSpec((1, 8, ADA_TN), lambda l, j: (l, 0, j)),
        compiler_params=_cparams(("arbitrary", "arbitrary")),
        name="ada",
    )(cv, mod_w, mod_b.reshape(depth, 1, n))


INPROJ_TM = 512


def _inproj_kernel(x_ref, g_ref, mod_ref, w_ref, qkv_ref, z_ref, xbc_ref, dt_ref):
    m = mod_ref[0]
    h = _rms(x_ref[...], g_ref[...]) * (1 + m[1:2]) + m[0:1]
    p = _dot(h.astype(BF16), w_ref[...])
    qkv_ref[...] = p[:, 0:QKV_DIM]
    z_ref[...] = p[:, QKV_DIM:QKV_DIM + SSD_D_INNER]
    xbc_ref[...] = p[:, QKV_DIM + SSD_D_INNER:QKV_DIM + SSD_D_INNER + SSD_CONV_DIM]
    dt_ref[...] = p[:, QKV_DIM + SSD_D_INNER + SSD_CONV_DIM:AB_PAD]


def _inproj(x, g, mod, w_pad):
    tm = INPROJ_TM
    row = lambda i: (i, 0)
    return pl.pallas_call(
        _inproj_kernel,
        out_shape=(
            jax.ShapeDtypeStruct((R_ALL, QKV_DIM), F32),
            jax.ShapeDtypeStruct((R_ALL, SSD_D_INNER), F32),
            jax.ShapeDtypeStruct((R_ALL, SSD_CONV_DIM), F32),
            jax.ShapeDtypeStruct((R_ALL, DT_PAD), F32),
        ),
        grid=(R_ALL // tm,),
        in_specs=[
            pl.BlockSpec((tm, D_MODEL), row),
            pl.BlockSpec((1, D_MODEL), lambda i: (0, 0)),
            pl.BlockSpec((1, 8, D_MODEL), lambda i: (_mod_index(i, tm), 0, 0)),
            pl.BlockSpec((D_MODEL, AB_PAD), lambda i: (0, 0)),
        ],
        out_specs=(
            pl.BlockSpec((tm, QKV_DIM), row),
            pl.BlockSpec((tm, SSD_D_INNER), row),
            pl.BlockSpec((tm, SSD_CONV_DIM), row),
            pl.BlockSpec((tm, DT_PAD), row),
        ),
        compiler_params=_cparams(("arbitrary",)),
        name="inproj",
    )(x, g, mod, w_pad)


QK_TM = 256


def _qkprep_kernel(qkv_ref, qg_ref, kg_ref, cos_ref, sin_ref, qn_ref, kn_ref):
    cos = cos_ref[...]
    sin = sin_ref[...]

    def norm_rope(x, g, reps):
        ms = _group_allsum(x * x, HEAD_DIM) * (1.0 / HEAD_DIM)
        y = x * lax.rsqrt(ms + RMS_EPS) * g
        n = y.shape[-1]
        lane = lax.broadcasted_iota(jnp.int32, y.shape, 1)
        half = HEAD_DIM // 2
        swapped = jnp.where((lane & half) == 0, pltpu.roll(y, n - half, axis=1), pltpu.roll(y, half, axis=1))
        c = jnp.concatenate([cos] * reps, axis=1) if reps > 1 else cos
        s = jnp.concatenate([sin] * reps, axis=1) if reps > 1 else sin
        return y * c + swapped * s

    q = qkv_ref[:, 0:ATTN_Q_DIM]
    k = qkv_ref[:, ATTN_Q_DIM:ATTN_Q_DIM + ATTN_KV_DIM]
    qn_ref[...] = norm_rope(q, qg_ref[...], ATTN_Q_DIM // LANES)
    kn_ref[...] = norm_rope(k, kg_ref[...], 1)


def _qkprep(qkv, qg, kg, cos_t, sin_t):
    tm = QK_TM
    n_ctx = R_CTX // tm
    per = DEC_SEQ // tm
    tab = lambda i: (jnp.where(i < n_ctx, 0, 1 + jnp.maximum(i - n_ctx, 0) % per), 0)
    return pl.pallas_call(
        _qkprep_kernel,
        out_shape=(
            jax.ShapeDtypeStruct((R_ALL, ATTN_Q_DIM), F32),
            jax.ShapeDtypeStruct((R_ALL, ATTN_KV_DIM), F32),
        ),
        grid=(R_ALL // tm,),
        in_specs=[
            pl.BlockSpec((tm, QKV_DIM), lambda i: (i, 0)),
            pl.BlockSpec((1, ATTN_Q_DIM), lambda i: (0, 0)),
            pl.BlockSpec((1, ATTN_KV_DIM), lambda i: (0, 0)),
            pl.BlockSpec((tm, LANES), tab),
            pl.BlockSpec((tm, LANES), tab),
        ],
        out_specs=(
            pl.BlockSpec((tm, ATTN_Q_DIM), lambda i: (i, 0)),
            pl.BlockSpec((tm, ATTN_KV_DIM), lambda i: (i, 0)),
        ),
        compiler_params=_cparams(("arbitrary",)),
        name="qkprep",
    )(qkv, qg, kg, cos_t, sin_t)


def _attn_core(q, ks, vs):
    tq = q.shape[0]
    grp = ATTN_HEADS // ATTN_KV_HEADS
    scale = HEAD_DIM ** -0.5
    outs = []
    for g in range(ATTN_KV_HEADS):
        sl = slice(g * HEAD_DIM, (g + 1) * HEAD_DIM)
        qs = jnp.concatenate(
            [q[:, (g * grp + j) * HEAD_DIM:(g * grp + j + 1) * HEAD_DIM] for j in range(grp)], axis=0).astype(BF16)
        ss = [lax.dot_general(qs, k[:, sl].astype(BF16), (((1,), (1,)), ((), ())),
                              preferred_element_type=F32) * scale for k in ks]
        m = ss[0].max(axis=-1, keepdims=True)
        for s in ss[1:]:
            m = jnp.maximum(m, s.max(axis=-1, keepdims=True))
        ps = [jnp.exp(s - m) for s in ss]
        l = ps[0].sum(axis=-1, keepdims=True)
        for p in ps[1:]:
            l = l + p.sum(axis=-1, keepdims=True)
        inv = 1.0 / l
        o = None
        for p, v in zip(ps, vs):
            t = _dot((p * inv).astype(BF16), v[:, sl].astype(BF16))
            o = t if o is None else o + t
        outs += [o[j * tq:(j + 1) * tq] for j in range(grp)]
    return jnp.concatenate(outs, axis=1)


def _attn_ctx_kernel(q_ref, k_ref, v_ref, o_ref):
    o_ref[...] = _attn_core(q_ref[...], [k_ref[...]], [v_ref[...]]).astype(BF16)


def _attn_smp_kernel(q_ref, k_ref, v_ref, ck_ref, cv_ref, o_ref):
    ks = [ck_ref[0], k_ref[...]]
    vs = [cv_ref[0], v_ref[...]]
    o_ref[...] = _attn_core(q_ref[...], ks, vs).astype(BF16)


ATTN_TQ = 128


def _attention(qn, kn, qkv, cache_k, cache_v):
    v_col = (ATTN_Q_DIM + ATTN_KV_DIM) // ATTN_KV_DIM
    ctx = pl.pallas_call(
        _attn_ctx_kernel,
        out_shape=jax.ShapeDtypeStruct((R_CTX, ATTN_Q_DIM), BF16),
        grid=(BATCH,),
        in_specs=[
            pl.BlockSpec((SEQ, ATTN_Q_DIM), lambda b: (b, 0)),
            pl.BlockSpec((SEQ, ATTN_KV_DIM), lambda b: (b, 0)),
            pl.BlockSpec((SEQ, ATTN_KV_DIM), lambda b: (b, v_col)),
        ],
        out_specs=pl.BlockSpec((SEQ, ATTN_Q_DIM), lambda b: (b, 0)),
        compiler_params=_cparams(("arbitrary",)),
        name="attn_ctx",
    )(qn, kn, qkv)
    nq = DEC_SEQ // ATTN_TQ
    q0 = R_CTX // ATTN_TQ
    s0 = R_CTX // DEC_SEQ
    smp = pl.pallas_call(
        _attn_smp_kernel,
        out_shape=jax.ShapeDtypeStruct((R_SMP, ATTN_Q_DIM), BF16),
        grid=(DEC_BATCH, nq),
        in_specs=[
            pl.BlockSpec((ATTN_TQ, ATTN_Q_DIM), lambda b, i: (q0 + b * nq + i, 0)),
            pl.BlockSpec((DEC_SEQ, ATTN_KV_DIM), lambda b, i: (s0 + b, 0)),
            pl.BlockSpec((DEC_SEQ, ATTN_KV_DIM), lambda b, i: (s0 + b, v_col)),
            pl.BlockSpec((1, PAST_LEN, ATTN_KV_DIM), lambda b, i: (b, 0, 0)),
            pl.BlockSpec((1, PAST_LEN, ATTN_KV_DIM), lambda b, i: (b, 0, 0)),
        ],
        out_specs=pl.BlockSpec((ATTN_TQ, ATTN_Q_DIM), lambda b, i: (b * nq + i, 0)),
        compiler_params=_cparams(("arbitrary", "arbitrary")),
        name="attn_smp",
    )(qn, kn, qkv, cache_k, cache_v)
    return ctx, smp


CONV_HALO = 8


def _cumsum_rows(a, reverse):
    n = a.shape[0]
    row = lax.broadcasted_iota(jnp.int32, a.shape, 0)
    s = 1
    while s < n:
        if reverse:
            a = a + jnp.where(row < n - s, pltpu.roll(a, n - s, axis=0), 0.0)
        else:
            a = a + jnp.where(row >= s, pltpu.roll(a, s, axis=0), 0.0)
        s *= 2
    return a


def _ssd_kernel(z_ref, xbc_ref, dt_ref, cw_ref, cb_ref, dtb_ref, a_ref, d_ref, g_ref, h0f_ref, h0b_ref,
                y_ref, hf_ref, hb_ref, pad_sc, xc_sc, dt_sc, y_sc, h_sc, *, seq):
    L = SSD_CHUNK
    nc = seq // L
    pad = SSD_CONV_K // 2
    zeros = jnp.zeros((CONV_HALO, SSD_CONV_DIM), F32)
    pad_sc[0:CONV_HALO, :] = zeros
    pad_sc[seq + CONV_HALO:seq + 2 * CONV_HALO, :] = zeros
    pad_sc[CONV_HALO:seq + CONV_HALO, :] = xbc_ref[...]
    h_sc[0] = h0f_ref[0]
    h_sc[1] = h0b_ref[0]

    def conv_chunk(c, carry):
        r0 = pl.multiple_of(c * L, L)
        win = pad_sc[pl.ds(r0, L + 2 * CONV_HALO), :]
        acc = cb_ref[...]
        for i in range(SSD_CONV_K):
            acc = acc + win[CONV_HALO - pad + i:CONV_HALO - pad + i + L, :] * cw_ref[i:i + 1, :]
        xc_sc[pl.ds(r0, L), :] = _silu(acc)
        dt_sc[pl.ds(r0, L), :] = _softplus(dt_ref[pl.ds(r0, L), :] + dtb_ref[...])
        return carry

    lax.fori_loop(0, nc, conv_chunk, 0)

    rr = lax.broadcasted_iota(jnp.int32, (L, L), 0)
    cc = lax.broadcasted_iota(jnp.int32, (L, L), 1)
    grp = SSD_HEADS // SSD_GROUPS
    P = SSD_HEAD_DIM
    N = SSD_STATE

    def make_chunk(dirn):
        mask = (rr >= cc) if dirn == 0 else (rr <= cc)

        def chunk(ci, carry):
            c = ci if dirn == 0 else nc - 1 - ci
            r0 = pl.multiple_of(c * L, L)
            xs = xc_sc[pl.ds(r0, L), 0:SSD_D_INNER]
            bm = xc_sc[pl.ds(r0, L), SSD_D_INNER:SSD_D_INNER + SSD_GROUPS * N]
            cm = xc_sc[pl.ds(r0, L), SSD_D_INNER + SSD_GROUPS * N:SSD_CONV_DIM]
            dtc = dt_sc[pl.ds(r0, L), :]
            acs = _cumsum_rows(dtc * a_ref[...], reverse=(dirn == 1))
            acs_t = acs.T
            tot = acs[L - 1:L, :] if dirn == 0 else acs[0:1, :]
            dec_end = jnp.exp(tot - acs)
            eacs = jnp.exp(acs)
            cdec = jnp.exp(tot)
            cb = [lax.dot_general(cm[:, g * N:(g + 1) * N].astype(BF16), bm[:, g * N:(g + 1) * N].astype(BF16),
                                  (((1,), (1,)), ((), ())), preferred_element_type=F32)
                  for g in range(SSD_GROUPS)]
            if dirn == 0:
                y_sc[pl.ds(r0, L), :] = xs * d_ref[...]
            for h in range(SSD_HEADS):
                g = h // grp
                ln = h + SSD_HEADS * dirn
                seg = jnp.exp(jnp.where(mask, acs[:, ln:ln + 1] - acs_t[ln:ln + 1, :], -jnp.inf))
                sc = (cb[g] * seg).astype(BF16)
                xdt = (xs[:, h * P:(h + 1) * P] * dtc[:, ln:ln + 1]).astype(BF16)
                hin = h_sc[dirn, h]
                ce = (cm[:, g * N:(g + 1) * N] * eacs[:, ln:ln + 1]).astype(BF16)
                yh = _dot(sc, xdt) + lax.dot_general(ce, hin.astype(BF16), (((1,), (1,)), ((), ())),
                                                     preferred_element_type=F32)
                bd = (bm[:, g * N:(g + 1) * N] * dec_end[:, ln:ln + 1]).astype(BF16)
                st = lax.dot_general(xdt, bd, (((0,), (0,)), ((), ())), preferred_element_type=F32)
                h_sc[dirn, h] = hin * cdec[:, ln:ln + 1] + st
                y_sc[pl.ds(r0, L), h * P:(h + 1) * P] += yh
            return carry

        return chunk

    lax.fori_loop(0, nc, make_chunk(0), 0)
    lax.fori_loop(0, nc, make_chunk(1), 0)

    def out_chunk(c, carry):
        r0 = pl.multiple_of(c * L, L)
        y = y_sc[pl.ds(r0, L), :] * _silu(z_ref[pl.ds(r0, L), :])
        y_ref[pl.ds(r0, L), :] = _rms(y, g_ref[...]).astype(BF16)
        return carry

    lax.fori_loop(0, nc, out_chunk, 0)
    hf_ref[0] = h_sc[0]
    hb_ref[0] = h_sc[1]


def _ssd(z, xbc, dt, cw, cb, dtb, a_row, d_row, g, h0f, h0b, *, seq, nb, row0):
    blk0 = row0 // seq
    row = lambda b: (blk0 + b, 0)
    fixed = lambda b: (0, 0)
    st = lambda b: (b, 0, 0, 0)
    st_shape = (nb, SSD_HEADS, SSD_HEAD_DIM, SSD_STATE)
    st_blk = (1, SSD_HEADS, SSD_HEAD_DIM, SSD_STATE)
    return pl.pallas_call(
        functools.partial(_ssd_kernel, seq=seq),
        out_shape=(
            jax.ShapeDtypeStruct((nb * seq, SSD_D_INNER), BF16),
            jax.ShapeDtypeStruct(st_shape, F32),
            jax.ShapeDtypeStruct(st_shape, F32),
        ),
        grid=(nb,),
        in_specs=[
            pl.BlockSpec((seq, SSD_D_INNER), row),
            pl.BlockSpec((seq, SSD_CONV_DIM), row),
            pl.BlockSpec((seq, DT_PAD), row),
            pl.BlockSpec((SSD_CONV_K, SSD_CONV_DIM), fixed),
            pl.BlockSpec((1, SSD_CONV_DIM), fixed),
            pl.BlockSpec((1, DT_PAD), fixed),
            pl.BlockSpec((1, DT_PAD), fixed),
            pl.BlockSpec((1, SSD_D_INNER), fixed),
            pl.BlockSpec((1, SSD_D_INNER), fixed),
            pl.BlockSpec(st_blk, st),
            pl.BlockSpec(st_blk, st),
        ],
        out_specs=(
            pl.BlockSpec((seq, SSD_D_INNER), lambda b: (b, 0)),
            pl.BlockSpec(st_blk, st),
            pl.BlockSpec(st_blk, st),
        ),
        scratch_shapes=[
            pltpu.VMEM((seq + 2 * CONV_HALO, SSD_CONV_DIM), F32),
            pltpu.VMEM((seq, SSD_CONV_DIM), F32),
            pltpu.VMEM((seq, DT_PAD), F32),
            pltpu.VMEM((seq, SSD_D_INNER), F32),
            pltpu.VMEM((2, SSD_HEADS, SSD_HEAD_DIM, SSD_STATE), F32),
        ],
        compiler_params=_cparams(("arbitrary",)),
        name=f"ssd_{seq}",
    )(z, xbc, dt, cw, cb, dtb, a_row, d_row, g, h0f, h0b)


RES_TM = 512


def _mixres_kernel(x_ref, mod_ref, a1_ref, a2_ref, w_ref, o_ref):
    k1 = a1_ref.shape[1]
    out = _dot(a1_ref[...], w_ref[0:k1, :]) + _dot(a2_ref[...], w_ref[k1:, :])
    o_ref[...] = x_ref[...] + mod_ref[0][2:3] * out


def _mixres(x, mod, a1, a2, w):
    tm = RES_TM
    row = lambda i: (i, 0)
    return pl.pallas_call(
        _mixres_kernel,
        out_shape=jax.ShapeDtypeStruct((R_ALL, D_MODEL), F32),
        grid=(R_ALL // tm,),
        in_specs=[
            pl.BlockSpec((tm, D_MODEL), row),
            pl.BlockSpec((1, 8, D_MODEL), lambda i: (_mod_index(i, tm), 0, 0)),
            pl.BlockSpec((tm, a1.shape[1]), row),
            pl.BlockSpec((tm, a2.shape[1]), row),
            pl.BlockSpec(w.shape, lambda i: (0, 0)),
        ],
        out_specs=pl.BlockSpec((tm, D_MODEL), row),
        compiler_params=_cparams(("arbitrary",)),
        name="mixres",
    )(x, mod, a1, a2, w)


FFN_TM = 1024
FFN_TF = 256


def _ffn_kernel(x_ref, g_ref, mod_ref, wg_ref, wu_ref, wd_ref, fg_ref, o_ref, h_sc, acc_sc, *, final):
    j = pl.program_id(1)

    @pl.when(j == 0)
    def _():
        m = mod_ref[0]
        h = _rms(x_ref[...], g_ref[...]) * (1 + m[4:5]) + m[3:4]
        h_sc[...] = h.astype(BF16)
        acc_sc[...] = jnp.zeros_like(acc_sc)

    h = h_sc[...]
    hid = _silu(_dot(h, wg_ref[...])) * _dot(h, wu_ref[...])
    acc_sc[...] += _dot(hid.astype(BF16), wd_ref[...])

    @pl.when(j == pl.num_programs(1) - 1)
    def _():
        y = x_ref[...] + mod_ref[0][5:6] * acc_sc[...]
        if final:
            y = _rms(y, fg_ref[...])
        o_ref[...] = y


def _ffn(x, g, mod, wg, wu, wd, fg, *, final):
    tm, tf = FFN_TM, FFN_TF
    row = lambda i, j: (i, 0)
    return pl.pallas_call(
        functools.partial(_ffn_kernel, final=final),
        out_shape=jax.ShapeDtypeStruct((R_ALL, D_MODEL), F32),
        grid=(R_ALL // tm, FFN_DIM // tf),
        in_specs=[
            pl.BlockSpec((tm, D_MODEL), row),
            pl.BlockSpec((1, D_MODEL), lambda i, j: (0, 0)),
            pl.BlockSpec((1, 8, D_MODEL), lambda i, j: (_mod_index(i, tm), 0, 0)),
            pl.BlockSpec((D_MODEL, tf), lambda i, j: (0, j)),
            pl.BlockSpec((D_MODEL, tf), lambda i, j: (0, j)),
            pl.BlockSpec((tf, D_MODEL), lambda i, j: (j, 0)),
            pl.BlockSpec((1, D_MODEL), lambda i, j: (0, 0)),
        ],
        out_specs=pl.BlockSpec((tm, D_MODEL), row),
        scratch_shapes=[pltpu.VMEM((tm, D_MODEL), BF16), pltpu.VMEM((tm, D_MODEL), F32)],
        compiler_params=_cparams(("arbitrary", "arbitrary")),
        name="ffn_final" if final else "ffn",
    )(x, g, mod, wg, wu, wd, fg)


RW_TM = 128
HALO = 8


def _rwkv_pre_kernel(x_ref, xp_ref, xn_ref, g_ref, mod_ref, mu_ref, wr_ref, wk_ref, wv_ref, w1_ref, w2_ref,
                     a1_ref, a2_ref, g1_ref, g2_ref, w0_ref, a0_ref, kk_ref, ka_ref, rk_ref,
                     rk_o, v_o, wka_o, g_o, bon_o):
    i = pl.program_id(0)
    tm = RW_TM
    n_ctx = R_CTX // tm
    per_c = SEQ // tm
    per_s = DEC_SEQ // tm
    rel = jnp.where(i < n_ctx, i % per_c, jnp.maximum(i - n_ctx, 0) % per_s)
    last = jnp.where(i < n_ctx, per_c - 1, per_s - 1)
    m = mod_ref[0]

    def nm(x):
        return _rms(x, g_ref[...]) * (1 + m[1:2]) + m[0:1]

    h = nm(x_ref[...])
    prev_row = jnp.where(rel == 0, 0.0, nm(xp_ref[...])[HALO - 1:HALO, :])
    next_row = jnp.where(rel == last, 0.0, nm(xn_ref[...])[0:1, :])
    row = lax.broadcasted_iota(jnp.int32, h.shape, 0)
    hp = jnp.where(row == 0, prev_row, pltpu.roll(h, 1, axis=0))
    hn = jnp.where(row == tm - 1, next_row, pltpu.roll(h, tm - 1, axis=0))
    dp = hp - h
    dn = hn - h

    def mix(idx):
        return (h + dp * mu_ref[0, idx:idx + 1, :] + dn * mu_ref[1, idx:idx + 1, :]).astype(BF16)

    r = _dot(mix(0), wr_ref[...])
    k = _dot(mix(2), wk_ref[...])
    v = _dot(mix(3), wv_ref[...])
    lw = jnp.tanh(_dot(mix(1), w1_ref[...])).astype(BF16)
    la = _dot(mix(4), a1_ref[...]).astype(BF16)
    gg = _dot(_sigmoid(_dot(mix(5), g1_ref[...])).astype(BF16), g2_ref[...])

    kk = k * kk_ref[...]
    nrep = D_MODEL // LANES
    kk = kk * _tile_lanes(lax.rsqrt(_head_allsum(kk * kk) + L2_EPS), nrep)
    rk_o[0] = r
    rk_o[1] = kk
    v_o[...] = v
    g_o[...] = gg
    bsum = None
    for j in range(2):
        wl = w0_ref[j:j + 1, :] + _dot(lw, w2_ref[j])
        wka_o[j, 0] = jnp.exp(-jnp.exp(-_softplus(-wl) - 0.5))
        a = _sigmoid(a0_ref[j:j + 1, :] + _dot(la, a2_ref[j]))
        kd = k * (1 + (a - 1) * ka_ref[...])
        wka_o[j, 1] = kd
        wka_o[j, 2] = kk * a
        t = r * kd * rk_ref[...]
        bsum = t if bsum is None else bsum + t
    bon_o[...] = _tile_lanes(_head_allsum(bsum), nrep) * v


def _rwkv_pre(x, g, mod, mu, wr, wk, wv, w1, w2, a1, a2, g1, g2, w0, a0, k_k, k_a, r_k):
    tm = RW_TM
    nblk = R_ALL // HALO
    per = tm // HALO
    row = lambda i: (i, 0)
    fixed2 = lambda i: (0, 0)
    fixed3 = lambda i: (0, 0, 0)
    full = lambda a: pl.BlockSpec(a.shape, fixed2 if a.ndim == 2 else fixed3)
    out = jax.ShapeDtypeStruct((R_ALL, D_MODEL), F32)
    orow = pl.BlockSpec((tm, D_MODEL), row)
    return pl.pallas_call(
        _rwkv_pre_kernel,
        out_shape=(jax.ShapeDtypeStruct((2, R_ALL, D_MODEL), F32), out,
                   jax.ShapeDtypeStruct((2, 3, R_ALL, D_MODEL), F32), out, out),
        grid=(R_ALL // tm,),
        in_specs=[
            pl.BlockSpec((tm, D_MODEL), row),
            pl.BlockSpec((HALO, D_MODEL), lambda i: (jnp.maximum(i * per - 1, 0), 0)),
            pl.BlockSpec((HALO, D_MODEL), lambda i: (jnp.minimum((i + 1) * per, nblk - 1), 0)),
            pl.BlockSpec((1, D_MODEL), fixed2),
            pl.BlockSpec((1, 8, D_MODEL), lambda i: (_mod_index(i, tm), 0, 0)),
            full(mu), full(wr), full(wk), full(wv), full(w1), full(w2), full(a1), full(a2), full(g1), full(g2),
            full(w0), full(a0), full(k_k), full(k_a), full(r_k),
        ],
        out_specs=(pl.BlockSpec((2, tm, D_MODEL), lambda i: (0, i, 0)), orow,
                   pl.BlockSpec((2, 3, tm, D_MODEL), lambda i: (0, 0, i, 0)), orow, orow),
        compiler_params=_cparams(("arbitrary",)),
        name="rwkv_pre",
    )(x, x, x, g, mod, mu, wr, wk, wv, w1, w2, a1, a2, g1, g2, w0, a0, k_k, k_a, r_k)


SCAN_TT = 32
NK = RWKV_HEAD_DIM


def _scan_kernel(kt_ref, vt_ref, dk_ref, s0_ref, y_ref, st_ref, s_sc):
    d = pl.program_id(0)
    j = pl.program_id(2)

    @pl.when(j == 0)
    def _():
        s_sc[...] = s0_ref[0]

    def step(i, carry):
        t = jnp.where(d == 0, i, SCAN_TT - 1 - i)
        vv = vt_ref[t]
        sa = jnp.zeros_like(vv)
        for k in range(NK):
            sa = sa + s_sc[k] * kt_ref[1, t, k:k + 1, :]
        y = jnp.zeros_like(vv)
        for k in range(NK):
            sn = (s_sc[k] * dk_ref[0, 0, t, k:k + 1, :] - sa * dk_ref[0, 2, t, k:k + 1, :]
                  + vv * dk_ref[0, 1, t, k:k + 1, :])
            s_sc[k] = sn
            y = y + sn * kt_ref[0, t, k:k + 1, :]
        y_ref[0, t] = y
        return carry

    lax.fori_loop(0, SCAN_TT, step, 0)

    @pl.when(j == pl.num_programs(2) - 1)
    def _():
        st_ref[0] = s_sc[...]


def _scan(kt, vt, dk, s0):
    seq, nv, chains = vt.shape
    tt = SCAN_TT
    nt = seq // tt
    tb = lambda d, j: jnp.where(d == 0, j, nt - 1 - j)
    sblk = pl.BlockSpec((1, NK, nv, LANES), lambda d, c, j: (d, 0, 0, c))
    yblk = pl.BlockSpec((1, tt, nv, LANES), lambda d, c, j: (d, tb(d, j), 0, c))
    return pl.pallas_call(
        _scan_kernel,
        out_shape=(jax.ShapeDtypeStruct((2, seq, nv, chains), F32), jax.ShapeDtypeStruct((2, NK, nv, chains), F32)),
        grid=(2, chains // LANES, nt),
        in_specs=[
            pl.BlockSpec((2, tt, NK, LANES), lambda d, c, j: (0, tb(d, j), 0, c)),
            pl.BlockSpec((tt, nv, LANES), lambda d, c, j: (tb(d, j), 0, c)),
            pl.BlockSpec((1, 3, tt, NK, LANES), lambda d, c, j: (d, 0, tb(d, j), 0, c)),
            sblk,
        ],
        out_specs=(yblk, sblk),
        scratch_shapes=[pltpu.VMEM((NK, nv, LANES), F32)],
        compiler_params=_cparams(("arbitrary", "arbitrary", "arbitrary")),
        name=f"rwkv_scan_{seq}",
    )(kt, vt, dk, s0)


POST_TM = 256


def _rwkv_post_kernel(x_ref, mod_ref, yc_ref, ys_ref, bon_ref, g_ref, lng_ref, lnb_ref, wo_ref, o_ref):
    is_ctx = pl.program_id(0) < R_CTX // POST_TM
    y = jnp.where(is_ctx, yc_ref[0] + yc_ref[1], ys_ref[0] + ys_ref[1])
    inv = 1.0 / RWKV_HEAD_DIM
    nrep = D_MODEL // LANES
    d = y - _tile_lanes(_head_allsum(y) * inv, nrep)
    var = _head_allsum(d * d) * inv
    yn = d * _tile_lanes(lax.rsqrt(var + GN_EPS), nrep) * lng_ref[...] + lnb_ref[...]
    out = ((yn + bon_ref[...]) * g_ref[...]).astype(BF16)
    o_ref[...] = x_ref[...] + mod_ref[0][2:3] * _dot(out, wo_ref[...])


def _rwkv_post(x, mod, yc, ys, bon, g, lng, lnb, wo):
    tm = POST_TM
    n_ctx = R_CTX // tm
    row = lambda i: (i, 0)
    fixed = lambda i: (0, 0)
    big = pl.BlockSpec((tm, D_MODEL), row)
    ycb = pl.BlockSpec((2, tm, D_MODEL), lambda i: (0, jnp.minimum(i, n_ctx - 1), 0))
    ysb = pl.BlockSpec((2, tm, D_MODEL), lambda i: (0, jnp.maximum(i - n_ctx, 0), 0))
    return pl.pallas_call(
        _rwkv_post_kernel,
        out_shape=jax.ShapeDtypeStruct((R_ALL, D_MODEL), F32),
        grid=(R_ALL // tm,),
        in_specs=[
            big,
            pl.BlockSpec((1, 8, D_MODEL), lambda i: (_mod_index(i, tm), 0, 0)),
            ycb, ysb, big, big,
            pl.BlockSpec((1, D_MODEL), fixed),
            pl.BlockSpec((1, D_MODEL), fixed),
            pl.BlockSpec((D_MODEL, D_MODEL), fixed),
        ],
        out_specs=big,
        compiler_params=_cparams(("arbitrary",)),
        name="rwkv_post",
    )(x, mod, yc, ys, bon, g, lng, lnb, wo)


def _scan_pass(rk, v, wka, s0, nb, seq, row0):
    H = RWKV_HEADS
    nch = nb * H
    vq = max(1, LANES // nch)
    nv = NK // vq
    rows = slice(row0, row0 + nb * seq)

    def ktype(a):
        lead = a.shape[:-2]
        n = len(lead)
        a = a.reshape(lead + (nb, seq, NK, H))
        a = a.transpose(tuple(range(n)) + (n + 1, n + 2, n, n + 3)).reshape(lead + (seq, NK, 1, nch))
        return jnp.broadcast_to(a, lead + (seq, NK, vq, nch)).reshape(lead + (seq, NK, vq * nch))

    kt = ktype(rk[:, rows])
    dk = ktype(wka[:, :, rows])
    vt = v[rows].reshape(nb, seq, vq, nv, H).transpose(1, 3, 2, 0, 4).reshape(seq, nv, vq * nch)
    if s0 is None:
        s0 = jnp.zeros((2, NK, nv, vq * nch), F32)
    else:
        s0 = s0.reshape(2, nb, H, vq, nv, NK).transpose(0, 5, 4, 3, 1, 2).reshape(2, NK, nv, vq * nch)
    y, st = _scan(kt, vt, dk, s0)
    y = y.reshape(2, seq, nv, vq, nb, H).transpose(0, 4, 1, 3, 2, 5).reshape(2, nb * seq, D_MODEL)
    st = st.reshape(2, NK, nv, vq, nb, H).transpose(0, 4, 5, 3, 2, 1).reshape(2, nb, H, NK, NK)
    return y, st


def _rope_tables():
    rows = DEC_SEQ // GRID_W
    row = jnp.repeat(jnp.arange(rows, dtype=F32), GRID_W)
    col = jnp.tile(jnp.arange(GRID_W, dtype=F32), rows)
    n_freq = HEAD_DIM // 4
    inv_freq = ROPE_THETA ** (-jnp.arange(n_freq, dtype=F32) / n_freq)
    ang = jnp.concatenate([row[:, None] * inv_freq, col[:, None] * inv_freq], axis=-1)
    cos, sin = jnp.cos(ang), jnp.sin(ang)
    reps = LANES // HEAD_DIM
    cos_t = jnp.tile(jnp.concatenate([cos, cos], axis=-1), (1, reps))
    sin_t = jnp.tile(jnp.concatenate([-sin, sin], axis=-1), (1, reps))
    ident_c = jnp.ones((QK_TM, LANES), F32)
    ident_s = jnp.zeros((QK_TM, LANES), F32)
    return jnp.concatenate([ident_c, cos_t], axis=0), jnp.concatenate([ident_s, sin_t], axis=0)


def kernel(x_prompt, x_sample, cache_attn_k, cache_attn_v, state_ssd_fwd, state_ssd_bwd, state_rwkv_fwd, state_rwkv_bwd, c, c_ctx, mod_w, mod_b, norm_mix_g, norm_ffn_g, ffn_w_gate, ffn_w_up, ffn_w_down, ab_w_in, ab_w_out, attn_q_g, attn_k_g, ssd_conv_w, ssd_conv_b, ssd_dt_bias, ssd_a_log, ssd_d, ssd_norm_g, rwkv_mu, rwkv_w_r, rwkv_w_k, rwkv_w_v, rwkv_w0, rwkv_w1, rwkv_w2, rwkv_a0, rwkv_a1, rwkv_a2, rwkv_g1, rwkv_g2, rwkv_k_k, rwkv_k_a, rwkv_r_k, rwkv_ln_g, rwkv_ln_b, rwkv_w_o, final_norm_g):
    bf = lambda a: a.astype(BF16)
    x = jnp.concatenate([x_prompt.reshape(R_CTX, D_MODEL), x_sample.reshape(R_SMP, D_MODEL)], axis=0)

    cv = jnp.concatenate([c_ctx[None], c, jnp.zeros((8 - 1 - DEC_BATCH, D_MODEL), F32)], axis=0)
    m = _ada(cv, mod_w, mod_b)
    m = m[:, :1 + DEC_BATCH].reshape(2, 1 + DEC_BATCH, 6, D_MODEL)
    m = jnp.pad(m, ((0, 0), (0, 0), (0, 2), (0, 0)))
    row2 = lambda a: a.reshape(1, -1)

    w_in = jnp.pad(bf(ab_w_in[0]), ((0, 0), (0, AB_PAD - AB_IN_DIM)))
    qkv, z, xbc, dt = _inproj(x, row2(norm_mix_g[0]), m[0], w_in)
    cos_t, sin_t = _rope_tables()
    qg = jnp.tile(attn_q_g[0], ATTN_HEADS).reshape(1, -1)
    kg = jnp.tile(attn_k_g[0], ATTN_KV_HEADS).reshape(1, -1)
    qn, kn = _qkprep(qkv, qg, kg, cos_t, sin_t)
    ck = cache_attn_k[:, 0].reshape(DEC_BATCH, PAST_LEN, ATTN_KV_DIM)
    cvv = cache_attn_v[:, 0].reshape(DEC_BATCH, PAST_LEN, ATTN_KV_DIM)
    attn_c, attn_s = _attention(qn, kn, qkv, ck, cvv)
    attn = jnp.concatenate([attn_c, attn_s], axis=0)

    dtb = jnp.pad(ssd_dt_bias[0].reshape(1, -1), ((0, 0), (0, DT_PAD - 2 * SSD_HEADS)))
    a_row = jnp.pad((-jnp.exp(ssd_a_log[0])).reshape(1, -1), ((0, 0), (0, DT_PAD - 2 * SSD_HEADS)))
    d_row = jnp.repeat(ssd_d[0], SSD_HEAD_DIM).reshape(1, -1)
    ssd_args = (ssd_conv_w[0], row2(ssd_conv_b[0]), dtb, a_row, d_row, row2(ssd_norm_g[0]))
    zero_st = jnp.zeros((BATCH, SSD_HEADS, SSD_HEAD_DIM, SSD_STATE), F32)
    y_c, hf_c, hb_c = _ssd(z, xbc, dt, *ssd_args, zero_st, zero_st, seq=SEQ, nb=BATCH, row0=0)
    y_s, _, _ = _ssd(z, xbc, dt, *ssd_args, state_ssd_fwd[:, 0], state_ssd_bwd[:, 0],
                     seq=DEC_SEQ, nb=DEC_BATCH, row0=R_CTX)
    y_ssd = jnp.concatenate([y_c, y_s], axis=0)
    x = _mixres(x, m[0], attn, y_ssd, bf(ab_w_out[0]))
    x = _ffn(x, row2(norm_ffn_g[0]), m[0], bf(ffn_w_gate[0]), bf(ffn_w_up[0]), bf(ffn_w_down[0]),
             row2(final_norm_g), final=False)

    pc = lambda w: w.reshape(w.shape[:-1] + (RWKV_HEADS, NK)).swapaxes(-1, -2).reshape(w.shape)
    w1 = bf(jnp.concatenate([rwkv_w1[0, 0], rwkv_w1[0, 1]], axis=1))
    a1 = bf(jnp.concatenate([rwkv_a1[0, 0], rwkv_a1[0, 1]], axis=1))
    zpad = lambda w: bf(jnp.stack([jnp.concatenate([w[0], jnp.zeros_like(w[1])], axis=0),
                                   jnp.concatenate([jnp.zeros_like(w[0]), w[1]], axis=0)]))
    pre = _rwkv_pre(x, row2(norm_mix_g[1]), m[1], rwkv_mu[0], bf(pc(rwkv_w_r[0])), bf(pc(rwkv_w_k[0])),
                    bf(pc(rwkv_w_v[0])), w1, zpad(pc(rwkv_w2[0])), a1, zpad(pc(rwkv_a2[0])), bf(rwkv_g1[0]),
                    bf(pc(rwkv_g2[0])), pc(rwkv_w0[0]), pc(rwkv_a0[0]), row2(pc(rwkv_k_k[0])),
                    row2(pc(rwkv_k_a[0])), row2(pc(rwkv_r_k[0].reshape(-1))))
    rk, v, wka, gg, bon = pre
    y_c, st_c = _scan_pass(rk, v, wka, None, BATCH, SEQ, 0)
    s0 = jnp.stack([state_rwkv_fwd[:, 0], state_rwkv_bwd[:, 0]])
    y_s, _ = _scan_pass(rk, v, wka, s0, DEC_BATCH, DEC_SEQ, R_CTX)
    x = _rwkv_post(x, m[1], y_c, y_s, bon, gg, row2(pc(rwkv_ln_g[0])), row2(pc(rwkv_ln_b[0])),
                   bf(rwkv_w_o[0].reshape(RWKV_HEADS, NK, D_MODEL).swapaxes(0, 1).reshape(D_MODEL, D_MODEL)))
    x = _ffn(x, row2(norm_ffn_g[1]), m[1], bf(ffn_w_gate[1]), bf(ffn_w_up[1]), bf(ffn_w_down[1]),
             row2(final_norm_g), final=True)

    y_prompt = x[:R_CTX].reshape(BATCH, SEQ, D_MODEL)
    y_sample = x[R_CTX:].reshape(DEC_BATCH, DEC_SEQ, D_MODEL)
    new_k = kn[:R_CTX].reshape(BATCH, 1, SEQ, ATTN_KV_HEADS, HEAD_DIM)
    new_v = qkv[:R_CTX, ATTN_Q_DIM + ATTN_KV_DIM:].reshape(BATCH, 1, SEQ, ATTN_KV_HEADS, HEAD_DIM)
    return (y_prompt, y_sample, new_k, new_v, hf_c[:, None], hb_c[:, None], st_c[0][:, None], st_c[1][:, None])
```

```python
import functools

import jax
import jax.numpy as jnp
from jax import lax
from jax.experimental import pallas as pl
from jax.experimental.pallas import tpu as pltpu

F32 = jnp.float32
BF16 = jnp.bfloat16

D_MODEL = 1024
BATCH = 16
SEQ = 256
DEC_BATCH = 2
DEC_SEQ = 1024
PAST_LEN = 256
GRID_W = 64
ATTN_HEADS = 8
ATTN_KV_HEADS = 2
HEAD_DIM = 64
ROPE_THETA = 10000.0
ATTN_Q_DIM = ATTN_HEADS * HEAD_DIM
ATTN_KV_DIM = ATTN_KV_HEADS * HEAD_DIM
SSD_HEADS = 8
SSD_HEAD_DIM = 64
SSD_D_INNER = SSD_HEADS * SSD_HEAD_DIM
SSD_GROUPS = 2
SSD_STATE = 64
SSD_CONV_K = 5
SSD_CHUNK = 128
SSD_CONV_DIM = SSD_D_INNER + 2 * SSD_GROUPS * SSD_STATE
AB_IN_DIM = ATTN_Q_DIM + 2 * ATTN_KV_DIM + SSD_D_INNER + SSD_CONV_DIM + 2 * SSD_HEADS
RWKV_HEAD_DIM = 64
RWKV_HEADS = D_MODEL // RWKV_HEAD_DIM
FFN_DIM = (((8 * D_MODEL + 2) // 3 + 255) // 256) * 256
RMS_EPS = 1e-6
GN_EPS = 64e-5
L2_EPS = 1e-12

R_CTX = BATCH * SEQ
R_SMP = DEC_BATCH * DEC_SEQ
R_ALL = R_CTX + R_SMP
LANES = 128
QKV_DIM = ATTN_Q_DIM + 2 * ATTN_KV_DIM
DT_PAD = LANES
AB_PAD = QKV_DIM + SSD_D_INNER + SSD_CONV_DIM + DT_PAD
VMEM_LIMIT = 56 * 1024 * 1024


def _cparams(sem):
    return pltpu.CompilerParams(dimension_semantics=sem, vmem_limit_bytes=VMEM_LIMIT)


def _mod_index(i, tm):
    n_ctx = R_CTX // tm
    per = DEC_SEQ // tm
    return jnp.where(i < n_ctx, 0, 1 + jnp.maximum(i - n_ctx, 0) // per)


def _sigmoid(x):
    return 1.0 / (1.0 + jnp.exp(-x))


def _silu(x):
    return x * _sigmoid(x)


def _softplus(x):
    return jnp.maximum(x, 0.0) + jnp.log1p(jnp.exp(-jnp.abs(x)))


def _rms(x, g):
    return x * lax.rsqrt(jnp.mean(x * x, axis=-1, keepdims=True) + RMS_EPS) * g


def _group_allsum(x, group):
    n = x.shape[-1]
    ax = x.ndim - 1
    lane = lax.broadcasted_iota(jnp.int32, x.shape, ax)
    s = 1
    while s < group:
        up = pltpu.roll(x, n - s, axis=ax)
        dn = pltpu.roll(x, s, axis=ax)
        x = x + jnp.where((lane & s) == 0, up, dn)
        s *= 2
    return x


def _lane_group_allsum(s, flat=False):
    ax = s.ndim - 1
    if flat:
        parts = [s] + [pltpu.roll(s, g * RWKV_HEADS, axis=ax) for g in range(1, LANES // RWKV_HEADS)]
        while len(parts) > 1:
            parts = [parts[i] + parts[i + 1] for i in range(0, len(parts), 2)]
        return parts[0]
    for sh in (LANES // 2, LANES // 4, LANES // 8):
        s = s + pltpu.roll(s, sh, axis=ax)
    return s


def _tile_lanes(s, n):
    return jnp.concatenate([s] * n, axis=s.ndim - 1)


def _head_allsum(x):
    s = x[:, 0:LANES]
    for j in range(1, x.shape[-1] // LANES):
        s = s + x[:, j * LANES:(j + 1) * LANES]
    return _lane_group_allsum(s)


def _dot(a, b):
    return jnp.dot(a, b, preferred_element_type=F32)


ADA_TN = 1536


def _ada_kernel(c_ref, w_ref, b_ref, o_ref):
    s = _silu(c_ref[...]).astype(BF16)
    o_ref[0] = _dot(s, w_ref[0].astype(BF16)) + b_ref[0]


def _ada(cv, mod_w, mod_b):
    depth = mod_w.shape[0]
    n = mod_w.shape[2]
    return pl.pallas_call(
        _ada_kernel,
        out_shape=jax.ShapeDtypeStruct((depth, 8, n), F32),
        grid=(depth, n // ADA_TN),
        in_specs=[
            pl.BlockSpec((8, D_MODEL), lambda l, j: (0, 0)),
            pl.BlockSpec((1, D_MODEL, ADA_TN), lambda l, j: (l, 0, j)),
            pl.BlockSpec((1, 1, ADA_TN), lambda l, j: (l, 0, j)),
        ],
        out_specs=pl.BlockSpec((1, 8, ADA_TN), lambda l, j: (l, 0, j)),
        compiler_params=_cparams(("arbitrary", "arbitrary")),
        name="ada",
    )(cv, mod_w, mod_b.reshape(depth, 1, n))


INPROJ_TM = 512


def _inproj_kernel(x_ref, g_ref, mod_ref, w_ref, qkv_ref, z_ref, xbc_ref, dt_ref):
    m = mod_ref[0]
    h = _rms(x_ref[...], g_ref[...]) * (1 + m[1:2]) + m[0:1]
    p = _dot(h.astype(BF16), w_ref[...])
    qkv_ref[...] = p[:, 0:QKV_DIM]
    z_ref[...] = p[:, QKV_DIM:QKV_DIM + SSD_D_INNER]
    xbc_ref[...] = p[:, QKV_DIM + SSD_D_INNER:QKV_DIM + SSD_D_INNER + SSD_CONV_DIM]
    dt_ref[...] = p[:, QKV_DIM + SSD_D_INNER + SSD_CONV_DIM:AB_PAD]


def _inproj(x, g, mod, w_pad):
    tm = INPROJ_TM
    row = lambda i: (i, 0)
    return pl.pallas_call(
        _inproj_kernel,
        out_shape=(
            jax.ShapeDtypeStruct((R_ALL, QKV_DIM), F32),
            jax.ShapeDtypeStruct((R_ALL, SSD_D_INNER), F32),
            jax.ShapeDtypeStruct((R_ALL, SSD_CONV_DIM), F32),
            jax.ShapeDtypeStruct((R_ALL, DT_PAD), F32),
        ),
        grid=(R_ALL // tm,),
        in_specs=[
            pl.BlockSpec((tm, D_MODEL), row),
            pl.BlockSpec((1, D_MODEL), lambda i: (0, 0)),
            pl.BlockSpec((1, 8, D_MODEL), lambda i: (_mod_index(i, tm), 0, 0)),
            pl.BlockSpec((D_MODEL, AB_PAD), lambda i: (0, 0)),
        ],
        out_specs=(
            pl.BlockSpec((tm, QKV_DIM), row),
            pl.BlockSpec((tm, SSD_D_INNER), row),
            pl.BlockSpec((tm, SSD_CONV_DIM), row),
            pl.BlockSpec((tm, DT_PAD), row),
        ),
        compiler_params=_cparams(("arbitrary",)),
        name="inproj",
    )(x, g, mod, w_pad)


QK_TM = 256


def _qkprep_kernel(qkv_ref, qg_ref, kg_ref, cos_ref, sin_ref, qn_ref, kn_ref):
    cos = cos_ref[...]
    sin = sin_ref[...]

    def norm_rope(x, g, reps):
        ms = _group_allsum(x * x, HEAD_DIM) * (1.0 / HEAD_DIM)
        y = x * lax.rsqrt(ms + RMS_EPS) * g
        n = y.shape[-1]
        lane = lax.broadcasted_iota(jnp.int32, y.shape, 1)
        half = HEAD_DIM // 2
        swapped = jnp.where((lane & half) == 0, pltpu.roll(y, n - half, axis=1), pltpu.roll(y, half, axis=1))
        c = jnp.concatenate([cos] * reps, axis=1) if reps > 1 else cos
        s = jnp.concatenate([sin] * reps, axis=1) if reps > 1 else sin
        return y * c + swapped * s

    q = qkv_ref[:, 0:ATTN_Q_DIM]
    k = qkv_ref[:, ATTN_Q_DIM:ATTN_Q_DIM + ATTN_KV_DIM]
    qn_ref[...] = norm_rope(q, qg_ref[...], ATTN_Q_DIM // LANES)
    kn_ref[...] = norm_rope(k, kg_ref[...], 1)


def _qkprep(qkv, qg, kg, cos_t, sin_t):
    tm = QK_TM
    n_ctx = R_CTX // tm
    per = DEC_SEQ // tm
    tab = lambda i: (jnp.where(i < n_ctx, 0, 1 + jnp.maximum(i - n_ctx, 0) % per), 0)
    return pl.pallas_call(
        _qkprep_kernel,
        out_shape=(
            jax.ShapeDtypeStruct((R_ALL, ATTN_Q_DIM), F32),
            jax.ShapeDtypeStruct((R_ALL, ATTN_KV_DIM), F32),
        ),
        grid=(R_ALL // tm,),
        in_specs=[
            pl.BlockSpec((tm, QKV_DIM), lambda i: (i, 0)),
            pl.BlockSpec((1, ATTN_Q_DIM), lambda i: (0, 0)),
            pl.BlockSpec((1, ATTN_KV_DIM), lambda i: (0, 0)),
            pl.BlockSpec((tm, LANES), tab),
            pl.BlockSpec((tm, LANES), tab),
        ],
        out_specs=(
            pl.BlockSpec((tm, ATTN_Q_DIM), lambda i: (i, 0)),
            pl.BlockSpec((tm, ATTN_KV_DIM), lambda i: (i, 0)),
        ),
        compiler_params=_cparams(("arbitrary",)),
        name="qkprep",
    )(qkv, qg, kg, cos_t, sin_t)


def _attn_core(q, ks, vs):
    tq = q.shape[0]
    grp = ATTN_HEADS // ATTN_KV_HEADS
    scale = HEAD_DIM ** -0.5
    outs = []
    for g in range(ATTN_KV_HEADS):
        sl = slice(g * HEAD_DIM, (g + 1) * HEAD_DIM)
        qs = jnp.concatenate(
            [q[:, (g * grp + j) * HEAD_DIM:(g * grp + j + 1) * HEAD_DIM] for j in range(grp)], axis=0).astype(BF16)
        ss = [lax.dot_general(qs, k[:, sl].astype(BF16), (((1,), (1,)), ((), ())),
                              preferred_element_type=F32) * scale for k in ks]
        m = ss[0].max(axis=-1, keepdims=True)
        for s in ss[1:]:
            m = jnp.maximum(m, s.max(axis=-1, keepdims=True))
        ps = [jnp.exp(s - m) for s in ss]
        l = ps[0].sum(axis=-1, keepdims=True)
        for p in ps[1:]:
            l = l + p.sum(axis=-1, keepdims=True)
        inv = 1.0 / l
        o = None
        for p, v in zip(ps, vs):
            t = _dot((p * inv).astype(BF16), v[:, sl].astype(BF16))
            o = t if o is None else o + t
        outs += [o[j * tq:(j + 1) * tq] for j in range(grp)]
    return jnp.concatenate(outs, axis=1)


def _attn_ctx_kernel(q_ref, k_ref, v_ref, o_ref):
    o_ref[...] = _attn_core(q_ref[...], [k_ref[...]], [v_ref[...]]).astype(BF16)


def _attn_smp_kernel(q_ref, k_ref, v_ref, ck_ref, cv_ref, o_ref):
    ks = [ck_ref[0], k_ref[...]]
    vs = [cv_ref[0], v_ref[...]]
    o_ref[...] = _attn_core(q_ref[...], ks, vs).astype(BF16)


ATTN_TQ = 128


def _attention(qn, kn, qkv, cache_k, cache_v):
    v_col = (ATTN_Q_DIM + ATTN_KV_DIM) // ATTN_KV_DIM
    ctx = pl.pallas_call(
        _attn_ctx_kernel,
        out_shape=jax.ShapeDtypeStruct((R_CTX, ATTN_Q_DIM), BF16),
        grid=(BATCH,),
        in_specs=[
            pl.BlockSpec((SEQ, ATTN_Q_DIM), lambda b: (b, 0)),
            pl.BlockSpec((SEQ, ATTN_KV_DIM), lambda b: (b, 0)),
            pl.BlockSpec((SEQ, ATTN_KV_DIM), lambda b: (b, v_col)),
        ],
        out_specs=pl.BlockSpec((SEQ, ATTN_Q_DIM), lambda b: (b, 0)),
        compiler_params=_cparams(("arbitrary",)),
        name="attn_ctx",
    )(qn, kn, qkv)
    nq = DEC_SEQ // ATTN_TQ
    q0 = R_CTX // ATTN_TQ
    s0 = R_CTX // DEC_SEQ
    smp = pl.pallas_call(
        _attn_smp_kernel,
        out_shape=jax.ShapeDtypeStruct((R_SMP, ATTN_Q_DIM), BF16),
        grid=(DEC_BATCH, nq),
        in_specs=[
            pl.BlockSpec((ATTN_TQ, ATTN_Q_DIM), lambda b, i: (q0 + b * nq + i, 0)),
            pl.BlockSpec((DEC_SEQ, ATTN_KV_DIM), lambda b, i: (s0 + b, 0)),
            pl.BlockSpec((DEC_SEQ, ATTN_KV_DIM), lambda b, i: (s0 + b, v_col)),
            pl.BlockSpec((1, PAST_LEN, ATTN_KV_DIM), lambda b, i: (b, 0, 0)),
            pl.BlockSpec((1, PAST_LEN, ATTN_KV_DIM), lambda b, i: (b, 0, 0)),
        ],
        out_specs=pl.BlockSpec((ATTN_TQ, ATTN_Q_DIM), lambda b, i: (b * nq + i, 0)),
        compiler_params=_cparams(("arbitrary", "arbitrary")),
        name="attn_smp",
    )(qn, kn, qkv, cache_k, cache_v)
    return ctx, smp


CONV_HALO = 8


def _cumsum_rows(a, reverse):
    n = a.shape[0]
    row = lax.broadcasted_iota(jnp.int32, a.shape, 0)
    s = 1
    while s < n:
        if reverse:
            a = a + jnp.where(row < n - s, pltpu.roll(a, n - s, axis=0), 0.0)
        else:
            a = a + jnp.where(row >= s, pltpu.roll(a, s, axis=0), 0.0)
        s *= 2
    return a


def _ssd_kernel(z_ref, xbc_ref, dt_ref, cw_ref, cb_ref, dtb_ref, a_ref, d_ref, g_ref, h0f_ref, h0b_ref,
                y_ref, hf_ref, hb_ref, pad_sc, xc_sc, dt_sc, y_sc, h_sc, *, seq):
    L = SSD_CHUNK
    nc = seq // L
    pad = SSD_CONV_K // 2
    zeros = jnp.zeros((CONV_HALO, SSD_CONV_DIM), F32)
    pad_sc[0:CONV_HALO, :] = zeros
    pad_sc[seq + CONV_HALO:seq + 2 * CONV_HALO, :] = zeros
    pad_sc[CONV_HALO:seq + CONV_HALO, :] = xbc_ref[...]
    h_sc[0] = h0f_ref[0]
    h_sc[1] = h0b_ref[0]

    def conv_chunk(c, carry):
        r0 = pl.multiple_of(c * L, L)
        win = pad_sc[pl.ds(r0, L + 2 * CONV_HALO), :]
        acc = cb_ref[...]
        for i in range(SSD_CONV_K):
            acc = acc + win[CONV_HALO - pad + i:CONV_HALO - pad + i + L, :] * cw_ref[i:i + 1, :]
        xc_sc[pl.ds(r0, L), :] = _silu(acc)
        dt_sc[pl.ds(r0, L), :] = _softplus(dt_ref[pl.ds(r0, L), :] + dtb_ref[...])
        return carry

    lax.fori_loop(0, nc, conv_chunk, 0)

    rr = lax.broadcasted_iota(jnp.int32, (L, L), 0)
    cc = lax.broadcasted_iota(jnp.int32, (L, L), 1)
    grp = SSD_HEADS // SSD_GROUPS
    P = SSD_HEAD_DIM
    N = SSD_STATE

    def make_chunk(dirn):
        mask = (rr >= cc) if dirn == 0 else (rr <= cc)

        def chunk(ci, carry):
            c = ci if dirn == 0 else nc - 1 - ci
            r0 = pl.multiple_of(c * L, L)
            xs = xc_sc[pl.ds(r0, L), 0:SSD_D_INNER]
            bm = xc_sc[pl.ds(r0, L), SSD_D_INNER:SSD_D_INNER + SSD_GROUPS * N]
            cm = xc_sc[pl.ds(r0, L), SSD_D_INNER + SSD_GROUPS * N:SSD_CONV_DIM]
            dtc = dt_sc[pl.ds(r0, L), :]
            acs = _cumsum_rows(dtc * a_ref[...], reverse=(dirn == 1))
            acs_t = acs.T
            tot = acs[L - 1:L, :] if dirn == 0 else acs[0:1, :]
            dec_end = jnp.exp(tot - acs)
            eacs = jnp.exp(acs)
            cdec = jnp.exp(tot)
            cb = [lax.dot_general(cm[:, g * N:(g + 1) * N].astype(BF16), bm[:, g * N:(g + 1) * N].astype(BF16),
                                  (((1,), (1,)), ((), ())), preferred_element_type=F32)
                  for g in range(SSD_GROUPS)]
            if dirn == 0:
                y_sc[pl.ds(r0, L), :] = xs * d_ref[...]
            for h in range(SSD_HEADS):
                g = h // grp
                ln = h + SSD_HEADS * dirn
                seg = jnp.exp(jnp.where(mask, acs[:, ln:ln + 1] - acs_t[ln:ln + 1, :], -jnp.inf))
                sc = (cb[g] * seg).astype(BF16)
                xdt = (xs[:, h * P:(h + 1) * P] * dtc[:, ln:ln + 1]).astype(BF16)
                hin = h_sc[dirn, h]
                ce = (cm[:, g * N:(g + 1) * N] * eacs[:, ln:ln + 1]).astype(BF16)
                yh = _dot(sc, xdt) + lax.dot_general(ce, hin.astype(BF16), (((1,), (1,)), ((), ())),
                                                     preferred_element_type=F32)
                bd = (bm[:, g * N:(g + 1) * N] * dec_end[:, ln:ln + 1]).astype(BF16)
                st = lax.dot_general(xdt, bd, (((0,), (0,)), ((), ())), preferred_element_type=F32)
                h_sc[dirn, h] = hin * cdec[:, ln:ln + 1] + st
                y_sc[pl.ds(r0, L), h * P:(h + 1) * P] += yh
            return carry

        return chunk

    lax.fori_loop(0, nc, make_chunk(0), 0)
    lax.fori_loop(0, nc, make_chunk(1), 0)

    def out_chunk(c, carry):
        r0 = pl.multiple_of(c * L, L)
        y = y_sc[pl.ds(r0, L), :] * _silu(z_ref[pl.ds(r0, L), :])
        y_ref[pl.ds(r0, L), :] = _rms(y, g_ref[...]).astype(BF16)
        return carry

    lax.fori_loop(0, nc, out_chunk, 0)
    hf_ref[0] = h_sc[0]
    hb_ref[0] = h_sc[1]


def _ssd(z, xbc, dt, cw, cb, dtb, a_row, d_row, g, h0f, h0b, *, seq, nb, row0):
    blk0 = row0 // seq
    row = lambda b: (blk0 + b, 0)
    fixed = lambda b: (0, 0)
    st = lambda b: (b, 0, 0, 0)
    st_shape = (nb, SSD_HEADS, SSD_HEAD_DIM, SSD_STATE)
    st_blk = (1, SSD_HEADS, SSD_HEAD_DIM, SSD_STATE)
    return pl.pallas_call(
        functools.partial(_ssd_kernel, seq=seq),
        out_shape=(
            jax.ShapeDtypeStruct((nb * seq, SSD_D_INNER), BF16),
            jax.ShapeDtypeStruct(st_shape, F32),
            jax.ShapeDtypeStruct(st_shape, F32),
        ),
        grid=(nb,),
        in_specs=[
            pl.BlockSpec((seq, SSD_D_INNER), row),
            pl.BlockSpec((seq, SSD_CONV_DIM), row),
            pl.BlockSpec((seq, DT_PAD), row),
            pl.BlockSpec((SSD_CONV_K, SSD_CONV_DIM), fixed),
            pl.BlockSpec((1, SSD_CONV_DIM), fixed),
            pl.BlockSpec((1, DT_PAD), fixed),
            pl.BlockSpec((1, DT_PAD), fixed),
            pl.BlockSpec((1, SSD_D_INNER), fixed),
            pl.BlockSpec((1, SSD_D_INNER), fixed),
            pl.BlockSpec(st_blk, st),
            pl.BlockSpec(st_blk, st),
        ],
        out_specs=(
            pl.BlockSpec((seq, SSD_D_INNER), lambda b: (b, 0)),
            pl.BlockSpec(st_blk, st),
            pl.BlockSpec(st_blk, st),
        ),
        scratch_shapes=[
            pltpu.VMEM((seq + 2 * CONV_HALO, SSD_CONV_DIM), F32),
            pltpu.VMEM((seq, SSD_CONV_DIM), F32),
            pltpu.VMEM((seq, DT_PAD), F32),
            pltpu.VMEM((seq, SSD_D_INNER), F32),
            pltpu.VMEM((2, SSD_HEADS, SSD_HEAD_DIM, SSD_STATE), F32),
        ],
        compiler_params=_cparams(("arbitrary",)),
        name=f"ssd_{seq}",
    )(z, xbc, dt, cw, cb, dtb, a_row, d_row, g, h0f, h0b)


RES_TM = 512


def _mixres_kernel(x_ref, mod_ref, a1_ref, a2_ref, w_ref, o_ref):
    k1 = a1_ref.shape[1]
    out = _dot(a1_ref[...], w_ref[0:k1, :]) + _dot(a2_ref[...], w_ref[k1:, :])
    o_ref[...] = x_ref[...] + mod_ref[0][2:3] * out


def _mixres(x, mod, a1, a2, w):
    tm = RES_TM
    row = lambda i: (i, 0)
    return pl.pallas_call(
        _mixres_kernel,
        out_shape=jax.ShapeDtypeStruct((R_ALL, D_MODEL), F32),
        grid=(R_ALL // tm,),
        in_specs=[
            pl.BlockSpec((tm, D_MODEL), row),
            pl.BlockSpec((1, 8, D_MODEL), lambda i: (_mod_index(i, tm), 0, 0)),
            pl.BlockSpec((tm, a1.shape[1]), row),
            pl.BlockSpec((tm, a2.shape[1]), row),
            pl.BlockSpec(w.shape, lambda i: (0, 0)),
        ],
        out_specs=pl.BlockSpec((tm, D_MODEL), row),
        compiler_params=_cparams(("arbitrary",)),
        name="mixres",
    )(x, mod, a1, a2, w)


FFN_TM = 1024
FFN_TF = 256


def _ffn_kernel(x_ref, g_ref, mod_ref, wg_ref, wu_ref, wd_ref, fg_ref, o_ref, h_sc, acc_sc, *, final):
    j = pl.program_id(1)

    @pl.when(j == 0)
    def _():
        m = mod_ref[0]
        h = _rms(x_ref[...], g_ref[...]) * (1 + m[4:5]) + m[3:4]
        h_sc[...] = h.astype(BF16)
        acc_sc[...] = jnp.zeros_like(acc_sc)

    h = h_sc[...]
    hid = _silu(_dot(h, wg_ref[...])) * _dot(h, wu_ref[...])
    acc_sc[...] += _dot(hid.astype(BF16), wd_ref[...])

    @pl.when(j == pl.num_programs(1) - 1)
    def _():
        y = x_ref[...] + mod_ref[0][5:6] * acc_sc[...]
        if final:
            y = _rms(y, fg_ref[...])
        o_ref[...] = y


def _ffn(x, g, mod, wg, wu, wd, fg, *, final):
    tm, tf = FFN_TM, FFN_TF
    row = lambda i, j: (i, 0)
    return pl.pallas_call(
        functools.partial(_ffn_kernel, final=final),
        out_shape=jax.ShapeDtypeStruct((R_ALL, D_MODEL), F32),
        grid=(R_ALL // tm, FFN_DIM // tf),
        in_specs=[
            pl.BlockSpec((tm, D_MODEL), row),
            pl.BlockSpec((1, D_MODEL), lambda i, j: (0, 0)),
            pl.BlockSpec((1, 8, D_MODEL), lambda i, j: (_mod_index(i, tm), 0, 0)),
            pl.BlockSpec((D_MODEL, tf), lambda i, j: (0, j)),
            pl.BlockSpec((D_MODEL, tf), lambda i, j: (0, j)),
            pl.BlockSpec((tf, D_MODEL), lambda i, j: (j, 0)),
            pl.BlockSpec((1, D_MODEL), lambda i, j: (0, 0)),
        ],
        out_specs=pl.BlockSpec((tm, D_MODEL), row),
        scratch_shapes=[pltpu.VMEM((tm, D_MODEL), BF16), pltpu.VMEM((tm, D_MODEL), F32)],
        compiler_params=_cparams(("arbitrary", "arbitrary")),
        name="ffn_final" if final else "ffn",
    )(x, g, mod, wg, wu, wd, fg)


RW_TM = 128
HALO = 8


def _rwkv_pre_kernel(x_ref, xp_ref, xn_ref, g_ref, mod_ref, mu_ref, wr_ref, wk_ref, wv_ref, w1_ref, w2_ref,
                     a1_ref, a2_ref, g1_ref, g2_ref, w0_ref, a0_ref, kk_ref, ka_ref, rk_ref,
                     rk_o, v_o, wka_o, g_o, bon_o):
    i = pl.program_id(0)
    tm = RW_TM
    n_ctx = R_CTX // tm
    per_c = SEQ // tm
    per_s = DEC_SEQ // tm
    rel = jnp.where(i < n_ctx, i % per_c, jnp.maximum(i - n_ctx, 0) % per_s)
    last = jnp.where(i < n_ctx, per_c - 1, per_s - 1)
    m = mod_ref[0]

    def nm(x):
        return _rms(x, g_ref[...]) * (1 + m[1:2]) + m[0:1]

    h = nm(x_ref[...])
    prev_row = jnp.where(rel == 0, 0.0, nm(xp_ref[...])[HALO - 1:HALO, :])
    next_row = jnp.where(rel == last, 0.0, nm(xn_ref[...])[0:1, :])
    row = lax.broadcasted_iota(jnp.int32, h.shape, 0)
    hp = jnp.where(row == 0, prev_row, pltpu.roll(h, 1, axis=0))
    hn = jnp.where(row == tm - 1, next_row, pltpu.roll(h, tm - 1, axis=0))
    dp = hp - h
    dn = hn - h

    def mix(idx):
        return (h + dp * mu_ref[0, idx:idx + 1, :] + dn * mu_ref[1, idx:idx + 1, :]).astype(BF16)

    r = _dot(mix(0), wr_ref[...])
    k = _dot(mix(2), wk_ref[...])
    v = _dot(mix(3), wv_ref[...])
    lw = jnp.tanh(_dot(mix(1), w1_ref[...])).astype(BF16)
    la = _dot(mix(4), a1_ref[...]).astype(BF16)
    gg = _dot(_sigmoid(_dot(mix(5), g1_ref[...])).astype(BF16), g2_ref[...])

    kk = k * kk_ref[...]
    nrep = D_MODEL // LANES
    kk = kk * _tile_lanes(lax.rsqrt(_head_allsum(kk * kk) + L2_EPS), nrep)
    rk_o[0] = r
    rk_o[1] = kk
    v_o[...] = v
    g_o[...] = gg
    bsum = None
    for j in range(2):
        wl = w0_ref[j:j + 1, :] + _dot(lw, w2_ref[j])
        wka_o[j, 0] = jnp.exp(-jnp.exp(-_softplus(-wl) - 0.5))
        a = _sigmoid(a0_ref[j:j + 1, :] + _dot(la, a2_ref[j]))
        kd = k * (1 + (a - 1) * ka_ref[...])
        wka_o[j, 1] = kd
        wka_o[j, 2] = kk * a
        t = r * kd * rk_ref[...]
        bsum = t if bsum is None else bsum + t
    bon_o[...] = _tile_lanes(_head_allsum(bsum), nrep) * v


def _rwkv_pre(x, g, mod, mu, wr, wk, wv, w1, w2, a1, a2, g1, g2, w0, a0, k_k, k_a, r_k):
    tm = RW_TM
    nblk = R_ALL // HALO
    per = tm // HALO
    row = lambda i: (i, 0)
    fixed2 = lambda i: (0, 0)
    fixed3 = lambda i: (0, 0, 0)
    full = lambda a: pl.BlockSpec(a.shape, fixed2 if a.ndim == 2 else fixed3)
    out = jax.ShapeDtypeStruct((R_ALL, D_MODEL), F32)
    orow = pl.BlockSpec((tm, D_MODEL), row)
    return pl.pallas_call(
        _rwkv_pre_kernel,
        out_shape=(jax.ShapeDtypeStruct((2, R_ALL, D_MODEL), F32), out,
                   jax.ShapeDtypeStruct((2, 3, R_ALL, D_MODEL), F32), out, out),
        grid=(R_ALL // tm,),
        in_specs=[
            pl.BlockSpec((tm, D_MODEL), row),
            pl.BlockSpec((HALO, D_MODEL), lambda i: (jnp.maximum(i * per - 1, 0), 0)),
            pl.BlockSpec((HALO, D_MODEL), lambda i: (jnp.minimum((i + 1) * per, nblk - 1), 0)),
            pl.BlockSpec((1, D_MODEL), fixed2),
            pl.BlockSpec((1, 8, D_MODEL), lambda i: (_mod_index(i, tm), 0, 0)),
            full(mu), full(wr), full(wk), full(wv), full(w1), full(w2), full(a1), full(a2), full(g1), full(g2),
            full(w0), full(a0), full(k_k), full(k_a), full(r_k),
        ],
        out_specs=(pl.BlockSpec((2, tm, D_MODEL), lambda i: (0, i, 0)), orow,
                   pl.BlockSpec((2, 3, tm, D_MODEL), lambda i: (0, 0, i, 0)), orow, orow),
        compiler_params=_cparams(("arbitrary",)),
        name="rwkv_pre",
    )(x, x, x, g, mod, mu, wr, wk, wv, w1, w2, a1, a2, g1, g2, w0, a0, k_k, k_a, r_k)


SCAN_TT = 32
SCAN_ACC = 4
NK = RWKV_HEAD_DIM


def _scan_kernel(kt_ref, vt_ref, dk_ref, s0_ref, y_ref, st_ref, s_sc):
    d = pl.program_id(0)
    j = pl.program_id(2)

    @pl.when(j == 0)
    def _():
        s_sc[...] = s0_ref[0]

    def tix(i):
        return jnp.where(d == 0, i, SCAN_TT - 1 - i)

    def project(i):
        t = tix(i)
        parts = [None] * SCAN_ACC
        for k in range(NK):
            term = s_sc[k] * kt_ref[1, t, k:k + 1, :]
            parts[k % SCAN_ACC] = term if parts[k % SCAN_ACC] is None else parts[k % SCAN_ACC] + term
        while len(parts) > 1:
            parts = [parts[a] + parts[a + 1] for a in range(0, len(parts), 2)]
        return parts[0]

    def advance(i, sa, with_next):
        t = tix(i)
        tn = tix(i + 1)
        vv = vt_ref[t]
        y = None
        sa_next = None
        for k in range(NK):
            sn = (s_sc[k] * dk_ref[0, 0, t, k:k + 1, :] - sa * dk_ref[0, 2, t, k:k + 1, :]
                  + vv * dk_ref[0, 1, t, k:k + 1, :])
            s_sc[k] = sn
            yk = sn * kt_ref[0, t, k:k + 1, :]
            y = yk if y is None else y + yk
            if with_next:
                ak = sn * kt_ref[1, tn, k:k + 1, :]
                sa_next = ak if sa_next is None else sa_next + ak
        y_ref[0, t] = y
        return sa_next

    sa_last = lax.fori_loop(0, SCAN_TT - 1, lambda i, sa: advance(i, sa, True), project(0))
    advance(SCAN_TT - 1, sa_last, False)

    @pl.when(j == pl.num_programs(2) - 1)
    def _():
        st_ref[0] = s_sc[...]


def _scan(kt, vt, dk, s0):
    seq, nv, chains = vt.shape
    tt = SCAN_TT
    nt = seq // tt
    tb = lambda d, j: jnp.where(d == 0, j, nt - 1 - j)
    sblk = pl.BlockSpec((1, NK, nv, LANES), lambda d, c, j: (d, 0, 0, c))
    yblk = pl.BlockSpec((1, tt, nv, LANES), lambda d, c, j: (d, tb(d, j), 0, c))
    return pl.pallas_call(
        _scan_kernel,
        out_shape=(jax.ShapeDtypeStruct((2, seq, nv, chains), F32), jax.ShapeDtypeStruct((2, NK, nv, chains), F32)),
        grid=(2, chains // LANES, nt),
        in_specs=[
            pl.BlockSpec((2, tt, NK, LANES), lambda d, c, j: (0, tb(d, j), 0, c)),
            pl.BlockSpec((tt, nv, LANES), lambda d, c, j: (tb(d, j), 0, c)),
            pl.BlockSpec((1, 3, tt, NK, LANES), lambda d, c, j: (d, 0, tb(d, j), 0, c)),
            sblk,
        ],
        out_specs=(yblk, sblk),
        scratch_shapes=[pltpu.VMEM((NK, nv, LANES), F32)],
        compiler_params=_cparams(("arbitrary", "arbitrary", "arbitrary")),
        name=f"rwkv_scan_{seq}",
    )(kt, vt, dk, s0)


POST_TM = 256


def _rwkv_post_kernel(x_ref, mod_ref, yc_ref, ys_ref, bon_ref, g_ref, lng_ref, lnb_ref, wo_ref, o_ref):
    is_ctx = pl.program_id(0) < R_CTX // POST_TM
    y = jnp.where(is_ctx, yc_ref[0] + yc_ref[1], ys_ref[0] + ys_ref[1])
    inv = 1.0 / RWKV_HEAD_DIM
    nrep = D_MODEL // LANES
    d = y - _tile_lanes(_head_allsum(y) * inv, nrep)
    var = _head_allsum(d * d) * inv
    yn = d * _tile_lanes(lax.rsqrt(var + GN_EPS), nrep) * lng_ref[...] + lnb_ref[...]
    out = ((yn + bon_ref[...]) * g_ref[...]).astype(BF16)
    o_ref[...] = x_ref[...] + mod_ref[0][2:3] * _dot(out, wo_ref[...])


def _rwkv_post(x, mod, yc, ys, bon, g, lng, lnb, wo):
    tm = POST_TM
    n_ctx = R_CTX // tm
    row = lambda i: (i, 0)
    fixed = lambda i: (0, 0)
    big = pl.BlockSpec((tm, D_MODEL), row)
    ycb = pl.BlockSpec((2, tm, D_MODEL), lambda i: (0, jnp.minimum(i, n_ctx - 1), 0))
    ysb = pl.BlockSpec((2, tm, D_MODEL), lambda i: (0, jnp.maximum(i - n_ctx, 0), 0))
    return pl.pallas_call(
        _rwkv_post_kernel,
        out_shape=jax.ShapeDtypeStruct((R_ALL, D_MODEL), F32),
        grid=(R_ALL // tm,),
        in_specs=[
            big,
            pl.BlockSpec((1, 8, D_MODEL), lambda i: (_mod_index(i, tm), 0, 0)),
            ycb, ysb, big, big,
            pl.BlockSpec((1, D_MODEL), fixed),
            pl.BlockSpec((1, D_MODEL), fixed),
            pl.BlockSpec((D_MODEL, D_MODEL), fixed),
        ],
        out_specs=big,
        compiler_params=_cparams(("arbitrary",)),
        name="rwkv_post",
    )(x, mod, yc, ys, bon, g, lng, lnb, wo)


def _scan_pass(rk, v, wka, s0, nb, seq, row0):
    H = RWKV_HEADS
    nch = nb * H
    vq = max(1, LANES // nch)
    nv = NK // vq
    rows = slice(row0, row0 + nb * seq)

    def ktype(a):
        lead = a.shape[:-2]
        n = len(lead)
        a = a.reshape(lead + (nb, seq, NK, H))
        a = a.transpose(tuple(range(n)) + (n + 1, n + 2, n, n + 3)).reshape(lead + (seq, NK, 1, nch))
        return jnp.broadcast_to(a, lead + (seq, NK, vq, nch)).reshape(lead + (seq, NK, vq * nch))

    kt = ktype(rk[:, rows])
    dk = ktype(wka[:, :, rows])
    vt = v[rows].reshape(nb, seq, vq, nv, H).transpose(1, 3, 2, 0, 4).reshape(seq, nv, vq * nch)
    if s0 is None:
        s0 = jnp.zeros((2, NK, nv, vq * nch), F32)
    else:
        s0 = s0.reshape(2, nb, H, vq, nv, NK).transpose(0, 5, 4, 3, 1, 2).reshape(2, NK, nv, vq * nch)
    y, st = _scan(kt, vt, dk, s0)
    y = y.reshape(2, seq, nv, vq, nb, H).transpose(0, 4, 1, 3, 2, 5).reshape(2, nb * seq, D_MODEL)
    st = st.reshape(2, NK, nv, vq, nb, H).transpose(0, 4, 5, 3, 2, 1).reshape(2, nb, H, NK, NK)
    return y, st


def _rope_tables():
    rows = DEC_SEQ // GRID_W
    row = jnp.repeat(jnp.arange(rows, dtype=F32), GRID_W)
    col = jnp.tile(jnp.arange(GRID_W, dtype=F32), rows)
    n_freq = HEAD_DIM // 4
    inv_freq = ROPE_THETA ** (-jnp.arange(n_freq, dtype=F32) / n_freq)
    ang = jnp.concatenate([row[:, None] * inv_freq, col[:, None] * inv_freq], axis=-1)
    cos, sin = jnp.cos(ang), jnp.sin(ang)
    reps = LANES // HEAD_DIM
    cos_t = jnp.tile(jnp.concatenate([cos, cos], axis=-1), (1, reps))
    sin_t = jnp.tile(jnp.concatenate([-sin, sin], axis=-1), (1, reps))
    ident_c = jnp.ones((QK_TM, LANES), F32)
    ident_s = jnp.zeros((QK_TM, LANES), F32)
    return jnp.concatenate([ident_c, cos_t], axis=0), jnp.concatenate([ident_s, sin_t], axis=0)


def kernel(x_prompt, x_sample, cache_attn_k, cache_attn_v, state_ssd_fwd, state_ssd_bwd, state_rwkv_fwd, state_rwkv_bwd, c, c_ctx, mod_w, mod_b, norm_mix_g, norm_ffn_g, ffn_w_gate, ffn_w_up, ffn_w_down, ab_w_in, ab_w_out, attn_q_g, attn_k_g, ssd_conv_w, ssd_conv_b, ssd_dt_bias, ssd_a_log, ssd_d, ssd_norm_g, rwkv_mu, rwkv_w_r, rwkv_w_k, rwkv_w_v, rwkv_w0, rwkv_w1, rwkv_w2, rwkv_a0, rwkv_a1, rwkv_a2, rwkv_g1, rwkv_g2, rwkv_k_k, rwkv_k_a, rwkv_r_k, rwkv_ln_g, rwkv_ln_b, rwkv_w_o, final_norm_g):
    bf = lambda a: a.astype(BF16)
    x = jnp.concatenate([x_prompt.reshape(R_CTX, D_MODEL), x_sample.reshape(R_SMP, D_MODEL)], axis=0)

    cv = jnp.concatenate([c_ctx[None], c, jnp.zeros((8 - 1 - DEC_BATCH, D_MODEL), F32)], axis=0)
    m = _ada(cv, mod_w, mod_b)
    m = m[:, :1 + DEC_BATCH].reshape(2, 1 + DEC_BATCH, 6, D_MODEL)
    m = jnp.pad(m, ((0, 0), (0, 0), (0, 2), (0, 0)))
    row2 = lambda a: a.reshape(1, -1)

    w_in = jnp.pad(bf(ab_w_in[0]), ((0, 0), (0, AB_PAD - AB_IN_DIM)))
    qkv, z, xbc, dt = _inproj(x, row2(norm_mix_g[0]), m[0], w_in)
    cos_t, sin_t = _rope_tables()
    qg = jnp.tile(attn_q_g[0], ATTN_HEADS).reshape(1, -1)
    kg = jnp.tile(attn_k_g[0], ATTN_KV_HEADS).reshape(1, -1)
    qn, kn = _qkprep(qkv, qg, kg, cos_t, sin_t)
    ck = cache_attn_k[:, 0].reshape(DEC_BATCH, PAST_LEN, ATTN_KV_DIM)
    cvv = cache_attn_v[:, 0].reshape(DEC_BATCH, PAST_LEN, ATTN_KV_DIM)
    attn_c, attn_s = _attention(qn, kn, qkv, ck, cvv)
    attn = jnp.concatenate([attn_c, attn_s], axis=0)

    dtb = jnp.pad(ssd_dt_bias[0].reshape(1, -1), ((0, 0), (0, DT_PAD - 2 * SSD_HEADS)))
    a_row = jnp.pad((-jnp.exp(ssd_a_log[0])).reshape(1, -1), ((0, 0), (0, DT_PAD - 2 * SSD_HEADS)))
    d_row = jnp.repeat(ssd_d[0], SSD_HEAD_DIM).reshape(1, -1)
    ssd_args = (ssd_conv_w[0], row2(ssd_conv_b[0]), dtb, a_row, d_row, row2(ssd_norm_g[0]))
    zero_st = jnp.zeros((BATCH, SSD_HEADS, SSD_HEAD_DIM, SSD_STATE), F32)
    y_c, hf_c, hb_c = _ssd(z, xbc, dt, *ssd_args, zero_st, zero_st, seq=SEQ, nb=BATCH, row0=0)
    y_s, _, _ = _ssd(z, xbc, dt, *ssd_args, state_ssd_fwd[:, 0], state_ssd_bwd[:, 0],
                     seq=DEC_SEQ, nb=DEC_BATCH, row0=R_CTX)
    y_ssd = jnp.concatenate([y_c, y_s], axis=0)
    x = _mixres(x, m[0], attn, y_ssd, bf(ab_w_out[0]))
    x = _ffn(x, row2(norm_ffn_g[0]), m[0], bf(ffn_w_gate[0]), bf(ffn_w_up[0]), bf(ffn_w_down[0]),
             row2(final_norm_g), final=False)

    pc = lambda w: w.reshape(w.shape[:-1] + (RWKV_HEADS, NK)).swapaxes(-1, -2).reshape(w.shape)
    w1 = bf(jnp.concatenate([rwkv_w1[0, 0], rwkv_w1[0, 1]], axis=1))
    a1 = bf(jnp.concatenate([rwkv_a1[0, 0], rwkv_a1[0, 1]], axis=1))
    zpad = lambda w: bf(jnp.stack([jnp.concatenate([w[0], jnp.zeros_like(w[1])], axis=0),
                                   jnp.concatenate([jnp.zeros_like(w[0]), w[1]], axis=0)]))
    pre = _rwkv_pre(x, row2(norm_mix_g[1]), m[1], rwkv_mu[0], bf(pc(rwkv_w_r[0])), bf(pc(rwkv_w_k[0])),
                    bf(pc(rwkv_w_v[0])), w1, zpad(pc(rwkv_w2[0])), a1, zpad(pc(rwkv_a2[0])), bf(rwkv_g1[0]),
                    bf(pc(rwkv_g2[0])), pc(rwkv_w0[0]), pc(rwkv_a0[0]), row2(pc(rwkv_k_k[0])),
                    row2(pc(rwkv_k_a[0])), row2(pc(rwkv_r_k[0].reshape(-1))))
    rk, v, wka, gg, bon = pre
    y_c, st_c = _scan_pass(rk, v, wka, None, BATCH, SEQ, 0)
    s0 = jnp.stack([state_rwkv_fwd[:, 0], state_rwkv_bwd[:, 0]])
    y_s, _ = _scan_pass(rk, v, wka, s0, DEC_BATCH, DEC_SEQ, R_CTX)
    x = _rwkv_post(x, m[1], y_c, y_s, bon, gg, row2(pc(rwkv_ln_g[0])), row2(pc(rwkv_ln_b[0])),
                   bf(rwkv_w_o[0].reshape(RWKV_HEADS, NK, D_MODEL).swapaxes(0, 1).reshape(D_MODEL, D_MODEL)))
    x = _ffn(x, row2(norm_ffn_g[1]), m[1], bf(ffn_w_gate[1]), bf(ffn_w_up[1]), bf(ffn_w_down[1]),
             row2(final_norm_g), final=True)

    y_prompt = x[:R_CTX].reshape(BATCH, SEQ, D_MODEL)
    y_sample = x[R_CTX:].reshape(DEC_BATCH, DEC_SEQ, D_MODEL)
    new_k = kn[:R_CTX].reshape(BATCH, 1, SEQ, ATTN_KV_HEADS, HEAD_DIM)
    new_v = qkv[:R_CTX, ATTN_Q_DIM + ATTN_KV_DIM:].reshape(BATCH, 1, SEQ, ATTN_KV_HEADS, HEAD_DIM)
    return (y_prompt, y_sample, new_k, new_v, hf_c[:, None], hb_c[:, None], st_c[0][:, None], st_c[1][:, None])
```

```python
import functools

import jax
import jax.numpy as jnp
from jax import lax
from jax.experimental import pallas as pl
from jax.experimental.pallas import tpu as pltpu

F32 = jnp.float32
BF16 = jnp.bfloat16

D_MODEL = 1024
BATCH = 16
SEQ = 256
DEC_BATCH = 2
DEC_SEQ = 1024
PAST_LEN = 256
GRID_W = 64
ATTN_HEADS = 8
ATTN_KV_HEADS = 2
HEAD_DIM = 64
ROPE_THETA = 10000.0
ATTN_Q_DIM = ATTN_HEADS * HEAD_DIM
ATTN_KV_DIM = ATTN_KV_HEADS * HEAD_DIM
SSD_HEADS = 8
SSD_HEAD_DIM = 64
SSD_D_INNER = SSD_HEADS * SSD_HEAD_DIM
SSD_GROUPS = 2
SSD_STATE = 64
SSD_CONV_K = 5
SSD_CHUNK = 128
SSD_CONV_DIM = SSD_D_INNER + 2 * SSD_GROUPS * SSD_STATE
AB_IN_DIM = ATTN_Q_DIM + 2 * ATTN_KV_DIM + SSD_D_INNER + SSD_CONV_DIM + 2 * SSD_HEADS
RWKV_HEAD_DIM = 64
RWKV_HEADS = D_MODEL // RWKV_HEAD_DIM
FFN_DIM = (((8 * D_MODEL + 2) // 3 + 255) // 256) * 256
RMS_EPS = 1e-6
GN_EPS = 64e-5
L2_EPS = 1e-12

R_CTX = BATCH * SEQ
R_SMP = DEC_BATCH * DEC_SEQ
R_ALL = R_CTX + R_SMP
LANES = 128
QKV_DIM = ATTN_Q_DIM + 2 * ATTN_KV_DIM
DT_PAD = LANES
AB_PAD = QKV_DIM + SSD_D_INNER + SSD_CONV_DIM + DT_PAD
VMEM_LIMIT = 56 * 1024 * 1024


def _cparams(sem):
    return pltpu.CompilerParams(dimension_semantics=sem, vmem_limit_bytes=VMEM_LIMIT)


def _mod_index(i, tm):
    n_ctx = R_CTX // tm
    per = DEC_SEQ // tm
    return jnp.where(i < n_ctx, 0, 1 + jnp.maximum(i - n_ctx, 0) // per)


def _sigmoid(x):
    return 1.0 / (1.0 + jnp.exp(-x))


def _silu(x):
    return x * _sigmoid(x)


def _softplus(x):
    return jnp.maximum(x, 0.0) + jnp.log1p(jnp.exp(-jnp.abs(x)))


def _rms(x, g):
    return x * lax.rsqrt(jnp.mean(x * x, axis=-1, keepdims=True) + RMS_EPS) * g


def _group_allsum(x, group):
    n = x.shape[-1]
    ax = x.ndim - 1
    lane = lax.broadcasted_iota(jnp.int32, x.shape, ax)
    s = 1
    while s < group:
        up = pltpu.roll(x, n - s, axis=ax)
        dn = pltpu.roll(x, s, axis=ax)
        x = x + jnp.where((lane & s) == 0, up, dn)
        s *= 2
    return x


def _head_allsum(x, hs):
    parts = [jnp.dot(x[:, j * LANES:(j + 1) * LANES], hs, precision=lax.Precision.HIGHEST,
                     preferred_element_type=F32) for j in range(x.shape[-1] // LANES)]
    return jnp.concatenate(parts, axis=1)


def _dot(a, b):
    return jnp.dot(a, b, preferred_element_type=F32)


ADA_TN = 1536


def _ada_kernel(c_ref, w_ref, b_ref, o_ref):
    s = _silu(c_ref[...]).astype(BF16)
    o_ref[0] = _dot(s, w_ref[0].astype(BF16)) + b_ref[0]


def _ada(cv, mod_w, mod_b):
    depth = mod_w.shape[0]
    n = mod_w.shape[2]
    return pl.pallas_call(
        _ada_kernel,
        out_shape=jax.ShapeDtypeStruct((depth, 8, n), F32),
        grid=(depth, n // ADA_TN),
        in_specs=[
            pl.BlockSpec((8, D_MODEL), lambda l, j: (0, 0)),
            pl.BlockSpec((1, D_MODEL, ADA_TN), lambda l, j: (l, 0, j)),
            pl.BlockSpec((1, 1, ADA_TN), lambda l, j: (l, 0, j)),
        ],
        out_specs=pl.BlockSpec((1, 8, ADA_TN), lambda l, j: (l, 0, j)),
        compiler_params=_cparams(("arbitrary", "arbitrary")),
        name="ada",
    )(cv, mod_w, mod_b.reshape(depth, 1, n))


INPROJ_TM = 512


def _inproj_kernel(x_ref, g_ref, mod_ref, w_ref, qkv_ref, z_ref, xbc_ref, dt_ref):
    m = mod_ref[0]
    h = _rms(x_ref[...], g_ref[...]) * (1 + m[1:2]) + m[0:1]
    p = _dot(h.astype(BF16), w_ref[...])
    qkv_ref[...] = p[:, 0:QKV_DIM]
    z_ref[...] = p[:, QKV_DIM:QKV_DIM + SSD_D_INNER]
    xbc_ref[...] = p[:, QKV_DIM + SSD_D_INNER:QKV_DIM + SSD_D_INNER + SSD_CONV_DIM]
    dt_ref[...] = p[:, QKV_DIM + SSD_D_INNER + SSD_CONV_DIM:AB_PAD]


def _inproj(x, g, mod, w_pad):
    tm = INPROJ_TM
    row = lambda i: (i, 0)
    return pl.pallas_call(
        _inproj_kernel,
        out_shape=(
            jax.ShapeDtypeStruct((R_ALL, QKV_DIM), F32),
            jax.ShapeDtypeStruct((R_ALL, SSD_D_INNER), F32),
            jax.ShapeDtypeStruct((R_ALL, SSD_CONV_DIM), F32),
            jax.ShapeDtypeStruct((R_ALL, DT_PAD), F32),
        ),
        grid=(R_ALL // tm,),
        in_specs=[
            pl.BlockSpec((tm, D_MODEL), row),
            pl.BlockSpec((1, D_MODEL), lambda i: (0, 0)),
            pl.BlockSpec((1, 8, D_MODEL), lambda i: (_mod_index(i, tm), 0, 0)),
            pl.BlockSpec((D_MODEL, AB_PAD), lambda i: (0, 0)),
        ],
        out_specs=(
            pl.BlockSpec((tm, QKV_DIM), row),
            pl.BlockSpec((tm, SSD_D_INNER), row),
            pl.BlockSpec((tm, SSD_CONV_DIM), row),
            pl.BlockSpec((tm, DT_PAD), row),
        ),
        compiler_params=_cparams(("arbitrary",)),
        name="inproj",
    )(x, g, mod, w_pad)


QK_TM = 256


def _qkprep_kernel(qkv_ref, qg_ref, kg_ref, cos_ref, sin_ref, qn_ref, kn_ref):
    cos = cos_ref[...]
    sin = sin_ref[...]

    def norm_rope(x, g, reps):
        ms = _group_allsum(x * x, HEAD_DIM) * (1.0 / HEAD_DIM)
        y = x * lax.rsqrt(ms + RMS_EPS) * g
        n = y.shape[-1]
        lane = lax.broadcasted_iota(jnp.int32, y.shape, 1)
        half = HEAD_DIM // 2
        swapped = jnp.where((lane & half) == 0, pltpu.roll(y, n - half, axis=1), pltpu.roll(y, half, axis=1))
        c = jnp.concatenate([cos] * reps, axis=1) if reps > 1 else cos
        s = jnp.concatenate([sin] * reps, axis=1) if reps > 1 else sin
        return y * c + swapped * s

    q = qkv_ref[:, 0:ATTN_Q_DIM]
    k = qkv_ref[:, ATTN_Q_DIM:ATTN_Q_DIM + ATTN_KV_DIM]
    qn_ref[...] = norm_rope(q, qg_ref[...], ATTN_Q_DIM // LANES)
    kn_ref[...] = norm_rope(k, kg_ref[...], 1)


def _qkprep(qkv, qg, kg, cos_t, sin_t):
    tm = QK_TM
    n_ctx = R_CTX // tm
    per = DEC_SEQ // tm
    tab = lambda i: (jnp.where(i < n_ctx, 0, 1 + jnp.maximum(i - n_ctx, 0) % per), 0)
    return pl.pallas_call(
        _qkprep_kernel,
        out_shape=(
            jax.ShapeDtypeStruct((R_ALL, ATTN_Q_DIM), F32),
            jax.ShapeDtypeStruct((R_ALL, ATTN_KV_DIM), F32),
        ),
        grid=(R_ALL // tm,),
        in_specs=[
            pl.BlockSpec((tm, QKV_DIM), lambda i: (i, 0)),
            pl.BlockSpec((1, ATTN_Q_DIM), lambda i: (0, 0)),
            pl.BlockSpec((1, ATTN_KV_DIM), lambda i: (0, 0)),
            pl.BlockSpec((tm, LANES), tab),
            pl.BlockSpec((tm, LANES), tab),
        ],
        out_specs=(
            pl.BlockSpec((tm, ATTN_Q_DIM), lambda i: (i, 0)),
            pl.BlockSpec((tm, ATTN_KV_DIM), lambda i: (i, 0)),
        ),
        compiler_params=_cparams(("arbitrary",)),
        name="qkprep",
    )(qkv, qg, kg, cos_t, sin_t)


def _attn_core(q, ks, vs):
    tq = q.shape[0]
    grp = ATTN_HEADS // ATTN_KV_HEADS
    scale = HEAD_DIM ** -0.5
    outs = []
    for g in range(ATTN_KV_HEADS):
        sl = slice(g * HEAD_DIM, (g + 1) * HEAD_DIM)
        qs = jnp.concatenate(
            [q[:, (g * grp + j) * HEAD_DIM:(g * grp + j + 1) * HEAD_DIM] for j in range(grp)], axis=0).astype(BF16)
        ss = [lax.dot_general(qs, k[:, sl].astype(BF16), (((1,), (1,)), ((), ())),
                              preferred_element_type=F32) * scale for k in ks]
        m = ss[0].max(axis=-1, keepdims=True)
        for s in ss[1:]:
            m = jnp.maximum(m, s.max(axis=-1, keepdims=True))
        ps = [jnp.exp(s - m) for s in ss]
        l = ps[0].sum(axis=-1, keepdims=True)
        for p in ps[1:]:
            l = l + p.sum(axis=-1, keepdims=True)
        inv = 1.0 / l
        o = None
        for p, v in zip(ps, vs):
            t = _dot((p * inv).astype(BF16), v[:, sl].astype(BF16))
            o = t if o is None else o + t
        outs += [o[j * tq:(j + 1) * tq] for j in range(grp)]
    return jnp.concatenate(outs, axis=1)


def _attn_ctx_kernel(q_ref, k_ref, v_ref, o_ref):
    o_ref[...] = _attn_core(q_ref[...], [k_ref[...]], [v_ref[...]]).astype(BF16)


def _attn_smp_kernel(q_ref, k_ref, v_ref, ck_ref, cv_ref, o_ref):
    ks = [ck_ref[0], k_ref[...]]
    vs = [cv_ref[0], v_ref[...]]
    o_ref[...] = _attn_core(q_ref[...], ks, vs).astype(BF16)


ATTN_TQ = 128


def _attention(qn, kn, qkv, cache_k, cache_v):
    v_col = (ATTN_Q_DIM + ATTN_KV_DIM) // ATTN_KV_DIM
    ctx = pl.pallas_call(
        _attn_ctx_kernel,
        out_shape=jax.ShapeDtypeStruct((R_CTX, ATTN_Q_DIM), BF16),
        grid=(BATCH,),
        in_specs=[
            pl.BlockSpec((SEQ, ATTN_Q_DIM), lambda b: (b, 0)),
            pl.BlockSpec((SEQ, ATTN_KV_DIM), lambda b: (b, 0)),
            pl.BlockSpec((SEQ, ATTN_KV_DIM), lambda b: (b, v_col)),
        ],
        out_specs=pl.BlockSpec((SEQ, ATTN_Q_DIM), lambda b: (b, 0)),
        compiler_params=_cparams(("arbitrary",)),
        name="attn_ctx",
    )(qn, kn, qkv)
    nq = DEC_SEQ // ATTN_TQ
    q0 = R_CTX // ATTN_TQ
    s0 = R_CTX // DEC_SEQ
    smp = pl.pallas_call(
        _attn_smp_kernel,
        out_shape=jax.ShapeDtypeStruct((R_SMP, ATTN_Q_DIM), BF16),
        grid=(DEC_BATCH, nq),
        in_specs=[
            pl.BlockSpec((ATTN_TQ, ATTN_Q_DIM), lambda b, i: (q0 + b * nq + i, 0)),
            pl.BlockSpec((DEC_SEQ, ATTN_KV_DIM), lambda b, i: (s0 + b, 0)),
            pl.BlockSpec((DEC_SEQ, ATTN_KV_DIM), lambda b, i: (s0 + b, v_col)),
            pl.BlockSpec((1, PAST_LEN, ATTN_KV_DIM), lambda b, i: (b, 0, 0)),
            pl.BlockSpec((1, PAST_LEN, ATTN_KV_DIM), lambda b, i: (b, 0, 0)),
        ],
        out_specs=pl.BlockSpec((ATTN_TQ, ATTN_Q_DIM), lambda b, i: (b * nq + i, 0)),
        compiler_params=_cparams(("arbitrary", "arbitrary")),
        name="attn_smp",
    )(qn, kn, qkv, cache_k, cache_v)
    return ctx, smp


CONV_HALO = 8


def _cumsum_rows(a, reverse):
    n = a.shape[0]
    row = lax.broadcasted_iota(jnp.int32, a.shape, 0)
    s = 1
    while s < n:
        if reverse:
            a = a + jnp.where(row < n - s, pltpu.roll(a, n - s, axis=0), 0.0)
        else:
            a = a + jnp.where(row >= s, pltpu.roll(a, s, axis=0), 0.0)
        s *= 2
    return a


def _ssd_kernel(z_ref, xbc_ref, dt_ref, cw_ref, cb_ref, dtb_ref, a_ref, d_ref, g_ref, h0f_ref, h0b_ref,
                y_ref, hf_ref, hb_ref, pad_sc, xc_sc, dt_sc, y_sc, h_sc, *, seq):
    L = SSD_CHUNK
    nc = seq // L
    pad = SSD_CONV_K // 2
    zeros = jnp.zeros((CONV_HALO, SSD_CONV_DIM), F32)
    pad_sc[0:CONV_HALO, :] = zeros
    pad_sc[seq + CONV_HALO:seq + 2 * CONV_HALO, :] = zeros
    pad_sc[CONV_HALO:seq + CONV_HALO, :] = xbc_ref[...]
    h_sc[0] = h0f_ref[0]
    h_sc[1] = h0b_ref[0]

    def conv_chunk(c, carry):
        r0 = pl.multiple_of(c * L, L)
        win = pad_sc[pl.ds(r0, L + 2 * CONV_HALO), :]
        acc = cb_ref[...]
        for i in range(SSD_CONV_K):
            acc = acc + win[CONV_HALO - pad + i:CONV_HALO - pad + i + L, :] * cw_ref[i:i + 1, :]
        xc_sc[pl.ds(r0, L), :] = _silu(acc)
        dt_sc[pl.ds(r0, L), :] = _softplus(dt_ref[pl.ds(r0, L), :] + dtb_ref[...])
        return carry

    lax.fori_loop(0, nc, conv_chunk, 0)

    rr = lax.broadcasted_iota(jnp.int32, (L, L), 0)
    cc = lax.broadcasted_iota(jnp.int32, (L, L), 1)
    grp = SSD_HEADS // SSD_GROUPS
    P = SSD_HEAD_DIM
    N = SSD_STATE

    def make_chunk(dirn):
        mask = (rr >= cc) if dirn == 0 else (rr <= cc)

        def chunk(ci, carry):
            c = ci if dirn == 0 else nc - 1 - ci
            r0 = pl.multiple_of(c * L, L)
            xs = xc_sc[pl.ds(r0, L), 0:SSD_D_INNER]
            bm = xc_sc[pl.ds(r0, L), SSD_D_INNER:SSD_D_INNER + SSD_GROUPS * N]
            cm = xc_sc[pl.ds(r0, L), SSD_D_INNER + SSD_GROUPS * N:SSD_CONV_DIM]
            dtc = dt_sc[pl.ds(r0, L), :]
            acs = _cumsum_rows(dtc * a_ref[...], reverse=(dirn == 1))
            acs_t = acs.T
            tot = acs[L - 1:L, :] if dirn == 0 else acs[0:1, :]
            dec_end = jnp.exp(tot - acs)
            eacs = jnp.exp(acs)
            cdec = jnp.exp(tot)
            cb = [lax.dot_general(cm[:, g * N:(g + 1) * N].astype(BF16), bm[:, g * N:(g + 1) * N].astype(BF16),
                                  (((1,), (1,)), ((), ())), preferred_element_type=F32)
                  for g in range(SSD_GROUPS)]
            if dirn == 0:
                y_sc[pl.ds(r0, L), :] = xs * d_ref[...]
            for h in range(SSD_HEADS):
                g = h // grp
                ln = h + SSD_HEADS * dirn
                seg = jnp.exp(jnp.where(mask, acs[:, ln:ln + 1] - acs_t[ln:ln + 1, :], -jnp.inf))
                sc = (cb[g] * seg).astype(BF16)
                xdt = (xs[:, h * P:(h + 1) * P] * dtc[:, ln:ln + 1]).astype(BF16)
                hin = h_sc[dirn, h]
                ce = (cm[:, g * N:(g + 1) * N] * eacs[:, ln:ln + 1]).astype(BF16)
                yh = _dot(sc, xdt) + lax.dot_general(ce, hin.astype(BF16), (((1,), (1,)), ((), ())),
                                                     preferred_element_type=F32)
                bd = (bm[:, g * N:(g + 1) * N] * dec_end[:, ln:ln + 1]).astype(BF16)
                st = lax.dot_general(xdt, bd, (((0,), (0,)), ((), ())), preferred_element_type=F32)
                h_sc[dirn, h] = hin * cdec[:, ln:ln + 1] + st
                y_sc[pl.ds(r0, L), h * P:(h + 1) * P] += yh
            return carry

        return chunk

    lax.fori_loop(0, nc, make_chunk(0), 0)
    lax.fori_loop(0, nc, make_chunk(1), 0)

    def out_chunk(c, carry):
        r0 = pl.multiple_of(c * L, L)
        y = y_sc[pl.ds(r0, L), :] * _silu(z_ref[pl.ds(r0, L), :])
        y_ref[pl.ds(r0, L), :] = _rms(y, g_ref[...]).astype(BF16)
        return carry

    lax.fori_loop(0, nc, out_chunk, 0)
    hf_ref[0] = h_sc[0]
    hb_ref[0] = h_sc[1]


def _ssd(z, xbc, dt, cw, cb, dtb, a_row, d_row, g, h0f, h0b, *, seq, nb, row0):
    blk0 = row0 // seq
    row = lambda b: (blk0 + b, 0)
    fixed = lambda b: (0, 0)
    st = lambda b: (b, 0, 0, 0)
    st_shape = (nb, SSD_HEADS, SSD_HEAD_DIM, SSD_STATE)
    st_blk = (1, SSD_HEADS, SSD_HEAD_DIM, SSD_STATE)
    return pl.pallas_call(
        functools.partial(_ssd_kernel, seq=seq),
        out_shape=(
            jax.ShapeDtypeStruct((nb * seq, SSD_D_INNER), BF16),
            jax.ShapeDtypeStruct(st_shape, F32),
            jax.ShapeDtypeStruct(st_shape, F32),
        ),
        grid=(nb,),
        in_specs=[
            pl.BlockSpec((seq, SSD_D_INNER), row),
            pl.BlockSpec((seq, SSD_CONV_DIM), row),
            pl.BlockSpec((seq, DT_PAD), row),
            pl.BlockSpec((SSD_CONV_K, SSD_CONV_DIM), fixed),
            pl.BlockSpec((1, SSD_CONV_DIM), fixed),
            pl.BlockSpec((1, DT_PAD), fixed),
            pl.BlockSpec((1, DT_PAD), fixed),
            pl.BlockSpec((1, SSD_D_INNER), fixed),
            pl.BlockSpec((1, SSD_D_INNER), fixed),
            pl.BlockSpec(st_blk, st),
            pl.BlockSpec(st_blk, st),
        ],
        out_specs=(
            pl.BlockSpec((seq, SSD_D_INNER), lambda b: (b, 0)),
            pl.BlockSpec(st_blk, st),
            pl.BlockSpec(st_blk, st),
        ),
        scratch_shapes=[
            pltpu.VMEM((seq + 2 * CONV_HALO, SSD_CONV_DIM), F32),
            pltpu.VMEM((seq, SSD_CONV_DIM), F32),
            pltpu.VMEM((seq, DT_PAD), F32),
            pltpu.VMEM((seq, SSD_D_INNER), F32),
            pltpu.VMEM((2, SSD_HEADS, SSD_HEAD_DIM, SSD_STATE), F32),
        ],
        compiler_params=_cparams(("arbitrary",)),
        name=f"ssd_{seq}",
    )(z, xbc, dt, cw, cb, dtb, a_row, d_row, g, h0f, h0b)


RES_TM = 512


def _mixres_kernel(x_ref, mod_ref, a1_ref, a2_ref, w_ref, o_ref):
    k1 = a1_ref.shape[1]
    out = _dot(a1_ref[...], w_ref[0:k1, :]) + _dot(a2_ref[...], w_ref[k1:, :])
    o_ref[...] = x_ref[...] + mod_ref[0][2:3] * out


def _mixres(x, mod, a1, a2, w):
    tm = RES_TM
    row = lambda i: (i, 0)
    return pl.pallas_call(
        _mixres_kernel,
        out_shape=jax.ShapeDtypeStruct((R_ALL, D_MODEL), F32),
        grid=(R_ALL // tm,),
        in_specs=[
            pl.BlockSpec((tm, D_MODEL), row),
            pl.BlockSpec((1, 8, D_MODEL), lambda i: (_mod_index(i, tm), 0, 0)),
            pl.BlockSpec((tm, a1.shape[1]), row),
            pl.BlockSpec((tm, a2.shape[1]), row),
            pl.BlockSpec(w.shape, lambda i: (0, 0)),
        ],
        out_specs=pl.BlockSpec((tm, D_MODEL), row),
        compiler_params=_cparams(("arbitrary",)),
        name="mixres",
    )(x, mod, a1, a2, w)


FFN_TM = 1024
FFN_TF = 256


def _ffn_kernel(x_ref, g_ref, mod_ref, wg_ref, wu_ref, wd_ref, fg_ref, o_ref, h_sc, acc_sc, *, final):
    j = pl.program_id(1)

    @pl.when(j == 0)
    def _():
        m = mod_ref[0]
        h = _rms(x_ref[...], g_ref[...]) * (1 + m[4:5]) + m[3:4]
        h_sc[...] = h.astype(BF16)
        acc_sc[...] = jnp.zeros_like(acc_sc)

    h = h_sc[...]
    hid = _silu(_dot(h, wg_ref[...])) * _dot(h, wu_ref[...])
    acc_sc[...] += _dot(hid.astype(BF16), wd_ref[...])

    @pl.when(j == pl.num_programs(1) - 1)
    def _():
        y = x_ref[...] + mod_ref[0][5:6] * acc_sc[...]
        if final:
            y = _rms(y, fg_ref[...])
        o_ref[...] = y


def _ffn(x, g, mod, wg, wu, wd, fg, *, final):
    tm, tf = FFN_TM, FFN_TF
    row = lambda i, j: (i, 0)
    return pl.pallas_call(
        functools.partial(_ffn_kernel, final=final),
        out_shape=jax.ShapeDtypeStruct((R_ALL, D_MODEL), F32),
        grid=(R_ALL // tm, FFN_DIM // tf),
        in_specs=[
            pl.BlockSpec((tm, D_MODEL), row),
            pl.BlockSpec((1, D_MODEL), lambda i, j: (0, 0)),
            pl.BlockSpec((1, 8, D_MODEL), lambda i, j: (_mod_index(i, tm), 0, 0)),
            pl.BlockSpec((D_MODEL, tf), lambda i, j: (0, j)),
            pl.BlockSpec((D_MODEL, tf), lambda i, j: (0, j)),
            pl.BlockSpec((tf, D_MODEL), lambda i, j: (j, 0)),
            pl.BlockSpec((1, D_MODEL), lambda i, j: (0, 0)),
        ],
        out_specs=pl.BlockSpec((tm, D_MODEL), row),
        scratch_shapes=[pltpu.VMEM((tm, D_MODEL), BF16), pltpu.VMEM((tm, D_MODEL), F32)],
        compiler_params=_cparams(("arbitrary", "arbitrary")),
        name="ffn_final" if final else "ffn",
    )(x, g, mod, wg, wu, wd, fg)


RW_TM = 128
HALO = 8
N_OPS = 9


def _rwkv_pre_kernel(x_ref, xp_ref, xn_ref, g_ref, mod_ref, mu_ref, wr_ref, wk_ref, wv_ref, w1_ref, w2_ref,
                     a1_ref, a2_ref, g1_ref, g2_ref, w0_ref, a0_ref, kk_ref, ka_ref, rk_ref, hs_ref,
                     ops_o, g_o, bon_o):
    i = pl.program_id(0)
    tm = RW_TM
    n_ctx = R_CTX // tm
    per_c = SEQ // tm
    per_s = DEC_SEQ // tm
    rel = jnp.where(i < n_ctx, i % per_c, jnp.maximum(i - n_ctx, 0) % per_s)
    last = jnp.where(i < n_ctx, per_c - 1, per_s - 1)
    m = mod_ref[0]

    def nm(x):
        return _rms(x, g_ref[...]) * (1 + m[1:2]) + m[0:1]

    h = nm(x_ref[...])
    prev_row = jnp.where(rel == 0, 0.0, nm(xp_ref[...])[HALO - 1:HALO, :])
    next_row = jnp.where(rel == last, 0.0, nm(xn_ref[...])[0:1, :])
    row = lax.broadcasted_iota(jnp.int32, h.shape, 0)
    hp = jnp.where(row == 0, prev_row, pltpu.roll(h, 1, axis=0))
    hn = jnp.where(row == tm - 1, next_row, pltpu.roll(h, tm - 1, axis=0))
    dp = hp - h
    dn = hn - h

    def mix(idx):
        return (h + dp * mu_ref[0, idx:idx + 1, :] + dn * mu_ref[1, idx:idx + 1, :]).astype(BF16)

    r = _dot(mix(0), wr_ref[...])
    k = _dot(mix(2), wk_ref[...])
    v = _dot(mix(3), wv_ref[...])
    lw = jnp.tanh(_dot(mix(1), w1_ref[...])).astype(BF16)
    la = _dot(mix(4), a1_ref[...]).astype(BF16)
    gg = _dot(_sigmoid(_dot(mix(5), g1_ref[...])).astype(BF16), g2_ref[...])

    hs = hs_ref[...]
    kk = k * kk_ref[...]
    kk = kk * lax.rsqrt(_head_allsum(kk * kk, hs) + L2_EPS)
    ops_o[0] = r
    ops_o[1] = kk
    ops_o[2] = v
    g_o[...] = gg
    bsum = None
    for j in range(2):
        wl = w0_ref[j:j + 1, :] + _dot(lw, w2_ref[j])
        ops_o[3 + 3 * j] = jnp.exp(-jnp.exp(-_softplus(-wl) - 0.5))
        a = _sigmoid(a0_ref[j:j + 1, :] + _dot(la, a2_ref[j]))
        kd = k * (1 + (a - 1) * ka_ref[...])
        ops_o[4 + 3 * j] = kd
        ops_o[5 + 3 * j] = kk * a
        t = r * kd * rk_ref[...]
        bsum = t if bsum is None else bsum + t
    bon_o[...] = _head_allsum(bsum, hs) * v


def _rwkv_pre(x, g, mod, mu, wr, wk, wv, w1, w2, a1, a2, g1, g2, w0, a0, k_k, k_a, r_k, hs):
    tm = RW_TM
    nblk = R_ALL // HALO
    per = tm // HALO
    row = lambda i: (i, 0)
    fixed2 = lambda i: (0, 0)
    fixed3 = lambda i: (0, 0, 0)
    full = lambda a: pl.BlockSpec(a.shape, fixed2 if a.ndim == 2 else fixed3)
    out = jax.ShapeDtypeStruct((R_ALL, D_MODEL), F32)
    orow = pl.BlockSpec((tm, D_MODEL), row)
    return pl.pallas_call(
        _rwkv_pre_kernel,
        out_shape=(jax.ShapeDtypeStruct((N_OPS, R_ALL, D_MODEL), F32), out, out),
        grid=(R_ALL // tm,),
        in_specs=[
            pl.BlockSpec((tm, D_MODEL), row),
            pl.BlockSpec((HALO, D_MODEL), lambda i: (jnp.maximum(i * per - 1, 0), 0)),
            pl.BlockSpec((HALO, D_MODEL), lambda i: (jnp.minimum((i + 1) * per, nblk - 1), 0)),
            pl.BlockSpec((1, D_MODEL), fixed2),
            pl.BlockSpec((1, 8, D_MODEL), lambda i: (_mod_index(i, tm), 0, 0)),
            full(mu), full(wr), full(wk), full(wv), full(w1), full(w2), full(a1), full(a2), full(g1), full(g2),
            full(w0), full(a0), full(k_k), full(k_a), full(r_k), full(hs),
        ],
        out_specs=(pl.BlockSpec((N_OPS, tm, D_MODEL), lambda i: (0, i, 0)), orow, orow),
        compiler_params=_cparams(("arbitrary",)),
        name="rwkv_pre",
    )(x, x, x, g, mod, mu, wr, wk, wv, w1, w2, a1, a2, g1, g2, w0, a0, k_k, k_a, r_k, hs)


SCAN_TT = 32
SCAN_ACC = 4
NK = RWKV_HEAD_DIM


def _scan_kernel(kt_ref, vt_ref, dk_ref, s0_ref, y_ref, st_ref, s_sc):
    d = pl.program_id(0)
    j = pl.program_id(2)

    @pl.when(j == 0)
    def _():
        s_sc[...] = s0_ref[0]

    def tix(i):
        return jnp.where(d == 0, i, SCAN_TT - 1 - i)

    def project(i):
        t = tix(i)
        parts = [None] * SCAN_ACC
        for k in range(NK):
            term = s_sc[k] * kt_ref[1, t, k:k + 1, :]
            parts[k % SCAN_ACC] = term if parts[k % SCAN_ACC] is None else parts[k % SCAN_ACC] + term
        while len(parts) > 1:
            parts = [parts[a] + parts[a + 1] for a in range(0, len(parts), 2)]
        return parts[0]

    def advance(i, sa, with_next):
        t = tix(i)
        tn = tix(i + 1)
        vv = vt_ref[0, t]
        y = None
        sa_next = None
        for k in range(NK):
            sn = (s_sc[k] * dk_ref[0, t, k:k + 1, :] - sa * dk_ref[2, t, k:k + 1, :]
                  + vv * dk_ref[1, t, k:k + 1, :])
            s_sc[k] = sn
            yk = sn * kt_ref[0, t, k:k + 1, :]
            y = yk if y is None else y + yk
            if with_next:
                ak = sn * kt_ref[1, tn, k:k + 1, :]
                sa_next = ak if sa_next is None else sa_next + ak
        y_ref[0, t] = y
        return sa_next

    sa_last = lax.fori_loop(0, SCAN_TT - 1, lambda i, sa: advance(i, sa, True), project(0))
    advance(SCAN_TT - 1, sa_last, False)

    @pl.when(j == pl.num_programs(2) - 1)
    def _():
        st_ref[0] = s_sc[...]


def _scan(kt, vt, dk, s0, *, vt_slab=0, dk_blk=0):
    _, seq, nv, chains = vt.shape
    tt = SCAN_TT
    nt = seq // tt
    tb = lambda d, j: jnp.where(d == 0, j, nt - 1 - j)
    sblk = pl.BlockSpec((1, NK, nv, LANES), lambda d, c, j: (d, 0, 0, c))
    yblk = pl.BlockSpec((1, tt, nv, LANES), lambda d, c, j: (d, tb(d, j), 0, c))
    return pl.pallas_call(
        _scan_kernel,
        out_shape=(jax.ShapeDtypeStruct((2, seq, nv, chains), F32), jax.ShapeDtypeStruct((2, NK, nv, chains), F32)),
        grid=(2, chains // LANES, nt),
        in_specs=[
            pl.BlockSpec((2, tt, NK, LANES), lambda d, c, j: (0, tb(d, j), 0, c)),
            pl.BlockSpec((1, tt, nv, LANES), lambda d, c, j: (vt_slab, tb(d, j), 0, c)),
            pl.BlockSpec((3, tt, NK, LANES), lambda d, c, j: (dk_blk + d, tb(d, j), 0, c)),
            sblk,
        ],
        out_specs=(yblk, sblk),
        scratch_shapes=[pltpu.VMEM((NK, nv, LANES), F32)],
        compiler_params=_cparams(("arbitrary", "arbitrary", "arbitrary")),
        name=f"rwkv_scan_{seq}",
    )(kt, vt, dk, s0)


POST_TM = 256


def _rwkv_post_kernel(x_ref, mod_ref, yc_ref, ys_ref, bon_ref, g_ref, lng_ref, lnb_ref, wo_ref, hs_ref, o_ref):
    is_ctx = pl.program_id(0) < R_CTX // POST_TM
    y = jnp.where(is_ctx, yc_ref[0] + yc_ref[1], ys_ref[0] + ys_ref[1])
    inv = 1.0 / RWKV_HEAD_DIM
    hs = hs_ref[...]
    d = y - _head_allsum(y, hs) * inv
    var = _head_allsum(d * d, hs) * inv
    yn = d * lax.rsqrt(var + GN_EPS) * lng_ref[...] + lnb_ref[...]
    out = ((yn + bon_ref[...]) * g_ref[...]).astype(BF16)
    o_ref[...] = x_ref[...] + mod_ref[0][2:3] * _dot(out, wo_ref[...])


def _rwkv_post(x, mod, yc, ys, bon, g, lng, lnb, wo, hs):
    tm = POST_TM
    n_ctx = R_CTX // tm
    row = lambda i: (i, 0)
    fixed = lambda i: (0, 0)
    big = pl.BlockSpec((tm, D_MODEL), row)
    ycb = pl.BlockSpec((2, tm, D_MODEL), lambda i: (0, jnp.minimum(i, n_ctx - 1), 0))
    ysb = pl.BlockSpec((2, tm, D_MODEL), lambda i: (0, jnp.maximum(i - n_ctx, 0), 0))
    return pl.pallas_call(
        _rwkv_post_kernel,
        out_shape=jax.ShapeDtypeStruct((R_ALL, D_MODEL), F32),
        grid=(R_ALL // tm,),
        in_specs=[
            big,
            pl.BlockSpec((1, 8, D_MODEL), lambda i: (_mod_index(i, tm), 0, 0)),
            ycb, ysb, big, big,
            pl.BlockSpec((1, D_MODEL), fixed),
            pl.BlockSpec((1, D_MODEL), fixed),
            pl.BlockSpec((D_MODEL, D_MODEL), fixed),
            pl.BlockSpec((LANES, LANES), fixed),
        ],
        out_specs=big,
        compiler_params=_cparams(("arbitrary",)),
        name="rwkv_post",
    )(x, mod, yc, ys, bon, g, lng, lnb, wo, hs)


RL_T = 8
NPAIR = D_MODEL // LANES


def _to_chains_kernel(x_ref, o_ref):
    cols = [jnp.swapaxes(x_ref[0, :, :, p * LANES:(p + 1) * LANES], 0, 1) for p in range(NPAIR)]
    for t in range(RL_T):
        tile = jnp.concatenate([c[t] for c in cols], axis=0).T
        o_ref[0, t, :, 0:LANES] = tile[0:NK]
        o_ref[0, t, :, LANES:2 * LANES] = tile[NK:2 * NK]


def _from_chains_kernel(y_ref, o_ref):
    tiles = []
    for t in range(RL_T):
        m = jnp.concatenate([y_ref[0, t, :, 0:LANES], y_ref[0, t, :, LANES:2 * LANES]], axis=0)
        tiles.append(m.T)
    for p in range(NPAIR):
        blk = jnp.stack([m[p * BATCH:(p + 1) * BATCH] for m in tiles], axis=0)
        o_ref[0, :, :, p * LANES:(p + 1) * LANES] = jnp.swapaxes(blk, 0, 1)


def _to_chains(ops):
    n = ops.shape[0]
    return pl.pallas_call(
        _to_chains_kernel,
        out_shape=jax.ShapeDtypeStruct((n, SEQ, NK, 2 * LANES), F32),
        grid=(n, SEQ // RL_T),
        in_specs=[pl.BlockSpec((1, BATCH, RL_T, D_MODEL), lambda o, j: (o, 0, j, 0))],
        out_specs=pl.BlockSpec((1, RL_T, NK, 2 * LANES), lambda o, j: (o, j, 0, 0)),
        compiler_params=_cparams(("arbitrary", "arbitrary")),
        name="to_chains",
    )(ops.reshape(n, R_ALL // SEQ, SEQ, D_MODEL))


def _from_chains(y):
    n = y.shape[0]
    out = pl.pallas_call(
        _from_chains_kernel,
        out_shape=jax.ShapeDtypeStruct((n, BATCH, SEQ, D_MODEL), F32),
        grid=(n, SEQ // RL_T),
        in_specs=[pl.BlockSpec((1, RL_T, NK, 2 * LANES), lambda o, j: (o, j, 0, 0))],
        out_specs=pl.BlockSpec((1, BATCH, RL_T, D_MODEL), lambda o, j: (o, 0, j, 0)),
        compiler_params=_cparams(("arbitrary", "arbitrary")),
        name="from_chains",
    )(y)
    return out.reshape(n, R_CTX, D_MODEL)


def _scan_ctx(ops):
    ch = _to_chains(ops)
    s0 = jnp.zeros((2, NK, NK, 2 * LANES), F32)
    y, st = _scan(ch, ch, ch, s0, vt_slab=2, dk_blk=1)
    st = st.reshape(2, NK, NK, 2, NPAIR, BATCH).transpose(0, 5, 4, 3, 2, 1)
    return _from_chains(y), st.reshape(2, BATCH, RWKV_HEADS, NK, NK)


def _scan_smp(ops, s0):
    H = RWKV_HEADS
    nb, seq = DEC_BATCH, DEC_SEQ
    nch = nb * H
    vq = LANES // nch
    nv = NK // vq
    rows = slice(R_CTX, R_ALL)

    def ktype(a):
        n = a.shape[0]
        a = a.reshape(n, nb, seq, H, NK).transpose(0, 2, 4, 1, 3).reshape(n, seq, NK, 1, nch)
        return jnp.broadcast_to(a, (n, seq, NK, vq, nch)).reshape(n, seq, NK, vq * nch)

    kt = ktype(ops[0:2, rows])
    dk = ktype(ops[3:N_OPS, rows])
    vt = ops[2, rows].reshape(nb, seq, H, vq, nv).transpose(1, 4, 3, 0, 2).reshape(1, seq, nv, vq * nch)
    s0 = s0.reshape(2, nb, H, vq, nv, NK).transpose(0, 5, 4, 3, 1, 2).reshape(2, NK, nv, vq * nch)
    y, _ = _scan(kt, vt, dk, s0)
    return y.reshape(2, seq, nv, vq, nb, H).transpose(0, 4, 1, 5, 3, 2).reshape(2, nb * seq, D_MODEL)


def _rope_tables():
    rows = DEC_SEQ // GRID_W
    row = jnp.repeat(jnp.arange(rows, dtype=F32), GRID_W)
    col = jnp.tile(jnp.arange(GRID_W, dtype=F32), rows)
    n_freq = HEAD_DIM // 4
    inv_freq = ROPE_THETA ** (-jnp.arange(n_freq, dtype=F32) / n_freq)
    ang = jnp.concatenate([row[:, None] * inv_freq, col[:, None] * inv_freq], axis=-1)
    cos, sin = jnp.cos(ang), jnp.sin(ang)
    reps = LANES // HEAD_DIM
    cos_t = jnp.tile(jnp.concatenate([cos, cos], axis=-1), (1, reps))
    sin_t = jnp.tile(jnp.concatenate([-sin, sin], axis=-1), (1, reps))
    ident_c = jnp.ones((QK_TM, LANES), F32)
    ident_s = jnp.zeros((QK_TM, LANES), F32)
    return jnp.concatenate([ident_c, cos_t], axis=0), jnp.concatenate([ident_s, sin_t], axis=0)


def kernel(x_prompt, x_sample, cache_attn_k, cache_attn_v, state_ssd_fwd, state_ssd_bwd, state_rwkv_fwd, state_rwkv_bwd, c, c_ctx, mod_w, mod_b, norm_mix_g, norm_ffn_g, ffn_w_gate, ffn_w_up, ffn_w_down, ab_w_in, ab_w_out, attn_q_g, attn_k_g, ssd_conv_w, ssd_conv_b, ssd_dt_bias, ssd_a_log, ssd_d, ssd_norm_g, rwkv_mu, rwkv_w_r, rwkv_w_k, rwkv_w_v, rwkv_w0, rwkv_w1, rwkv_w2, rwkv_a0, rwkv_a1, rwkv_a2, rwkv_g1, rwkv_g2, rwkv_k_k, rwkv_k_a, rwkv_r_k, rwkv_ln_g, rwkv_ln_b, rwkv_w_o, final_norm_g):
    bf = lambda a: a.astype(BF16)
    x = jnp.concatenate([x_prompt.reshape(R_CTX, D_MODEL), x_sample.reshape(R_SMP, D_MODEL)], axis=0)

    cv = jnp.concatenate([c_ctx[None], c, jnp.zeros((8 - 1 - DEC_BATCH, D_MODEL), F32)], axis=0)
    m = _ada(cv, mod_w, mod_b)
    m = m[:, :1 + DEC_BATCH].reshape(2, 1 + DEC_BATCH, 6, D_MODEL)
    m = jnp.pad(m, ((0, 0), (0, 0), (0, 2), (0, 0)))
    row2 = lambda a: a.reshape(1, -1)

    w_in = jnp.pad(bf(ab_w_in[0]), ((0, 0), (0, AB_PAD - AB_IN_DIM)))
    qkv, z, xbc, dt = _inproj(x, row2(norm_mix_g[0]), m[0], w_in)
    cos_t, sin_t = _rope_tables()
    qg = jnp.tile(attn_q_g[0], ATTN_HEADS).reshape(1, -1)
    kg = jnp.tile(attn_k_g[0], ATTN_KV_HEADS).reshape(1, -1)
    qn, kn = _qkprep(qkv, qg, kg, cos_t, sin_t)
    ck = cache_attn_k[:, 0].reshape(DEC_BATCH, PAST_LEN, ATTN_KV_DIM)
    cvv = cache_attn_v[:, 0].reshape(DEC_BATCH, PAST_LEN, ATTN_KV_DIM)
    attn_c, attn_s = _attention(qn, kn, qkv, ck, cvv)
    attn = jnp.concatenate([attn_c, attn_s], axis=0)

    dtb = jnp.pad(ssd_dt_bias[0].reshape(1, -1), ((0, 0), (0, DT_PAD - 2 * SSD_HEADS)))
    a_row = jnp.pad((-jnp.exp(ssd_a_log[0])).reshape(1, -1), ((0, 0), (0, DT_PAD - 2 * SSD_HEADS)))
    d_row = jnp.repeat(ssd_d[0], SSD_HEAD_DIM).reshape(1, -1)
    ssd_args = (ssd_conv_w[0], row2(ssd_conv_b[0]), dtb, a_row, d_row, row2(ssd_norm_g[0]))
    zero_st = jnp.zeros((BATCH, SSD_HEADS, SSD_HEAD_DIM, SSD_STATE), F32)
    y_c, hf_c, hb_c = _ssd(z, xbc, dt, *ssd_args, zero_st, zero_st, seq=SEQ, nb=BATCH, row0=0)
    y_s, _, _ = _ssd(z, xbc, dt, *ssd_args, state_ssd_fwd[:, 0], state_ssd_bwd[:, 0],
                     seq=DEC_SEQ, nb=DEC_BATCH, row0=R_CTX)
    y_ssd = jnp.concatenate([y_c, y_s], axis=0)
    x = _mixres(x, m[0], attn, y_ssd, bf(ab_w_out[0]))
    x = _ffn(x, row2(norm_ffn_g[0]), m[0], bf(ffn_w_gate[0]), bf(ffn_w_up[0]), bf(ffn_w_down[0]),
             row2(final_norm_g), final=False)

    w1 = bf(jnp.concatenate([rwkv_w1[0, 0], rwkv_w1[0, 1]], axis=1))
    a1 = bf(jnp.concatenate([rwkv_a1[0, 0], rwkv_a1[0, 1]], axis=1))
    zpad = lambda w: bf(jnp.stack([jnp.concatenate([w[0], jnp.zeros_like(w[1])], axis=0),
                                   jnp.concatenate([jnp.zeros_like(w[0]), w[1]], axis=0)]))
    lane = jnp.arange(LANES)
    hs = (lane[:, None] // RWKV_HEAD_DIM == lane[None, :] // RWKV_HEAD_DIM).astype(F32)
    ops, gg, bon = _rwkv_pre(x, row2(norm_mix_g[1]), m[1], rwkv_mu[0], bf(rwkv_w_r[0]), bf(rwkv_w_k[0]),
                             bf(rwkv_w_v[0]), w1, zpad(rwkv_w2[0]), a1, zpad(rwkv_a2[0]), bf(rwkv_g1[0]),
                             bf(rwkv_g2[0]), rwkv_w0[0], rwkv_a0[0], row2(rwkv_k_k[0]), row2(rwkv_k_a[0]),
                             rwkv_r_k[0].reshape(1, -1), hs)
    y_c, st_c = _scan_ctx(ops)
    y_s = _scan_smp(ops, jnp.stack([state_rwkv_fwd[:, 0], state_rwkv_bwd[:, 0]]))
    x = _rwkv_post(x, m[1], y_c, y_s, bon, gg, row2(rwkv_ln_g[0]), row2(rwkv_ln_b[0]), bf(rwkv_w_o[0]), hs)
    x = _ffn(x, row2(norm_ffn_g[1]), m[1], bf(ffn_w_gate[1]), bf(ffn_w_up[1]), bf(ffn_w_down[1]),
             row2(final_norm_g), final=True)

    y_prompt = x[:R_CTX].reshape(BATCH, SEQ, D_MODEL)
    y_sample = x[R_CTX:].reshape(DEC_BATCH, DEC_SEQ, D_MODEL)
    new_k = kn[:R_CTX].reshape(BATCH, 1, SEQ, ATTN_KV_HEADS, HEAD_DIM)
    new_v = qkv[:R_CTX, ATTN_Q_DIM + ATTN_KV_DIM:].reshape(BATCH, 1, SEQ, ATTN_KV_HEADS, HEAD_DIM)
    return (y_prompt, y_sample, new_k, new_v, hf_c[:, None], hb_c[:, None], st_c[0][:, None], st_c[1][:, None])
```

```python
import functools

import jax
import jax.numpy as jnp
from jax import lax
from jax.experimental import pallas as pl
from jax.experimental.pallas import tpu as pltpu

F32 = jnp.float32
BF16 = jnp.bfloat16

D_MODEL = 1024
BATCH = 16
SEQ = 256
DEC_BATCH = 2
DEC_SEQ = 1024
PAST_LEN = 256
GRID_W = 64
ATTN_HEADS = 8
ATTN_KV_HEADS = 2
HEAD_DIM = 64
ROPE_THETA = 10000.0
ATTN_Q_DIM = ATTN_HEADS * HEAD_DIM
ATTN_KV_DIM = ATTN_KV_HEADS * HEAD_DIM
SSD_HEADS = 8
SSD_HEAD_DIM = 64
SSD_D_INNER = SSD_HEADS * SSD_HEAD_DIM
SSD_GROUPS = 2
SSD_STATE = 64
SSD_CONV_K = 5
SSD_CHUNK = 128
SSD_CONV_DIM = SSD_D_INNER + 2 * SSD_GROUPS * SSD_STATE
AB_IN_DIM = ATTN_Q_DIM + 2 * ATTN_KV_DIM + SSD_D_INNER + SSD_CONV_DIM + 2 * SSD_HEADS
RWKV_HEAD_DIM = 64
RWKV_HEADS = D_MODEL // RWKV_HEAD_DIM
FFN_DIM = (((8 * D_MODEL + 2) // 3 + 255) // 256) * 256
RMS_EPS = 1e-6
GN_EPS = 64e-5
L2_EPS = 1e-12

R_CTX = BATCH * SEQ
R_SMP = DEC_BATCH * DEC_SEQ
R_ALL = R_CTX + R_SMP
LANES = 128
QKV_DIM = ATTN_Q_DIM + 2 * ATTN_KV_DIM
DT_PAD = LANES
AB_PAD = QKV_DIM + SSD_D_INNER + SSD_CONV_DIM + DT_PAD
VMEM_LIMIT = 56 * 1024 * 1024


def _cparams(sem):
    return pltpu.CompilerParams(dimension_semantics=sem, vmem_limit_bytes=VMEM_LIMIT)


def _mod_index(i, tm):
    n_ctx = R_CTX // tm
    per = DEC_SEQ // tm
    return jnp.where(i < n_ctx, 0, 1 + jnp.maximum(i - n_ctx, 0) // per)


def _sigmoid(x):
    return 1.0 / (1.0 + jnp.exp(-x))


def _silu(x):
    return x * _sigmoid(x)


def _softplus(x):
    return jnp.maximum(x, 0.0) + jnp.log1p(jnp.exp(-jnp.abs(x)))


def _rms(x, g):
    return x * lax.rsqrt(jnp.mean(x * x, axis=-1, keepdims=True) + RMS_EPS) * g


def _head_allsum(x, hs):
    parts = [jnp.dot(x[:, j * LANES:(j + 1) * LANES], hs, precision=lax.Precision.HIGHEST,
                     preferred_element_type=F32) for j in range(x.shape[-1] // LANES)]
    return parts[0] if len(parts) == 1 else jnp.concatenate(parts, axis=1)


def _head_mask():
    lane = jnp.arange(LANES)
    return (lane[:, None] // HEAD_DIM == lane[None, :] // HEAD_DIM).astype(F32)


def _dot(a, b):
    return jnp.dot(a, b, preferred_element_type=F32)


ADA_TN = 1536


def _ada_kernel(c_ref, w_ref, b_ref, o_ref):
    s = _silu(c_ref[...]).astype(BF16)
    o_ref[0] = _dot(s, w_ref[0].astype(BF16)) + b_ref[0]


def _ada(cv, mod_w, mod_b):
    depth = mod_w.shape[0]
    n = mod_w.shape[2]
    return pl.pallas_call(
        _ada_kernel,
        out_shape=jax.ShapeDtypeStruct((depth, 8, n), F32),
        grid=(depth, n // ADA_TN),
        in_specs=[
            pl.BlockSpec((8, D_MODEL), lambda l, j: (0, 0)),
            pl.BlockSpec((1, D_MODEL, ADA_TN), lambda l, j: (l, 0, j)),
            pl.BlockSpec((1, 1, ADA_TN), lambda l, j: (l, 0, j)),
        ],
        out_specs=pl.BlockSpec((1, 8, ADA_TN), lambda l, j: (l, 0, j)),
        compiler_params=_cparams(("arbitrary", "arbitrary")),
        name="ada",
    )(cv, mod_w, mod_b.reshape(depth, 1, n))


INPROJ_TM = 512


def _inproj_kernel(x_ref, g_ref, mod_ref, w_ref, qkv_ref, z_ref, xbc_ref, dt_ref):
    m = mod_ref[0]
    h = _rms(x_ref[...], g_ref[...]) * (1 + m[1:2]) + m[0:1]
    p = _dot(h.astype(BF16), w_ref[...])
    qkv_ref[...] = p[:, 0:QKV_DIM]
    z_ref[...] = p[:, QKV_DIM:QKV_DIM + SSD_D_INNER]
    xbc_ref[...] = p[:, QKV_DIM + SSD_D_INNER:QKV_DIM + SSD_D_INNER + SSD_CONV_DIM]
    dt_ref[...] = p[:, QKV_DIM + SSD_D_INNER + SSD_CONV_DIM:AB_PAD]


def _inproj(x, g, mod, w_pad):
    tm = INPROJ_TM
    row = lambda i: (i, 0)
    return pl.pallas_call(
        _inproj_kernel,
        out_shape=(
            jax.ShapeDtypeStruct((R_ALL, QKV_DIM), F32),
            jax.ShapeDtypeStruct((R_ALL, SSD_D_INNER), F32),
            jax.ShapeDtypeStruct((R_ALL, SSD_CONV_DIM), F32),
            jax.ShapeDtypeStruct((R_ALL, DT_PAD), F32),
        ),
        grid=(R_ALL // tm,),
        in_specs=[
            pl.BlockSpec((tm, D_MODEL), row),
            pl.BlockSpec((1, D_MODEL), lambda i: (0, 0)),
            pl.BlockSpec((1, 8, D_MODEL), lambda i: (_mod_index(i, tm), 0, 0)),
            pl.BlockSpec((D_MODEL, AB_PAD), lambda i: (0, 0)),
        ],
        out_specs=(
            pl.BlockSpec((tm, QKV_DIM), row),
            pl.BlockSpec((tm, SSD_D_INNER), row),
            pl.BlockSpec((tm, SSD_CONV_DIM), row),
            pl.BlockSpec((tm, DT_PAD), row),
        ),
        compiler_params=_cparams(("arbitrary",)),
        name="inproj",
    )(x, g, mod, w_pad)


QK_TM = 256


def _qkprep_kernel(qkv_ref, qg_ref, kg_ref, cos_ref, sin_ref, hs_ref, qn_ref, kn_ref):
    cos = cos_ref[...]
    sin = sin_ref[...]
    hs = hs_ref[...]

    def norm_rope(x, g, reps):
        ms = _head_allsum(x * x, hs) * (1.0 / HEAD_DIM)
        y = x * lax.rsqrt(ms + RMS_EPS) * g
        n = y.shape[-1]
        lane = lax.broadcasted_iota(jnp.int32, y.shape, 1)
        half = HEAD_DIM // 2
        swapped = jnp.where((lane & half) == 0, pltpu.roll(y, n - half, axis=1), pltpu.roll(y, half, axis=1))
        c = jnp.concatenate([cos] * reps, axis=1) if reps > 1 else cos
        s = jnp.concatenate([sin] * reps, axis=1) if reps > 1 else sin
        return y * c + swapped * s

    q = qkv_ref[:, 0:ATTN_Q_DIM]
    k = qkv_ref[:, ATTN_Q_DIM:ATTN_Q_DIM + ATTN_KV_DIM]
    qn_ref[...] = norm_rope(q, qg_ref[...], ATTN_Q_DIM // LANES)
    kn_ref[...] = norm_rope(k, kg_ref[...], 1)


def _qkprep(qkv, qg, kg, cos_t, sin_t):
    tm = QK_TM
    n_ctx = R_CTX // tm
    per = DEC_SEQ // tm
    tab = lambda i: (jnp.where(i < n_ctx, 0, 1 + jnp.maximum(i - n_ctx, 0) % per), 0)
    return pl.pallas_call(
        _qkprep_kernel,
        out_shape=(
            jax.ShapeDtypeStruct((R_ALL, ATTN_Q_DIM), F32),
            jax.ShapeDtypeStruct((R_ALL, ATTN_KV_DIM), F32),
        ),
        grid=(R_ALL // tm,),
        in_specs=[
            pl.BlockSpec((tm, QKV_DIM), lambda i: (i, 0)),
            pl.BlockSpec((1, ATTN_Q_DIM), lambda i: (0, 0)),
            pl.BlockSpec((1, ATTN_KV_DIM), lambda i: (0, 0)),
            pl.BlockSpec((tm, LANES), tab),
            pl.BlockSpec((tm, LANES), tab),
            pl.BlockSpec((LANES, LANES), lambda i: (0, 0)),
        ],
        out_specs=(
            pl.BlockSpec((tm, ATTN_Q_DIM), lambda i: (i, 0)),
            pl.BlockSpec((tm, ATTN_KV_DIM), lambda i: (i, 0)),
        ),
        compiler_params=_cparams(("arbitrary",)),
        name="qkprep",
    )(qkv, qg, kg, cos_t, sin_t, _head_mask())


def _attn_core(q, ks, vs):
    tq = q.shape[0]
    grp = ATTN_HEADS // ATTN_KV_HEADS
    scale = HEAD_DIM ** -0.5
    outs = []
    for g in range(ATTN_KV_HEADS):
        sl = slice(g * HEAD_DIM, (g + 1) * HEAD_DIM)
        qs = jnp.concatenate(
            [q[:, (g * grp + j) * HEAD_DIM:(g * grp + j + 1) * HEAD_DIM] for j in range(grp)], axis=0).astype(BF16)
        ss = [lax.dot_general(qs, k[:, sl].astype(BF16), (((1,), (1,)), ((), ())),
                              preferred_element_type=F32) * scale for k in ks]
        m = ss[0].max(axis=-1, keepdims=True)
        for s in ss[1:]:
            m = jnp.maximum(m, s.max(axis=-1, keepdims=True))
        ps = [jnp.exp(s - m) for s in ss]
        l = ps[0].sum(axis=-1, keepdims=True)
        for p in ps[1:]:
            l = l + p.sum(axis=-1, keepdims=True)
        inv = 1.0 / l
        o = None
        for p, v in zip(ps, vs):
            t = _dot((p * inv).astype(BF16), v[:, sl].astype(BF16))
            o = t if o is None else o + t
        outs += [o[j * tq:(j + 1) * tq] for j in range(grp)]
    return jnp.concatenate(outs, axis=1)


def _attn_ctx_kernel(q_ref, k_ref, v_ref, o_ref):
    o_ref[...] = _attn_core(q_ref[...], [k_ref[...]], [v_ref[...]]).astype(BF16)


def _attn_smp_kernel(q_ref, k_ref, v_ref, ck_ref, cv_ref, o_ref):
    ks = [ck_ref[0], k_ref[...]]
    vs = [cv_ref[0], v_ref[...]]
    o_ref[...] = _attn_core(q_ref[...], ks, vs).astype(BF16)


ATTN_TQ = 128


def _attention(qn, kn, qkv, cache_k, cache_v):
    v_col = (ATTN_Q_DIM + ATTN_KV_DIM) // ATTN_KV_DIM
    ctx = pl.pallas_call(
        _attn_ctx_kernel,
        out_shape=jax.ShapeDtypeStruct((R_CTX, ATTN_Q_DIM), BF16),
        grid=(BATCH,),
        in_specs=[
            pl.BlockSpec((SEQ, ATTN_Q_DIM), lambda b: (b, 0)),
            pl.BlockSpec((SEQ, ATTN_KV_DIM), lambda b: (b, 0)),
            pl.BlockSpec((SEQ, ATTN_KV_DIM), lambda b: (b, v_col)),
        ],
        out_specs=pl.BlockSpec((SEQ, ATTN_Q_DIM), lambda b: (b, 0)),
        compiler_params=_cparams(("arbitrary",)),
        name="attn_ctx",
    )(qn, kn, qkv)
    nq = DEC_SEQ // ATTN_TQ
    q0 = R_CTX // ATTN_TQ
    s0 = R_CTX // DEC_SEQ
    smp = pl.pallas_call(
        _attn_smp_kernel,
        out_shape=jax.ShapeDtypeStruct((R_SMP, ATTN_Q_DIM), BF16),
        grid=(DEC_BATCH, nq),
        in_specs=[
            pl.BlockSpec((ATTN_TQ, ATTN_Q_DIM), lambda b, i: (q0 + b * nq + i, 0)),
            pl.BlockSpec((DEC_SEQ, ATTN_KV_DIM), lambda b, i: (s0 + b, 0)),
            pl.BlockSpec((DEC_SEQ, ATTN_KV_DIM), lambda b, i: (s0 + b, v_col)),
            pl.BlockSpec((1, PAST_LEN, ATTN_KV_DIM), lambda b, i: (b, 0, 0)),
            pl.BlockSpec((1, PAST_LEN, ATTN_KV_DIM), lambda b, i: (b, 0, 0)),
        ],
        out_specs=pl.BlockSpec((ATTN_TQ, ATTN_Q_DIM), lambda b, i: (b * nq + i, 0)),
        compiler_params=_cparams(("arbitrary", "arbitrary")),
        name="attn_smp",
    )(qn, kn, qkv, cache_k, cache_v)
    return ctx, smp


CONV_HALO = 8


def _cumsum_rows(a, reverse):
    n = a.shape[0]
    row = lax.broadcasted_iota(jnp.int32, a.shape, 0)
    s = 1
    while s < n:
        if reverse:
            a = a + jnp.where(row < n - s, pltpu.roll(a, n - s, axis=0), 0.0)
        else:
            a = a + jnp.where(row >= s, pltpu.roll(a, s, axis=0), 0.0)
        s *= 2
    return a


def _ssd_kernel(z_ref, xbc_ref, dt_ref, cw_ref, cb_ref, dtb_ref, a_ref, d_ref, g_ref, h0f_ref, h0b_ref,
                y_ref, hf_ref, hb_ref, pad_sc, xc_sc, dt_sc, y_sc, h_sc, *, seq):
    L = SSD_CHUNK
    nc = seq // L
    pad = SSD_CONV_K // 2
    zeros = jnp.zeros((CONV_HALO, SSD_CONV_DIM), F32)
    pad_sc[0:CONV_HALO, :] = zeros
    pad_sc[seq + CONV_HALO:seq + 2 * CONV_HALO, :] = zeros
    pad_sc[CONV_HALO:seq + CONV_HALO, :] = xbc_ref[...]
    h_sc[0] = h0f_ref[0]
    h_sc[1] = h0b_ref[0]

    def conv_chunk(c, carry):
        r0 = pl.multiple_of(c * L, L)
        win = pad_sc[pl.ds(r0, L + 2 * CONV_HALO), :]
        acc = cb_ref[...]
        for i in range(SSD_CONV_K):
            acc = acc + win[CONV_HALO - pad + i:CONV_HALO - pad + i + L, :] * cw_ref[i:i + 1, :]
        xc = _silu(acc)
        xc_sc[pl.ds(r0, L), :] = xc
        y_sc[pl.ds(r0, L), :] = xc[:, 0:SSD_D_INNER] * d_ref[...]
        dt_sc[pl.ds(r0, L), :] = _softplus(dt_ref[pl.ds(r0, L), :] + dtb_ref[...])
        return carry

    lax.fori_loop(0, nc, conv_chunk, 0)

    rr = lax.broadcasted_iota(jnp.int32, (L, L), 0)
    cc = lax.broadcasted_iota(jnp.int32, (L, L), 1)
    grp = SSD_HEADS // SSD_GROUPS
    P = SSD_HEAD_DIM
    N = SSD_STATE

    def make_chunk(dirn):
        mask = (rr >= cc) if dirn == 0 else (rr <= cc)

        def chunk(ci, carry):
            c = ci if dirn == 0 else nc - 1 - ci
            r0 = pl.multiple_of(c * L, L)
            xs = xc_sc[pl.ds(r0, L), 0:SSD_D_INNER]
            bm = xc_sc[pl.ds(r0, L), SSD_D_INNER:SSD_D_INNER + SSD_GROUPS * N]
            cm = xc_sc[pl.ds(r0, L), SSD_D_INNER + SSD_GROUPS * N:SSD_CONV_DIM]
            dtc = dt_sc[pl.ds(r0, L), :]
            acs = _cumsum_rows(dtc * a_ref[...], reverse=(dirn == 1))
            acs_t = acs.T
            tot = acs[L - 1:L, :] if dirn == 0 else acs[0:1, :]
            dec_end = jnp.exp(tot - acs)
            eacs = jnp.exp(acs)
            cdec = jnp.exp(tot)
            cb = [lax.dot_general(cm[:, g * N:(g + 1) * N].astype(BF16), bm[:, g * N:(g + 1) * N].astype(BF16),
                                  (((1,), (1,)), ((), ())), preferred_element_type=F32)
                  for g in range(SSD_GROUPS)]
            for h in range(SSD_HEADS):
                g = h // grp
                ln = h + SSD_HEADS * dirn
                seg = jnp.exp(jnp.where(mask, acs[:, ln:ln + 1] - acs_t[ln:ln + 1, :], -jnp.inf))
                sc = (cb[g] * seg).astype(BF16)
                xdt = (xs[:, h * P:(h + 1) * P] * dtc[:, ln:ln + 1]).astype(BF16)
                hin = h_sc[dirn, h]
                ce = (cm[:, g * N:(g + 1) * N] * eacs[:, ln:ln + 1]).astype(BF16)
                yh = _dot(sc, xdt) + lax.dot_general(ce, hin.astype(BF16), (((1,), (1,)), ((), ())),
                                                     preferred_element_type=F32)
                bd = (bm[:, g * N:(g + 1) * N] * dec_end[:, ln:ln + 1]).astype(BF16)
                st = lax.dot_general(xdt, bd, (((0,), (0,)), ((), ())), preferred_element_type=F32)
                h_sc[dirn, h] = hin * cdec[:, ln:ln + 1] + st
                y_sc[pl.ds(r0, L), h * P:(h + 1) * P] += yh
            return carry

        return chunk

    fwd_chunk, bwd_chunk = make_chunk(0), make_chunk(1)

    def both(ci, carry):
        return bwd_chunk(ci, fwd_chunk(ci, carry))

    lax.fori_loop(0, nc, both, 0)

    def out_chunk(c, carry):
        r0 = pl.multiple_of(c * L, L)
        y = y_sc[pl.ds(r0, L), :] * _silu(z_ref[pl.ds(r0, L), :])
        y_ref[pl.ds(r0, L), :] = _rms(y, g_ref[...]).astype(BF16)
        return carry

    lax.fori_loop(0, nc, out_chunk, 0)
    hf_ref[0] = h_sc[0]
    hb_ref[0] = h_sc[1]


def _ssd(z, xbc, dt, cw, cb, dtb, a_row, d_row, g, h0f, h0b, *, seq, nb, row0):
    blk0 = row0 // seq
    row = lambda b: (blk0 + b, 0)
    fixed = lambda b: (0, 0)
    st = lambda b: (b, 0, 0, 0)
    st_shape = (nb, SSD_HEADS, SSD_HEAD_DIM, SSD_STATE)
    st_blk = (1, SSD_HEADS, SSD_HEAD_DIM, SSD_STATE)
    return pl.pallas_call(
        functools.partial(_ssd_kernel, seq=seq),
        out_shape=(
            jax.ShapeDtypeStruct((nb * seq, SSD_D_INNER), BF16),
            jax.ShapeDtypeStruct(st_shape, F32),
            jax.ShapeDtypeStruct(st_shape, F32),
        ),
        grid=(nb,),
        in_specs=[
            pl.BlockSpec((seq, SSD_D_INNER), row),
            pl.BlockSpec((seq, SSD_CONV_DIM), row),
            pl.BlockSpec((seq, DT_PAD), row),
            pl.BlockSpec((SSD_CONV_K, SSD_CONV_DIM), fixed),
            pl.BlockSpec((1, SSD_CONV_DIM), fixed),
            pl.BlockSpec((1, DT_PAD), fixed),
            pl.BlockSpec((1, DT_PAD), fixed),
            pl.BlockSpec((1, SSD_D_INNER), fixed),
            pl.BlockSpec((1, SSD_D_INNER), fixed),
            pl.BlockSpec(st_blk, st),
            pl.BlockSpec(st_blk, st),
        ],
        out_specs=(
            pl.BlockSpec((seq, SSD_D_INNER), lambda b: (b, 0)),
            pl.BlockSpec(st_blk, st),
            pl.BlockSpec(st_blk, st),
        ),
        scratch_shapes=[
            pltpu.VMEM((seq + 2 * CONV_HALO, SSD_CONV_DIM), F32),
            pltpu.VMEM((seq, SSD_CONV_DIM), F32),
            pltpu.VMEM((seq, DT_PAD), F32),
            pltpu.VMEM((seq, SSD_D_INNER), F32),
            pltpu.VMEM((2, SSD_HEADS, SSD_HEAD_DIM, SSD_STATE), F32),
        ],
        compiler_params=_cparams(("arbitrary",)),
        name=f"ssd_{seq}",
    )(z, xbc, dt, cw, cb, dtb, a_row, d_row, g, h0f, h0b)


RES_TM = 512


def _mixres_kernel(x_ref, mod_ref, a1_ref, a2_ref, w_ref, o_ref):
    k1 = a1_ref.shape[1]
    out = _dot(a1_ref[...], w_ref[0:k1, :]) + _dot(a2_ref[...], w_ref[k1:, :])
    o_ref[...] = x_ref[...] + mod_ref[0][2:3] * out


def _mixres(x, mod, a1, a2, w):
    tm = RES_TM
    row = lambda i: (i, 0)
    return pl.pallas_call(
        _mixres_kernel,
        out_shape=jax.ShapeDtypeStruct((R_ALL, D_MODEL), F32),
        grid=(R_ALL // tm,),
        in_specs=[
            pl.BlockSpec((tm, D_MODEL), row),
            pl.BlockSpec((1, 8, D_MODEL), lambda i: (_mod_index(i, tm), 0, 0)),
            pl.BlockSpec((tm, a1.shape[1]), row),
            pl.BlockSpec((tm, a2.shape[1]), row),
            pl.BlockSpec(w.shape, lambda i: (0, 0)),
        ],
        out_specs=pl.BlockSpec((tm, D_MODEL), row),
        compiler_params=_cparams(("arbitrary",)),
        name="mixres",
    )(x, mod, a1, a2, w)


FFN_TM = 1024
FFN_TF = 256


def _ffn_kernel(x_ref, g_ref, mod_ref, wg_ref, wu_ref, wd_ref, fg_ref, o_ref, h_sc, acc_sc, *, final):
    j = pl.program_id(1)

    @pl.when(j == 0)
    def _():
        m = mod_ref[0]
        h = _rms(x_ref[...], g_ref[...]) * (1 + m[4:5]) + m[3:4]
        h_sc[...] = h.astype(BF16)
        acc_sc[...] = jnp.zeros_like(acc_sc)

    h = h_sc[...]
    hid = _silu(_dot(h, wg_ref[...])) * _dot(h, wu_ref[...])
    acc_sc[...] += _dot(hid.astype(BF16), wd_ref[...])

    @pl.when(j == pl.num_programs(1) - 1)
    def _():
        y = x_ref[...] + mod_ref[0][5:6] * acc_sc[...]
        if final:
            y = _rms(y, fg_ref[...])
        o_ref[...] = y


def _ffn(x, g, mod, wg, wu, wd, fg, *, final):
    tm, tf = FFN_TM, FFN_TF
    row = lambda i, j: (i, 0)
    return pl.pallas_call(
        functools.partial(_ffn_kernel, final=final),
        out_shape=jax.ShapeDtypeStruct((R_ALL, D_MODEL), F32),
        grid=(R_ALL // tm, FFN_DIM // tf),
        in_specs=[
            pl.BlockSpec((tm, D_MODEL), row),
            pl.BlockSpec((1, D_MODEL), lambda i, j: (0, 0)),
            pl.BlockSpec((1, 8, D_MODEL), lambda i, j: (_mod_index(i, tm), 0, 0)),
            pl.BlockSpec((D_MODEL, tf), lambda i, j: (0, j)),
            pl.BlockSpec((D_MODEL, tf), lambda i, j: (0, j)),
            pl.BlockSpec((tf, D_MODEL), lambda i, j: (j, 0)),
            pl.BlockSpec((1, D_MODEL), lambda i, j: (0, 0)),
        ],
        out_specs=pl.BlockSpec((tm, D_MODEL), row),
        scratch_shapes=[pltpu.VMEM((tm, D_MODEL), BF16), pltpu.VMEM((tm, D_MODEL), F32)],
        compiler_params=_cparams(("arbitrary", "arbitrary")),
        name="ffn_final" if final else "ffn",
    )(x, g, mod, wg, wu, wd, fg)


RW_TM = 128
HALO = 8
N_OPS = 9


def _rwkv_pre_kernel(x_ref, xp_ref, xn_ref, g_ref, mod_ref, mu_ref, wr_ref, wk_ref, wv_ref, w1_ref, w2_ref,
                     a1_ref, a2_ref, g1_ref, g2_ref, w0_ref, a0_ref, kk_ref, ka_ref, rk_ref, hs_ref,
                     ops_o, g_o, bon_o):
    i = pl.program_id(0)
    tm = RW_TM
    n_ctx = R_CTX // tm
    per_c = SEQ // tm
    per_s = DEC_SEQ // tm
    rel = jnp.where(i < n_ctx, i % per_c, jnp.maximum(i - n_ctx, 0) % per_s)
    last = jnp.where(i < n_ctx, per_c - 1, per_s - 1)
    m = mod_ref[0]

    def nm(x):
        return _rms(x, g_ref[...]) * (1 + m[1:2]) + m[0:1]

    h = nm(x_ref[...])
    prev_row = jnp.where(rel == 0, 0.0, nm(xp_ref[...])[HALO - 1:HALO, :])
    next_row = jnp.where(rel == last, 0.0, nm(xn_ref[...])[0:1, :])
    row = lax.broadcasted_iota(jnp.int32, h.shape, 0)
    hp = jnp.where(row == 0, prev_row, pltpu.roll(h, 1, axis=0))
    hn = jnp.where(row == tm - 1, next_row, pltpu.roll(h, tm - 1, axis=0))
    dp = hp - h
    dn = hn - h

    def mix(idx):
        return (h + dp * mu_ref[0, idx:idx + 1, :] + dn * mu_ref[1, idx:idx + 1, :]).astype(BF16)

    r = _dot(mix(0), wr_ref[...])
    k = _dot(mix(2), wk_ref[...])
    v = _dot(mix(3), wv_ref[...])
    lw = jnp.tanh(_dot(mix(1), w1_ref[...])).astype(BF16)
    la = _dot(mix(4), a1_ref[...]).astype(BF16)
    gg = _dot(_sigmoid(_dot(mix(5), g1_ref[...])).astype(BF16), g2_ref[...])

    hs = hs_ref[...]
    kk = k * kk_ref[...]
    kk = kk * lax.rsqrt(_head_allsum(kk * kk, hs) + L2_EPS)
    ops_o[0] = r
    ops_o[1] = kk
    ops_o[2] = v
    g_o[...] = gg
    bsum = None
    for j in range(2):
        wl = w0_ref[j:j + 1, :] + _dot(lw, w2_ref[j])
        ops_o[3 + 3 * j] = jnp.exp(-jnp.exp(-_softplus(-wl) - 0.5))
        a = _sigmoid(a0_ref[j:j + 1, :] + _dot(la, a2_ref[j]))
        kd = k * (1 + (a - 1) * ka_ref[...])
        ops_o[4 + 3 * j] = kd
        ops_o[5 + 3 * j] = kk * a
        t = r * kd * rk_ref[...]
        bsum = t if bsum is None else bsum + t
    bon_o[...] = _head_allsum(bsum, hs) * v


def _rwkv_pre(x, g, mod, mu, wr, wk, wv, w1, w2, a1, a2, g1, g2, w0, a0, k_k, k_a, r_k, hs):
    tm = RW_TM
    nblk = R_ALL // HALO
    per = tm // HALO
    row = lambda i: (i, 0)
    fixed2 = lambda i: (0, 0)
    fixed3 = lambda i: (0, 0, 0)
    full = lambda a: pl.BlockSpec(a.shape, fixed2 if a.ndim == 2 else fixed3)
    out = jax.ShapeDtypeStruct((R_ALL, D_MODEL), F32)
    orow = pl.BlockSpec((tm, D_MODEL), row)
    return pl.pallas_call(
        _rwkv_pre_kernel,
        out_shape=(jax.ShapeDtypeStruct((N_OPS, R_ALL, D_MODEL), F32), out, out),
        grid=(R_ALL // tm,),
        in_specs=[
            pl.BlockSpec((tm, D_MODEL), row),
            pl.BlockSpec((HALO, D_MODEL), lambda i: (jnp.maximum(i * per - 1, 0), 0)),
            pl.BlockSpec((HALO, D_MODEL), lambda i: (jnp.minimum((i + 1) * per, nblk - 1), 0)),
            pl.BlockSpec((1, D_MODEL), fixed2),
            pl.BlockSpec((1, 8, D_MODEL), lambda i: (_mod_index(i, tm), 0, 0)),
            full(mu), full(wr), full(wk), full(wv), full(w1), full(w2), full(a1), full(a2), full(g1), full(g2),
            full(w0), full(a0), full(k_k), full(k_a), full(r_k), full(hs),
        ],
        out_specs=(pl.BlockSpec((N_OPS, tm, D_MODEL), lambda i: (0, i, 0)), orow, orow),
        compiler_params=_cparams(("arbitrary",)),
        name="rwkv_pre",
    )(x, x, x, g, mod, mu, wr, wk, wv, w1, w2, a1, a2, g1, g2, w0, a0, k_k, k_a, r_k, hs)


SCAN_TT = 32
SCAN_ACC = 4
NK = RWKV_HEAD_DIM


def _scan_kernel(kt_ref, vt_ref, dk_ref, s0_ref, y_ref, st_ref, s_sc):
    d = pl.program_id(0)
    j = pl.program_id(2)

    @pl.when(j == 0)
    def _():
        s_sc[...] = s0_ref[0]

    def tix(i):
        return jnp.where(d == 0, i, SCAN_TT - 1 - i)

    def project(i):
        t = tix(i)
        parts = [None] * SCAN_ACC
        for k in range(NK):
            term = s_sc[k] * kt_ref[1, t, k:k + 1, :]
            parts[k % SCAN_ACC] = term if parts[k % SCAN_ACC] is None else parts[k % SCAN_ACC] + term
        while len(parts) > 1:
            parts = [parts[a] + parts[a + 1] for a in range(0, len(parts), 2)]
        return parts[0]

    def advance(i, sa, with_next):
        t = tix(i)
        tn = tix(i + 1)
        vv = vt_ref[0, t]
        y = None
        sa_next = None
        for k in range(NK):
            sn = (s_sc[k] * dk_ref[0, t, k:k + 1, :] - sa * dk_ref[2, t, k:k + 1, :]
                  + vv * dk_ref[1, t, k:k + 1, :])
            s_sc[k] = sn
            yk = sn * kt_ref[0, t, k:k + 1, :]
            y = yk if y is None else y + yk
            if with_next:
                ak = sn * kt_ref[1, tn, k:k + 1, :]
                sa_next = ak if sa_next is None else sa_next + ak
        y_ref[0, t] = y
        return sa_next

    sa_last = lax.fori_loop(0, SCAN_TT - 1, lambda i, sa: advance(i, sa, True), project(0))
    advance(SCAN_TT - 1, sa_last, False)

    @pl.when(j == pl.num_programs(2) - 1)
    def _():
        st_ref[0] = s_sc[...]


def _scan(kt, vt, dk, s0, *, vt_slab=0, dk_blk=0):
    _, seq, nv, chains = vt.shape
    tt = SCAN_TT
    nt = seq // tt
    tb = lambda d, j: jnp.where(d == 0, j, nt - 1 - j)
    sblk = pl.BlockSpec((1, NK, nv, LANES), lambda d, c, j: (d, 0, 0, c))
    yblk = pl.BlockSpec((1, tt, nv, LANES), lambda d, c, j: (d, tb(d, j), 0, c))
    return pl.pallas_call(
        _scan_kernel,
        out_shape=(jax.ShapeDtypeStruct((2, seq, nv, chains), F32), jax.ShapeDtypeStruct((2, NK, nv, chains), F32)),
        grid=(2, chains // LANES, nt),
        in_specs=[
            pl.BlockSpec((2, tt, NK, LANES), lambda d, c, j: (0, tb(d, j), 0, c)),
            pl.BlockSpec((1, tt, nv, LANES), lambda d, c, j: (vt_slab, tb(d, j), 0, c)),
            pl.BlockSpec((3, tt, NK, LANES), lambda d, c, j: (dk_blk + d, tb(d, j), 0, c)),
            sblk,
        ],
        out_specs=(yblk, sblk),
        scratch_shapes=[pltpu.VMEM((NK, nv, LANES), F32)],
        compiler_params=_cparams(("arbitrary", "arbitrary", "arbitrary")),
        name=f"rwkv_scan_{seq}",
    )(kt, vt, dk, s0)


POST_TM = 256


def _rwkv_post_kernel(x_ref, mod_ref, yc_ref, ys_ref, bon_ref, g_ref, lng_ref, lnb_ref, wo_ref, hs_ref, o_ref):
    is_ctx = pl.program_id(0) < R_CTX // POST_TM
    y = jnp.where(is_ctx, yc_ref[0] + yc_ref[1], ys_ref[0] + ys_ref[1])
    inv = 1.0 / RWKV_HEAD_DIM
    hs = hs_ref[...]
    d = y - _head_allsum(y, hs) * inv
    var = _head_allsum(d * d, hs) * inv
    yn = d * lax.rsqrt(var + GN_EPS) * lng_ref[...] + lnb_ref[...]
    out = ((yn + bon_ref[...]) * g_ref[...]).astype(BF16)
    o_ref[...] = x_ref[...] + mod_ref[0][2:3] * _dot(out, wo_ref[...])


def _rwkv_post(x, mod, yc, ys, bon, g, lng, lnb, wo, hs):
    tm = POST_TM
    n_ctx = R_CTX // tm
    row = lambda i: (i, 0)
    fixed = lambda i: (0, 0)
    big = pl.BlockSpec((tm, D_MODEL), row)
    ycb = pl.BlockSpec((2, tm, D_MODEL), lambda i: (0, jnp.minimum(i, n_ctx - 1), 0))
    ysb = pl.BlockSpec((2, tm, D_MODEL), lambda i: (0, jnp.maximum(i - n_ctx, 0), 0))
    return pl.pallas_call(
        _rwkv_post_kernel,
        out_shape=jax.ShapeDtypeStruct((R_ALL, D_MODEL), F32),
        grid=(R_ALL // tm,),
        in_specs=[
            big,
            pl.BlockSpec((1, 8, D_MODEL), lambda i: (_mod_index(i, tm), 0, 0)),
            ycb, ysb, big, big,
            pl.BlockSpec((1, D_MODEL), fixed),
            pl.BlockSpec((1, D_MODEL), fixed),
            pl.BlockSpec((D_MODEL, D_MODEL), fixed),
            pl.BlockSpec((LANES, LANES), fixed),
        ],
        out_specs=big,
        compiler_params=_cparams(("arbitrary",)),
        name="rwkv_post",
    )(x, mod, yc, ys, bon, g, lng, lnb, wo, hs)


RL_T = 32
NPAIR = D_MODEL // LANES


def _to_chains_kernel(x_ref, o_ref):
    cols = [jnp.swapaxes(x_ref[0, :, :, p * LANES:(p + 1) * LANES], 0, 1) for p in range(NPAIR)]
    for t in range(RL_T):
        tile = jnp.concatenate([c[t] for c in cols], axis=0).T
        o_ref[0, t, :, 0:LANES] = tile[0:NK]
        o_ref[0, t, :, LANES:2 * LANES] = tile[NK:2 * NK]


def _from_chains_kernel(y_ref, o_ref):
    tiles = []
    for t in range(RL_T):
        m = jnp.concatenate([y_ref[0, t, :, 0:LANES], y_ref[0, t, :, LANES:2 * LANES]], axis=0)
        tiles.append(m.T)
    for p in range(NPAIR):
        blk = jnp.stack([m[p * BATCH:(p + 1) * BATCH] for m in tiles], axis=0)
        o_ref[0, :, :, p * LANES:(p + 1) * LANES] = jnp.swapaxes(blk, 0, 1)


def _to_chains(ops):
    n = ops.shape[0]
    return pl.pallas_call(
        _to_chains_kernel,
        out_shape=jax.ShapeDtypeStruct((n, SEQ, NK, 2 * LANES), F32),
        grid=(n, SEQ // RL_T),
        in_specs=[pl.BlockSpec((1, BATCH, RL_T, D_MODEL), lambda o, j: (o, 0, j, 0))],
        out_specs=pl.BlockSpec((1, RL_T, NK, 2 * LANES), lambda o, j: (o, j, 0, 0)),
        compiler_params=_cparams(("arbitrary", "arbitrary")),
        name="to_chains",
    )(ops.reshape(n, R_ALL // SEQ, SEQ, D_MODEL))


def _from_chains(y):
    n = y.shape[0]
    out = pl.pallas_call(
        _from_chains_kernel,
        out_shape=jax.ShapeDtypeStruct((n, BATCH, SEQ, D_MODEL), F32),
        grid=(n, SEQ // RL_T),
        in_specs=[pl.BlockSpec((1, RL_T, NK, 2 * LANES), lambda o, j: (o, j, 0, 0))],
        out_specs=pl.BlockSpec((1, BATCH, RL_T, D_MODEL), lambda o, j: (o, 0, j, 0)),
        compiler_params=_cparams(("arbitrary", "arbitrary")),
        name="from_chains",
    )(y)
    return out.reshape(n, R_CTX, D_MODEL)


def _scan_ctx(ops):
    ch = _to_chains(ops)
    s0 = jnp.zeros((2, NK, NK, 2 * LANES), F32)
    y, st = _scan(ch, ch, ch, s0, vt_slab=2, dk_blk=1)
    st = st.reshape(2, NK, NK, 2, NPAIR, BATCH).transpose(0, 5, 4, 3, 2, 1)
    return _from_chains(y), st.reshape(2, BATCH, RWKV_HEADS, NK, NK)


def _scan_smp(ops, s0):
    H = RWKV_HEADS
    nb, seq = DEC_BATCH, DEC_SEQ
    nch = nb * H
    vq = LANES // nch
    nv = NK // vq
    rows = slice(R_CTX, R_ALL)

    def ktype(a):
        n = a.shape[0]
        a = a.reshape(n, nb, seq, H, NK).transpose(0, 2, 4, 1, 3).reshape(n, seq, NK, 1, nch)
        return jnp.broadcast_to(a, (n, seq, NK, vq, nch)).reshape(n, seq, NK, vq * nch)

    kt = ktype(ops[0:2, rows])
    dk = ktype(ops[3:N_OPS, rows])
    vt = ops[2, rows].reshape(nb, seq, H, vq, nv).transpose(1, 4, 3, 0, 2).reshape(1, seq, nv, vq * nch)
    s0 = s0.reshape(2, nb, H, vq, nv, NK).transpose(0, 5, 4, 3, 1, 2).reshape(2, NK, nv, vq * nch)
    y, _ = _scan(kt, vt, dk, s0)
    return y.reshape(2, seq, nv, vq, nb, H).transpose(0, 4, 1, 5, 3, 2).reshape(2, nb * seq, D_MODEL)


def _rope_tables():
    rows = DEC_SEQ // GRID_W
    row = jnp.repeat(jnp.arange(rows, dtype=F32), GRID_W)
    col = jnp.tile(jnp.arange(GRID_W, dtype=F32), rows)
    n_freq = HEAD_DIM // 4
    inv_freq = ROPE_THETA ** (-jnp.arange(n_freq, dtype=F32) / n_freq)
    ang = jnp.concatenate([row[:, None] * inv_freq, col[:, None] * inv_freq], axis=-1)
    cos, sin = jnp.cos(ang), jnp.sin(ang)
    reps = LANES // HEAD_DIM
    cos_t = jnp.tile(jnp.concatenate([cos, cos], axis=-1), (1, reps))
    sin_t = jnp.tile(jnp.concatenate([-sin, sin], axis=-1), (1, reps))
    ident_c = jnp.ones((QK_TM, LANES), F32)
    ident_s = jnp.zeros((QK_TM, LANES), F32)
    return jnp.concatenate([ident_c, cos_t], axis=0), jnp.concatenate([ident_s, sin_t], axis=0)


def kernel(x_prompt, x_sample, cache_attn_k, cache_attn_v, state_ssd_fwd, state_ssd_bwd, state_rwkv_fwd, state_rwkv_bwd, c, c_ctx, mod_w, mod_b, norm_mix_g, norm_ffn_g, ffn_w_gate, ffn_w_up, ffn_w_down, ab_w_in, ab_w_out, attn_q_g, attn_k_g, ssd_conv_w, ssd_conv_b, ssd_dt_bias, ssd_a_log, ssd_d, ssd_norm_g, rwkv_mu, rwkv_w_r, rwkv_w_k, rwkv_w_v, rwkv_w0, rwkv_w1, rwkv_w2, rwkv_a0, rwkv_a1, rwkv_a2, rwkv_g1, rwkv_g2, rwkv_k_k, rwkv_k_a, rwkv_r_k, rwkv_ln_g, rwkv_ln_b, rwkv_w_o, final_norm_g):
    bf = lambda a: a.astype(BF16)
    x = jnp.concatenate([x_prompt.reshape(R_CTX, D_MODEL), x_sample.reshape(R_SMP, D_MODEL)], axis=0)

    cv = jnp.concatenate([c_ctx[None], c, jnp.zeros((8 - 1 - DEC_BATCH, D_MODEL), F32)], axis=0)
    m = _ada(cv, mod_w, mod_b)
    m = m[:, :1 + DEC_BATCH].reshape(2, 1 + DEC_BATCH, 6, D_MODEL)
    m = jnp.pad(m, ((0, 0), (0, 0), (0, 2), (0, 0)))
    row2 = lambda a: a.reshape(1, -1)

    w_in = jnp.pad(bf(ab_w_in[0]), ((0, 0), (0, AB_PAD - AB_IN_DIM)))
    qkv, z, xbc, dt = _inproj(x, row2(norm_mix_g[0]), m[0], w_in)
    cos_t, sin_t = _rope_tables()
    qg = jnp.tile(attn_q_g[0], ATTN_HEADS).reshape(1, -1)
    kg = jnp.tile(attn_k_g[0], ATTN_KV_HEADS).reshape(1, -1)
    qn, kn = _qkprep(qkv, qg, kg, cos_t, sin_t)
    ck = cache_attn_k[:, 0].reshape(DEC_BATCH, PAST_LEN, ATTN_KV_DIM)
    cvv = cache_attn_v[:, 0].reshape(DEC_BATCH, PAST_LEN, ATTN_KV_DIM)
    attn_c, attn_s = _attention(qn, kn, qkv, ck, cvv)
    attn = jnp.concatenate([attn_c, attn_s], axis=0)

    dtb = jnp.pad(ssd_dt_bias[0].reshape(1, -1), ((0, 0), (0, DT_PAD - 2 * SSD_HEADS)))
    a_row = jnp.pad((-jnp.exp(ssd_a_log[0])).reshape(1, -1), ((0, 0), (0, DT_PAD - 2 * SSD_HEADS)))
    d_row = jnp.repeat(ssd_d[0], SSD_HEAD_DIM).reshape(1, -1)
    ssd_args = (ssd_conv_w[0], row2(ssd_conv_b[0]), dtb, a_row, d_row, row2(ssd_norm_g[0]))
    zero_st = jnp.zeros((BATCH, SSD_HEADS, SSD_HEAD_DIM, SSD_STATE), F32)
    y_c, hf_c, hb_c = _ssd(z, xbc, dt, *ssd_args, zero_st, zero_st, seq=SEQ, nb=BATCH, row0=0)
    y_s, _, _ = _ssd(z, xbc, dt, *ssd_args, state_ssd_fwd[:, 0], state_ssd_bwd[:, 0],
                     seq=DEC_SEQ, nb=DEC_BATCH, row0=R_CTX)
    y_ssd = jnp.concatenate([y_c, y_s], axis=0)
    x = _mixres(x, m[0], attn, y_ssd, bf(ab_w_out[0]))
    x = _ffn(x, row2(norm_ffn_g[0]), m[0], bf(ffn_w_gate[0]), bf(ffn_w_up[0]), bf(ffn_w_down[0]),
             row2(final_norm_g), final=False)

    w1 = bf(jnp.concatenate([rwkv_w1[0, 0], rwkv_w1[0, 1]], axis=1))
    a1 = bf(jnp.concatenate([rwkv_a1[0, 0], rwkv_a1[0, 1]], axis=1))
    zpad = lambda w: bf(jnp.stack([jnp.concatenate([w[0], jnp.zeros_like(w[1])], axis=0),
                                   jnp.concatenate([jnp.zeros_like(w[0]), w[1]], axis=0)]))
    hs = _head_mask()
    ops, gg, bon = _rwkv_pre(x, row2(norm_mix_g[1]), m[1], rwkv_mu[0], bf(rwkv_w_r[0]), bf(rwkv_w_k[0]),
                             bf(rwkv_w_v[0]), w1, zpad(rwkv_w2[0]), a1, zpad(rwkv_a2[0]), bf(rwkv_g1[0]),
                             bf(rwkv_g2[0]), rwkv_w0[0], rwkv_a0[0], row2(rwkv_k_k[0]), row2(rwkv_k_a[0]),
                             rwkv_r_k[0].reshape(1, -1), hs)
    y_c, st_c = _scan_ctx(ops)
    y_s = _scan_smp(ops, jnp.stack([state_rwkv_fwd[:, 0], state_rwkv_bwd[:, 0]]))
    x = _rwkv_post(x, m[1], y_c, y_s, bon, gg, row2(rwkv_ln_g[0]), row2(rwkv_ln_b[0]), bf(rwkv_w_o[0]), hs)
    x = _ffn(x, row2(norm_ffn_g[1]), m[1], bf(ffn_w_gate[1]), bf(ffn_w_up[1]), bf(ffn_w_down[1]),
             row2(final_norm_g), final=True)

    y_prompt = x[:R_CTX].reshape(BATCH, SEQ, D_MODEL)
    y_sample = x[R_CTX:].reshape(DEC_BATCH, DEC_SEQ, D_MODEL)
    new_k = kn[:R_CTX].reshape(BATCH, 1, SEQ, ATTN_KV_HEADS, HEAD_DIM)
    new_v = qkv[:R_CTX, ATTN_Q_DIM + ATTN_KV_DIM:].reshape(BATCH, 1, SEQ, ATTN_KV_HEADS, HEAD_DIM)
    return (y_prompt, y_sample, new_k, new_v, hf_c[:, None], hb_c[:, None], st_c[0][:, None], st_c[1][:, None])
```

```python
import functools

import jax
import jax.numpy as jnp
from jax import lax
from jax.experimental import pallas as pl
from jax.experimental.pallas import tpu as pltpu

F32 = jnp.float32
BF16 = jnp.bfloat16

D_MODEL = 1024
BATCH = 16
SEQ = 256
DEC_BATCH = 2
DEC_SEQ = 1024
PAST_LEN = 256
GRID_W = 64
ATTN_HEADS = 8
ATTN_KV_HEADS = 2
HEAD_DIM = 64
ROPE_THETA = 10000.0
ATTN_Q_DIM = ATTN_HEADS * HEAD_DIM
ATTN_KV_DIM = ATTN_KV_HEADS * HEAD_DIM
SSD_HEADS = 8
SSD_HEAD_DIM = 64
SSD_D_INNER = SSD_HEADS * SSD_HEAD_DIM
SSD_GROUPS = 2
SSD_STATE = 64
SSD_CONV_K = 5
SSD_CHUNK = 128
SSD_CONV_DIM = SSD_D_INNER + 2 * SSD_GROUPS * SSD_STATE
AB_IN_DIM = ATTN_Q_DIM + 2 * ATTN_KV_DIM + SSD_D_INNER + SSD_CONV_DIM + 2 * SSD_HEADS
RWKV_HEAD_DIM = 64
RWKV_HEADS = D_MODEL // RWKV_HEAD_DIM
FFN_DIM = (((8 * D_MODEL + 2) // 3 + 255) // 256) * 256
RMS_EPS = 1e-6
GN_EPS = 64e-5
L2_EPS = 1e-12

R_CTX = BATCH * SEQ
R_SMP = DEC_BATCH * DEC_SEQ
R_ALL = R_CTX + R_SMP
LANES = 128
QKV_DIM = ATTN_Q_DIM + 2 * ATTN_KV_DIM
DT_PAD = LANES
AB_PAD = QKV_DIM + SSD_D_INNER + SSD_CONV_DIM + DT_PAD
VMEM_LIMIT = 56 * 1024 * 1024


def _cparams(sem):
    return pltpu.CompilerParams(dimension_semantics=sem, vmem_limit_bytes=VMEM_LIMIT)


def _mod_index(i, tm):
    n_ctx = R_CTX // tm
    per = DEC_SEQ // tm
    return jnp.where(i < n_ctx, 0, 1 + jnp.maximum(i - n_ctx, 0) // per)


def _sigmoid(x):
    return 1.0 / (1.0 + jnp.exp(-x))


def _silu(x):
    return x * _sigmoid(x)


def _softplus(x):
    return jnp.maximum(x, 0.0) + jnp.log1p(jnp.exp(-jnp.abs(x)))


def _rms(x, g):
    return x * lax.rsqrt(jnp.mean(x * x, axis=-1, keepdims=True) + RMS_EPS) * g


def _head_allsum(x, hs):
    parts = [jnp.dot(x[:, j * LANES:(j + 1) * LANES], hs, precision=lax.Precision.HIGHEST,
                     preferred_element_type=F32) for j in range(x.shape[-1] // LANES)]
    return parts[0] if len(parts) == 1 else jnp.concatenate(parts, axis=1)


def _head_mask():
    lane = jnp.arange(LANES)
    return (lane[:, None] // HEAD_DIM == lane[None, :] // HEAD_DIM).astype(F32)


def _dot(a, b):
    return jnp.dot(a, b, preferred_element_type=F32)


ADA_TN = 1536


def _ada_kernel(c_ref, w_ref, b_ref, o_ref):
    s = _silu(c_ref[...]).astype(BF16)
    o_ref[0] = _dot(s, w_ref[0].astype(BF16)) + b_ref[0]


def _ada(cv, mod_w, mod_b):
    depth = mod_w.shape[0]
    n = mod_w.shape[2]
    return pl.pallas_call(
        _ada_kernel,
        out_shape=jax.ShapeDtypeStruct((depth, 8, n), F32),
        grid=(depth, n // ADA_TN),
        in_specs=[
            pl.BlockSpec((8, D_MODEL), lambda l, j: (0, 0)),
            pl.BlockSpec((1, D_MODEL, ADA_TN), lambda l, j: (l, 0, j)),
            pl.BlockSpec((1, 1, ADA_TN), lambda l, j: (l, 0, j)),
        ],
        out_specs=pl.BlockSpec((1, 8, ADA_TN), lambda l, j: (l, 0, j)),
        compiler_params=_cparams(("arbitrary", "arbitrary")),
        name="ada",
    )(cv, mod_w, mod_b.reshape(depth, 1, n))


INPROJ_TM = 512


def _inproj_kernel(x_ref, g_ref, mod_ref, w_ref, qkv_ref, z_ref, xbc_ref, dt_ref):
    m = mod_ref[0]
    h = _rms(x_ref[...], g_ref[...]) * (1 + m[1:2]) + m[0:1]
    p = _dot(h.astype(BF16), w_ref[...])
    qkv_ref[...] = p[:, 0:QKV_DIM]
    z_ref[...] = p[:, QKV_DIM:QKV_DIM + SSD_D_INNER]
    xbc_ref[...] = p[:, QKV_DIM + SSD_D_INNER:QKV_DIM + SSD_D_INNER + SSD_CONV_DIM]
    dt_ref[...] = p[:, QKV_DIM + SSD_D_INNER + SSD_CONV_DIM:AB_PAD]


def _inproj(x, g, mod, w_pad):
    tm = INPROJ_TM
    row = lambda i: (i, 0)
    return pl.pallas_call(
        _inproj_kernel,
        out_shape=(
            jax.ShapeDtypeStruct((R_ALL, QKV_DIM), F32),
            jax.ShapeDtypeStruct((R_ALL, SSD_D_INNER), F32),
            jax.ShapeDtypeStruct((R_ALL, SSD_CONV_DIM), F32),
            jax.ShapeDtypeStruct((R_ALL, DT_PAD), F32),
        ),
        grid=(R_ALL // tm,),
        in_specs=[
            pl.BlockSpec((tm, D_MODEL), row),
            pl.BlockSpec((1, D_MODEL), lambda i: (0, 0)),
            pl.BlockSpec((1, 8, D_MODEL), lambda i: (_mod_index(i, tm), 0, 0)),
            pl.BlockSpec((D_MODEL, AB_PAD), lambda i: (0, 0)),
        ],
        out_specs=(
            pl.BlockSpec((tm, QKV_DIM), row),
            pl.BlockSpec((tm, SSD_D_INNER), row),
            pl.BlockSpec((tm, SSD_CONV_DIM), row),
            pl.BlockSpec((tm, DT_PAD), row),
        ),
        compiler_params=_cparams(("arbitrary",)),
        name="inproj",
    )(x, g, mod, w_pad)


QK_TM = 256


def _qkprep_kernel(qkv_ref, qg_ref, kg_ref, cos_ref, sin_ref, hs_ref, qn_ref, kn_ref):
    cos = cos_ref[...]
    sin = sin_ref[...]
    hs = hs_ref[...]

    def norm_rope(x, g, reps):
        ms = _head_allsum(x * x, hs) * (1.0 / HEAD_DIM)
        y = x * lax.rsqrt(ms + RMS_EPS) * g
        n = y.shape[-1]
        lane = lax.broadcasted_iota(jnp.int32, y.shape, 1)
        half = HEAD_DIM // 2
        swapped = jnp.where((lane & half) == 0, pltpu.roll(y, n - half, axis=1), pltpu.roll(y, half, axis=1))
        c = jnp.concatenate([cos] * reps, axis=1) if reps > 1 else cos
        s = jnp.concatenate([sin] * reps, axis=1) if reps > 1 else sin
        return y * c + swapped * s

    q = qkv_ref[:, 0:ATTN_Q_DIM]
    k = qkv_ref[:, ATTN_Q_DIM:ATTN_Q_DIM + ATTN_KV_DIM]
    qn_ref[...] = norm_rope(q, qg_ref[...], ATTN_Q_DIM // LANES)
    kn_ref[...] = norm_rope(k, kg_ref[...], 1)


def _qkprep(qkv, qg, kg, cos_t, sin_t):
    tm = QK_TM
    n_ctx = R_CTX // tm
    per = DEC_SEQ // tm
    tab = lambda i: (jnp.where(i < n_ctx, 0, 1 + jnp.maximum(i - n_ctx, 0) % per), 0)
    return pl.pallas_call(
        _qkprep_kernel,
        out_shape=(
            jax.ShapeDtypeStruct((R_ALL, ATTN_Q_DIM), F32),
            jax.ShapeDtypeStruct((R_ALL, ATTN_KV_DIM), F32),
        ),
        grid=(R_ALL // tm,),
        in_specs=[
            pl.BlockSpec((tm, QKV_DIM), lambda i: (i, 0)),
            pl.BlockSpec((1, ATTN_Q_DIM), lambda i: (0, 0)),
            pl.BlockSpec((1, ATTN_KV_DIM), lambda i: (0, 0)),
            pl.BlockSpec((tm, LANES), tab),
            pl.BlockSpec((tm, LANES), tab),
            pl.BlockSpec((LANES, LANES), lambda i: (0, 0)),
        ],
        out_specs=(
            pl.BlockSpec((tm, ATTN_Q_DIM), lambda i: (i, 0)),
            pl.BlockSpec((tm, ATTN_KV_DIM), lambda i: (i, 0)),
        ),
        compiler_params=_cparams(("arbitrary",)),
        name="qkprep",
    )(qkv, qg, kg, cos_t, sin_t, _head_mask())


def _attn_core(q, ks, vs):
    tq = q.shape[0]
    grp = ATTN_HEADS // ATTN_KV_HEADS
    scale = HEAD_DIM ** -0.5
    outs = []
    for g in range(ATTN_KV_HEADS):
        sl = slice(g * HEAD_DIM, (g + 1) * HEAD_DIM)
        qs = jnp.concatenate(
            [q[:, (g * grp + j) * HEAD_DIM:(g * grp + j + 1) * HEAD_DIM] for j in range(grp)], axis=0).astype(BF16)
        ss = [lax.dot_general(qs, k[:, sl].astype(BF16), (((1,), (1,)), ((), ())),
                              preferred_element_type=F32) * scale for k in ks]
        m = ss[0].max(axis=-1, keepdims=True)
        for s in ss[1:]:
            m = jnp.maximum(m, s.max(axis=-1, keepdims=True))
        ps = [jnp.exp(s - m) for s in ss]
        l = ps[0].sum(axis=-1, keepdims=True)
        for p in ps[1:]:
            l = l + p.sum(axis=-1, keepdims=True)
        inv = 1.0 / l
        o = None
        for p, v in zip(ps, vs):
            t = _dot((p * inv).astype(BF16), v[:, sl].astype(BF16))
            o = t if o is None else o + t
        outs += [o[j * tq:(j + 1) * tq] for j in range(grp)]
    return jnp.concatenate(outs, axis=1)


def _attn_ctx_kernel(q_ref, k_ref, v_ref, o_ref):
    o_ref[...] = _attn_core(q_ref[...], [k_ref[...]], [v_ref[...]]).astype(BF16)


def _attn_smp_kernel(q_ref, k_ref, v_ref, ck_ref, cv_ref, o_ref):
    ks = [ck_ref[0], k_ref[...]]
    vs = [cv_ref[0], v_ref[...]]
    o_ref[...] = _attn_core(q_ref[...], ks, vs).astype(BF16)


ATTN_TQ = 128


def _attention(qn, kn, qkv, cache_k, cache_v):
    v_col = (ATTN_Q_DIM + ATTN_KV_DIM) // ATTN_KV_DIM
    ctx = pl.pallas_call(
        _attn_ctx_kernel,
        out_shape=jax.ShapeDtypeStruct((R_CTX, ATTN_Q_DIM), BF16),
        grid=(BATCH,),
        in_specs=[
            pl.BlockSpec((SEQ, ATTN_Q_DIM), lambda b: (b, 0)),
            pl.BlockSpec((SEQ, ATTN_KV_DIM), lambda b: (b, 0)),
            pl.BlockSpec((SEQ, ATTN_KV_DIM), lambda b: (b, v_col)),
        ],
        out_specs=pl.BlockSpec((SEQ, ATTN_Q_DIM), lambda b: (b, 0)),
        compiler_params=_cparams(("arbitrary",)),
        name="attn_ctx",
    )(qn, kn, qkv)
    nq = DEC_SEQ // ATTN_TQ
    q0 = R_CTX // ATTN_TQ
    s0 = R_CTX // DEC_SEQ
    smp = pl.pallas_call(
        _attn_smp_kernel,
        out_shape=jax.ShapeDtypeStruct((R_SMP, ATTN_Q_DIM), BF16),
        grid=(DEC_BATCH, nq),
        in_specs=[
            pl.BlockSpec((ATTN_TQ, ATTN_Q_DIM), lambda b, i: (q0 + b * nq + i, 0)),
            pl.BlockSpec((DEC_SEQ, ATTN_KV_DIM), lambda b, i: (s0 + b, 0)),
            pl.BlockSpec((DEC_SEQ, ATTN_KV_DIM), lambda b, i: (s0 + b, v_col)),
            pl.BlockSpec((1, PAST_LEN, ATTN_KV_DIM), lambda b, i: (b, 0, 0)),
            pl.BlockSpec((1, PAST_LEN, ATTN_KV_DIM), lambda b, i: (b, 0, 0)),
        ],
        out_specs=pl.BlockSpec((ATTN_TQ, ATTN_Q_DIM), lambda b, i: (b * nq + i, 0)),
        compiler_params=_cparams(("arbitrary", "arbitrary")),
        name="attn_smp",
    )(qn, kn, qkv, cache_k, cache_v)
    return ctx, smp


CONV_HALO = 8


def _cumsum_rows(a, reverse):
    n = a.shape[0]
    row = lax.broadcasted_iota(jnp.int32, a.shape, 0)
    s = 1
    while s < n:
        if reverse:
            a = a + jnp.where(row < n - s, pltpu.roll(a, n - s, axis=0), 0.0)
        else:
            a = a + jnp.where(row >= s, pltpu.roll(a, s, axis=0), 0.0)
        s *= 2
    return a


def _ssd_kernel(z_ref, xbc_ref, dt_ref, cw_ref, cb_ref, dtb_ref, a_ref, d_ref, g_ref, h0f_ref, h0b_ref,
                y_ref, hf_ref, hb_ref, pad_sc, xc_sc, dt_sc, y_sc, h_sc, *, seq):
    L = SSD_CHUNK
    nc = seq // L
    pad = SSD_CONV_K // 2
    zeros = jnp.zeros((CONV_HALO, SSD_CONV_DIM), F32)
    pad_sc[0:CONV_HALO, :] = zeros
    pad_sc[seq + CONV_HALO:seq + 2 * CONV_HALO, :] = zeros
    pad_sc[CONV_HALO:seq + CONV_HALO, :] = xbc_ref[...]
    h_sc[0] = h0f_ref[0]
    h_sc[1] = h0b_ref[0]

    def conv_chunk(c, carry):
        r0 = pl.multiple_of(c * L, L)
        win = pad_sc[pl.ds(r0, L + 2 * CONV_HALO), :]
        acc = cb_ref[...]
        for i in range(SSD_CONV_K):
            acc = acc + win[CONV_HALO - pad + i:CONV_HALO - pad + i + L, :] * cw_ref[i:i + 1, :]
        xc = _silu(acc)
        xc_sc[pl.ds(r0, L), :] = xc
        y_sc[pl.ds(r0, L), :] = xc[:, 0:SSD_D_INNER] * d_ref[...]
        dt_sc[pl.ds(r0, L), :] = _softplus(dt_ref[pl.ds(r0, L), :] + dtb_ref[...])
        return carry

    lax.fori_loop(0, nc, conv_chunk, 0)

    rr = lax.broadcasted_iota(jnp.int32, (L, L), 0)
    cc = lax.broadcasted_iota(jnp.int32, (L, L), 1)
    grp = SSD_HEADS // SSD_GROUPS
    P = SSD_HEAD_DIM
    N = SSD_STATE

    def make_chunk(dirn):
        mask = (rr >= cc) if dirn == 0 else (rr <= cc)

        def chunk(ci, carry):
            c = ci if dirn == 0 else nc - 1 - ci
            r0 = pl.multiple_of(c * L, L)
            xs = xc_sc[pl.ds(r0, L), 0:SSD_D_INNER]
            bm = xc_sc[pl.ds(r0, L), SSD_D_INNER:SSD_D_INNER + SSD_GROUPS * N]
            cm = xc_sc[pl.ds(r0, L), SSD_D_INNER + SSD_GROUPS * N:SSD_CONV_DIM]
            dtc = dt_sc[pl.ds(r0, L), :]
            acs = _cumsum_rows(dtc * a_ref[...], reverse=(dirn == 1))
            acs_t = acs.T
            tot = acs[L - 1:L, :] if dirn == 0 else acs[0:1, :]
            dec_end = jnp.exp(tot - acs)
            eacs = jnp.exp(acs)
            cdec = jnp.exp(tot)
            cb = [lax.dot_general(cm[:, g * N:(g + 1) * N].astype(BF16), bm[:, g * N:(g + 1) * N].astype(BF16),
                                  (((1,), (1,)), ((), ())), preferred_element_type=F32)
                  for g in range(SSD_GROUPS)]
            for h in range(SSD_HEADS):
                g = h // grp
                ln = h + SSD_HEADS * dirn
                seg = jnp.exp(jnp.where(mask, acs[:, ln:ln + 1] - acs_t[ln:ln + 1, :], -jnp.inf))
                sc = (cb[g] * seg).astype(BF16)
                xdt = (xs[:, h * P:(h + 1) * P] * dtc[:, ln:ln + 1]).astype(BF16)
                hin = h_sc[dirn, h]
                ce = (cm[:, g * N:(g + 1) * N] * eacs[:, ln:ln + 1]).astype(BF16)
                yh = _dot(sc, xdt) + lax.dot_general(ce, hin.astype(BF16), (((1,), (1,)), ((), ())),
                                                     preferred_element_type=F32)
                bd = (bm[:, g * N:(g + 1) * N] * dec_end[:, ln:ln + 1]).astype(BF16)
                st = lax.dot_general(xdt, bd, (((0,), (0,)), ((), ())), preferred_element_type=F32)
                h_sc[dirn, h] = hin * cdec[:, ln:ln + 1] + st
                y_sc[pl.ds(r0, L), h * P:(h + 1) * P] += yh
            return carry

        return chunk

    fwd_chunk, bwd_chunk = make_chunk(0), make_chunk(1)

    def both(ci, carry):
        return bwd_chunk(ci, fwd_chunk(ci, carry))

    lax.fori_loop(0, nc, both, 0)

    def out_chunk(c, carry):
        r0 = pl.multiple_of(c * L, L)
        y = y_sc[pl.ds(r0, L), :] * _silu(z_ref[pl.ds(r0, L), :])
        y_ref[pl.ds(r0, L), :] = _rms(y, g_ref[...]).astype(BF16)
        return carry

    lax.fori_loop(0, nc, out_chunk, 0)
    hf_ref[0] = h_sc[0]
    hb_ref[0] = h_sc[1]


def _ssd(z, xbc, dt, cw, cb, dtb, a_row, d_row, g, h0f, h0b, *, seq, nb, row0):
    blk0 = row0 // seq
    row = lambda b: (blk0 + b, 0)
    fixed = lambda b: (0, 0)
    st = lambda b: (b, 0, 0, 0)
    st_shape = (nb, SSD_HEADS, SSD_HEAD_DIM, SSD_STATE)
    st_blk = (1, SSD_HEADS, SSD_HEAD_DIM, SSD_STATE)
    return pl.pallas_call(
        functools.partial(_ssd_kernel, seq=seq),
        out_shape=(
            jax.ShapeDtypeStruct((nb * seq, SSD_D_INNER), BF16),
            jax.ShapeDtypeStruct(st_shape, F32),
            jax.ShapeDtypeStruct(st_shape, F32),
        ),
        grid=(nb,),
        in_specs=[
            pl.BlockSpec((seq, SSD_D_INNER), row),
            pl.BlockSpec((seq, SSD_CONV_DIM), row),
            pl.BlockSpec((seq, DT_PAD), row),
            pl.BlockSpec((SSD_CONV_K, SSD_CONV_DIM), fixed),
            pl.BlockSpec((1, SSD_CONV_DIM), fixed),
            pl.BlockSpec((1, DT_PAD), fixed),
            pl.BlockSpec((1, DT_PAD), fixed),
            pl.BlockSpec((1, SSD_D_INNER), fixed),
            pl.BlockSpec((1, SSD_D_INNER), fixed),
            pl.BlockSpec(st_blk, st),
            pl.BlockSpec(st_blk, st),
        ],
        out_specs=(
            pl.BlockSpec((seq, SSD_D_INNER), lambda b: (b, 0)),
            pl.BlockSpec(st_blk, st),
            pl.BlockSpec(st_blk, st),
        ),
        scratch_shapes=[
            pltpu.VMEM((seq + 2 * CONV_HALO, SSD_CONV_DIM), F32),
            pltpu.VMEM((seq, SSD_CONV_DIM), F32),
            pltpu.VMEM((seq, DT_PAD), F32),
            pltpu.VMEM((seq, SSD_D_INNER), F32),
            pltpu.VMEM((2, SSD_HEADS, SSD_HEAD_DIM, SSD_STATE), F32),
        ],
        compiler_params=_cparams(("arbitrary",)),
        name=f"ssd_{seq}",
    )(z, xbc, dt, cw, cb, dtb, a_row, d_row, g, h0f, h0b)


RES_TM = 512


def _mixres_kernel(x_ref, mod_ref, a1_ref, a2_ref, w_ref, o_ref):
    k1 = a1_ref.shape[1]
    out = _dot(a1_ref[...], w_ref[0:k1, :]) + _dot(a2_ref[...], w_ref[k1:, :])
    o_ref[...] = x_ref[...] + mod_ref[0][2:3] * out


def _mixres(x, mod, a1, a2, w):
    tm = RES_TM
    row = lambda i: (i, 0)
    return pl.pallas_call(
        _mixres_kernel,
        out_shape=jax.ShapeDtypeStruct((R_ALL, D_MODEL), F32),
        grid=(R_ALL // tm,),
        in_specs=[
            pl.BlockSpec((tm, D_MODEL), row),
            pl.BlockSpec((1, 8, D_MODEL), lambda i: (_mod_index(i, tm), 0, 0)),
            pl.BlockSpec((tm, a1.shape[1]), row),
            pl.BlockSpec((tm, a2.shape[1]), row),
            pl.BlockSpec(w.shape, lambda i: (0, 0)),
        ],
        out_specs=pl.BlockSpec((tm, D_MODEL), row),
        compiler_params=_cparams(("arbitrary",)),
        name="mixres",
    )(x, mod, a1, a2, w)


FFN_TM = 1024
FFN_TF = 256


def _ffn_kernel(x_ref, g_ref, mod_ref, wg_ref, wu_ref, wd_ref, fg_ref, o_ref, h_sc, acc_sc, *, final):
    j = pl.program_id(1)

    @pl.when(j == 0)
    def _():
        m = mod_ref[0]
        h = _rms(x_ref[...], g_ref[...]) * (1 + m[4:5]) + m[3:4]
        h_sc[...] = h.astype(BF16)
        acc_sc[...] = jnp.zeros_like(acc_sc)

    h = h_sc[...]
    hid = _silu(_dot(h, wg_ref[...])) * _dot(h, wu_ref[...])
    acc_sc[...] += _dot(hid.astype(BF16), wd_ref[...])

    @pl.when(j == pl.num_programs(1) - 1)
    def _():
        y = x_ref[...] + mod_ref[0][5:6] * acc_sc[...]
        if final:
            y = _rms(y, fg_ref[...])
        o_ref[...] = y


def _ffn(x, g, mod, wg, wu, wd, fg, *, final):
    tm, tf = FFN_TM, FFN_TF
    row = lambda i, j: (i, 0)
    return pl.pallas_call(
        functools.partial(_ffn_kernel, final=final),
        out_shape=jax.ShapeDtypeStruct((R_ALL, D_MODEL), F32),
        grid=(R_ALL // tm, FFN_DIM // tf),
        in_specs=[
            pl.BlockSpec((tm, D_MODEL), row),
            pl.BlockSpec((1, D_MODEL), lambda i, j: (0, 0)),
            pl.BlockSpec((1, 8, D_MODEL), lambda i, j: (_mod_index(i, tm), 0, 0)),
            pl.BlockSpec((D_MODEL, tf), lambda i, j: (0, j)),
            pl.BlockSpec((D_MODEL, tf), lambda i, j: (0, j)),
            pl.BlockSpec((tf, D_MODEL), lambda i, j: (j, 0)),
            pl.BlockSpec((1, D_MODEL), lambda i, j: (0, 0)),
        ],
        out_specs=pl.BlockSpec((tm, D_MODEL), row),
        scratch_shapes=[pltpu.VMEM((tm, D_MODEL), BF16), pltpu.VMEM((tm, D_MODEL), F32)],
        compiler_params=_cparams(("arbitrary", "arbitrary")),
        name="ffn_final" if final else "ffn",
    )(x, g, mod, wg, wu, wd, fg)


RW_TM = 128
HALO = 8
N_OPS = 9


def _rwkv_pre_kernel(x_ref, xp_ref, xn_ref, g_ref, mod_ref, mu_ref, wr_ref, wk_ref, wv_ref, w1_ref, w2_ref,
                     a1_ref, a2_ref, g1_ref, g2_ref, w0_ref, a0_ref, kk_ref, ka_ref, rk_ref, hs_ref,
                     ops_o, g_o, bon_o):
    i = pl.program_id(0)
    tm = RW_TM
    n_ctx = R_CTX // tm
    per_c = SEQ // tm
    per_s = DEC_SEQ // tm
    rel = jnp.where(i < n_ctx, i % per_c, jnp.maximum(i - n_ctx, 0) % per_s)
    last = jnp.where(i < n_ctx, per_c - 1, per_s - 1)
    m = mod_ref[0]

    def nm(x):
        return _rms(x, g_ref[...]) * (1 + m[1:2]) + m[0:1]

    h = nm(x_ref[...])
    prev_row = jnp.where(rel == 0, 0.0, nm(xp_ref[...])[HALO - 1:HALO, :])
    next_row = jnp.where(rel == last, 0.0, nm(xn_ref[...])[0:1, :])
    row = lax.broadcasted_iota(jnp.int32, h.shape, 0)
    hp = jnp.where(row == 0, prev_row, pltpu.roll(h, 1, axis=0))
    hn = jnp.where(row == tm - 1, next_row, pltpu.roll(h, tm - 1, axis=0))
    dp = hp - h
    dn = hn - h

    def mix(idx):
        return (h + dp * mu_ref[0, idx:idx + 1, :] + dn * mu_ref[1, idx:idx + 1, :]).astype(BF16)

    r = _dot(mix(0), wr_ref[...])
    k = _dot(mix(2), wk_ref[...])
    v = _dot(mix(3), wv_ref[...])
    lw = jnp.tanh(_dot(mix(1), w1_ref[...])).astype(BF16)
    la = _dot(mix(4), a1_ref[...]).astype(BF16)
    gg = _dot(_sigmoid(_dot(mix(5), g1_ref[...])).astype(BF16), g2_ref[...])

    hs = hs_ref[...]
    kk = k * kk_ref[...]
    kk = kk * lax.rsqrt(_head_allsum(kk * kk, hs) + L2_EPS)
    ops_o[0] = r
    ops_o[1] = kk
    ops_o[2] = v
    g_o[...] = gg
    bsum = None
    for j in range(2):
        wl = w0_ref[j:j + 1, :] + _dot(lw, w2_ref[j])
        ops_o[3 + 3 * j] = jnp.exp(-jnp.exp(-_softplus(-wl) - 0.5))
        a = _sigmoid(a0_ref[j:j + 1, :] + _dot(la, a2_ref[j]))
        kd = k * (1 + (a - 1) * ka_ref[...])
        ops_o[4 + 3 * j] = kd
        ops_o[5 + 3 * j] = kk * a
        t = r * kd * rk_ref[...]
        bsum = t if bsum is None else bsum + t
    bon_o[...] = _head_allsum(bsum, hs) * v


def _rwkv_pre(x, g, mod, mu, wr, wk, wv, w1, w2, a1, a2, g1, g2, w0, a0, k_k, k_a, r_k, hs):
    tm = RW_TM
    nblk = R_ALL // HALO
    per = tm // HALO
    row = lambda i: (i, 0)
    fixed2 = lambda i: (0, 0)
    fixed3 = lambda i: (0, 0, 0)
    full = lambda a: pl.BlockSpec(a.shape, fixed2 if a.ndim == 2 else fixed3)
    out = jax.ShapeDtypeStruct((R_ALL, D_MODEL), F32)
    orow = pl.BlockSpec((tm, D_MODEL), row)
    return pl.pallas_call(
        _rwkv_pre_kernel,
        out_shape=(jax.ShapeDtypeStruct((N_OPS, R_ALL, D_MODEL), F32), out, out),
        grid=(R_ALL // tm,),
        in_specs=[
            pl.BlockSpec((tm, D_MODEL), row),
            pl.BlockSpec((HALO, D_MODEL), lambda i: (jnp.maximum(i * per - 1, 0), 0)),
            pl.BlockSpec((HALO, D_MODEL), lambda i: (jnp.minimum((i + 1) * per, nblk - 1), 0)),
            pl.BlockSpec((1, D_MODEL), fixed2),
            pl.BlockSpec((1, 8, D_MODEL), lambda i: (_mod_index(i, tm), 0, 0)),
            full(mu), full(wr), full(wk), full(wv), full(w1), full(w2), full(a1), full(a2), full(g1), full(g2),
            full(w0), full(a0), full(k_k), full(k_a), full(r_k), full(hs),
        ],
        out_specs=(pl.BlockSpec((N_OPS, tm, D_MODEL), lambda i: (0, i, 0)), orow, orow),
        compiler_params=_cparams(("arbitrary",)),
        name="rwkv_pre",
    )(x, x, x, g, mod, mu, wr, wk, wv, w1, w2, a1, a2, g1, g2, w0, a0, k_k, k_a, r_k, hs)


SCAN_TT = 32
SCAN_ACC = 4
NK = RWKV_HEAD_DIM


def _scan_kernel(kt_ref, vt_ref, dk_ref, s0_ref, y_ref, st_ref, s_sc):
    d = pl.program_id(0)
    j = pl.program_id(2)

    @pl.when(j == 0)
    def _():
        s_sc[...] = s0_ref[0]

    def tix(i):
        return jnp.where(d == 0, i, SCAN_TT - 1 - i)

    def project(i):
        t = tix(i)
        parts = [None] * SCAN_ACC
        for k in range(NK):
            term = s_sc[k] * kt_ref[1, t, k:k + 1, :]
            parts[k % SCAN_ACC] = term if parts[k % SCAN_ACC] is None else parts[k % SCAN_ACC] + term
        while len(parts) > 1:
            parts = [parts[a] + parts[a + 1] for a in range(0, len(parts), 2)]
        return parts[0]

    def advance(i, sa, with_next):
        t = tix(i)
        tn = tix(i + 1)
        vv = vt_ref[0, t]
        y = None
        sa_next = None
        for k in range(NK):
            sn = (s_sc[k] * dk_ref[0, t, k:k + 1, :] - sa * dk_ref[2, t, k:k + 1, :]
                  + vv * dk_ref[1, t, k:k + 1, :])
            s_sc[k] = sn
            yk = sn * kt_ref[0, t, k:k + 1, :]
            y = yk if y is None else y + yk
            if with_next:
                ak = sn * kt_ref[1, tn, k:k + 1, :]
                sa_next = ak if sa_next is None else sa_next + ak
        y_ref[0, t] = y
        return sa_next

    sa_last = lax.fori_loop(0, SCAN_TT - 1, lambda i, sa: advance(i, sa, True), project(0))
    advance(SCAN_TT - 1, sa_last, False)

    @pl.when(j == pl.num_programs(2) - 1)
    def _():
        st_ref[0] = s_sc[...]


def _scan(kt, vt, dk, s0, *, kt_blk=0, vt_slab=0, dk_blk=0):
    _, seq, nv, chains = vt.shape
    tt = SCAN_TT
    nt = seq // tt
    tb = lambda d, j: jnp.where(d == 0, j, nt - 1 - j)
    sblk = pl.BlockSpec((1, NK, nv, LANES), lambda d, c, j: (d, 0, 0, c))
    yblk = pl.BlockSpec((1, tt, nv, LANES), lambda d, c, j: (d, tb(d, j), 0, c))
    return pl.pallas_call(
        _scan_kernel,
        out_shape=(jax.ShapeDtypeStruct((2, seq, nv, chains), F32), jax.ShapeDtypeStruct((2, NK, nv, chains), F32)),
        grid=(2, chains // LANES, nt),
        in_specs=[
            pl.BlockSpec((2, tt, NK, LANES), lambda d, c, j: (kt_blk, tb(d, j), 0, c)),
            pl.BlockSpec((1, tt, nv, LANES), lambda d, c, j: (vt_slab, tb(d, j), 0, c)),
            pl.BlockSpec((3, tt, NK, LANES), lambda d, c, j: (dk_blk + d, tb(d, j), 0, c)),
            sblk,
        ],
        out_specs=(yblk, sblk),
        scratch_shapes=[pltpu.VMEM((NK, nv, LANES), F32)],
        compiler_params=_cparams(("arbitrary", "arbitrary", "arbitrary")),
        name=f"rwkv_scan_{seq}",
    )(kt, vt, dk, s0)


POST_TM = 256


def _rwkv_post_kernel(x_ref, mod_ref, yc_ref, ys_ref, bon_ref, g_ref, lng_ref, lnb_ref, wo_ref, hs_ref, o_ref):
    is_ctx = pl.program_id(0) < R_CTX // POST_TM
    y = jnp.where(is_ctx, yc_ref[0] + yc_ref[1], ys_ref[0] + ys_ref[1])
    inv = 1.0 / RWKV_HEAD_DIM
    hs = hs_ref[...]
    d = y - _head_allsum(y, hs) * inv
    var = _head_allsum(d * d, hs) * inv
    yn = d * lax.rsqrt(var + GN_EPS) * lng_ref[...] + lnb_ref[...]
    out = ((yn + bon_ref[...]) * g_ref[...]).astype(BF16)
    o_ref[...] = x_ref[...] + mod_ref[0][2:3] * _dot(out, wo_ref[...])


def _rwkv_post(x, mod, yc, ys, bon, g, lng, lnb, wo, hs):
    tm = POST_TM
    n_ctx = R_CTX // tm
    row = lambda i: (i, 0)
    fixed = lambda i: (0, 0)
    big = pl.BlockSpec((tm, D_MODEL), row)
    ycb = pl.BlockSpec((2, tm, D_MODEL), lambda i: (0, jnp.minimum(i, n_ctx - 1), 0))
    ysb = pl.BlockSpec((2, tm, D_MODEL), lambda i: (0, jnp.maximum(i - n_ctx, 0), 0))
    return pl.pallas_call(
        _rwkv_post_kernel,
        out_shape=jax.ShapeDtypeStruct((R_ALL, D_MODEL), F32),
        grid=(R_ALL // tm,),
        in_specs=[
            big,
            pl.BlockSpec((1, 8, D_MODEL), lambda i: (_mod_index(i, tm), 0, 0)),
            ycb, ysb, big, big,
            pl.BlockSpec((1, D_MODEL), fixed),
            pl.BlockSpec((1, D_MODEL), fixed),
            pl.BlockSpec((D_MODEL, D_MODEL), fixed),
            pl.BlockSpec((LANES, LANES), fixed),
        ],
        out_specs=big,
        compiler_params=_cparams(("arbitrary",)),
        name="rwkv_post",
    )(x, mod, yc, ys, bon, g, lng, lnb, wo, hs)


RL_T = 32
NPAIR = D_MODEL // LANES


def _to_chains_kernel(x_ref, o_ref):
    cols = [jnp.swapaxes(x_ref[0, :, :, p * LANES:(p + 1) * LANES], 0, 1) for p in range(NPAIR)]
    for t in range(RL_T):
        tile = jnp.concatenate([c[t] for c in cols], axis=0).T
        o_ref[0, t, :, 0:LANES] = tile[0:NK]
        o_ref[0, t, :, LANES:2 * LANES] = tile[NK:2 * NK]


def _from_chains_kernel(y_ref, o_ref):
    tiles = []
    for t in range(RL_T):
        m = jnp.concatenate([y_ref[0, t, :, 0:LANES], y_ref[0, t, :, LANES:2 * LANES]], axis=0)
        tiles.append(m.T)
    for p in range(NPAIR):
        blk = jnp.stack([m[p * BATCH:(p + 1) * BATCH] for m in tiles], axis=0)
        o_ref[0, :, :, p * LANES:(p + 1) * LANES] = jnp.swapaxes(blk, 0, 1)


def _to_chains(ops):
    n = ops.shape[0]
    return pl.pallas_call(
        _to_chains_kernel,
        out_shape=jax.ShapeDtypeStruct((n, SEQ, NK, 2 * LANES), F32),
        grid=(n, SEQ // RL_T),
        in_specs=[pl.BlockSpec((1, BATCH, RL_T, D_MODEL), lambda o, j: (o, 0, j, 0))],
        out_specs=pl.BlockSpec((1, RL_T, NK, 2 * LANES), lambda o, j: (o, j, 0, 0)),
        compiler_params=_cparams(("arbitrary", "arbitrary")),
        name="to_chains",
    )(ops.reshape(n, R_ALL // SEQ, SEQ, D_MODEL))


def _from_chains(y):
    n = y.shape[0]
    out = pl.pallas_call(
        _from_chains_kernel,
        out_shape=jax.ShapeDtypeStruct((n, BATCH, SEQ, D_MODEL), F32),
        grid=(n, SEQ // RL_T),
        in_specs=[pl.BlockSpec((1, RL_T, NK, 2 * LANES), lambda o, j: (o, j, 0, 0))],
        out_specs=pl.BlockSpec((1, BATCH, RL_T, D_MODEL), lambda o, j: (o, 0, j, 0)),
        compiler_params=_cparams(("arbitrary", "arbitrary")),
        name="from_chains",
    )(y)
    return out.reshape(n, R_CTX, D_MODEL)


def _scan_ctx(ops):
    ch = _to_chains(ops)
    s0 = jnp.zeros((2, NK, NK, 2 * LANES), F32)
    y, st = _scan(ch, ch, ch, s0, vt_slab=2, dk_blk=1)
    st = st.reshape(2, NK, NK, 2, NPAIR, BATCH).transpose(0, 5, 4, 3, 2, 1)
    return _from_chains(y), st.reshape(2, BATCH, RWKV_HEADS, NK, NK)


SMP_REP = LANES // (NPAIR * DEC_BATCH)
SMP_VQ = SMP_REP // 2
SMP_NV = NK // SMP_VQ
SMP_PB = NPAIR * DEC_BATCH


def _smp_tiles(x_ref):
    rows = jnp.stack([x_ref[0, b, :, p * LANES:(p + 1) * LANES] for p in range(NPAIR) for b in range(DEC_BATCH)],
                     axis=0)
    rows = jnp.swapaxes(rows, 0, 1)
    return [jnp.concatenate([rows[t]] * SMP_REP, axis=0).T for t in range(RL_T)]


def _lane_replica(nrows):
    return lax.broadcasted_iota(jnp.int32, (nrows, LANES), 1) // SMP_PB


def _to_chains_smp_k_kernel(x_ref, o_ref):
    odd = _lane_replica(NK) >= SMP_VQ
    for t, tile in enumerate(_smp_tiles(x_ref)):
        o_ref[0, t] = jnp.where(odd, tile[NK:2 * NK], tile[0:NK])


def _to_chains_smp_v_kernel(x_ref, o_ref):
    rep = _lane_replica(SMP_NV)
    for t, tile in enumerate(_smp_tiles(x_ref)):
        acc = tile[0:SMP_NV]
        for r in range(1, SMP_REP):
            acc = jnp.where(rep == r, tile[r * SMP_NV:(r + 1) * SMP_NV], acc)
        o_ref[0, t] = acc


def _from_chains_smp_kernel(y_ref, o_ref):
    rep = _lane_replica(SMP_NV)
    res = []
    for t in range(RL_T):
        y = y_ref[0, t]
        m = jnp.concatenate([jnp.where(rep == r, y, 0.0) for r in range(SMP_REP)], axis=0).T
        acc = m[0:SMP_PB]
        for r in range(1, SMP_REP):
            acc = acc + m[r * SMP_PB:(r + 1) * SMP_PB]
        res.append(acc)
    out = jnp.swapaxes(jnp.stack(res, axis=0), 0, 1)
    for p in range(NPAIR):
        for b in range(DEC_BATCH):
            o_ref[0, b, :, p * LANES:(p + 1) * LANES] = out[p * DEC_BATCH + b]


def _scan_smp(ops, s0):
    nseq = R_ALL // DEC_SEQ
    blk0 = R_CTX // DEC_SEQ // DEC_BATCH
    ops4 = ops.reshape(N_OPS, nseq, DEC_SEQ, D_MODEL)
    nt = DEC_SEQ // RL_T
    in_blk = (1, DEC_BATCH, RL_T, D_MODEL)
    nk_ops = N_OPS - 1
    kidx = lambda o, j: (jnp.where(o < 6, o + 3, o - 6), blk0, j, 0)
    chk = pl.pallas_call(
        _to_chains_smp_k_kernel,
        out_shape=jax.ShapeDtypeStruct((nk_ops, DEC_SEQ, NK, LANES), F32),
        grid=(nk_ops, nt),
        in_specs=[pl.BlockSpec(in_blk, kidx)],
        out_specs=pl.BlockSpec((1, RL_T, NK, LANES), lambda o, j: (o, j, 0, 0)),
        compiler_params=_cparams(("arbitrary", "arbitrary")),
        name="to_chains_smp_k",
    )(ops4)
    chv = pl.pallas_call(
        _to_chains_smp_v_kernel,
        out_shape=jax.ShapeDtypeStruct((1, DEC_SEQ, SMP_NV, LANES), F32),
        grid=(1, nt),
        in_specs=[pl.BlockSpec(in_blk, lambda o, j: (2, blk0, j, 0))],
        out_specs=pl.BlockSpec((1, RL_T, SMP_NV, LANES), lambda o, j: (0, j, 0, 0)),
        compiler_params=_cparams(("arbitrary", "arbitrary")),
        name="to_chains_smp_v",
    )(ops4)
    s0 = s0.reshape(2, DEC_BATCH, NPAIR, 2, SMP_VQ, SMP_NV, NK).transpose(0, 6, 5, 3, 4, 2, 1)
    y, _ = _scan(chk, chv, chk, s0.reshape(2, NK, SMP_NV, LANES), kt_blk=3, vt_slab=0, dk_blk=0)
    out = pl.pallas_call(
        _from_chains_smp_kernel,
        out_shape=jax.ShapeDtypeStruct((2, DEC_BATCH, DEC_SEQ, D_MODEL), F32),
        grid=(2, nt),
        in_specs=[pl.BlockSpec((1, RL_T, SMP_NV, LANES), lambda o, j: (o, j, 0, 0))],
        out_specs=pl.BlockSpec((1, DEC_BATCH, RL_T, D_MODEL), lambda o, j: (o, 0, j, 0)),
        compiler_params=_cparams(("arbitrary", "arbitrary")),
        name="from_chains_smp",
    )(y)
    return out.reshape(2, R_SMP, D_MODEL)


def _rope_tables():
    rows = DEC_SEQ // GRID_W
    row = jnp.repeat(jnp.arange(rows, dtype=F32), GRID_W)
    col = jnp.tile(jnp.arange(GRID_W, dtype=F32), rows)
    n_freq = HEAD_DIM // 4
    inv_freq = ROPE_THETA ** (-jnp.arange(n_freq, dtype=F32) / n_freq)
    ang = jnp.concatenate([row[:, None] * inv_freq, col[:, None] * inv_freq], axis=-1)
    cos, sin = jnp.cos(ang), jnp.sin(ang)
    reps = LANES // HEAD_DIM
    cos_t = jnp.tile(jnp.concatenate([cos, cos], axis=-1), (1, reps))
    sin_t = jnp.tile(jnp.concatenate([-sin, sin], axis=-1), (1, reps))
    ident_c = jnp.ones((QK_TM, LANES), F32)
    ident_s = jnp.zeros((QK_TM, LANES), F32)
    return jnp.concatenate([ident_c, cos_t], axis=0), jnp.concatenate([ident_s, sin_t], axis=0)


def kernel(x_prompt, x_sample, cache_attn_k, cache_attn_v, state_ssd_fwd, state_ssd_bwd, state_rwkv_fwd, state_rwkv_bwd, c, c_ctx, mod_w, mod_b, norm_mix_g, norm_ffn_g, ffn_w_gate, ffn_w_up, ffn_w_down, ab_w_in, ab_w_out, attn_q_g, attn_k_g, ssd_conv_w, ssd_conv_b, ssd_dt_bias, ssd_a_log, ssd_d, ssd_norm_g, rwkv_mu, rwkv_w_r, rwkv_w_k, rwkv_w_v, rwkv_w0, rwkv_w1, rwkv_w2, rwkv_a0, rwkv_a1, rwkv_a2, rwkv_g1, rwkv_g2, rwkv_k_k, rwkv_k_a, rwkv_r_k, rwkv_ln_g, rwkv_ln_b, rwkv_w_o, final_norm_g):
    bf = lambda a: a.astype(BF16)
    x = jnp.concatenate([x_prompt.reshape(R_CTX, D_MODEL), x_sample.reshape(R_SMP, D_MODEL)], axis=0)

    cv = jnp.concatenate([c_ctx[None], c, jnp.zeros((8 - 1 - DEC_BATCH, D_MODEL), F32)], axis=0)
    m = _ada(cv, mod_w, mod_b)
    m = m[:, :1 + DEC_BATCH].reshape(2, 1 + DEC_BATCH, 6, D_MODEL)
    m = jnp.pad(m, ((0, 0), (0, 0), (0, 2), (0, 0)))
    row2 = lambda a: a.reshape(1, -1)

    w_in = jnp.pad(bf(ab_w_in[0]), ((0, 0), (0, AB_PAD - AB_IN_DIM)))
    qkv, z, xbc, dt = _inproj(x, row2(norm_mix_g[0]), m[0], w_in)
    cos_t, sin_t = _rope_tables()
    qg = jnp.tile(attn_q_g[0], ATTN_HEADS).reshape(1, -1)
    kg = jnp.tile(attn_k_g[0], ATTN_KV_HEADS).reshape(1, -1)
    qn, kn = _qkprep(qkv, qg, kg, cos_t, sin_t)
    ck = cache_attn_k[:, 0].reshape(DEC_BATCH, PAST_LEN, ATTN_KV_DIM)
    cvv = cache_attn_v[:, 0].reshape(DEC_BATCH, PAST_LEN, ATTN_KV_DIM)
    attn_c, attn_s = _attention(qn, kn, qkv, ck, cvv)
    attn = jnp.concatenate([attn_c, attn_s], axis=0)

    dtb = jnp.pad(ssd_dt_bias[0].reshape(1, -1), ((0, 0), (0, DT_PAD - 2 * SSD_HEADS)))
    a_row = jnp.pad((-jnp.exp(ssd_a_log[0])).reshape(1, -1), ((0, 0), (0, DT_PAD - 2 * SSD_HEADS)))
    d_row = jnp.repeat(ssd_d[0], SSD_HEAD_DIM).reshape(1, -1)
    ssd_args = (ssd_conv_w[0], row2(ssd_conv_b[0]), dtb, a_row, d_row, row2(ssd_norm_g[0]))
    zero_st = jnp.zeros((BATCH, SSD_HEADS, SSD_HEAD_DIM, SSD_STATE), F32)
    y_c, hf_c, hb_c = _ssd(z, xbc, dt, *ssd_args, zero_st, zero_st, seq=SEQ, nb=BATCH, row0=0)
    y_s, _, _ = _ssd(z, xbc, dt, *ssd_args, state_ssd_fwd[:, 0], state_ssd_bwd[:, 0],
                     seq=DEC_SEQ, nb=DEC_BATCH, row0=R_CTX)
    y_ssd = jnp.concatenate([y_c, y_s], axis=0)
    x = _mixres(x, m[0], attn, y_ssd, bf(ab_w_out[0]))
    x = _ffn(x, row2(norm_ffn_g[0]), m[0], bf(ffn_w_gate[0]), bf(ffn_w_up[0]), bf(ffn_w_down[0]),
             row2(final_norm_g), final=False)

    w1 = bf(jnp.concatenate([rwkv_w1[0, 0], rwkv_w1[0, 1]], axis=1))
    a1 = bf(jnp.concatenate([rwkv_a1[0, 0], rwkv_a1[0, 1]], axis=1))
    zpad = lambda w: bf(jnp.stack([jnp.concatenate([w[0], jnp.zeros_like(w[1])], axis=0),
                                   jnp.concatenate([jnp.zeros_like(w[0]), w[1]], axis=0)]))
    hs = _head_mask()
    ops, gg, bon = _rwkv_pre(x, row2(norm_mix_g[1]), m[1], rwkv_mu[0], bf(rwkv_w_r[0]), bf(rwkv_w_k[0]),
                             bf(rwkv_w_v[0]), w1, zpad(rwkv_w2[0]), a1, zpad(rwkv_a2[0]), bf(rwkv_g1[0]),
                             bf(rwkv_g2[0]), rwkv_w0[0], rwkv_a0[0], row2(rwkv_k_k[0]), row2(rwkv_k_a[0]),
                             rwkv_r_k[0].reshape(1, -1), hs)
    y_c, st_c = _scan_ctx(ops)
    y_s = _scan_smp(ops, jnp.stack([state_rwkv_fwd[:, 0], state_rwkv_bwd[:, 0]]))
    x = _rwkv_post(x, m[1], y_c, y_s, bon, gg, row2(rwkv_ln_g[0]), row2(rwkv_ln_b[0]), bf(rwkv_w_o[0]), hs)
    x = _ffn(x, row2(norm_ffn_g[1]), m[1], bf(ffn_w_gate[1]), bf(ffn_w_up[1]), bf(ffn_w_down[1]),
             row2(final_norm_g), final=True)

    y_prompt = x[:R_CTX].reshape(BATCH, SEQ, D_MODEL)
    y_sample = x[R_CTX:].reshape(DEC_BATCH, DEC_SEQ, D_MODEL)
    new_k = kn[:R_CTX].reshape(BATCH, 1, SEQ, ATTN_KV_HEADS, HEAD_DIM)
    new_v = qkv[:R_CTX, ATTN_Q_DIM + ATTN_KV_DIM:].reshape(BATCH, 1, SEQ, ATTN_KV_HEADS, HEAD_DIM)
    return (y_prompt, y_sample, new_k, new_v, hf_c[:, None], hb_c[:, None], st_c[0][:, None], st_c[1][:, None])
```

```python
import functools

import jax
import jax.numpy as jnp
from jax import lax
from jax.experimental import pallas as pl
from jax.experimental.pallas import tpu as pltpu

F32 = jnp.float32
BF16 = jnp.bfloat16

D_MODEL = 1024
BATCH = 16
SEQ = 256
DEC_BATCH = 2
DEC_SEQ = 1024
PAST_LEN = 256
GRID_W = 64
ATTN_HEADS = 8
ATTN_KV_HEADS = 2
HEAD_DIM = 64
ROPE_THETA = 10000.0
ATTN_Q_DIM = ATTN_HEADS * HEAD_DIM
ATTN_KV_DIM = ATTN_KV_HEADS * HEAD_DIM
SSD_HEADS = 8
SSD_HEAD_DIM = 64
SSD_D_INNER = SSD_HEADS * SSD_HEAD_DIM
SSD_GROUPS = 2
SSD_STATE = 64
SSD_CONV_K = 5
SSD_CHUNK = 128
SSD_CONV_DIM = SSD_D_INNER + 2 * SSD_GROUPS * SSD_STATE
AB_IN_DIM = ATTN_Q_DIM + 2 * ATTN_KV_DIM + SSD_D_INNER + SSD_CONV_DIM + 2 * SSD_HEADS
RWKV_HEAD_DIM = 64
RWKV_HEADS = D_MODEL // RWKV_HEAD_DIM
FFN_DIM = (((8 * D_MODEL + 2) // 3 + 255) // 256) * 256
RMS_EPS = 1e-6
GN_EPS = 64e-5
L2_EPS = 1e-12

R_CTX = BATCH * SEQ
R_SMP = DEC_BATCH * DEC_SEQ
R_ALL = R_CTX + R_SMP
LANES = 128
QKV_DIM = ATTN_Q_DIM + 2 * ATTN_KV_DIM
DT_PAD = LANES
AB_PAD = QKV_DIM + SSD_D_INNER + SSD_CONV_DIM + DT_PAD
VMEM_LIMIT = 56 * 1024 * 1024


def _cparams(sem):
    return pltpu.CompilerParams(dimension_semantics=sem, vmem_limit_bytes=VMEM_LIMIT)


def _mod_index(i, tm):
    n_ctx = R_CTX // tm
    per = DEC_SEQ // tm
    return jnp.where(i < n_ctx, 0, 1 + jnp.maximum(i - n_ctx, 0) // per)


def _sigmoid(x):
    return 1.0 / (1.0 + jnp.exp(-x))


def _silu(x):
    return x * _sigmoid(x)


def _softplus(x):
    return jnp.maximum(x, 0.0) + jnp.log1p(jnp.exp(-jnp.abs(x)))


def _rms(x, g):
    return x * lax.rsqrt(jnp.mean(x * x, axis=-1, keepdims=True) + RMS_EPS) * g


def _head_allsum(x, hs):
    parts = [jnp.dot(x[:, j * LANES:(j + 1) * LANES], hs, precision=lax.Precision.HIGHEST,
                     preferred_element_type=F32) for j in range(x.shape[-1] // LANES)]
    return parts[0] if len(parts) == 1 else jnp.concatenate(parts, axis=1)


def _head_mask():
    lane = jnp.arange(LANES)
    return (lane[:, None] // HEAD_DIM == lane[None, :] // HEAD_DIM).astype(F32)


def _dot(a, b):
    return jnp.dot(a, b, preferred_element_type=F32)


ADA_TN = 1536


def _ada_kernel(c_ref, w_ref, b_ref, o_ref):
    s = _silu(c_ref[...]).astype(BF16)
    o_ref[0] = _dot(s, w_ref[0].astype(BF16)) + b_ref[0]


def _ada(cv, mod_w, mod_b):
    depth = mod_w.shape[0]
    n = mod_w.shape[2]
    return pl.pallas_call(
        _ada_kernel,
        out_shape=jax.ShapeDtypeStruct((depth, 8, n), F32),
        grid=(depth, n // ADA_TN),
        in_specs=[
            pl.BlockSpec((8, D_MODEL), lambda l, j: (0, 0)),
            pl.BlockSpec((1, D_MODEL, ADA_TN), lambda l, j: (l, 0, j)),
            pl.BlockSpec((1, 1, ADA_TN), lambda l, j: (l, 0, j)),
        ],
        out_specs=pl.BlockSpec((1, 8, ADA_TN), lambda l, j: (l, 0, j)),
        compiler_params=_cparams(("arbitrary", "arbitrary")),
        name="ada",
    )(cv, mod_w, mod_b.reshape(depth, 1, n))


INPROJ_TM = 512


def _inproj_kernel(x_ref, g_ref, mod_ref, w_ref, qkv_ref, z_ref, xbc_ref, dt_ref):
    m = mod_ref[0]
    h = _rms(x_ref[...], g_ref[...]) * (1 + m[1:2]) + m[0:1]
    p = _dot(h.astype(BF16), w_ref[...])
    qkv_ref[...] = p[:, 0:QKV_DIM]
    z_ref[...] = p[:, QKV_DIM:QKV_DIM + SSD_D_INNER]
    xbc_ref[...] = p[:, QKV_DIM + SSD_D_INNER:QKV_DIM + SSD_D_INNER + SSD_CONV_DIM]
    dt_ref[...] = p[:, QKV_DIM + SSD_D_INNER + SSD_CONV_DIM:AB_PAD]


def _inproj(x, g, mod, w_pad):
    tm = INPROJ_TM
    row = lambda i: (i, 0)
    return pl.pallas_call(
        _inproj_kernel,
        out_shape=(
            jax.ShapeDtypeStruct((R_ALL, QKV_DIM), F32),
            jax.ShapeDtypeStruct((R_ALL, SSD_D_INNER), F32),
            jax.ShapeDtypeStruct((R_ALL, SSD_CONV_DIM), F32),
            jax.ShapeDtypeStruct((R_ALL, DT_PAD), F32),
        ),
        grid=(R_ALL // tm,),
        in_specs=[
            pl.BlockSpec((tm, D_MODEL), row),
            pl.BlockSpec((1, D_MODEL), lambda i: (0, 0)),
            pl.BlockSpec((1, 8, D_MODEL), lambda i: (_mod_index(i, tm), 0, 0)),
            pl.BlockSpec((D_MODEL, AB_PAD), lambda i: (0, 0)),
        ],
        out_specs=(
            pl.BlockSpec((tm, QKV_DIM), row),
            pl.BlockSpec((tm, SSD_D_INNER), row),
            pl.BlockSpec((tm, SSD_CONV_DIM), row),
            pl.BlockSpec((tm, DT_PAD), row),
        ),
        compiler_params=_cparams(("arbitrary",)),
        name="inproj",
    )(x, g, mod, w_pad)


QK_TM = 256


def _qkprep_kernel(qkv_ref, qg_ref, kg_ref, cos_ref, sin_ref, hs_ref, qn_ref, kn_ref):
    cos = cos_ref[...]
    sin = sin_ref[...]
    hs = hs_ref[...]

    def norm_rope(x, g, reps):
        ms = _head_allsum(x * x, hs) * (1.0 / HEAD_DIM)
        y = x * lax.rsqrt(ms + RMS_EPS) * g
        n = y.shape[-1]
        lane = lax.broadcasted_iota(jnp.int32, y.shape, 1)
        half = HEAD_DIM // 2
        swapped = jnp.where((lane & half) == 0, pltpu.roll(y, n - half, axis=1), pltpu.roll(y, half, axis=1))
        c = jnp.concatenate([cos] * reps, axis=1) if reps > 1 else cos
        s = jnp.concatenate([sin] * reps, axis=1) if reps > 1 else sin
        return y * c + swapped * s

    q = qkv_ref[:, 0:ATTN_Q_DIM]
    k = qkv_ref[:, ATTN_Q_DIM:ATTN_Q_DIM + ATTN_KV_DIM]
    qn_ref[...] = norm_rope(q, qg_ref[...], ATTN_Q_DIM // LANES)
    kn_ref[...] = norm_rope(k, kg_ref[...], 1)


def _qkprep(qkv, qg, kg, cos_t, sin_t):
    tm = QK_TM
    n_ctx = R_CTX // tm
    per = DEC_SEQ // tm
    tab = lambda i: (jnp.where(i < n_ctx, 0, 1 + jnp.maximum(i - n_ctx, 0) % per), 0)
    return pl.pallas_call(
        _qkprep_kernel,
        out_shape=(
            jax.ShapeDtypeStruct((R_ALL, ATTN_Q_DIM), F32),
            jax.ShapeDtypeStruct((R_ALL, ATTN_KV_DIM), F32),
        ),
        grid=(R_ALL // tm,),
        in_specs=[
            pl.BlockSpec((tm, QKV_DIM), lambda i: (i, 0)),
            pl.BlockSpec((1, ATTN_Q_DIM), lambda i: (0, 0)),
            pl.BlockSpec((1, ATTN_KV_DIM), lambda i: (0, 0)),
            pl.BlockSpec((tm, LANES), tab),
            pl.BlockSpec((tm, LANES), tab),
            pl.BlockSpec((LANES, LANES), lambda i: (0, 0)),
        ],
        out_specs=(
            pl.BlockSpec((tm, ATTN_Q_DIM), lambda i: (i, 0)),
            pl.BlockSpec((tm, ATTN_KV_DIM), lambda i: (i, 0)),
        ),
        compiler_params=_cparams(("arbitrary",)),
        name="qkprep",
    )(qkv, qg, kg, cos_t, sin_t, _head_mask())


def _attn_core(q, ks, vs):
    tq = q.shape[0]
    grp = ATTN_HEADS // ATTN_KV_HEADS
    scale = HEAD_DIM ** -0.5
    outs = []
    for g in range(ATTN_KV_HEADS):
        sl = slice(g * HEAD_DIM, (g + 1) * HEAD_DIM)
        qs = jnp.concatenate(
            [q[:, (g * grp + j) * HEAD_DIM:(g * grp + j + 1) * HEAD_DIM] for j in range(grp)], axis=0).astype(BF16)
        ss = [lax.dot_general(qs, k[:, sl].astype(BF16), (((1,), (1,)), ((), ())),
                              preferred_element_type=F32) * scale for k in ks]
        m = ss[0].max(axis=-1, keepdims=True)
        for s in ss[1:]:
            m = jnp.maximum(m, s.max(axis=-1, keepdims=True))
        ps = [jnp.exp(s - m) for s in ss]
        l = ps[0].sum(axis=-1, keepdims=True)
        for p in ps[1:]:
            l = l + p.sum(axis=-1, keepdims=True)
        inv = 1.0 / l
        o = None
        for p, v in zip(ps, vs):
            t = _dot((p * inv).astype(BF16), v[:, sl].astype(BF16))
            o = t if o is None else o + t
        outs += [o[j * tq:(j + 1) * tq] for j in range(grp)]
    return jnp.concatenate(outs, axis=1)


def _attn_ctx_kernel(q_ref, k_ref, v_ref, o_ref):
    o_ref[...] = _attn_core(q_ref[...], [k_ref[...]], [v_ref[...]]).astype(BF16)


def _attn_smp_kernel(q_ref, k_ref, v_ref, ck_ref, cv_ref, o_ref):
    ks = [ck_ref[0], k_ref[...]]
    vs = [cv_ref[0], v_ref[...]]
    o_ref[...] = _attn_core(q_ref[...], ks, vs).astype(BF16)


ATTN_TQ = 128


def _attention(qn, kn, qkv, cache_k, cache_v):
    v_col = (ATTN_Q_DIM + ATTN_KV_DIM) // ATTN_KV_DIM
    ctx = pl.pallas_call(
        _attn_ctx_kernel,
        out_shape=jax.ShapeDtypeStruct((R_CTX, ATTN_Q_DIM), BF16),
        grid=(BATCH,),
        in_specs=[
            pl.BlockSpec((SEQ, ATTN_Q_DIM), lambda b: (b, 0)),
            pl.BlockSpec((SEQ, ATTN_KV_DIM), lambda b: (b, 0)),
            pl.BlockSpec((SEQ, ATTN_KV_DIM), lambda b: (b, v_col)),
        ],
        out_specs=pl.BlockSpec((SEQ, ATTN_Q_DIM), lambda b: (b, 0)),
        compiler_params=_cparams(("arbitrary",)),
        name="attn_ctx",
    )(qn, kn, qkv)
    nq = DEC_SEQ // ATTN_TQ
    q0 = R_CTX // ATTN_TQ
    s0 = R_CTX // DEC_SEQ
    smp = pl.pallas_call(
        _attn_smp_kernel,
        out_shape=jax.ShapeDtypeStruct((R_SMP, ATTN_Q_DIM), BF16),
        grid=(DEC_BATCH, nq),
        in_specs=[
            pl.BlockSpec((ATTN_TQ, ATTN_Q_DIM), lambda b, i: (q0 + b * nq + i, 0)),
            pl.BlockSpec((DEC_SEQ, ATTN_KV_DIM), lambda b, i: (s0 + b, 0)),
            pl.BlockSpec((DEC_SEQ, ATTN_KV_DIM), lambda b, i: (s0 + b, v_col)),
            pl.BlockSpec((1, PAST_LEN, ATTN_KV_DIM), lambda b, i: (b, 0, 0)),
            pl.BlockSpec((1, PAST_LEN, ATTN_KV_DIM), lambda b, i: (b, 0, 0)),
        ],
        out_specs=pl.BlockSpec((ATTN_TQ, ATTN_Q_DIM), lambda b, i: (b * nq + i, 0)),
        compiler_params=_cparams(("arbitrary", "arbitrary")),
        name="attn_smp",
    )(qn, kn, qkv, cache_k, cache_v)
    return ctx, smp


CONV_HALO = 8
SSD_NS = 1


def _cumsum_rows(a, reverse):
    n = a.shape[0]
    row = lax.broadcasted_iota(jnp.int32, a.shape, 0)
    s = 1
    while s < n:
        if reverse:
            a = a + jnp.where(row < n - s, pltpu.roll(a, n - s, axis=0), 0.0)
        else:
            a = a + jnp.where(row >= s, pltpu.roll(a, s, axis=0), 0.0)
        s *= 2
    return a


def _ssd_kernel(z_ref, xbc_ref, dt_ref, cw_ref, cb_ref, dtb_ref, a_ref, d_ref, g_ref, h0f_ref, h0b_ref,
                y_ref, hf_ref, hb_ref, pad_sc, xc_sc, dt_sc, y_sc, h_sc, *, seq):
    L = SSD_CHUNK
    nc = seq // L
    pad = SSD_CONV_K // 2
    zeros = jnp.zeros((CONV_HALO, SSD_CONV_DIM), F32)
    for s in range(SSD_NS):
        pad_sc[s, 0:CONV_HALO, :] = zeros
        pad_sc[s, seq + CONV_HALO:seq + 2 * CONV_HALO, :] = zeros
        pad_sc[s, CONV_HALO:seq + CONV_HALO, :] = xbc_ref[s * seq:(s + 1) * seq, :]
        h_sc[s, 0] = h0f_ref[s]
        h_sc[s, 1] = h0b_ref[s]

    def conv_chunk(c, carry):
        for s in range(SSD_NS):
            r0 = pl.multiple_of(c * L, L)
            win = pad_sc[s, pl.ds(r0, L + 2 * CONV_HALO), :]
            acc = cb_ref[...]
            for i in range(SSD_CONV_K):
                acc = acc + win[CONV_HALO - pad + i:CONV_HALO - pad + i + L, :] * cw_ref[i:i + 1, :]
            xc = _silu(acc)
            g0 = pl.multiple_of(s * seq + c * L, L)
            xc_sc[pl.ds(g0, L), :] = xc
            y_sc[pl.ds(g0, L), :] = xc[:, 0:SSD_D_INNER] * d_ref[...]
            dt_sc[pl.ds(g0, L), :] = _softplus(dt_ref[pl.ds(g0, L), :] + dtb_ref[...])
        return carry

    lax.fori_loop(0, nc, conv_chunk, 0)

    rr = lax.broadcasted_iota(jnp.int32, (L, L), 0)
    cc = lax.broadcasted_iota(jnp.int32, (L, L), 1)
    grp = SSD_HEADS // SSD_GROUPS
    P = SSD_HEAD_DIM
    N = SSD_STATE

    def make_chunk(s, dirn):
        mask = (rr >= cc) if dirn == 0 else (rr <= cc)

        def chunk(ci, carry):
            c = ci if dirn == 0 else nc - 1 - ci
            r0 = pl.multiple_of(s * seq + c * L, L)
            xs = xc_sc[pl.ds(r0, L), 0:SSD_D_INNER]
            bm = xc_sc[pl.ds(r0, L), SSD_D_INNER:SSD_D_INNER + SSD_GROUPS * N]
            cm = xc_sc[pl.ds(r0, L), SSD_D_INNER + SSD_GROUPS * N:SSD_CONV_DIM]
            dtc = dt_sc[pl.ds(r0, L), :]
            acs = _cumsum_rows(dtc * a_ref[...], reverse=(dirn == 1))
            acs_t = acs.T
            tot = acs[L - 1:L, :] if dirn == 0 else acs[0:1, :]
            dec_end = jnp.exp(tot - acs)
            eacs = jnp.exp(acs)
            cdec = jnp.exp(tot)
            cb = [lax.dot_general(cm[:, g * N:(g + 1) * N].astype(BF16), bm[:, g * N:(g + 1) * N].astype(BF16),
                                  (((1,), (1,)), ((), ())), preferred_element_type=F32)
                  for g in range(SSD_GROUPS)]
            for h in range(SSD_HEADS):
                g = h // grp
                ln = h + SSD_HEADS * dirn
                seg = jnp.exp(jnp.where(mask, acs[:, ln:ln + 1] - acs_t[ln:ln + 1, :], -jnp.inf))
                sc = (cb[g] * seg).astype(BF16)
                xdt = (xs[:, h * P:(h + 1) * P] * dtc[:, ln:ln + 1]).astype(BF16)
                hin = h_sc[s, dirn, h]
                ce = (cm[:, g * N:(g + 1) * N] * eacs[:, ln:ln + 1]).astype(BF16)
                yh = _dot(sc, xdt) + lax.dot_general(ce, hin.astype(BF16), (((1,), (1,)), ((), ())),
                                                     preferred_element_type=F32)
                bd = (bm[:, g * N:(g + 1) * N] * dec_end[:, ln:ln + 1]).astype(BF16)
                st = lax.dot_general(xdt, bd, (((0,), (0,)), ((), ())), preferred_element_type=F32)
                h_sc[s, dirn, h] = hin * cdec[:, ln:ln + 1] + st
                y_sc[pl.ds(r0, L), h * P:(h + 1) * P] += yh
            return carry

        return chunk

    chunks = [make_chunk(s, dirn) for s in range(SSD_NS) for dirn in range(2)]

    def all_chunks(ci, carry):
        for fn in chunks:
            carry = fn(ci, carry)
        return carry

    lax.fori_loop(0, nc, all_chunks, 0)

    def out_chunk(c, carry):
        r0 = pl.multiple_of(c * L, L)
        y = y_sc[pl.ds(r0, L), :] * _silu(z_ref[pl.ds(r0, L), :])
        y_ref[pl.ds(r0, L), :] = _rms(y, g_ref[...]).astype(BF16)
        return carry

    lax.fori_loop(0, SSD_NS * nc, out_chunk, 0)
    for s in range(SSD_NS):
        hf_ref[s] = h_sc[s, 0]
        hb_ref[s] = h_sc[s, 1]


def _ssd(z, xbc, dt, cw, cb, dtb, a_row, d_row, g, h0f, h0b, *, seq, nb, row0):
    ns = SSD_NS
    rows = ns * seq
    blk0 = row0 // rows
    row = lambda b: (blk0 + b, 0)
    fixed = lambda b: (0, 0)
    st = lambda b: (b, 0, 0, 0)
    st_shape = (nb, SSD_HEADS, SSD_HEAD_DIM, SSD_STATE)
    st_blk = (ns, SSD_HEADS, SSD_HEAD_DIM, SSD_STATE)
    return pl.pallas_call(
        functools.partial(_ssd_kernel, seq=seq),
        out_shape=(
            jax.ShapeDtypeStruct((nb * seq, SSD_D_INNER), BF16),
            jax.ShapeDtypeStruct(st_shape, F32),
            jax.ShapeDtypeStruct(st_shape, F32),
        ),
        grid=(nb // ns,),
        in_specs=[
            pl.BlockSpec((rows, SSD_D_INNER), row),
            pl.BlockSpec((rows, SSD_CONV_DIM), row),
            pl.BlockSpec((rows, DT_PAD), row),
            pl.BlockSpec((SSD_CONV_K, SSD_CONV_DIM), fixed),
            pl.BlockSpec((1, SSD_CONV_DIM), fixed),
            pl.BlockSpec((1, DT_PAD), fixed),
            pl.BlockSpec((1, DT_PAD), fixed),
            pl.BlockSpec((1, SSD_D_INNER), fixed),
            pl.BlockSpec((1, SSD_D_INNER), fixed),
            pl.BlockSpec(st_blk, st),
            pl.BlockSpec(st_blk, st),
        ],
        out_specs=(
            pl.BlockSpec((rows, SSD_D_INNER), lambda b: (b, 0)),
            pl.BlockSpec(st_blk, st),
            pl.BlockSpec(st_blk, st),
        ),
        scratch_shapes=[
            pltpu.VMEM((ns, seq + 2 * CONV_HALO, SSD_CONV_DIM), F32),
            pltpu.VMEM((rows, SSD_CONV_DIM), F32),
            pltpu.VMEM((rows, DT_PAD), F32),
            pltpu.VMEM((rows, SSD_D_INNER), F32),
            pltpu.VMEM((ns, 2, SSD_HEADS, SSD_HEAD_DIM, SSD_STATE), F32),
        ],
        compiler_params=_cparams(("arbitrary",)),
        name=f"ssd_{seq}",
    )(z, xbc, dt, cw, cb, dtb, a_row, d_row, g, h0f, h0b)


RES_TM = 512


def _mixres_kernel(x_ref, mod_ref, a1_ref, a2_ref, w_ref, o_ref):
    k1 = a1_ref.shape[1]
    out = _dot(a1_ref[...], w_ref[0:k1, :]) + _dot(a2_ref[...], w_ref[k1:, :])
    o_ref[...] = x_ref[...] + mod_ref[0][2:3] * out


def _mixres(x, mod, a1, a2, w):
    tm = RES_TM
    row = lambda i: (i, 0)
    return pl.pallas_call(
        _mixres_kernel,
        out_shape=jax.ShapeDtypeStruct((R_ALL, D_MODEL), F32),
        grid=(R_ALL // tm,),
        in_specs=[
            pl.BlockSpec((tm, D_MODEL), row),
            pl.BlockSpec((1, 8, D_MODEL), lambda i: (_mod_index(i, tm), 0, 0)),
            pl.BlockSpec((tm, a1.shape[1]), row),
            pl.BlockSpec((tm, a2.shape[1]), row),
            pl.BlockSpec(w.shape, lambda i: (0, 0)),
        ],
        out_specs=pl.BlockSpec((tm, D_MODEL), row),
        compiler_params=_cparams(("arbitrary",)),
        name="mixres",
    )(x, mod, a1, a2, w)


FFN_TM = 1024
FFN_TF = 256


def _ffn_kernel(x_ref, g_ref, mod_ref, wg_ref, wu_ref, wd_ref, fg_ref, o_ref, h_sc, acc_sc, *, final):
    j = pl.program_id(1)

    @pl.when(j == 0)
    def _():
        m = mod_ref[0]
        h = _rms(x_ref[...], g_ref[...]) * (1 + m[4:5]) + m[3:4]
        h_sc[...] = h.astype(BF16)
        acc_sc[...] = jnp.zeros_like(acc_sc)

    h = h_sc[...]
    hid = _silu(_dot(h, wg_ref[...])) * _dot(h, wu_ref[...])
    acc_sc[...] += _dot(hid.astype(BF16), wd_ref[...])

    @pl.when(j == pl.num_programs(1) - 1)
    def _():
        y = x_ref[...] + mod_ref[0][5:6] * acc_sc[...]
        if final:
            y = _rms(y, fg_ref[...])
        o_ref[...] = y


def _ffn(x, g, mod, wg, wu, wd, fg, *, final):
    tm, tf = FFN_TM, FFN_TF
    row = lambda i, j: (i, 0)
    return pl.pallas_call(
        functools.partial(_ffn_kernel, final=final),
        out_shape=jax.ShapeDtypeStruct((R_ALL, D_MODEL), F32),
        grid=(R_ALL // tm, FFN_DIM // tf),
        in_specs=[
            pl.BlockSpec((tm, D_MODEL), row),
            pl.BlockSpec((1, D_MODEL), lambda i, j: (0, 0)),
            pl.BlockSpec((1, 8, D_MODEL), lambda i, j: (_mod_index(i, tm), 0, 0)),
            pl.BlockSpec((D_MODEL, tf), lambda i, j: (0, j)),
            pl.BlockSpec((D_MODEL, tf), lambda i, j: (0, j)),
            pl.BlockSpec((tf, D_MODEL), lambda i, j: (j, 0)),
            pl.BlockSpec((1, D_MODEL), lambda i, j: (0, 0)),
        ],
        out_specs=pl.BlockSpec((tm, D_MODEL), row),
        scratch_shapes=[pltpu.VMEM((tm, D_MODEL), BF16), pltpu.VMEM((tm, D_MODEL), F32)],
        compiler_params=_cparams(("arbitrary", "arbitrary")),
        name="ffn_final" if final else "ffn",
    )(x, g, mod, wg, wu, wd, fg)


RW_TM = 256
HALO = 8
N_OPS = 9


def _rwkv_pre_kernel(x_ref, xp_ref, xn_ref, g_ref, mod_ref, mu_ref, wr_ref, wk_ref, wv_ref, w1_ref, w2_ref,
                     a1_ref, a2_ref, g1_ref, g2_ref, w0_ref, a0_ref, kk_ref, ka_ref, rk_ref, hs_ref,
                     ops_o, g_o, bon_o):
    i = pl.program_id(0)
    tm = RW_TM
    n_ctx = R_CTX // tm
    per_c = SEQ // tm
    per_s = DEC_SEQ // tm
    rel = jnp.where(i < n_ctx, i % per_c, jnp.maximum(i - n_ctx, 0) % per_s)
    last = jnp.where(i < n_ctx, per_c - 1, per_s - 1)
    m = mod_ref[0]

    def nm(x):
        return _rms(x, g_ref[...]) * (1 + m[1:2]) + m[0:1]

    h = nm(x_ref[...])
    prev_row = jnp.where(rel == 0, 0.0, nm(xp_ref[...])[HALO - 1:HALO, :])
    next_row = jnp.where(rel == last, 0.0, nm(xn_ref[...])[0:1, :])
    row = lax.broadcasted_iota(jnp.int32, h.shape, 0)
    hp = jnp.where(row == 0, prev_row, pltpu.roll(h, 1, axis=0))
    hn = jnp.where(row == tm - 1, next_row, pltpu.roll(h, tm - 1, axis=0))
    dp = hp - h
    dn = hn - h

    def mix(idx):
        return (h + dp * mu_ref[0, idx:idx + 1, :] + dn * mu_ref[1, idx:idx + 1, :]).astype(BF16)

    r = _dot(mix(0), wr_ref[...])
    k = _dot(mix(2), wk_ref[...])
    v = _dot(mix(3), wv_ref[...])
    lw = jnp.tanh(_dot(mix(1), w1_ref[...])).astype(BF16)
    la = _dot(mix(4), a1_ref[...]).astype(BF16)
    gg = _dot(_sigmoid(_dot(mix(5), g1_ref[...])).astype(BF16), g2_ref[...])

    hs = hs_ref[...]
    kk = k * kk_ref[...]
    kk = kk * lax.rsqrt(_head_allsum(kk * kk, hs) + L2_EPS)
    ops_o[0] = r
    ops_o[1] = kk
    ops_o[2] = v
    g_o[...] = gg
    bsum = None
    for j in range(2):
        wl = w0_ref[j:j + 1, :] + _dot(lw, w2_ref[j])
        ops_o[3 + 3 * j] = jnp.exp(-jnp.exp(-_softplus(-wl) - 0.5))
        a = _sigmoid(a0_ref[j:j + 1, :] + _dot(la, a2_ref[j]))
        kd = k * (1 + (a - 1) * ka_ref[...])
        ops_o[4 + 3 * j] = kd
        ops_o[5 + 3 * j] = kk * a
        t = r * kd * rk_ref[...]
        bsum = t if bsum is None else bsum + t
    bon_o[...] = _head_allsum(bsum, hs) * v


def _rwkv_pre(x, g, mod, mu, wr, wk, wv, w1, w2, a1, a2, g1, g2, w0, a0, k_k, k_a, r_k, hs):
    tm = RW_TM
    nblk = R_ALL // HALO
    per = tm // HALO
    row = lambda i: (i, 0)
    fixed2 = lambda i: (0, 0)
    fixed3 = lambda i: (0, 0, 0)
    full = lambda a: pl.BlockSpec(a.shape, fixed2 if a.ndim == 2 else fixed3)
    out = jax.ShapeDtypeStruct((R_ALL, D_MODEL), F32)
    orow = pl.BlockSpec((tm, D_MODEL), row)
    return pl.pallas_call(
        _rwkv_pre_kernel,
        out_shape=(jax.ShapeDtypeStruct((N_OPS, R_ALL, D_MODEL), F32), out, out),
        grid=(R_ALL // tm,),
        in_specs=[
            pl.BlockSpec((tm, D_MODEL), row),
            pl.BlockSpec((HALO, D_MODEL), lambda i: (jnp.maximum(i * per - 1, 0), 0)),
            pl.BlockSpec((HALO, D_MODEL), lambda i: (jnp.minimum((i + 1) * per, nblk - 1), 0)),
            pl.BlockSpec((1, D_MODEL), fixed2),
            pl.BlockSpec((1, 8, D_MODEL), lambda i: (_mod_index(i, tm), 0, 0)),
            full(mu), full(wr), full(wk), full(wv), full(w1), full(w2), full(a1), full(a2), full(g1), full(g2),
            full(w0), full(a0), full(k_k), full(k_a), full(r_k), full(hs),
        ],
        out_specs=(pl.BlockSpec((N_OPS, tm, D_MODEL), lambda i: (0, i, 0)), orow, orow),
        compiler_params=_cparams(("arbitrary",)),
        name="rwkv_pre",
    )(x, x, x, g, mod, mu, wr, wk, wv, w1, w2, a1, a2, g1, g2, w0, a0, k_k, k_a, r_k, hs)


SCAN_TT = 64
SCAN_ACC = 4
NK = RWKV_HEAD_DIM


def _scan_kernel(kt_ref, vt_ref, dk_ref, s0_ref, y_ref, st_ref, s_sc):
    d = pl.program_id(0)
    j = pl.program_id(2)

    @pl.when(j == 0)
    def _():
        s_sc[...] = s0_ref[0]

    def tix(i):
        return jnp.where(d == 0, i, SCAN_TT - 1 - i)

    def project(i):
        t = tix(i)
        parts = [None] * SCAN_ACC
        for k in range(NK):
            term = s_sc[k] * kt_ref[1, t, k:k + 1, :]
            parts[k % SCAN_ACC] = term if parts[k % SCAN_ACC] is None else parts[k % SCAN_ACC] + term
        while len(parts) > 1:
            parts = [parts[a] + parts[a + 1] for a in range(0, len(parts), 2)]
        return parts[0]

    def advance(i, sa, with_next):
        t = tix(i)
        tn = tix(i + 1)
        vv = vt_ref[0, t]
        y = None
        sa_next = None
        for k in range(NK):
            sn = (s_sc[k] * dk_ref[0, t, k:k + 1, :] - sa * dk_ref[2, t, k:k + 1, :]
                  + vv * dk_ref[1, t, k:k + 1, :])
            s_sc[k] = sn
            yk = sn * kt_ref[0, t, k:k + 1, :]
            y = yk if y is None else y + yk
            if with_next:
                ak = sn * kt_ref[1, tn, k:k + 1, :]
                sa_next = ak if sa_next is None else sa_next + ak
        y_ref[0, t] = y
        return sa_next

    sa_last = lax.fori_loop(0, SCAN_TT - 1, lambda i, sa: advance(i, sa, True), project(0))
    advance(SCAN_TT - 1, sa_last, False)

    @pl.when(j == pl.num_programs(2) - 1)
    def _():
        st_ref[0] = s_sc[...]


def _scan(kt, vt, dk, s0, *, kt_blk=0, vt_slab=0, dk_blk=0):
    _, seq, nv, chains = vt.shape
    tt = SCAN_TT
    nt = seq // tt
    tb = lambda d, j: jnp.where(d == 0, j, nt - 1 - j)
    sblk = pl.BlockSpec((1, NK, nv, LANES), lambda d, c, j: (d, 0, 0, c))
    yblk = pl.BlockSpec((1, tt, nv, LANES), lambda d, c, j: (d, tb(d, j), 0, c))
    return pl.pallas_call(
        _scan_kernel,
        out_shape=(jax.ShapeDtypeStruct((2, seq, nv, chains), F32), jax.ShapeDtypeStruct((2, NK, nv, chains), F32)),
        grid=(2, chains // LANES, nt),
        in_specs=[
            pl.BlockSpec((2, tt, NK, LANES), lambda d, c, j: (kt_blk, tb(d, j), 0, c)),
            pl.BlockSpec((1, tt, nv, LANES), lambda d, c, j: (vt_slab, tb(d, j), 0, c)),
            pl.BlockSpec((3, tt, NK, LANES), lambda d, c, j: (dk_blk + d, tb(d, j), 0, c)),
            sblk,
        ],
        out_specs=(yblk, sblk),
        scratch_shapes=[pltpu.VMEM((NK, nv, LANES), F32)],
        compiler_params=_cparams(("arbitrary", "arbitrary", "arbitrary")),
        name=f"rwkv_scan_{seq}",
    )(kt, vt, dk, s0)


POST_TM = 256


def _rwkv_post_kernel(x_ref, mod_ref, yc_ref, ys_ref, bon_ref, g_ref, lng_ref, lnb_ref, wo_ref, hs_ref, o_ref):
    is_ctx = pl.program_id(0) < R_CTX // POST_TM
    y = jnp.where(is_ctx, yc_ref[0] + yc_ref[1], ys_ref[0] + ys_ref[1])
    inv = 1.0 / RWKV_HEAD_DIM
    hs = hs_ref[...]
    d = y - _head_allsum(y, hs) * inv
    var = _head_allsum(d * d, hs) * inv
    yn = d * lax.rsqrt(var + GN_EPS) * lng_ref[...] + lnb_ref[...]
    out = ((yn + bon_ref[...]) * g_ref[...]).astype(BF16)
    o_ref[...] = x_ref[...] + mod_ref[0][2:3] * _dot(out, wo_ref[...])


def _rwkv_post(x, mod, yc, ys, bon, g, lng, lnb, wo, hs):
    tm = POST_TM
    n_ctx = R_CTX // tm
    row = lambda i: (i, 0)
    fixed = lambda i: (0, 0)
    big = pl.BlockSpec((tm, D_MODEL), row)
    ycb = pl.BlockSpec((2, tm, D_MODEL), lambda i: (0, jnp.minimum(i, n_ctx - 1), 0))
    ysb = pl.BlockSpec((2, tm, D_MODEL), lambda i: (0, jnp.maximum(i - n_ctx, 0), 0))
    return pl.pallas_call(
        _rwkv_post_kernel,
        out_shape=jax.ShapeDtypeStruct((R_ALL, D_MODEL), F32),
        grid=(R_ALL // tm,),
        in_specs=[
            big,
            pl.BlockSpec((1, 8, D_MODEL), lambda i: (_mod_index(i, tm), 0, 0)),
            ycb, ysb, big, big,
            pl.BlockSpec((1, D_MODEL), fixed),
            pl.BlockSpec((1, D_MODEL), fixed),
            pl.BlockSpec((D_MODEL, D_MODEL), fixed),
            pl.BlockSpec((LANES, LANES), fixed),
        ],
        out_specs=big,
        compiler_params=_cparams(("arbitrary",)),
        name="rwkv_post",
    )(x, mod, yc, ys, bon, g, lng, lnb, wo, hs)


RL_T = 32
NPAIR = D_MODEL // LANES


def _to_chains_kernel(x_ref, o_ref):
    cols = [jnp.swapaxes(x_ref[0, :, :, p * LANES:(p + 1) * LANES], 0, 1) for p in range(NPAIR)]
    for t in range(RL_T):
        tile = jnp.concatenate([c[t] for c in cols], axis=0).T
        o_ref[0, t, :, 0:LANES] = tile[0:NK]
        o_ref[0, t, :, LANES:2 * LANES] = tile[NK:2 * NK]


def _from_chains_kernel(y_ref, o_ref):
    tiles = []
    for t in range(RL_T):
        m = jnp.concatenate([y_ref[0, t, :, 0:LANES], y_ref[0, t, :, LANES:2 * LANES]], axis=0)
        tiles.append(m.T)
    for p in range(NPAIR):
        blk = jnp.stack([m[p * BATCH:(p + 1) * BATCH] for m in tiles], axis=0)
        o_ref[0, :, :, p * LANES:(p + 1) * LANES] = jnp.swapaxes(blk, 0, 1)


def _to_chains(ops):
    n = ops.shape[0]
    return pl.pallas_call(
        _to_chains_kernel,
        out_shape=jax.ShapeDtypeStruct((n, SEQ, NK, 2 * LANES), F32),
        grid=(n, SEQ // RL_T),
        in_specs=[pl.BlockSpec((1, BATCH, RL_T, D_MODEL), lambda o, j: (o, 0, j, 0))],
        out_specs=pl.BlockSpec((1, RL_T, NK, 2 * LANES), lambda o, j: (o, j, 0, 0)),
        compiler_params=_cparams(("arbitrary", "arbitrary")),
        name="to_chains",
    )(ops.reshape(n, R_ALL // SEQ, SEQ, D_MODEL))


def _from_chains(y):
    n = y.shape[0]
    out = pl.pallas_call(
        _from_chains_kernel,
        out_shape=jax.ShapeDtypeStruct((n, BATCH, SEQ, D_MODEL), F32),
        grid=(n, SEQ // RL_T),
        in_specs=[pl.BlockSpec((1, RL_T, NK, 2 * LANES), lambda o, j: (o, j, 0, 0))],
        out_specs=pl.BlockSpec((1, BATCH, RL_T, D_MODEL), lambda o, j: (o, 0, j, 0)),
        compiler_params=_cparams(("arbitrary", "arbitrary")),
        name="from_chains",
    )(y)
    return out.reshape(n, R_CTX, D_MODEL)


def _scan_ctx(ops):
    ch = _to_chains(ops)
    s0 = jnp.zeros((2, NK, NK, 2 * LANES), F32)
    y, st = _scan(ch, ch, ch, s0, vt_slab=2, dk_blk=1)
    st = st.reshape(2, NK, NK, 2, NPAIR, BATCH).transpose(0, 5, 4, 3, 2, 1)
    return _from_chains(y), st.reshape(2, BATCH, RWKV_HEADS, NK, NK)


SMP_REP = LANES // (NPAIR * DEC_BATCH)
SMP_VQ = SMP_REP // 2
SMP_NV = NK // SMP_VQ
SMP_PB = NPAIR * DEC_BATCH


def _smp_tiles(x_ref):
    rows = jnp.stack([x_ref[0, b, :, p * LANES:(p + 1) * LANES] for p in range(NPAIR) for b in range(DEC_BATCH)],
                     axis=0)
    rows = jnp.swapaxes(rows, 0, 1)
    return [jnp.concatenate([rows[t]] * SMP_REP, axis=0).T for t in range(RL_T)]


def _lane_replica(nrows):
    return lax.broadcasted_iota(jnp.int32, (nrows, LANES), 1) // SMP_PB


def _to_chains_smp_k_kernel(x_ref, o_ref):
    odd = _lane_replica(NK) >= SMP_VQ
    for t, tile in enumerate(_smp_tiles(x_ref)):
        o_ref[0, t] = jnp.where(odd, tile[NK:2 * NK], tile[0:NK])


def _to_chains_smp_v_kernel(x_ref, o_ref):
    rep = _lane_replica(SMP_NV)
    for t, tile in enumerate(_smp_tiles(x_ref)):
        acc = tile[0:SMP_NV]
        for r in range(1, SMP_REP):
            acc = jnp.where(rep == r, tile[r * SMP_NV:(r + 1) * SMP_NV], acc)
        o_ref[0, t] = acc


def _from_chains_smp_kernel(y_ref, o_ref):
    rep = _lane_replica(SMP_NV)
    res = []
    for t in range(RL_T):
        y = y_ref[0, t]
        m = jnp.concatenate([jnp.where(rep == r, y, 0.0) for r in range(SMP_REP)], axis=0).T
        acc = m[0:SMP_PB]
        for r in range(1, SMP_REP):
            acc = acc + m[r * SMP_PB:(r + 1) * SMP_PB]
        res.append(acc)
    out = jnp.swapaxes(jnp.stack(res, axis=0), 0, 1)
    for p in range(NPAIR):
        for b in range(DEC_BATCH):
            o_ref[0, b, :, p * LANES:(p + 1) * LANES] = out[p * DEC_BATCH + b]


def _scan_smp(ops, s0):
    nseq = R_ALL // DEC_SEQ
    blk0 = R_CTX // DEC_SEQ // DEC_BATCH
    ops4 = ops.reshape(N_OPS, nseq, DEC_SEQ, D_MODEL)
    nt = DEC_SEQ // RL_T
    in_blk = (1, DEC_BATCH, RL_T, D_MODEL)
    nk_ops = N_OPS - 1
    kidx = lambda o, j: (jnp.where(o < 6, o + 3, o - 6), blk0, j, 0)
    chk = pl.pallas_call(
        _to_chains_smp_k_kernel,
        out_shape=jax.ShapeDtypeStruct((nk_ops, DEC_SEQ, NK, LANES), F32),
        grid=(nk_ops, nt),
        in_specs=[pl.BlockSpec(in_blk, kidx)],
        out_specs=pl.BlockSpec((1, RL_T, NK, LANES), lambda o, j: (o, j, 0, 0)),
        compiler_params=_cparams(("arbitrary", "arbitrary")),
        name="to_chains_smp_k",
    )(ops4)
    chv = pl.pallas_call(
        _to_chains_smp_v_kernel,
        out_shape=jax.ShapeDtypeStruct((1, DEC_SEQ, SMP_NV, LANES), F32),
        grid=(1, nt),
        in_specs=[pl.BlockSpec(in_blk, lambda o, j: (2, blk0, j, 0))],
        out_specs=pl.BlockSpec((1, RL_T, SMP_NV, LANES), lambda o, j: (0, j, 0, 0)),
        compiler_params=_cparams(("arbitrary", "arbitrary")),
        name="to_chains_smp_v",
    )(ops4)
    s0 = s0.reshape(2, DEC_BATCH, NPAIR, 2, SMP_VQ, SMP_NV, NK).transpose(0, 6, 5, 3, 4, 2, 1)
    y, _ = _scan(chk, chv, chk, s0.reshape(2, NK, SMP_NV, LANES), kt_blk=3, vt_slab=0, dk_blk=0)
    out = pl.pallas_call(
        _from_chains_smp_kernel,
        out_shape=jax.ShapeDtypeStruct((2, DEC_BATCH, DEC_SEQ, D_MODEL), F32),
        grid=(2, nt),
        in_specs=[pl.BlockSpec((1, RL_T, SMP_NV, LANES), lambda o, j: (o, j, 0, 0))],
        out_specs=pl.BlockSpec((1, DEC_BATCH, RL_T, D_MODEL), lambda o, j: (o, 0, j, 0)),
        compiler_params=_cparams(("arbitrary", "arbitrary")),
        name="from_chains_smp",
    )(y)
    return out.reshape(2, R_SMP, D_MODEL)


def _rope_tables():
    rows = DEC_SEQ // GRID_W
    row = jnp.repeat(jnp.arange(rows, dtype=F32), GRID_W)
    col = jnp.tile(jnp.arange(GRID_W, dtype=F32), rows)
    n_freq = HEAD_DIM // 4
    inv_freq = ROPE_THETA ** (-jnp.arange(n_freq, dtype=F32) / n_freq)
    ang = jnp.concatenate([row[:, None] * inv_freq, col[:, None] * inv_freq], axis=-1)
    cos, sin = jnp.cos(ang), jnp.sin(ang)
    reps = LANES // HEAD_DIM
    cos_t = jnp.tile(jnp.concatenate([cos, cos], axis=-1), (1, reps))
    sin_t = jnp.tile(jnp.concatenate([-sin, sin], axis=-1), (1, reps))
    ident_c = jnp.ones((QK_TM, LANES), F32)
    ident_s = jnp.zeros((QK_TM, LANES), F32)
    return jnp.concatenate([ident_c, cos_t], axis=0), jnp.concatenate([ident_s, sin_t], axis=0)


def kernel(x_prompt, x_sample, cache_attn_k, cache_attn_v, state_ssd_fwd, state_ssd_bwd, state_rwkv_fwd, state_rwkv_bwd, c, c_ctx, mod_w, mod_b, norm_mix_g, norm_ffn_g, ffn_w_gate, ffn_w_up, ffn_w_down, ab_w_in, ab_w_out, attn_q_g, attn_k_g, ssd_conv_w, ssd_conv_b, ssd_dt_bias, ssd_a_log, ssd_d, ssd_norm_g, rwkv_mu, rwkv_w_r, rwkv_w_k, rwkv_w_v, rwkv_w0, rwkv_w1, rwkv_w2, rwkv_a0, rwkv_a1, rwkv_a2, rwkv_g1, rwkv_g2, rwkv_k_k, rwkv_k_a, rwkv_r_k, rwkv_ln_g, rwkv_ln_b, rwkv_w_o, final_norm_g):
    bf = lambda a: a.astype(BF16)
    x = jnp.concatenate([x_prompt.reshape(R_CTX, D_MODEL), x_sample.reshape(R_SMP, D_MODEL)], axis=0)

    cv = jnp.concatenate([c_ctx[None], c, jnp.zeros((8 - 1 - DEC_BATCH, D_MODEL), F32)], axis=0)
    m = _ada(cv, mod_w, mod_b)
    m = m[:, :1 + DEC_BATCH].reshape(2, 1 + DEC_BATCH, 6, D_MODEL)
    m = jnp.pad(m, ((0, 0), (0, 0), (0, 2), (0, 0)))
    row2 = lambda a: a.reshape(1, -1)

    w_in = jnp.pad(bf(ab_w_in[0]), ((0, 0), (0, AB_PAD - AB_IN_DIM)))
    qkv, z, xbc, dt = _inproj(x, row2(norm_mix_g[0]), m[0], w_in)
    cos_t, sin_t = _rope_tables()
    qg = jnp.tile(attn_q_g[0], ATTN_HEADS).reshape(1, -1)
    kg = jnp.tile(attn_k_g[0], ATTN_KV_HEADS).reshape(1, -1)
    qn, kn = _qkprep(qkv, qg, kg, cos_t, sin_t)
    ck = cache_attn_k[:, 0].reshape(DEC_BATCH, PAST_LEN, ATTN_KV_DIM)
    cvv = cache_attn_v[:, 0].reshape(DEC_BATCH, PAST_LEN, ATTN_KV_DIM)
    attn_c, attn_s = _attention(qn, kn, qkv, ck, cvv)
    attn = jnp.concatenate([attn_c, attn_s], axis=0)

    dtb = jnp.pad(ssd_dt_bias[0].reshape(1, -1), ((0, 0), (0, DT_PAD - 2 * SSD_HEADS)))
    a_row = jnp.pad((-jnp.exp(ssd_a_log[0])).reshape(1, -1), ((0, 0), (0, DT_PAD - 2 * SSD_HEADS)))
    d_row = jnp.repeat(ssd_d[0], SSD_HEAD_DIM).reshape(1, -1)
    ssd_args = (ssd_conv_w[0], row2(ssd_conv_b[0]), dtb, a_row, d_row, row2(ssd_norm_g[0]))
    zero_st = jnp.zeros((BATCH, SSD_HEADS, SSD_HEAD_DIM, SSD_STATE), F32)
    y_c, hf_c, hb_c = _ssd(z, xbc, dt, *ssd_args, zero_st, zero_st, seq=SEQ, nb=BATCH, row0=0)
    y_s, _, _ = _ssd(z, xbc, dt, *ssd_args, state_ssd_fwd[:, 0], state_ssd_bwd[:, 0],
                     seq=DEC_SEQ, nb=DEC_BATCH, row0=R_CTX)
    y_ssd = jnp.concatenate([y_c, y_s], axis=0)
    x = _mixres(x, m[0], attn, y_ssd, bf(ab_w_out[0]))
    x = _ffn(x, row2(norm_ffn_g[0]), m[0], bf(ffn_w_gate[0]), bf(ffn_w_up[0]), bf(ffn_w_down[0]),
             row2(final_norm_g), final=False)

    w1 = bf(jnp.concatenate([rwkv_w1[0, 0], rwkv_w1[0, 1]], axis=1))
    a1 = bf(jnp.concatenate([rwkv_a1[0, 0], rwkv_a1[0, 1]], axis=1))
    zpad = lambda w: bf(jnp.stack([jnp.concatenate([w[0], jnp.zeros_like(w[1])], axis=0),
                                   jnp.concatenate([jnp.zeros_like(w[0]), w[1]], axis=0)]))
    hs = _head_mask()
    ops, gg, bon = _rwkv_pre(x, row2(norm_mix_g[1]), m[1], rwkv_mu[0], bf(rwkv_w_r[0]), bf(rwkv_w_k[0]),
                             bf(rwkv_w_v[0]), w1, zpad(rwkv_w2[0]), a1, zpad(rwkv_a2[0]), bf(rwkv_g1[0]),
                             bf(rwkv_g2[0]), rwkv_w0[0], rwkv_a0[0], row2(rwkv_k_k[0]), row2(rwkv_k_a[0]),
                             rwkv_r_k[0].reshape(1, -1), hs)
    y_c, st_c = _scan_ctx(ops)
    y_s = _scan_smp(ops, jnp.stack([state_rwkv_fwd[:, 0], state_rwkv_bwd[:, 0]]))
    x = _rwkv_post(x, m[1], y_c, y_s, bon, gg, row2(rwkv_ln_g[0]), row2(rwkv_ln_b[0]), bf(rwkv_w_o[0]), hs)
    x = _ffn(x, row2(norm_ffn_g[1]), m[1], bf(ffn_w_gate[1]), bf(ffn_w_up[1]), bf(ffn_w_down[1]),
             row2(final_norm_g), final=True)

    y_prompt = x[:R_CTX].reshape(BATCH, SEQ, D_MODEL)
    y_sample = x[R_CTX:].reshape(DEC_BATCH, DEC_SEQ, D_MODEL)
    new_k = kn[:R_CTX].reshape(BATCH, 1, SEQ, ATTN_KV_HEADS, HEAD_DIM)
    new_v = qkv[:R_CTX, ATTN_Q_DIM + ATTN_KV_DIM:].reshape(BATCH, 1, SEQ, ATTN_KV_HEADS, HEAD_DIM)
    return (y_prompt, y_sample, new_k, new_v, hf_c[:, None], hb_c[:, None], st_c[0][:, None], st_c[1][:, None])
```

```python
import functools
import math

import jax
import jax.numpy as jnp
from jax import lax
from jax.experimental import pallas as pl
from jax.experimental.pallas import tpu as pltpu

F32 = jnp.float32
BF16 = jnp.bfloat16

D_MODEL = 1024
BATCH = 16
SEQ = 256
DEC_BATCH = 2
DEC_SEQ = 1024
PAST_LEN = 256
GRID_W = 64
ATTN_HEADS = 8
ATTN_KV_HEADS = 2
HEAD_DIM = 64
ROPE_THETA = 10000.0
ATTN_Q_DIM = ATTN_HEADS * HEAD_DIM
ATTN_KV_DIM = ATTN_KV_HEADS * HEAD_DIM
SSD_HEADS = 8
SSD_HEAD_DIM = 64
SSD_D_INNER = SSD_HEADS * SSD_HEAD_DIM
SSD_GROUPS = 2
SSD_STATE = 64
SSD_CONV_K = 5
SSD_CHUNK = 128
SSD_CONV_DIM = SSD_D_INNER + 2 * SSD_GROUPS * SSD_STATE
AB_IN_DIM = ATTN_Q_DIM + 2 * ATTN_KV_DIM + SSD_D_INNER + SSD_CONV_DIM + 2 * SSD_HEADS
RWKV_HEAD_DIM = 64
RWKV_HEADS = D_MODEL // RWKV_HEAD_DIM
FFN_DIM = (((8 * D_MODEL + 2) // 3 + 255) // 256) * 256
RMS_EPS = 1e-6
GN_EPS = 64e-5
L2_EPS = 1e-12

R_CTX = BATCH * SEQ
R_SMP = DEC_BATCH * DEC_SEQ
R_ALL = R_CTX + R_SMP
LANES = 128
QKV_DIM = ATTN_Q_DIM + 2 * ATTN_KV_DIM
DT_PAD = LANES
AB_PAD = QKV_DIM + SSD_D_INNER + SSD_CONV_DIM + DT_PAD
VMEM_LIMIT = 56 * 1024 * 1024


def _cparams(sem):
    return pltpu.CompilerParams(dimension_semantics=sem, vmem_limit_bytes=VMEM_LIMIT)


def _mod_index(i, tm):
    n_ctx = R_CTX // tm
    per = DEC_SEQ // tm
    return jnp.where(i < n_ctx, 0, 1 + jnp.maximum(i - n_ctx, 0) // per)


def _sigmoid(x):
    return 1.0 / (1.0 + jnp.exp(-x))


def _silu(x):
    return x * _sigmoid(x)


def _softplus(x):
    return jnp.maximum(x, 0.0) + jnp.log1p(jnp.exp(-jnp.abs(x)))


def _rms(x, g):
    return x * lax.rsqrt(jnp.mean(x * x, axis=-1, keepdims=True) + RMS_EPS) * g


def _head_allsum(x, hs):
    parts = [jnp.dot(x[:, j * LANES:(j + 1) * LANES], hs, precision=lax.Precision.HIGHEST,
                     preferred_element_type=F32) for j in range(x.shape[-1] // LANES)]
    return parts[0] if len(parts) == 1 else jnp.concatenate(parts, axis=1)


def _head_mask():
    lane = jnp.arange(LANES)
    return (lane[:, None] // HEAD_DIM == lane[None, :] // HEAD_DIM).astype(F32)


def _dot(a, b):
    return jnp.dot(a, b, preferred_element_type=F32)


ADA_TN = 1536


def _ada_kernel(c_ref, w_ref, b_ref, o_ref):
    s = _silu(c_ref[...]).astype(BF16)
    o_ref[0] = _dot(s, w_ref[0].astype(BF16)) + b_ref[0]


def _ada(cv, mod_w, mod_b):
    depth = mod_w.shape[0]
    n = mod_w.shape[2]
    return pl.pallas_call(
        _ada_kernel,
        out_shape=jax.ShapeDtypeStruct((depth, 8, n), F32),
        grid=(depth, n // ADA_TN),
        in_specs=[
            pl.BlockSpec((8, D_MODEL), lambda l, j: (0, 0)),
            pl.BlockSpec((1, D_MODEL, ADA_TN), lambda l, j: (l, 0, j)),
            pl.BlockSpec((1, 1, ADA_TN), lambda l, j: (l, 0, j)),
        ],
        out_specs=pl.BlockSpec((1, 8, ADA_TN), lambda l, j: (l, 0, j)),
        compiler_params=_cparams(("arbitrary", "arbitrary")),
        name="ada",
    )(cv, mod_w, mod_b.reshape(depth, 1, n))


INPROJ_TM = 512


def _inproj_kernel(x_ref, g_ref, mod_ref, w_ref, qkv_ref, z_ref, xbc_ref, dt_ref):
    m = mod_ref[0]
    h = _rms(x_ref[...], g_ref[...]) * (1 + m[1:2]) + m[0:1]
    p = _dot(h.astype(BF16), w_ref[...])
    qkv_ref[...] = p[:, 0:QKV_DIM]
    z_ref[...] = p[:, QKV_DIM:QKV_DIM + SSD_D_INNER]
    xbc_ref[...] = p[:, QKV_DIM + SSD_D_INNER:QKV_DIM + SSD_D_INNER + SSD_CONV_DIM]
    dt_ref[...] = p[:, QKV_DIM + SSD_D_INNER + SSD_CONV_DIM:AB_PAD]


def _inproj(x, g, mod, w_pad):
    tm = INPROJ_TM
    row = lambda i: (i, 0)
    return pl.pallas_call(
        _inproj_kernel,
        out_shape=(
            jax.ShapeDtypeStruct((R_ALL, QKV_DIM), F32),
            jax.ShapeDtypeStruct((R_ALL, SSD_D_INNER), F32),
            jax.ShapeDtypeStruct((R_ALL, SSD_CONV_DIM), F32),
            jax.ShapeDtypeStruct((R_ALL, DT_PAD), F32),
        ),
        grid=(R_ALL // tm,),
        in_specs=[
            pl.BlockSpec((tm, D_MODEL), row),
            pl.BlockSpec((1, D_MODEL), lambda i: (0, 0)),
            pl.BlockSpec((1, 8, D_MODEL), lambda i: (_mod_index(i, tm), 0, 0)),
            pl.BlockSpec((D_MODEL, AB_PAD), lambda i: (0, 0)),
        ],
        out_specs=(
            pl.BlockSpec((tm, QKV_DIM), row),
            pl.BlockSpec((tm, SSD_D_INNER), row),
            pl.BlockSpec((tm, SSD_CONV_DIM), row),
            pl.BlockSpec((tm, DT_PAD), row),
        ),
        compiler_params=_cparams(("arbitrary",)),
        name="inproj",
    )(x, g, mod, w_pad)


QK_TM = 256


def _qkprep_kernel(qkv_ref, qg_ref, kg_ref, cos_ref, sin_ref, hs_ref, qn_ref, kn_ref):
    cos = cos_ref[...]
    sin = sin_ref[...]
    hs = hs_ref[...]

    def norm_rope(x, g, reps):
        ms = _head_allsum(x * x, hs) * (1.0 / HEAD_DIM)
        y = x * lax.rsqrt(ms + RMS_EPS) * g
        n = y.shape[-1]
        lane = lax.broadcasted_iota(jnp.int32, y.shape, 1)
        half = HEAD_DIM // 2
        swapped = jnp.where((lane & half) == 0, pltpu.roll(y, n - half, axis=1), pltpu.roll(y, half, axis=1))
        c = jnp.concatenate([cos] * reps, axis=1) if reps > 1 else cos
        s = jnp.concatenate([sin] * reps, axis=1) if reps > 1 else sin
        return y * c + swapped * s

    q = qkv_ref[:, 0:ATTN_Q_DIM]
    k = qkv_ref[:, ATTN_Q_DIM:ATTN_Q_DIM + ATTN_KV_DIM]
    qn_ref[...] = norm_rope(q, qg_ref[...], ATTN_Q_DIM // LANES)
    kn_ref[...] = norm_rope(k, kg_ref[...], 1)


def _qkprep(qkv, qg, kg, cos_t, sin_t):
    tm = QK_TM
    n_ctx = R_CTX // tm
    per = DEC_SEQ // tm
    tab = lambda i: (jnp.where(i < n_ctx, 0, 1 + jnp.maximum(i - n_ctx, 0) % per), 0)
    return pl.pallas_call(
        _qkprep_kernel,
        out_shape=(
            jax.ShapeDtypeStruct((R_ALL, ATTN_Q_DIM), F32),
            jax.ShapeDtypeStruct((R_ALL, ATTN_KV_DIM), F32),
        ),
        grid=(R_ALL // tm,),
        in_specs=[
            pl.BlockSpec((tm, QKV_DIM), lambda i: (i, 0)),
            pl.BlockSpec((1, ATTN_Q_DIM), lambda i: (0, 0)),
            pl.BlockSpec((1, ATTN_KV_DIM), lambda i: (0, 0)),
            pl.BlockSpec((tm, LANES), tab),
            pl.BlockSpec((tm, LANES), tab),
            pl.BlockSpec((LANES, LANES), lambda i: (0, 0)),
        ],
        out_specs=(
            pl.BlockSpec((tm, ATTN_Q_DIM), lambda i: (i, 0)),
            pl.BlockSpec((tm, ATTN_KV_DIM), lambda i: (i, 0)),
        ),
        compiler_params=_cparams(("arbitrary",)),
        name="qkprep",
    )(qkv, qg, kg, cos_t, sin_t, _head_mask())


def _attn_core(q, ks, vs):
    tq = q.shape[0]
    grp = ATTN_HEADS // ATTN_KV_HEADS
    scale = HEAD_DIM ** -0.5
    outs = []
    for g in range(ATTN_KV_HEADS):
        sl = slice(g * HEAD_DIM, (g + 1) * HEAD_DIM)
        qs = jnp.concatenate(
            [q[:, (g * grp + j) * HEAD_DIM:(g * grp + j + 1) * HEAD_DIM] for j in range(grp)], axis=0).astype(BF16)
        ss = [lax.dot_general(qs, k[:, sl].astype(BF16), (((1,), (1,)), ((), ())),
                              preferred_element_type=F32) * scale for k in ks]
        m = ss[0].max(axis=-1, keepdims=True)
        for s in ss[1:]:
            m = jnp.maximum(m, s.max(axis=-1, keepdims=True))
        ps = [jnp.exp(s - m) for s in ss]
        l = ps[0].sum(axis=-1, keepdims=True)
        for p in ps[1:]:
            l = l + p.sum(axis=-1, keepdims=True)
        inv = 1.0 / l
        o = None
        for p, v in zip(ps, vs):
            t = _dot((p * inv).astype(BF16), v[:, sl].astype(BF16))
            o = t if o is None else o + t
        outs += [o[j * tq:(j + 1) * tq] for j in range(grp)]
    return jnp.concatenate(outs, axis=1)


def _attn_ctx_kernel(q_ref, k_ref, v_ref, o_ref):
    o_ref[...] = _attn_core(q_ref[...], [k_ref[...]], [v_ref[...]]).astype(BF16)


def _attn_smp_kernel(q_ref, k_ref, v_ref, ck_ref, cv_ref, o_ref):
    ks = [ck_ref[0], k_ref[...]]
    vs = [cv_ref[0], v_ref[...]]
    o_ref[...] = _attn_core(q_ref[...], ks, vs).astype(BF16)


ATTN_TQ = 128


def _attention(qn, kn, qkv, cache_k, cache_v):
    v_col = (ATTN_Q_DIM + ATTN_KV_DIM) // ATTN_KV_DIM
    ctx = pl.pallas_call(
        _attn_ctx_kernel,
        out_shape=jax.ShapeDtypeStruct((R_CTX, ATTN_Q_DIM), BF16),
        grid=(BATCH,),
        in_specs=[
            pl.BlockSpec((SEQ, ATTN_Q_DIM), lambda b: (b, 0)),
            pl.BlockSpec((SEQ, ATTN_KV_DIM), lambda b: (b, 0)),
            pl.BlockSpec((SEQ, ATTN_KV_DIM), lambda b: (b, v_col)),
        ],
        out_specs=pl.BlockSpec((SEQ, ATTN_Q_DIM), lambda b: (b, 0)),
        compiler_params=_cparams(("arbitrary",)),
        name="attn_ctx",
    )(qn, kn, qkv)
    nq = DEC_SEQ // ATTN_TQ
    q0 = R_CTX // ATTN_TQ
    s0 = R_CTX // DEC_SEQ
    smp = pl.pallas_call(
        _attn_smp_kernel,
        out_shape=jax.ShapeDtypeStruct((R_SMP, ATTN_Q_DIM), BF16),
        grid=(DEC_BATCH, nq),
        in_specs=[
            pl.BlockSpec((ATTN_TQ, ATTN_Q_DIM), lambda b, i: (q0 + b * nq + i, 0)),
            pl.BlockSpec((DEC_SEQ, ATTN_KV_DIM), lambda b, i: (s0 + b, 0)),
            pl.BlockSpec((DEC_SEQ, ATTN_KV_DIM), lambda b, i: (s0 + b, v_col)),
            pl.BlockSpec((1, PAST_LEN, ATTN_KV_DIM), lambda b, i: (b, 0, 0)),
            pl.BlockSpec((1, PAST_LEN, ATTN_KV_DIM), lambda b, i: (b, 0, 0)),
        ],
        out_specs=pl.BlockSpec((ATTN_TQ, ATTN_Q_DIM), lambda b, i: (b * nq + i, 0)),
        compiler_params=_cparams(("arbitrary", "arbitrary")),
        name="attn_smp",
    )(qn, kn, qkv, cache_k, cache_v)
    return ctx, smp


CONV_HALO = 8
SSD_NS = 1


def _cumsum_rows(a, reverse):
    n = a.shape[0]
    row = lax.broadcasted_iota(jnp.int32, a.shape, 0)
    s = 1
    while s < n:
        if reverse:
            a = a + jnp.where(row < n - s, pltpu.roll(a, n - s, axis=0), 0.0)
        else:
            a = a + jnp.where(row >= s, pltpu.roll(a, s, axis=0), 0.0)
        s *= 2
    return a


def _ssd_kernel(z_ref, xbc_ref, dt_ref, cw_ref, cb_ref, dtb_ref, a_ref, d_ref, g_ref, h0f_ref, h0b_ref,
                y_ref, hf_ref, hb_ref, pad_sc, xc_sc, dt_sc, y_sc, h_sc, *, seq):
    L = SSD_CHUNK
    nc = seq // L
    pad = SSD_CONV_K // 2
    zeros = jnp.zeros((CONV_HALO, SSD_CONV_DIM), F32)
    for s in range(SSD_NS):
        pad_sc[s, 0:CONV_HALO, :] = zeros
        pad_sc[s, seq + CONV_HALO:seq + 2 * CONV_HALO, :] = zeros
        pad_sc[s, CONV_HALO:seq + CONV_HALO, :] = xbc_ref[s * seq:(s + 1) * seq, :]
        h_sc[s, 0] = h0f_ref[s]
        h_sc[s, 1] = h0b_ref[s]

    def conv_chunk(c, carry):
        for s in range(SSD_NS):
            r0 = pl.multiple_of(c * L, L)
            win = pad_sc[s, pl.ds(r0, L + 2 * CONV_HALO), :]
            acc = cb_ref[...]
            for i in range(SSD_CONV_K):
                acc = acc + win[CONV_HALO - pad + i:CONV_HALO - pad + i + L, :] * cw_ref[i:i + 1, :]
            xc = _silu(acc)
            g0 = pl.multiple_of(s * seq + c * L, L)
            xc_sc[pl.ds(g0, L), :] = xc
            y_sc[pl.ds(g0, L), :] = xc[:, 0:SSD_D_INNER] * d_ref[...]
            dt_sc[pl.ds(g0, L), :] = _softplus(dt_ref[pl.ds(g0, L), :] + dtb_ref[...])
        return carry

    lax.fori_loop(0, nc, conv_chunk, 0)

    rr = lax.broadcasted_iota(jnp.int32, (L, L), 0)
    cc = lax.broadcasted_iota(jnp.int32, (L, L), 1)
    grp = SSD_HEADS // SSD_GROUPS
    P = SSD_HEAD_DIM
    N = SSD_STATE

    def make_chunk(s, dirn):
        mask = (rr >= cc) if dirn == 0 else (rr <= cc)

        def chunk(ci, carry):
            c = ci if dirn == 0 else nc - 1 - ci
            r0 = pl.multiple_of(s * seq + c * L, L)
            xs = xc_sc[pl.ds(r0, L), 0:SSD_D_INNER]
            bm = xc_sc[pl.ds(r0, L), SSD_D_INNER:SSD_D_INNER + SSD_GROUPS * N]
            cm = xc_sc[pl.ds(r0, L), SSD_D_INNER + SSD_GROUPS * N:SSD_CONV_DIM]
            dtc = dt_sc[pl.ds(r0, L), :]
            acs = _cumsum_rows(dtc * a_ref[...], reverse=(dirn == 1))
            acs_t = acs.T
            tot = acs[L - 1:L, :] if dirn == 0 else acs[0:1, :]
            dec_end = jnp.exp(tot - acs)
            eacs = jnp.exp(acs)
            cdec = jnp.exp(tot)
            cb = [lax.dot_general(cm[:, g * N:(g + 1) * N].astype(BF16), bm[:, g * N:(g + 1) * N].astype(BF16),
                                  (((1,), (1,)), ((), ())), preferred_element_type=F32)
                  for g in range(SSD_GROUPS)]
            for h in range(SSD_HEADS):
                g = h // grp
                ln = h + SSD_HEADS * dirn
                seg = jnp.exp(jnp.where(mask, acs[:, ln:ln + 1] - acs_t[ln:ln + 1, :], -jnp.inf))
                sc = (cb[g] * seg).astype(BF16)
                xdt = (xs[:, h * P:(h + 1) * P] * dtc[:, ln:ln + 1]).astype(BF16)
                hin = h_sc[s, dirn, h]
                ce = (cm[:, g * N:(g + 1) * N] * eacs[:, ln:ln + 1]).astype(BF16)
                yh = _dot(sc, xdt) + lax.dot_general(ce, hin.astype(BF16), (((1,), (1,)), ((), ())),
                                                     preferred_element_type=F32)
                bd = (bm[:, g * N:(g + 1) * N] * dec_end[:, ln:ln + 1]).astype(BF16)
                st = lax.dot_general(xdt, bd, (((0,), (0,)), ((), ())), preferred_element_type=F32)
                h_sc[s, dirn, h] = hin * cdec[:, ln:ln + 1] + st
                y_sc[pl.ds(r0, L), h * P:(h + 1) * P] += yh
            return carry

        return chunk

    chunks = [make_chunk(s, dirn) for s in range(SSD_NS) for dirn in range(2)]

    def all_chunks(ci, carry):
        for fn in chunks:
            carry = fn(ci, carry)
        return carry

    lax.fori_loop(0, nc, all_chunks, 0)

    def out_chunk(c, carry):
        r0 = pl.multiple_of(c * L, L)
        y = y_sc[pl.ds(r0, L), :] * _silu(z_ref[pl.ds(r0, L), :])
        y_ref[pl.ds(r0, L), :] = _rms(y, g_ref[...]).astype(BF16)
        return carry

    lax.fori_loop(0, SSD_NS * nc, out_chunk, 0)
    for s in range(SSD_NS):
        hf_ref[s] = h_sc[s, 0]
        hb_ref[s] = h_sc[s, 1]


def _ssd(z, xbc, dt, cw, cb, dtb, a_row, d_row, g, h0f, h0b, *, seq, nb, row0):
    ns = SSD_NS
    rows = ns * seq
    blk0 = row0 // rows
    row = lambda b: (blk0 + b, 0)
    fixed = lambda b: (0, 0)
    st = lambda b: (b, 0, 0, 0)
    st_shape = (nb, SSD_HEADS, SSD_HEAD_DIM, SSD_STATE)
    st_blk = (ns, SSD_HEADS, SSD_HEAD_DIM, SSD_STATE)
    return pl.pallas_call(
        functools.partial(_ssd_kernel, seq=seq),
        out_shape=(
            jax.ShapeDtypeStruct((nb * seq, SSD_D_INNER), BF16),
            jax.ShapeDtypeStruct(st_shape, F32),
            jax.ShapeDtypeStruct(st_shape, F32),
        ),
        grid=(nb // ns,),
        in_specs=[
            pl.BlockSpec((rows, SSD_D_INNER), row),
            pl.BlockSpec((rows, SSD_CONV_DIM), row),
            pl.BlockSpec((rows, DT_PAD), row),
            pl.BlockSpec((SSD_CONV_K, SSD_CONV_DIM), fixed),
            pl.BlockSpec((1, SSD_CONV_DIM), fixed),
            pl.BlockSpec((1, DT_PAD), fixed),
            pl.BlockSpec((1, DT_PAD), fixed),
            pl.BlockSpec((1, SSD_D_INNER), fixed),
            pl.BlockSpec((1, SSD_D_INNER), fixed),
            pl.BlockSpec(st_blk, st),
            pl.BlockSpec(st_blk, st),
        ],
        out_specs=(
            pl.BlockSpec((rows, SSD_D_INNER), lambda b: (b, 0)),
            pl.BlockSpec(st_blk, st),
            pl.BlockSpec(st_blk, st),
        ),
        scratch_shapes=[
            pltpu.VMEM((ns, seq + 2 * CONV_HALO, SSD_CONV_DIM), F32),
            pltpu.VMEM((rows, SSD_CONV_DIM), F32),
            pltpu.VMEM((rows, DT_PAD), F32),
            pltpu.VMEM((rows, SSD_D_INNER), F32),
            pltpu.VMEM((ns, 2, SSD_HEADS, SSD_HEAD_DIM, SSD_STATE), F32),
        ],
        compiler_params=_cparams(("arbitrary",)),
        name=f"ssd_{seq}",
    )(z, xbc, dt, cw, cb, dtb, a_row, d_row, g, h0f, h0b)


RES_TM = 512


def _mixres_kernel(x_ref, mod_ref, a1_ref, a2_ref, w_ref, o_ref):
    k1 = a1_ref.shape[1]
    out = _dot(a1_ref[...], w_ref[0:k1, :]) + _dot(a2_ref[...], w_ref[k1:, :])
    o_ref[...] = x_ref[...] + mod_ref[0][2:3] * out


def _mixres(x, mod, a1, a2, w):
    tm = RES_TM
    row = lambda i: (i, 0)
    return pl.pallas_call(
        _mixres_kernel,
        out_shape=jax.ShapeDtypeStruct((R_ALL, D_MODEL), F32),
        grid=(R_ALL // tm,),
        in_specs=[
            pl.BlockSpec((tm, D_MODEL), row),
            pl.BlockSpec((1, 8, D_MODEL), lambda i: (_mod_index(i, tm), 0, 0)),
            pl.BlockSpec((tm, a1.shape[1]), row),
            pl.BlockSpec((tm, a2.shape[1]), row),
            pl.BlockSpec(w.shape, lambda i: (0, 0)),
        ],
        out_specs=pl.BlockSpec((tm, D_MODEL), row),
        compiler_params=_cparams(("arbitrary",)),
        name="mixres",
    )(x, mod, a1, a2, w)


FFN_TM = 1024
FFN_TF = 256


def _ffn_kernel(x_ref, g_ref, mod_ref, wg_ref, wu_ref, wd_ref, fg_ref, o_ref, h_sc, acc_sc, *, final):
    j = pl.program_id(1)

    @pl.when(j == 0)
    def _():
        m = mod_ref[0]
        h = _rms(x_ref[...], g_ref[...]) * (1 + m[4:5]) + m[3:4]
        h_sc[...] = h.astype(BF16)
        acc_sc[...] = jnp.zeros_like(acc_sc)

    h = h_sc[...]
    hid = _silu(_dot(h, wg_ref[...])) * _dot(h, wu_ref[...])
    acc_sc[...] += _dot(hid.astype(BF16), wd_ref[...])

    @pl.when(j == pl.num_programs(1) - 1)
    def _():
        y = x_ref[...] + mod_ref[0][5:6] * acc_sc[...]
        if final:
            y = _rms(y, fg_ref[...])
        o_ref[...] = y


def _ffn(x, g, mod, wg, wu, wd, fg, *, final):
    tm, tf = FFN_TM, FFN_TF
    row = lambda i, j: (i, 0)
    return pl.pallas_call(
        functools.partial(_ffn_kernel, final=final),
        out_shape=jax.ShapeDtypeStruct((R_ALL, D_MODEL), F32),
        grid=(R_ALL // tm, FFN_DIM // tf),
        in_specs=[
            pl.BlockSpec((tm, D_MODEL), row),
            pl.BlockSpec((1, D_MODEL), lambda i, j: (0, 0)),
            pl.BlockSpec((1, 8, D_MODEL), lambda i, j: (_mod_index(i, tm), 0, 0)),
            pl.BlockSpec((D_MODEL, tf), lambda i, j: (0, j)),
            pl.BlockSpec((D_MODEL, tf), lambda i, j: (0, j)),
            pl.BlockSpec((tf, D_MODEL), lambda i, j: (j, 0)),
            pl.BlockSpec((1, D_MODEL), lambda i, j: (0, 0)),
        ],
        out_specs=pl.BlockSpec((tm, D_MODEL), row),
        scratch_shapes=[pltpu.VMEM((tm, D_MODEL), BF16), pltpu.VMEM((tm, D_MODEL), F32)],
        compiler_params=_cparams(("arbitrary", "arbitrary")),
        name="ffn_final" if final else "ffn",
    )(x, g, mod, wg, wu, wd, fg)


RW_TM = 256
HALO = 8
N_OPS = 9
DECAY_SCALE = float(math.exp(-0.5))


def _rwkv_pre_kernel(x_ref, xp_ref, xn_ref, g_ref, mod_ref, mu_ref, wr_ref, wk_ref, wv_ref, w1_ref, w2_ref,
                     a1_ref, a2_ref, g1_ref, g2_ref, w0_ref, a0_ref, kk_ref, ka_ref, rk_ref, hs_ref,
                     ops_o, g_o, bon_o):
    i = pl.program_id(0)
    tm = RW_TM
    n_ctx = R_CTX // tm
    per_c = SEQ // tm
    per_s = DEC_SEQ // tm
    rel = jnp.where(i < n_ctx, i % per_c, jnp.maximum(i - n_ctx, 0) % per_s)
    last = jnp.where(i < n_ctx, per_c - 1, per_s - 1)
    m = mod_ref[0]

    def nm(x):
        return _rms(x, g_ref[...]) * (1 + m[1:2]) + m[0:1]

    h = nm(x_ref[...])
    prev_row = jnp.where(rel == 0, 0.0, nm(xp_ref[...])[HALO - 1:HALO, :])
    next_row = jnp.where(rel == last, 0.0, nm(xn_ref[...])[0:1, :])
    row = lax.broadcasted_iota(jnp.int32, h.shape, 0)
    hp = jnp.where(row == 0, prev_row, pltpu.roll(h, 1, axis=0))
    hn = jnp.where(row == tm - 1, next_row, pltpu.roll(h, tm - 1, axis=0))
    dp = hp - h
    dn = hn - h

    def mix(idx):
        return (h + dp * mu_ref[0, idx:idx + 1, :] + dn * mu_ref[1, idx:idx + 1, :]).astype(BF16)

    r = _dot(mix(0), wr_ref[...])
    k = _dot(mix(2), wk_ref[...])
    v = _dot(mix(3), wv_ref[...])
    lw = jnp.tanh(_dot(mix(1), w1_ref[...])).astype(BF16)
    la = _dot(mix(4), a1_ref[...]).astype(BF16)
    gg = _dot(_sigmoid(_dot(mix(5), g1_ref[...])).astype(BF16), g2_ref[...])

    hs = hs_ref[...]
    kk = k * kk_ref[...]
    kk = kk * lax.rsqrt(_head_allsum(kk * kk, hs) + L2_EPS)
    ops_o[0] = r
    ops_o[1] = kk
    ops_o[2] = v
    g_o[...] = gg
    bsum = None
    for j in range(2):
        wl = w0_ref[j:j + 1, :] + _dot(lw, w2_ref[j])
        ops_o[3 + 3 * j] = jnp.exp(-(DECAY_SCALE * _sigmoid(wl)))
        a = _sigmoid(a0_ref[j:j + 1, :] + _dot(la, a2_ref[j]))
        kd = k * (1 + (a - 1) * ka_ref[...])
        ops_o[4 + 3 * j] = kd
        ops_o[5 + 3 * j] = kk * a
        t = r * kd * rk_ref[...]
        bsum = t if bsum is None else bsum + t
    bon_o[...] = _head_allsum(bsum, hs) * v


def _rwkv_pre(x, g, mod, mu, wr, wk, wv, w1, w2, a1, a2, g1, g2, w0, a0, k_k, k_a, r_k, hs):
    tm = RW_TM
    nblk = R_ALL // HALO
    per = tm // HALO
    row = lambda i: (i, 0)
    fixed2 = lambda i: (0, 0)
    fixed3 = lambda i: (0, 0, 0)
    full = lambda a: pl.BlockSpec(a.shape, fixed2 if a.ndim == 2 else fixed3)
    out = jax.ShapeDtypeStruct((R_ALL, D_MODEL), F32)
    orow = pl.BlockSpec((tm, D_MODEL), row)
    return pl.pallas_call(
        _rwkv_pre_kernel,
        out_shape=(jax.ShapeDtypeStruct((N_OPS, R_ALL, D_MODEL), F32), out, out),
        grid=(R_ALL // tm,),
        in_specs=[
            pl.BlockSpec((tm, D_MODEL), row),
            pl.BlockSpec((HALO, D_MODEL), lambda i: (jnp.maximum(i * per - 1, 0), 0)),
            pl.BlockSpec((HALO, D_MODEL), lambda i: (jnp.minimum((i + 1) * per, nblk - 1), 0)),
            pl.BlockSpec((1, D_MODEL), fixed2),
            pl.BlockSpec((1, 8, D_MODEL), lambda i: (_mod_index(i, tm), 0, 0)),
            full(mu), full(wr), full(wk), full(wv), full(w1), full(w2), full(a1), full(a2), full(g1), full(g2),
            full(w0), full(a0), full(k_k), full(k_a), full(r_k), full(hs),
        ],
        out_specs=(pl.BlockSpec((N_OPS, tm, D_MODEL), lambda i: (0, i, 0)), orow, orow),
        compiler_params=_cparams(("arbitrary",)),
        name="rwkv_pre",
    )(x, x, x, g, mod, mu, wr, wk, wv, w1, w2, a1, a2, g1, g2, w0, a0, k_k, k_a, r_k, hs)


SCAN_TT = 64
SCAN_ACC = 4
NK = RWKV_HEAD_DIM


def _scan_kernel(kt_ref, vt_ref, dk_ref, s0_ref, y_ref, st_ref, s_sc):
    d = pl.program_id(0)
    j = pl.program_id(2)

    @pl.when(j == 0)
    def _():
        s_sc[...] = s0_ref[0]

    def tix(i):
        return jnp.where(d == 0, i, SCAN_TT - 1 - i)

    def project(i):
        t = tix(i)
        parts = [None] * SCAN_ACC
        for k in range(NK):
            term = s_sc[k] * kt_ref[1, t, k:k + 1, :]
            parts[k % SCAN_ACC] = term if parts[k % SCAN_ACC] is None else parts[k % SCAN_ACC] + term
        while len(parts) > 1:
            parts = [parts[a] + parts[a + 1] for a in range(0, len(parts), 2)]
        return parts[0]

    def advance(i, sa, with_next):
        t = tix(i)
        tn = tix(i + 1)
        vv = vt_ref[0, t]
        y = None
        sa_next = None
        for k in range(NK):
            sn = (s_sc[k] * dk_ref[0, t, k:k + 1, :] - sa * dk_ref[2, t, k:k + 1, :]
                  + vv * dk_ref[1, t, k:k + 1, :])
            s_sc[k] = sn
            yk = sn * kt_ref[0, t, k:k + 1, :]
            y = yk if y is None else y + yk
            if with_next:
                ak = sn * kt_ref[1, tn, k:k + 1, :]
                sa_next = ak if sa_next is None else sa_next + ak
        y_ref[0, t] = y
        return sa_next

    sa_last = lax.fori_loop(0, SCAN_TT - 1, lambda i, sa: advance(i, sa, True), project(0))
    advance(SCAN_TT - 1, sa_last, False)

    @pl.when(j == pl.num_programs(2) - 1)
    def _():
        st_ref[0] = s_sc[...]


def _scan(kt, vt, dk, s0, *, kt_blk=0, vt_slab=0, dk_blk=0):
    _, seq, nv, chains = vt.shape
    tt = SCAN_TT
    nt = seq // tt
    tb = lambda d, j: jnp.where(d == 0, j, nt - 1 - j)
    sblk = pl.BlockSpec((1, NK, nv, LANES), lambda d, c, j: (d, 0, 0, c))
    yblk = pl.BlockSpec((1, tt, nv, LANES), lambda d, c, j: (d, tb(d, j), 0, c))
    return pl.pallas_call(
        _scan_kernel,
        out_shape=(jax.ShapeDtypeStruct((2, seq, nv, chains), F32), jax.ShapeDtypeStruct((2, NK, nv, chains), F32)),
        grid=(2, chains // LANES, nt),
        in_specs=[
            pl.BlockSpec((2, tt, NK, LANES), lambda d, c, j: (kt_blk, tb(d, j), 0, c)),
            pl.BlockSpec((1, tt, nv, LANES), lambda d, c, j: (vt_slab, tb(d, j), 0, c)),
            pl.BlockSpec((3, tt, NK, LANES), lambda d, c, j: (dk_blk + d, tb(d, j), 0, c)),
            sblk,
        ],
        out_specs=(yblk, sblk),
        scratch_shapes=[pltpu.VMEM((NK, nv, LANES), F32)],
        compiler_params=_cparams(("arbitrary", "arbitrary", "arbitrary")),
        name=f"rwkv_scan_{seq}",
    )(kt, vt, dk, s0)


POST_TM = 256


def _rwkv_post_kernel(x_ref, mod_ref, yc_ref, ys_ref, bon_ref, g_ref, lng_ref, lnb_ref, wo_ref, hs_ref, o_ref):
    is_ctx = pl.program_id(0) < R_CTX // POST_TM
    y = jnp.where(is_ctx, yc_ref[0] + yc_ref[1], ys_ref[0] + ys_ref[1])
    inv = 1.0 / RWKV_HEAD_DIM
    hs = hs_ref[...]
    d = y - _head_allsum(y, hs) * inv
    var = _head_allsum(d * d, hs) * inv
    yn = d * lax.rsqrt(var + GN_EPS) * lng_ref[...] + lnb_ref[...]
    out = ((yn + bon_ref[...]) * g_ref[...]).astype(BF16)
    o_ref[...] = x_ref[...] + mod_ref[0][2:3] * _dot(out, wo_ref[...])


def _rwkv_post(x, mod, yc, ys, bon, g, lng, lnb, wo, hs):
    tm = POST_TM
    n_ctx = R_CTX // tm
    row = lambda i: (i, 0)
    fixed = lambda i: (0, 0)
    big = pl.BlockSpec((tm, D_MODEL), row)
    ycb = pl.BlockSpec((2, tm, D_MODEL), lambda i: (0, jnp.minimum(i, n_ctx - 1), 0))
    ysb = pl.BlockSpec((2, tm, D_MODEL), lambda i: (0, jnp.maximum(i - n_ctx, 0), 0))
    return pl.pallas_call(
        _rwkv_post_kernel,
        out_shape=jax.ShapeDtypeStruct((R_ALL, D_MODEL), F32),
        grid=(R_ALL // tm,),
        in_specs=[
            big,
            pl.BlockSpec((1, 8, D_MODEL), lambda i: (_mod_index(i, tm), 0, 0)),
            ycb, ysb, big, big,
            pl.BlockSpec((1, D_MODEL), fixed),
            pl.BlockSpec((1, D_MODEL), fixed),
            pl.BlockSpec((D_MODEL, D_MODEL), fixed),
            pl.BlockSpec((LANES, LANES), fixed),
        ],
        out_specs=big,
        compiler_params=_cparams(("arbitrary",)),
        name="rwkv_post",
    )(x, mod, yc, ys, bon, g, lng, lnb, wo, hs)


RL_T = 32
NPAIR = D_MODEL // LANES


def _to_chains_kernel(x_ref, o_ref):
    cols = [jnp.swapaxes(x_ref[0, :, :, p * LANES:(p + 1) * LANES], 0, 1) for p in range(NPAIR)]
    for t in range(RL_T):
        tile = jnp.concatenate([c[t] for c in cols], axis=0).T
        o_ref[0, t, :, 0:LANES] = tile[0:NK]
        o_ref[0, t, :, LANES:2 * LANES] = tile[NK:2 * NK]


def _from_chains_kernel(y_ref, o_ref):
    tiles = []
    for t in range(RL_T):
        m = jnp.concatenate([y_ref[0, t, :, 0:LANES], y_ref[0, t, :, LANES:2 * LANES]], axis=0)
        tiles.append(m.T)
    for p in range(NPAIR):
        blk = jnp.stack([m[p * BATCH:(p + 1) * BATCH] for m in tiles], axis=0)
        o_ref[0, :, :, p * LANES:(p + 1) * LANES] = jnp.swapaxes(blk, 0, 1)


def _to_chains(ops):
    n = ops.shape[0]
    return pl.pallas_call(
        _to_chains_kernel,
        out_shape=jax.ShapeDtypeStruct((n, SEQ, NK, 2 * LANES), F32),
        grid=(n, SEQ // RL_T),
        in_specs=[pl.BlockSpec((1, BATCH, RL_T, D_MODEL), lambda o, j: (o, 0, j, 0))],
        out_specs=pl.BlockSpec((1, RL_T, NK, 2 * LANES), lambda o, j: (o, j, 0, 0)),
        compiler_params=_cparams(("arbitrary", "arbitrary")),
        name="to_chains",
    )(ops.reshape(n, R_ALL // SEQ, SEQ, D_MODEL))


def _from_chains(y):
    n = y.shape[0]
    out = pl.pallas_call(
        _from_chains_kernel,
        out_shape=jax.ShapeDtypeStruct((n, BATCH, SEQ, D_MODEL), F32),
        grid=(n, SEQ // RL_T),
        in_specs=[pl.BlockSpec((1, RL_T, NK, 2 * LANES), lambda o, j: (o, j, 0, 0))],
        out_specs=pl.BlockSpec((1, BATCH, RL_T, D_MODEL), lambda o, j: (o, 0, j, 0)),
        compiler_params=_cparams(("arbitrary", "arbitrary")),
        name="from_chains",
    )(y)
    return out.reshape(n, R_CTX, D_MODEL)


def _scan_ctx(ops):
    ch = _to_chains(ops)
    s0 = jnp.zeros((2, NK, NK, 2 * LANES), F32)
    y, st = _scan(ch, ch, ch, s0, vt_slab=2, dk_blk=1)
    st = st.reshape(2, NK, NK, 2, NPAIR, BATCH).transpose(0, 5, 4, 3, 2, 1)
    return _from_chains(y), st.reshape(2, BATCH, RWKV_HEADS, NK, NK)


SMP_REP = LANES // (NPAIR * DEC_BATCH)
SMP_VQ = SMP_REP // 2
SMP_NV = NK // SMP_VQ
SMP_PB = NPAIR * DEC_BATCH


def _smp_tiles(x_ref):
    rows = jnp.stack([x_ref[0, b, :, p * LANES:(p + 1) * LANES] for p in range(NPAIR) for b in range(DEC_BATCH)],
                     axis=0)
    rows = jnp.swapaxes(rows, 0, 1)
    return [jnp.concatenate([rows[t]] * SMP_REP, axis=0).T for t in range(RL_T)]


def _lane_replica(nrows):
    return lax.broadcasted_iota(jnp.int32, (nrows, LANES), 1) // SMP_PB


def _to_chains_smp_k_kernel(x_ref, o_ref):
    odd = _lane_replica(NK) >= SMP_VQ
    for t, tile in enumerate(_smp_tiles(x_ref)):
        o_ref[0, t] = jnp.where(odd, tile[NK:2 * NK], tile[0:NK])


def _to_chains_smp_v_kernel(x_ref, o_ref):
    rep = _lane_replica(SMP_NV)
    for t, tile in enumerate(_smp_tiles(x_ref)):
        acc = tile[0:SMP_NV]
        for r in range(1, SMP_REP):
            acc = jnp.where(rep == r, tile[r * SMP_NV:(r + 1) * SMP_NV], acc)
        o_ref[0, t] = acc


def _from_chains_smp_kernel(y_ref, o_ref):
    rep = _lane_replica(SMP_NV)
    res = []
    for t in range(RL_T):
        y = y_ref[0, t]
        m = jnp.concatenate([jnp.where(rep == r, y, 0.0) for r in range(SMP_REP)], axis=0).T
        acc = m[0:SMP_PB]
        for r in range(1, SMP_REP):
            acc = acc + m[r * SMP_PB:(r + 1) * SMP_PB]
        res.append(acc)
    out = jnp.swapaxes(jnp.stack(res, axis=0), 0, 1)
    for p in range(NPAIR):
        for b in range(DEC_BATCH):
            o_ref[0, b, :, p * LANES:(p + 1) * LANES] = out[p * DEC_BATCH + b]


def _scan_smp(ops, s0):
    nseq = R_ALL // DEC_SEQ
    blk0 = R_CTX // DEC_SEQ // DEC_BATCH
    ops4 = ops.reshape(N_OPS, nseq, DEC_SEQ, D_MODEL)
    nt = DEC_SEQ // RL_T
    in_blk = (1, DEC_BATCH, RL_T, D_MODEL)
    nk_ops = N_OPS - 1
    kidx = lambda o, j: (jnp.where(o < 6, o + 3, o - 6), blk0, j, 0)
    chk = pl.pallas_call(
        _to_chains_smp_k_kernel,
        out_shape=jax.ShapeDtypeStruct((nk_ops, DEC_SEQ, NK, LANES), F32),
        grid=(nk_ops, nt),
        in_specs=[pl.BlockSpec(in_blk, kidx)],
        out_specs=pl.BlockSpec((1, RL_T, NK, LANES), lambda o, j: (o, j, 0, 0)),
        compiler_params=_cparams(("arbitrary", "arbitrary")),
        name="to_chains_smp_k",
    )(ops4)
    chv = pl.pallas_call(
        _to_chains_smp_v_kernel,
        out_shape=jax.ShapeDtypeStruct((1, DEC_SEQ, SMP_NV, LANES), F32),
        grid=(1, nt),
        in_specs=[pl.BlockSpec(in_blk, lambda o, j: (2, blk0, j, 0))],
        out_specs=pl.BlockSpec((1, RL_T, SMP_NV, LANES), lambda o, j: (0, j, 0, 0)),
        compiler_params=_cparams(("arbitrary", "arbitrary")),
        name="to_chains_smp_v",
    )(ops4)
    s0 = s0.reshape(2, DEC_BATCH, NPAIR, 2, SMP_VQ, SMP_NV, NK).transpose(0, 6, 5, 3, 4, 2, 1)
    y, _ = _scan(chk, chv, chk, s0.reshape(2, NK, SMP_NV, LANES), kt_blk=3, vt_slab=0, dk_blk=0)
    out = pl.pallas_call(
        _from_chains_smp_kernel,
        out_shape=jax.ShapeDtypeStruct((2, DEC_BATCH, DEC_SEQ, D_MODEL), F32),
        grid=(2, nt),
        in_specs=[pl.BlockSpec((1, RL_T, SMP_NV, LANES), lambda o, j: (o, j, 0, 0))],
        out_specs=pl.BlockSpec((1, DEC_BATCH, RL_T, D_MODEL), lambda o, j: (o, 0, j, 0)),
        compiler_params=_cparams(("arbitrary", "arbitrary")),
        name="from_chains_smp",
    )(y)
    return out.reshape(2, R_SMP, D_MODEL)


def _rope_tables():
    rows = DEC_SEQ // GRID_W
    row = jnp.repeat(jnp.arange(rows, dtype=F32), GRID_W)
    col = jnp.tile(jnp.arange(GRID_W, dtype=F32), rows)
    n_freq = HEAD_DIM // 4
    inv_freq = ROPE_THETA ** (-jnp.arange(n_freq, dtype=F32) / n_freq)
    ang = jnp.concatenate([row[:, None] * inv_freq, col[:, None] * inv_freq], axis=-1)
    cos, sin = jnp.cos(ang), jnp.sin(ang)
    reps = LANES // HEAD_DIM
    cos_t = jnp.tile(jnp.concatenate([cos, cos], axis=-1), (1, reps))
    sin_t = jnp.tile(jnp.concatenate([-sin, sin], axis=-1), (1, reps))
    ident_c = jnp.ones((QK_TM, LANES), F32)
    ident_s = jnp.zeros((QK_TM, LANES), F32)
    return jnp.concatenate([ident_c, cos_t], axis=0), jnp.concatenate([ident_s, sin_t], axis=0)


def kernel(x_prompt, x_sample, cache_attn_k, cache_attn_v, state_ssd_fwd, state_ssd_bwd, state_rwkv_fwd, state_rwkv_bwd, c, c_ctx, mod_w, mod_b, norm_mix_g, norm_ffn_g, ffn_w_gate, ffn_w_up, ffn_w_down, ab_w_in, ab_w_out, attn_q_g, attn_k_g, ssd_conv_w, ssd_conv_b, ssd_dt_bias, ssd_a_log, ssd_d, ssd_norm_g, rwkv_mu, rwkv_w_r, rwkv_w_k, rwkv_w_v, rwkv_w0, rwkv_w1, rwkv_w2, rwkv_a0, rwkv_a1, rwkv_a2, rwkv_g1, rwkv_g2, rwkv_k_k, rwkv_k_a, rwkv_r_k, rwkv_ln_g, rwkv_ln_b, rwkv_w_o, final_norm_g):
    bf = lambda a: a.astype(BF16)
    x = jnp.concatenate([x_prompt.reshape(R_CTX, D_MODEL), x_sample.reshape(R_SMP, D_MODEL)], axis=0)

    cv = jnp.concatenate([c_ctx[None], c, jnp.zeros((8 - 1 - DEC_BATCH, D_MODEL), F32)], axis=0)
    m = _ada(cv, mod_w, mod_b)
    m = m[:, :1 + DEC_BATCH].reshape(2, 1 + DEC_BATCH, 6, D_MODEL)
    m = jnp.pad(m, ((0, 0), (0, 0), (0, 2), (0, 0)))
    row2 = lambda a: a.reshape(1, -1)

    w_in = jnp.pad(bf(ab_w_in[0]), ((0, 0), (0, AB_PAD - AB_IN_DIM)))
    qkv, z, xbc, dt = _inproj(x, row2(norm_mix_g[0]), m[0], w_in)
    cos_t, sin_t = _rope_tables()
    qg = jnp.tile(attn_q_g[0], ATTN_HEADS).reshape(1, -1)
    kg = jnp.tile(attn_k_g[0], ATTN_KV_HEADS).reshape(1, -1)
    qn, kn = _qkprep(qkv, qg, kg, cos_t, sin_t)
    ck = cache_attn_k[:, 0].reshape(DEC_BATCH, PAST_LEN, ATTN_KV_DIM)
    cvv = cache_attn_v[:, 0].reshape(DEC_BATCH, PAST_LEN, ATTN_KV_DIM)
    attn_c, attn_s = _attention(qn, kn, qkv, ck, cvv)
    attn = jnp.concatenate([attn_c, attn_s], axis=0)

    dtb = jnp.pad(ssd_dt_bias[0].reshape(1, -1), ((0, 0), (0, DT_PAD - 2 * SSD_HEADS)))
    a_row = jnp.pad((-jnp.exp(ssd_a_log[0])).reshape(1, -1), ((0, 0), (0, DT_PAD - 2 * SSD_HEADS)))
    d_row = jnp.repeat(ssd_d[0], SSD_HEAD_DIM).reshape(1, -1)
    ssd_args = (ssd_conv_w[0], row2(ssd_conv_b[0]), dtb, a_row, d_row, row2(ssd_norm_g[0]))
    zero_st = jnp.zeros((BATCH, SSD_HEADS, SSD_HEAD_DIM, SSD_STATE), F32)
    y_c, hf_c, hb_c = _ssd(z, xbc, dt, *ssd_args, zero_st, zero_st, seq=SEQ, nb=BATCH, row0=0)
    y_s, _, _ = _ssd(z, xbc, dt, *ssd_args, state_ssd_fwd[:, 0], state_ssd_bwd[:, 0],
                     seq=DEC_SEQ, nb=DEC_BATCH, row0=R_CTX)
    y_ssd = jnp.concatenate([y_c, y_s], axis=0)
    x = _mixres(x, m[0], attn, y_ssd, bf(ab_w_out[0]))
    x = _ffn(x, row2(norm_ffn_g[0]), m[0], bf(ffn_w_gate[0]), bf(ffn_w_up[0]), bf(ffn_w_down[0]),
             row2(final_norm_g), final=False)

    w1 = bf(jnp.concatenate([rwkv_w1[0, 0], rwkv_w1[0, 1]], axis=1))
    a1 = bf(jnp.concatenate([rwkv_a1[0, 0], rwkv_a1[0, 1]], axis=1))
    zpad = lambda w: bf(jnp.stack([jnp.concatenate([w[0], jnp.zeros_like(w[1])], axis=0),
                                   jnp.concatenate([jnp.zeros_like(w[0]), w[1]], axis=0)]))
    hs = _head_mask()
    ops, gg, bon = _rwkv_pre(x, row2(norm_mix_g[1]), m[1], rwkv_mu[0], bf(rwkv_w_r[0]), bf(rwkv_w_k[0]),
                             bf(rwkv_w_v[0]), w1, zpad(rwkv_w2[0]), a1, zpad(rwkv_a2[0]), bf(rwkv_g1[0]),
                             bf(rwkv_g2[0]), rwkv_w0[0], rwkv_a0[0], row2(rwkv_k_k[0]), row2(rwkv_k_a[0]),
                             rwkv_r_k[0].reshape(1, -1), hs)
    y_c, st_c = _scan_ctx(ops)
    y_s = _scan_smp(ops, jnp.stack([state_rwkv_fwd[:, 0], state_rwkv_bwd[:, 0]]))
    x = _rwkv_post(x, m[1], y_c, y_s, bon, gg, row2(rwkv_ln_g[0]), row2(rwkv_ln_b[0]), bf(rwkv_w_o[0]), hs)
    x = _ffn(x, row2(norm_ffn_g[1]), m[1], bf(ffn_w_gate[1]), bf(ffn_w_up[1]), bf(ffn_w_down[1]),
             row2(final_norm_g), final=True)

    y_prompt = x[:R_CTX].reshape(BATCH, SEQ, D_MODEL)
    y_sample = x[R_CTX:].reshape(DEC_BATCH, DEC_SEQ, D_MODEL)
    new_k = kn[:R_CTX].reshape(BATCH, 1, SEQ, ATTN_KV_HEADS, HEAD_DIM)
    new_v = qkv[:R_CTX, ATTN_Q_DIM + ATTN_KV_DIM:].reshape(BATCH, 1, SEQ, ATTN_KV_HEADS, HEAD_DIM)
    return (y_prompt, y_sample, new_k, new_v, hf_c[:, None], hb_c[:, None], st_c[0][:, None], st_c[1][:, None])
```

```python
import functools
import math

import jax
import jax.numpy as jnp
from jax import lax
from jax.experimental import pallas as pl
from jax.experimental.pallas import tpu as pltpu

F32 = jnp.float32
BF16 = jnp.bfloat16

D_MODEL = 1024
BATCH = 16
SEQ = 256
DEC_BATCH = 2
DEC_SEQ = 1024
PAST_LEN = 256
GRID_W = 64
ATTN_HEADS = 8
ATTN_KV_HEADS = 2
HEAD_DIM = 64
ROPE_THETA = 10000.0
ATTN_Q_DIM = ATTN_HEADS * HEAD_DIM
ATTN_KV_DIM = ATTN_KV_HEADS * HEAD_DIM
SSD_HEADS = 8
SSD_HEAD_DIM = 64
SSD_D_INNER = SSD_HEADS * SSD_HEAD_DIM
SSD_GROUPS = 2
SSD_STATE = 64
SSD_CONV_K = 5
SSD_CHUNK = 128
SSD_CONV_DIM = SSD_D_INNER + 2 * SSD_GROUPS * SSD_STATE
AB_IN_DIM = ATTN_Q_DIM + 2 * ATTN_KV_DIM + SSD_D_INNER + SSD_CONV_DIM + 2 * SSD_HEADS
RWKV_HEAD_DIM = 64
RWKV_HEADS = D_MODEL // RWKV_HEAD_DIM
FFN_DIM = (((8 * D_MODEL + 2) // 3 + 255) // 256) * 256
RMS_EPS = 1e-6
GN_EPS = 64e-5
L2_EPS = 1e-12

R_CTX = BATCH * SEQ
R_SMP = DEC_BATCH * DEC_SEQ
R_ALL = R_CTX + R_SMP
LANES = 128
QKV_DIM = ATTN_Q_DIM + 2 * ATTN_KV_DIM
DT_PAD = LANES
AB_PAD = QKV_DIM + SSD_D_INNER + SSD_CONV_DIM + DT_PAD
VMEM_LIMIT = 56 * 1024 * 1024


def _cparams(sem):
    return pltpu.CompilerParams(dimension_semantics=sem, vmem_limit_bytes=VMEM_LIMIT)


def _mod_index(i, tm):
    n_ctx = R_CTX // tm
    per = DEC_SEQ // tm
    return jnp.where(i < n_ctx, 0, 1 + jnp.maximum(i - n_ctx, 0) // per)


def _sigmoid(x):
    return 1.0 / (1.0 + jnp.exp(-x))


def _silu(x):
    return x * _sigmoid(x)


def _softplus(x):
    return jnp.maximum(x, 0.0) + jnp.log1p(jnp.exp(-jnp.abs(x)))


def _rms(x, g):
    return x * lax.rsqrt(jnp.mean(x * x, axis=-1, keepdims=True) + RMS_EPS) * g


def _head_allsum(x, hs):
    parts = [jnp.dot(x[:, j * LANES:(j + 1) * LANES], hs, precision=lax.Precision.HIGHEST,
                     preferred_element_type=F32) for j in range(x.shape[-1] // LANES)]
    return parts[0] if len(parts) == 1 else jnp.concatenate(parts, axis=1)


def _head_mask():
    lane = jnp.arange(LANES)
    return (lane[:, None] // HEAD_DIM == lane[None, :] // HEAD_DIM).astype(F32)


def _dot(a, b):
    return jnp.dot(a, b, preferred_element_type=F32)


ADA_TN = 1536


def _ada_kernel(c_ref, w_ref, b_ref, o_ref):
    s = _silu(c_ref[...]).astype(BF16)
    o_ref[0] = _dot(s, w_ref[0].astype(BF16)) + b_ref[0]


def _ada(cv, mod_w, mod_b):
    depth = mod_w.shape[0]
    n = mod_w.shape[2]
    return pl.pallas_call(
        _ada_kernel,
        out_shape=jax.ShapeDtypeStruct((depth, 8, n), F32),
        grid=(depth, n // ADA_TN),
        in_specs=[
            pl.BlockSpec((8, D_MODEL), lambda l, j: (0, 0)),
            pl.BlockSpec((1, D_MODEL, ADA_TN), lambda l, j: (l, 0, j)),
            pl.BlockSpec((1, 1, ADA_TN), lambda l, j: (l, 0, j)),
        ],
        out_specs=pl.BlockSpec((1, 8, ADA_TN), lambda l, j: (l, 0, j)),
        compiler_params=_cparams(("arbitrary", "arbitrary")),
        name="ada",
    )(cv, mod_w, mod_b.reshape(depth, 1, n))


INPROJ_TM = 512


def _inproj_kernel(x_ref, g_ref, mod_ref, w_ref, qkv_ref, z_ref, xbc_ref, dt_ref):
    m = mod_ref[0]
    h = _rms(x_ref[...], g_ref[...]) * (1 + m[1:2]) + m[0:1]
    p = _dot(h.astype(BF16), w_ref[...])
    qkv_ref[...] = p[:, 0:QKV_DIM]
    z_ref[...] = p[:, QKV_DIM:QKV_DIM + SSD_D_INNER]
    xbc_ref[...] = p[:, QKV_DIM + SSD_D_INNER:QKV_DIM + SSD_D_INNER + SSD_CONV_DIM]
    dt_ref[...] = p[:, QKV_DIM + SSD_D_INNER + SSD_CONV_DIM:AB_PAD]


def _inproj(x, g, mod, w_pad):
    tm = INPROJ_TM
    row = lambda i: (i, 0)
    return pl.pallas_call(
        _inproj_kernel,
        out_shape=(
            jax.ShapeDtypeStruct((R_ALL, QKV_DIM), F32),
            jax.ShapeDtypeStruct((R_ALL, SSD_D_INNER), F32),
            jax.ShapeDtypeStruct((R_ALL, SSD_CONV_DIM), F32),
            jax.ShapeDtypeStruct((R_ALL, DT_PAD), F32),
        ),
        grid=(R_ALL // tm,),
        in_specs=[
            pl.BlockSpec((tm, D_MODEL), row),
            pl.BlockSpec((1, D_MODEL), lambda i: (0, 0)),
            pl.BlockSpec((1, 8, D_MODEL), lambda i: (_mod_index(i, tm), 0, 0)),
            pl.BlockSpec((D_MODEL, AB_PAD), lambda i: (0, 0)),
        ],
        out_specs=(
            pl.BlockSpec((tm, QKV_DIM), row),
            pl.BlockSpec((tm, SSD_D_INNER), row),
            pl.BlockSpec((tm, SSD_CONV_DIM), row),
            pl.BlockSpec((tm, DT_PAD), row),
        ),
        compiler_params=_cparams(("arbitrary",)),
        name="inproj",
    )(x, g, mod, w_pad)


QK_TM = 256


def _qkprep_kernel(qkv_ref, qg_ref, kg_ref, cos_ref, sin_ref, hs_ref, qn_ref, kn_ref):
    cos = cos_ref[...]
    sin = sin_ref[...]
    hs = hs_ref[...]

    def norm_rope(x, g, reps):
        ms = _head_allsum(x * x, hs) * (1.0 / HEAD_DIM)
        y = x * lax.rsqrt(ms + RMS_EPS) * g
        n = y.shape[-1]
        lane = lax.broadcasted_iota(jnp.int32, y.shape, 1)
        half = HEAD_DIM // 2
        swapped = jnp.where((lane & half) == 0, pltpu.roll(y, n - half, axis=1), pltpu.roll(y, half, axis=1))
        c = jnp.concatenate([cos] * reps, axis=1) if reps > 1 else cos
        s = jnp.concatenate([sin] * reps, axis=1) if reps > 1 else sin
        return y * c + swapped * s

    q = qkv_ref[:, 0:ATTN_Q_DIM]
    k = qkv_ref[:, ATTN_Q_DIM:ATTN_Q_DIM + ATTN_KV_DIM]
    qn_ref[...] = norm_rope(q, qg_ref[...], ATTN_Q_DIM // LANES)
    kn_ref[...] = norm_rope(k, kg_ref[...], 1)


def _qkprep(qkv, qg, kg, cos_t, sin_t):
    tm = QK_TM
    n_ctx = R_CTX // tm
    per = DEC_SEQ // tm
    tab = lambda i: (jnp.where(i < n_ctx, 0, 1 + jnp.maximum(i - n_ctx, 0) % per), 0)
    return pl.pallas_call(
        _qkprep_kernel,
        out_shape=(
            jax.ShapeDtypeStruct((R_ALL, ATTN_Q_DIM), F32),
            jax.ShapeDtypeStruct((R_ALL, ATTN_KV_DIM), F32),
        ),
        grid=(R_ALL // tm,),
        in_specs=[
            pl.BlockSpec((tm, QKV_DIM), lambda i: (i, 0)),
            pl.BlockSpec((1, ATTN_Q_DIM), lambda i: (0, 0)),
            pl.BlockSpec((1, ATTN_KV_DIM), lambda i: (0, 0)),
            pl.BlockSpec((tm, LANES), tab),
            pl.BlockSpec((tm, LANES), tab),
            pl.BlockSpec((LANES, LANES), lambda i: (0, 0)),
        ],
        out_specs=(
            pl.BlockSpec((tm, ATTN_Q_DIM), lambda i: (i, 0)),
            pl.BlockSpec((tm, ATTN_KV_DIM), lambda i: (i, 0)),
        ),
        compiler_params=_cparams(("arbitrary",)),
        name="qkprep",
    )(qkv, qg, kg, cos_t, sin_t, _head_mask())


def _attn_core(q, ks, vs):
    tq = q.shape[0]
    grp = ATTN_HEADS // ATTN_KV_HEADS
    scale = HEAD_DIM ** -0.5
    outs = []
    for g in range(ATTN_KV_HEADS):
        sl = slice(g * HEAD_DIM, (g + 1) * HEAD_DIM)
        qs = jnp.concatenate(
            [q[:, (g * grp + j) * HEAD_DIM:(g * grp + j + 1) * HEAD_DIM] for j in range(grp)], axis=0).astype(BF16)
        ss = [lax.dot_general(qs, k[:, sl].astype(BF16), (((1,), (1,)), ((), ())),
                              preferred_element_type=F32) * scale for k in ks]
        m = ss[0].max(axis=-1, keepdims=True)
        for s in ss[1:]:
            m = jnp.maximum(m, s.max(axis=-1, keepdims=True))
        ps = [jnp.exp(s - m) for s in ss]
        l = ps[0].sum(axis=-1, keepdims=True)
        for p in ps[1:]:
            l = l + p.sum(axis=-1, keepdims=True)
        inv = 1.0 / l
        o = None
        for p, v in zip(ps, vs):
            t = _dot((p * inv).astype(BF16), v[:, sl].astype(BF16))
            o = t if o is None else o + t
        outs += [o[j * tq:(j + 1) * tq] for j in range(grp)]
    return jnp.concatenate(outs, axis=1)


def _attn_ctx_kernel(q_ref, k_ref, v_ref, o_ref):
    o_ref[...] = _attn_core(q_ref[...], [k_ref[...]], [v_ref[...]]).astype(BF16)


def _attn_smp_kernel(q_ref, k_ref, v_ref, ck_ref, cv_ref, o_ref):
    ks = [ck_ref[0], k_ref[...]]
    vs = [cv_ref[0], v_ref[...]]
    o_ref[...] = _attn_core(q_ref[...], ks, vs).astype(BF16)


ATTN_TQ = 128


def _attention(qn, kn, qkv, cache_k, cache_v):
    v_col = (ATTN_Q_DIM + ATTN_KV_DIM) // ATTN_KV_DIM
    ctx = pl.pallas_call(
        _attn_ctx_kernel,
        out_shape=jax.ShapeDtypeStruct((R_CTX, ATTN_Q_DIM), BF16),
        grid=(BATCH,),
        in_specs=[
            pl.BlockSpec((SEQ, ATTN_Q_DIM), lambda b: (b, 0)),
            pl.BlockSpec((SEQ, ATTN_KV_DIM), lambda b: (b, 0)),
            pl.BlockSpec((SEQ, ATTN_KV_DIM), lambda b: (b, v_col)),
        ],
        out_specs=pl.BlockSpec((SEQ, ATTN_Q_DIM), lambda b: (b, 0)),
        compiler_params=_cparams(("arbitrary",)),
        name="attn_ctx",
    )(qn, kn, qkv)
    nq = DEC_SEQ // ATTN_TQ
    q0 = R_CTX // ATTN_TQ
    s0 = R_CTX // DEC_SEQ
    smp = pl.pallas_call(
        _attn_smp_kernel,
        out_shape=jax.ShapeDtypeStruct((R_SMP, ATTN_Q_DIM), BF16),
        grid=(DEC_BATCH, nq),
        in_specs=[
            pl.BlockSpec((ATTN_TQ, ATTN_Q_DIM), lambda b, i: (q0 + b * nq + i, 0)),
            pl.BlockSpec((DEC_SEQ, ATTN_KV_DIM), lambda b, i: (s0 + b, 0)),
            pl.BlockSpec((DEC_SEQ, ATTN_KV_DIM), lambda b, i: (s0 + b, v_col)),
            pl.BlockSpec((1, PAST_LEN, ATTN_KV_DIM), lambda b, i: (b, 0, 0)),
            pl.BlockSpec((1, PAST_LEN, ATTN_KV_DIM), lambda b, i: (b, 0, 0)),
        ],
        out_specs=pl.BlockSpec((ATTN_TQ, ATTN_Q_DIM), lambda b, i: (b * nq + i, 0)),
        compiler_params=_cparams(("arbitrary", "arbitrary")),
        name="attn_smp",
    )(qn, kn, qkv, cache_k, cache_v)
    return ctx, smp


CONV_HALO = 8
SSD_NS = 1


def _cumsum_rows(a, reverse):
    n = a.shape[0]
    row = lax.broadcasted_iota(jnp.int32, a.shape, 0)
    s = 1
    while s < n:
        if reverse:
            a = a + jnp.where(row < n - s, pltpu.roll(a, n - s, axis=0), 0.0)
        else:
            a = a + jnp.where(row >= s, pltpu.roll(a, s, axis=0), 0.0)
        s *= 2
    return a


def _ssd_kernel(z_ref, xbc_ref, dt_ref, cw_ref, cb_ref, dtb_ref, a_ref, d_ref, g_ref, h0f_ref, h0b_ref,
                y_ref, hf_ref, hb_ref, pad_sc, xc_sc, dt_sc, y_sc, h_sc, *, seq):
    L = SSD_CHUNK
    nc = seq // L
    pad = SSD_CONV_K // 2
    zeros = jnp.zeros((CONV_HALO, SSD_CONV_DIM), F32)
    for s in range(SSD_NS):
        pad_sc[s, 0:CONV_HALO, :] = zeros
        pad_sc[s, seq + CONV_HALO:seq + 2 * CONV_HALO, :] = zeros
        pad_sc[s, CONV_HALO:seq + CONV_HALO, :] = xbc_ref[s * seq:(s + 1) * seq, :]
        h_sc[s, 0] = h0f_ref[s]
        h_sc[s, 1] = h0b_ref[s]

    def conv_chunk(c, carry):
        for s in range(SSD_NS):
            r0 = pl.multiple_of(c * L, L)
            win = pad_sc[s, pl.ds(r0, L + 2 * CONV_HALO), :]
            acc = cb_ref[...]
            for i in range(SSD_CONV_K):
                acc = acc + win[CONV_HALO - pad + i:CONV_HALO - pad + i + L, :] * cw_ref[i:i + 1, :]
            xc = _silu(acc)
            g0 = pl.multiple_of(s * seq + c * L, L)
            xc_sc[pl.ds(g0, L), :] = xc
            y_sc[pl.ds(g0, L), :] = xc[:, 0:SSD_D_INNER] * d_ref[...]
            dt_sc[pl.ds(g0, L), :] = _softplus(dt_ref[pl.ds(g0, L), :] + dtb_ref[...])
        return carry

    lax.fori_loop(0, nc, conv_chunk, 0)

    rr = lax.broadcasted_iota(jnp.int32, (L, L), 0)
    cc = lax.broadcasted_iota(jnp.int32, (L, L), 1)
    grp = SSD_HEADS // SSD_GROUPS
    P = SSD_HEAD_DIM
    N = SSD_STATE

    def make_chunk(s, dirn):
        mask = (rr >= cc) if dirn == 0 else (rr <= cc)

        def chunk(ci, carry):
            c = ci if dirn == 0 else nc - 1 - ci
            r0 = pl.multiple_of(s * seq + c * L, L)
            xs = xc_sc[pl.ds(r0, L), 0:SSD_D_INNER]
            bm = xc_sc[pl.ds(r0, L), SSD_D_INNER:SSD_D_INNER + SSD_GROUPS * N]
            cm = xc_sc[pl.ds(r0, L), SSD_D_INNER + SSD_GROUPS * N:SSD_CONV_DIM]
            dtc = dt_sc[pl.ds(r0, L), :]
            acs = _cumsum_rows(dtc * a_ref[...], reverse=(dirn == 1))
            acs_t = acs.T
            tot = acs[L - 1:L, :] if dirn == 0 else acs[0:1, :]
            dec_end = jnp.exp(tot - acs)
            eacs = jnp.exp(acs)
            cdec = jnp.exp(tot)
            cb = [lax.dot_general(cm[:, g * N:(g + 1) * N].astype(BF16), bm[:, g * N:(g + 1) * N].astype(BF16),
                                  (((1,), (1,)), ((), ())), preferred_element_type=F32)
                  for g in range(SSD_GROUPS)]
            for h in range(SSD_HEADS):
                g = h // grp
                ln = h + SSD_HEADS * dirn
                seg = jnp.exp(jnp.where(mask, acs[:, ln:ln + 1] - acs_t[ln:ln + 1, :], -jnp.inf))
                sc = (cb[g] * seg).astype(BF16)
                xdt = (xs[:, h * P:(h + 1) * P] * dtc[:, ln:ln + 1]).astype(BF16)
                hin = h_sc[s, dirn, h]
                ce = (cm[:, g * N:(g + 1) * N] * eacs[:, ln:ln + 1]).astype(BF16)
                yh = _dot(sc, xdt) + lax.dot_general(ce, hin.astype(BF16), (((1,), (1,)), ((), ())),
                                                     preferred_element_type=F32)
                bd = (bm[:, g * N:(g + 1) * N] * dec_end[:, ln:ln + 1]).astype(BF16)
                st = lax.dot_general(xdt, bd, (((0,), (0,)), ((), ())), preferred_element_type=F32)
                h_sc[s, dirn, h] = hin * cdec[:, ln:ln + 1] + st
                y_sc[pl.ds(r0, L), h * P:(h + 1) * P] += yh
            return carry

        return chunk

    chunks = [make_chunk(s, dirn) for s in range(SSD_NS) for dirn in range(2)]

    def all_chunks(ci, carry):
        for fn in chunks:
            carry = fn(ci, carry)
        return carry

    lax.fori_loop(0, nc, all_chunks, 0)

    def out_chunk(c, carry):
        r0 = pl.multiple_of(c * L, L)
        y = y_sc[pl.ds(r0, L), :] * _silu(z_ref[pl.ds(r0, L), :])
        y_ref[pl.ds(r0, L), :] = _rms(y, g_ref[...]).astype(BF16)
        return carry

    lax.fori_loop(0, SSD_NS * nc, out_chunk, 0)
    for s in range(SSD_NS):
        hf_ref[s] = h_sc[s, 0]
        hb_ref[s] = h_sc[s, 1]


def _ssd(z, xbc, dt, cw, cb, dtb, a_row, d_row, g, h0f, h0b, *, seq, nb, row0):
    ns = SSD_NS
    rows = ns * seq
    blk0 = row0 // rows
    row = lambda b: (blk0 + b, 0)
    fixed = lambda b: (0, 0)
    st = lambda b: (b, 0, 0, 0)
    st_shape = (nb, SSD_HEADS, SSD_HEAD_DIM, SSD_STATE)
    st_blk = (ns, SSD_HEADS, SSD_HEAD_DIM, SSD_STATE)
    return pl.pallas_call(
        functools.partial(_ssd_kernel, seq=seq),
        out_shape=(
            jax.ShapeDtypeStruct((nb * seq, SSD_D_INNER), BF16),
            jax.ShapeDtypeStruct(st_shape, F32),
            jax.ShapeDtypeStruct(st_shape, F32),
        ),
        grid=(nb // ns,),
        in_specs=[
            pl.BlockSpec((rows, SSD_D_INNER), row),
            pl.BlockSpec((rows, SSD_CONV_DIM), row),
            pl.BlockSpec((rows, DT_PAD), row),
            pl.BlockSpec((SSD_CONV_K, SSD_CONV_DIM), fixed),
            pl.BlockSpec((1, SSD_CONV_DIM), fixed),
            pl.BlockSpec((1, DT_PAD), fixed),
            pl.BlockSpec((1, DT_PAD), fixed),
            pl.BlockSpec((1, SSD_D_INNER), fixed),
            pl.BlockSpec((1, SSD_D_INNER), fixed),
            pl.BlockSpec(st_blk, st),
            pl.BlockSpec(st_blk, st),
        ],
        out_specs=(
            pl.BlockSpec((rows, SSD_D_INNER), lambda b: (b, 0)),
            pl.BlockSpec(st_blk, st),
            pl.BlockSpec(st_blk, st),
        ),
        scratch_shapes=[
            pltpu.VMEM((ns, seq + 2 * CONV_HALO, SSD_CONV_DIM), F32),
            pltpu.VMEM((rows, SSD_CONV_DIM), F32),
            pltpu.VMEM((rows, DT_PAD), F32),
            pltpu.VMEM((rows, SSD_D_INNER), F32),
            pltpu.VMEM((ns, 2, SSD_HEADS, SSD_HEAD_DIM, SSD_STATE), F32),
        ],
        compiler_params=_cparams(("arbitrary",)),
        name=f"ssd_{seq}",
    )(z, xbc, dt, cw, cb, dtb, a_row, d_row, g, h0f, h0b)


RES_TM = 512


def _mixres_kernel(x_ref, mod_ref, a1_ref, a2_ref, w_ref, o_ref):
    k1 = a1_ref.shape[1]
    out = _dot(a1_ref[...], w_ref[0:k1, :]) + _dot(a2_ref[...], w_ref[k1:, :])
    o_ref[...] = x_ref[...] + mod_ref[0][2:3] * out


def _mixres(x, mod, a1, a2, w):
    tm = RES_TM
    row = lambda i: (i, 0)
    return pl.pallas_call(
        _mixres_kernel,
        out_shape=jax.ShapeDtypeStruct((R_ALL, D_MODEL), F32),
        grid=(R_ALL // tm,),
        in_specs=[
            pl.BlockSpec((tm, D_MODEL), row),
            pl.BlockSpec((1, 8, D_MODEL), lambda i: (_mod_index(i, tm), 0, 0)),
            pl.BlockSpec((tm, a1.shape[1]), row),
            pl.BlockSpec((tm, a2.shape[1]), row),
            pl.BlockSpec(w.shape, lambda i: (0, 0)),
        ],
        out_specs=pl.BlockSpec((tm, D_MODEL), row),
        compiler_params=_cparams(("arbitrary",)),
        name="mixres",
    )(x, mod, a1, a2, w)


FFN_TM = 1024
FFN_TF = 256


def _ffn_kernel(x_ref, g_ref, mod_ref, wg_ref, wu_ref, wd_ref, fg_ref, o_ref, h_sc, acc_sc, *, final):
    j = pl.program_id(1)

    @pl.when(j == 0)
    def _():
        m = mod_ref[0]
        h = _rms(x_ref[...], g_ref[...]) * (1 + m[4:5]) + m[3:4]
        h_sc[...] = h.astype(BF16)
        acc_sc[...] = jnp.zeros_like(acc_sc)

    h = h_sc[...]
    hid = _silu(_dot(h, wg_ref[...])) * _dot(h, wu_ref[...])
    acc_sc[...] += _dot(hid.astype(BF16), wd_ref[...])

    @pl.when(j == pl.num_programs(1) - 1)
    def _():
        y = x_ref[...] + mod_ref[0][5:6] * acc_sc[...]
        if final:
            y = _rms(y, fg_ref[...])
        o_ref[...] = y


def _ffn(x, g, mod, wg, wu, wd, fg, *, final):
    tm, tf = FFN_TM, FFN_TF
    row = lambda i, j: (i, 0)
    return pl.pallas_call(
        functools.partial(_ffn_kernel, final=final),
        out_shape=jax.ShapeDtypeStruct((R_ALL, D_MODEL), F32),
        grid=(R_ALL // tm, FFN_DIM // tf),
        in_specs=[
            pl.BlockSpec((tm, D_MODEL), row),
            pl.BlockSpec((1, D_MODEL), lambda i, j: (0, 0)),
            pl.BlockSpec((1, 8, D_MODEL), lambda i, j: (_mod_index(i, tm), 0, 0)),
            pl.BlockSpec((D_MODEL, tf), lambda i, j: (0, j)),
            pl.BlockSpec((D_MODEL, tf), lambda i, j: (0, j)),
            pl.BlockSpec((tf, D_MODEL), lambda i, j: (j, 0)),
            pl.BlockSpec((1, D_MODEL), lambda i, j: (0, 0)),
        ],
        out_specs=pl.BlockSpec((tm, D_MODEL), row),
        scratch_shapes=[pltpu.VMEM((tm, D_MODEL), BF16), pltpu.VMEM((tm, D_MODEL), F32)],
        compiler_params=_cparams(("arbitrary", "arbitrary")),
        name="ffn_final" if final else "ffn",
    )(x, g, mod, wg, wu, wd, fg)


RW_TM = 256
HALO = 8
N_OPS = 9
DECAY_SCALE = float(math.exp(-0.5))


def _rwkv_pre_kernel(x_ref, xp_ref, xn_ref, g_ref, mod_ref, mu_ref, wr_ref, wk_ref, wv_ref, w1_ref, w2_ref,
                     a1_ref, a2_ref, g1_ref, g2_ref, w0_ref, a0_ref, kk_ref, ka_ref, rk_ref, hs_ref,
                     ops_o, g_o, bon_o):
    i = pl.program_id(0)
    tm = RW_TM
    n_ctx = R_CTX // tm
    per_c = SEQ // tm
    per_s = DEC_SEQ // tm
    rel = jnp.where(i < n_ctx, i % per_c, jnp.maximum(i - n_ctx, 0) % per_s)
    last = jnp.where(i < n_ctx, per_c - 1, per_s - 1)
    m = mod_ref[0]

    def nm(x):
        return _rms(x, g_ref[...]) * (1 + m[1:2]) + m[0:1]

    h = nm(x_ref[...])
    prev_row = jnp.where(rel == 0, 0.0, nm(xp_ref[...])[HALO - 1:HALO, :])
    next_row = jnp.where(rel == last, 0.0, nm(xn_ref[...])[0:1, :])
    row = lax.broadcasted_iota(jnp.int32, h.shape, 0)
    hp = jnp.where(row == 0, prev_row, pltpu.roll(h, 1, axis=0))
    hn = jnp.where(row == tm - 1, next_row, pltpu.roll(h, tm - 1, axis=0))
    dp = hp - h
    dn = hn - h

    def mix(idx):
        return (h + dp * mu_ref[0, idx:idx + 1, :] + dn * mu_ref[1, idx:idx + 1, :]).astype(BF16)

    r = _dot(mix(0), wr_ref[...])
    k = _dot(mix(2), wk_ref[...])
    v = _dot(mix(3), wv_ref[...])
    lw = jnp.tanh(_dot(mix(1), w1_ref[...])).astype(BF16)
    la = _dot(mix(4), a1_ref[...]).astype(BF16)
    gg = _dot(_sigmoid(_dot(mix(5), g1_ref[...])).astype(BF16), g2_ref[...])

    hs = hs_ref[...]
    kk = k * kk_ref[...]
    kk = kk * lax.rsqrt(_head_allsum(kk * kk, hs) + L2_EPS)
    ops_o[0] = r
    ops_o[1] = kk
    ops_o[2] = v
    g_o[...] = gg
    bsum = None
    for j in range(2):
        wl = w0_ref[j:j + 1, :] + _dot(lw, w2_ref[j])
        ops_o[3 + 3 * j] = jnp.exp(-(DECAY_SCALE * _sigmoid(wl)))
        a = _sigmoid(a0_ref[j:j + 1, :] + _dot(la, a2_ref[j]))
        kd = k * (1 + (a - 1) * ka_ref[...])
        ops_o[4 + 3 * j] = kd
        ops_o[5 + 3 * j] = kk * a
        t = r * kd * rk_ref[...]
        bsum = t if bsum is None else bsum + t
    bon_o[...] = _head_allsum(bsum, hs) * v


def _rwkv_pre(x, g, mod, mu, wr, wk, wv, w1, w2, a1, a2, g1, g2, w0, a0, k_k, k_a, r_k, hs):
    tm = RW_TM
    nblk = R_ALL // HALO
    per = tm // HALO
    row = lambda i: (i, 0)
    fixed2 = lambda i: (0, 0)
    fixed3 = lambda i: (0, 0, 0)
    full = lambda a: pl.BlockSpec(a.shape, fixed2 if a.ndim == 2 else fixed3)
    out = jax.ShapeDtypeStruct((R_ALL, D_MODEL), F32)
    orow = pl.BlockSpec((tm, D_MODEL), row)
    return pl.pallas_call(
        _rwkv_pre_kernel,
        out_shape=(jax.ShapeDtypeStruct((N_OPS, R_ALL, D_MODEL), F32), out, out),
        grid=(R_ALL // tm,),
        in_specs=[
            pl.BlockSpec((tm, D_MODEL), row),
            pl.BlockSpec((HALO, D_MODEL), lambda i: (jnp.maximum(i * per - 1, 0), 0)),
            pl.BlockSpec((HALO, D_MODEL), lambda i: (jnp.minimum((i + 1) * per, nblk - 1), 0)),
            pl.BlockSpec((1, D_MODEL), fixed2),
            pl.BlockSpec((1, 8, D_MODEL), lambda i: (_mod_index(i, tm), 0, 0)),
            full(mu), full(wr), full(wk), full(wv), full(w1), full(w2), full(a1), full(a2), full(g1), full(g2),
            full(w0), full(a0), full(k_k), full(k_a), full(r_k), full(hs),
        ],
        out_specs=(pl.BlockSpec((N_OPS, tm, D_MODEL), lambda i: (0, i, 0)), orow, orow),
        compiler_params=_cparams(("arbitrary",)),
        name="rwkv_pre",
    )(x, x, x, g, mod, mu, wr, wk, wv, w1, w2, a1, a2, g1, g2, w0, a0, k_k, k_a, r_k, hs)


SCAN_TT = 64
SCAN_ACC = 4
NK = RWKV_HEAD_DIM


def _scan_kernel(kt_ref, vt_ref, dk_ref, s0_ref, y_ref, st_ref, s_sc):
    d = pl.program_id(0)
    j = pl.program_id(2)

    @pl.when(j == 0)
    def _():
        s_sc[...] = s0_ref[0]

    def tix(i):
        return jnp.where(d == 0, i, SCAN_TT - 1 - i)

    def project(i):
        t = tix(i)
        parts = [None] * SCAN_ACC
        for k in range(NK):
            term = s_sc[k] * kt_ref[1, t, k:k + 1, :]
            parts[k % SCAN_ACC] = term if parts[k % SCAN_ACC] is None else parts[k % SCAN_ACC] + term
        while len(parts) > 1:
            parts = [parts[a] + parts[a + 1] for a in range(0, len(parts), 2)]
        return parts[0]

    def advance(i, sa, with_next):
        t = tix(i)
        tn = tix(i + 1)
        vv = vt_ref[0, t]
        y = None
        sa_next = None
        for k in range(NK):
            sn = (s_sc[k] * dk_ref[0, t, k:k + 1, :] - sa * dk_ref[2, t, k:k + 1, :]
                  + vv * dk_ref[1, t, k:k + 1, :])
            s_sc[k] = sn
            yk = sn * kt_ref[0, t, k:k + 1, :]
            y = yk if y is None else y + yk
            if with_next:
                ak = sn * kt_ref[1, tn, k:k + 1, :]
                sa_next = ak if sa_next is None else sa_next + ak
        y_ref[0, t] = y
        return sa_next

    sa_last = lax.fori_loop(0, SCAN_TT - 1, lambda i, sa: advance(i, sa, True), project(0))
    advance(SCAN_TT - 1, sa_last, False)

    @pl.when(j == pl.num_programs(2) - 1)
    def _():
        st_ref[0] = s_sc[...]


def _scan(kt, vt, dk, s0, *, kt_blk=0, vt_slab=0, dk_blk=0):
    _, seq, nv, chains = vt.shape
    tt = SCAN_TT
    nt = seq // tt
    tb = lambda d, j: jnp.where(d == 0, j, nt - 1 - j)
    sblk = pl.BlockSpec((1, NK, nv, LANES), lambda d, c, j: (d, 0, 0, c))
    yblk = pl.BlockSpec((1, tt, nv, LANES), lambda d, c, j: (d, tb(d, j), 0, c))
    return pl.pallas_call(
        _scan_kernel,
        out_shape=(jax.ShapeDtypeStruct((2, seq, nv, chains), F32), jax.ShapeDtypeStruct((2, NK, nv, chains), F32)),
        grid=(2, chains // LANES, nt),
        in_specs=[
            pl.BlockSpec((2, tt, NK, LANES), lambda d, c, j: (kt_blk, tb(d, j), 0, c)),
            pl.BlockSpec((1, tt, nv, LANES), lambda d, c, j: (vt_slab, tb(d, j), 0, c)),
            pl.BlockSpec((3, tt, NK, LANES), lambda d, c, j: (dk_blk + d, tb(d, j), 0, c)),
            sblk,
        ],
        out_specs=(yblk, sblk),
        scratch_shapes=[pltpu.VMEM((NK, nv, LANES), F32)],
        compiler_params=_cparams(("arbitrary", "arbitrary", "arbitrary")),
        name=f"rwkv_scan_{seq}",
    )(kt, vt, dk, s0)


POST_TM = 256


def _rwkv_post_kernel(x_ref, mod_ref, yc_ref, ys_ref, bon_ref, g_ref, lng_ref, lnb_ref, wo_ref, hs_ref, o_ref):
    is_ctx = pl.program_id(0) < R_CTX // POST_TM
    y = jnp.where(is_ctx, yc_ref[0] + yc_ref[1], ys_ref[0] + ys_ref[1])
    inv = 1.0 / RWKV_HEAD_DIM
    hs = hs_ref[...]
    d = y - _head_allsum(y, hs) * inv
    var = _head_allsum(d * d, hs) * inv
    yn = d * lax.rsqrt(var + GN_EPS) * lng_ref[...] + lnb_ref[...]
    out = ((yn + bon_ref[...]) * g_ref[...]).astype(BF16)
    o_ref[...] = x_ref[...] + mod_ref[0][2:3] * _dot(out, wo_ref[...])


def _rwkv_post(x, mod, yc, ys, bon, g, lng, lnb, wo, hs):
    tm = POST_TM
    n_ctx = R_CTX // tm
    row = lambda i: (i, 0)
    fixed = lambda i: (0, 0)
    big = pl.BlockSpec((tm, D_MODEL), row)
    ycb = pl.BlockSpec((2, tm, D_MODEL), lambda i: (0, jnp.minimum(i, n_ctx - 1), 0))
    ysb = pl.BlockSpec((2, tm, D_MODEL), lambda i: (0, jnp.maximum(i - n_ctx, 0), 0))
    return pl.pallas_call(
        _rwkv_post_kernel,
        out_shape=jax.ShapeDtypeStruct((R_ALL, D_MODEL), F32),
        grid=(R_ALL // tm,),
        in_specs=[
            big,
            pl.BlockSpec((1, 8, D_MODEL), lambda i: (_mod_index(i, tm), 0, 0)),
            ycb, ysb, big, big,
            pl.BlockSpec((1, D_MODEL), fixed),
            pl.BlockSpec((1, D_MODEL), fixed),
            pl.BlockSpec((D_MODEL, D_MODEL), fixed),
            pl.BlockSpec((LANES, LANES), fixed),
        ],
        out_specs=big,
        compiler_params=_cparams(("arbitrary",)),
        name="rwkv_post",
    )(x, mod, yc, ys, bon, g, lng, lnb, wo, hs)


RL_T = 64
NPAIR = D_MODEL // LANES


def _to_chains_kernel(x_ref, o_ref):
    cols = [jnp.swapaxes(x_ref[0, :, :, p * LANES:(p + 1) * LANES], 0, 1) for p in range(NPAIR)]
    for t in range(RL_T):
        tile = jnp.concatenate([c[t] for c in cols], axis=0).T
        o_ref[0, t, :, 0:LANES] = tile[0:NK]
        o_ref[0, t, :, LANES:2 * LANES] = tile[NK:2 * NK]


def _from_chains_kernel(y_ref, o_ref):
    tiles = []
    for t in range(RL_T):
        m = jnp.concatenate([y_ref[0, t, :, 0:LANES], y_ref[0, t, :, LANES:2 * LANES]], axis=0)
        tiles.append(m.T)
    for p in range(NPAIR):
        blk = jnp.stack([m[p * BATCH:(p + 1) * BATCH] for m in tiles], axis=0)
        o_ref[0, :, :, p * LANES:(p + 1) * LANES] = jnp.swapaxes(blk, 0, 1)


def _to_chains(ops):
    n = ops.shape[0]
    return pl.pallas_call(
        _to_chains_kernel,
        out_shape=jax.ShapeDtypeStruct((n, SEQ, NK, 2 * LANES), F32),
        grid=(n, SEQ // RL_T),
        in_specs=[pl.BlockSpec((1, BATCH, RL_T, D_MODEL), lambda o, j: (o, 0, j, 0))],
        out_specs=pl.BlockSpec((1, RL_T, NK, 2 * LANES), lambda o, j: (o, j, 0, 0)),
        compiler_params=_cparams(("arbitrary", "arbitrary")),
        name="to_chains",
    )(ops.reshape(n, R_ALL // SEQ, SEQ, D_MODEL))


def _from_chains(y):
    n = y.shape[0]
    out = pl.pallas_call(
        _from_chains_kernel,
        out_shape=jax.ShapeDtypeStruct((n, BATCH, SEQ, D_MODEL), F32),
        grid=(n, SEQ // RL_T),
        in_specs=[pl.BlockSpec((1, RL_T, NK, 2 * LANES), lambda o, j: (o, j, 0, 0))],
        out_specs=pl.BlockSpec((1, BATCH, RL_T, D_MODEL), lambda o, j: (o, 0, j, 0)),
        compiler_params=_cparams(("arbitrary", "arbitrary")),
        name="from_chains",
    )(y)
    return out.reshape(n, R_CTX, D_MODEL)


def _scan_ctx(ops):
    ch = _to_chains(ops)
    s0 = jnp.zeros((2, NK, NK, 2 * LANES), F32)
    y, st = _scan(ch, ch, ch, s0, vt_slab=2, dk_blk=1)
    st = st.reshape(2, NK, NK, 2, NPAIR, BATCH).transpose(0, 5, 4, 3, 2, 1)
    return _from_chains(y), st.reshape(2, BATCH, RWKV_HEADS, NK, NK)


SMP_REP = LANES // (NPAIR * DEC_BATCH)
SMP_VQ = SMP_REP // 2
SMP_NV = NK // SMP_VQ
SMP_PB = NPAIR * DEC_BATCH


def _smp_tiles(x_ref):
    rows = jnp.stack([x_ref[0, b, :, p * LANES:(p + 1) * LANES] for p in range(NPAIR) for b in range(DEC_BATCH)],
                     axis=0)
    rows = jnp.swapaxes(rows, 0, 1)
    return [jnp.concatenate([rows[t]] * SMP_REP, axis=0).T for t in range(RL_T)]


def _lane_replica(nrows):
    return lax.broadcasted_iota(jnp.int32, (nrows, LANES), 1) // SMP_PB


def _to_chains_smp_k_kernel(x_ref, o_ref):
    odd = _lane_replica(NK) >= SMP_VQ
    for t, tile in enumerate(_smp_tiles(x_ref)):
        o_ref[0, t] = jnp.where(odd, tile[NK:2 * NK], tile[0:NK])


def _to_chains_smp_v_kernel(x_ref, o_ref):
    rep = _lane_replica(SMP_NV)
    for t, tile in enumerate(_smp_tiles(x_ref)):
        acc = tile[0:SMP_NV]
        for r in range(1, SMP_REP):
            acc = jnp.where(rep == r, tile[r * SMP_NV:(r + 1) * SMP_NV], acc)
        o_ref[0, t] = acc


def _from_chains_smp_kernel(y_ref, o_ref):
    rep = _lane_replica(SMP_NV)
    res = []
    for t in range(RL_T):
        y = y_ref[0, t]
        m = jnp.concatenate([jnp.where(rep == r, y, 0.0) for r in range(SMP_REP)], axis=0).T
        acc = m[0:SMP_PB]
        for r in range(1, SMP_REP):
            acc = acc + m[r * SMP_PB:(r + 1) * SMP_PB]
        res.append(acc)
    out = jnp.swapaxes(jnp.stack(res, axis=0), 0, 1)
    for p in range(NPAIR):
        for b in range(DEC_BATCH):
            o_ref[0, b, :, p * LANES:(p + 1) * LANES] = out[p * DEC_BATCH + b]


def _scan_smp(ops, s0):
    nseq = R_ALL // DEC_SEQ
    blk0 = R_CTX // DEC_SEQ // DEC_BATCH
    ops4 = ops.reshape(N_OPS, nseq, DEC_SEQ, D_MODEL)
    nt = DEC_SEQ // RL_T
    in_blk = (1, DEC_BATCH, RL_T, D_MODEL)
    nk_ops = N_OPS - 1
    kidx = lambda o, j: (jnp.where(o < 6, o + 3, o - 6), blk0, j, 0)
    chk = pl.pallas_call(
        _to_chains_smp_k_kernel,
        out_shape=jax.ShapeDtypeStruct((nk_ops, DEC_SEQ, NK, LANES), F32),
        grid=(nk_ops, nt),
        in_specs=[pl.BlockSpec(in_blk, kidx)],
        out_specs=pl.BlockSpec((1, RL_T, NK, LANES), lambda o, j: (o, j, 0, 0)),
        compiler_params=_cparams(("arbitrary", "arbitrary")),
        name="to_chains_smp_k",
    )(ops4)
    chv = pl.pallas_call(
        _to_chains_smp_v_kernel,
        out_shape=jax.ShapeDtypeStruct((1, DEC_SEQ, SMP_NV, LANES), F32),
        grid=(1, nt),
        in_specs=[pl.BlockSpec(in_blk, lambda o, j: (2, blk0, j, 0))],
        out_specs=pl.BlockSpec((1, RL_T, SMP_NV, LANES), lambda o, j: (0, j, 0, 0)),
        compiler_params=_cparams(("arbitrary", "arbitrary")),
        name="to_chains_smp_v",
    )(ops4)
    s0 = s0.reshape(2, DEC_BATCH, NPAIR, 2, SMP_VQ, SMP_NV, NK).transpose(0, 6, 5, 3, 4, 2, 1)
    y, _ = _scan(chk, chv, chk, s0.reshape(2, NK, SMP_NV, LANES), kt_blk=3, vt_slab=0, dk_blk=0)
    out = pl.pallas_call(
        _from_chains_smp_kernel,
        out_shape=jax.ShapeDtypeStruct((2, DEC_BATCH, DEC_SEQ, D_MODEL), F32),
        grid=(2, nt),
        in_specs=[pl.BlockSpec((1, RL_T, SMP_NV, LANES), lambda o, j: (o, j, 0, 0))],
        out_specs=pl.BlockSpec((1, DEC_BATCH, RL_T, D_MODEL), lambda o, j: (o, 0, j, 0)),
        compiler_params=_cparams(("arbitrary", "arbitrary")),
        name="from_chains_smp",
    )(y)
    return out.reshape(2, R_SMP, D_MODEL)


def _rope_tables():
    rows = DEC_SEQ // GRID_W
    row = jnp.repeat(jnp.arange(rows, dtype=F32), GRID_W)
    col = jnp.tile(jnp.arange(GRID_W, dtype=F32), rows)
    n_freq = HEAD_DIM // 4
    inv_freq = ROPE_THETA ** (-jnp.arange(n_freq, dtype=F32) / n_freq)
    ang = jnp.concatenate([row[:, None] * inv_freq, col[:, None] * inv_freq], axis=-1)
    cos, sin = jnp.cos(ang), jnp.sin(ang)
    reps = LANES // HEAD_DIM
    cos_t = jnp.tile(jnp.concatenate([cos, cos], axis=-1), (1, reps))
    sin_t = jnp.tile(jnp.concatenate([-sin, sin], axis=-1), (1, reps))
    ident_c = jnp.ones((QK_TM, LANES), F32)
    ident_s = jnp.zeros((QK_TM, LANES), F32)
    return jnp.concatenate([ident_c, cos_t], axis=0), jnp.concatenate([ident_s, sin_t], axis=0)


def kernel(x_prompt, x_sample, cache_attn_k, cache_attn_v, state_ssd_fwd, state_ssd_bwd, state_rwkv_fwd, state_rwkv_bwd, c, c_ctx, mod_w, mod_b, norm_mix_g, norm_ffn_g, ffn_w_gate, ffn_w_up, ffn_w_down, ab_w_in, ab_w_out, attn_q_g, attn_k_g, ssd_conv_w, ssd_conv_b, ssd_dt_bias, ssd_a_log, ssd_d, ssd_norm_g, rwkv_mu, rwkv_w_r, rwkv_w_k, rwkv_w_v, rwkv_w0, rwkv_w1, rwkv_w2, rwkv_a0, rwkv_a1, rwkv_a2, rwkv_g1, rwkv_g2, rwkv_k_k, rwkv_k_a, rwkv_r_k, rwkv_ln_g, rwkv_ln_b, rwkv_w_o, final_norm_g):
    bf = lambda a: a.astype(BF16)
    x = jnp.concatenate([x_prompt.reshape(R_CTX, D_MODEL), x_sample.reshape(R_SMP, D_MODEL)], axis=0)

    cv = jnp.concatenate([c_ctx[None], c, jnp.zeros((8 - 1 - DEC_BATCH, D_MODEL), F32)], axis=0)
    m = _ada(cv, mod_w, mod_b)
    m = m[:, :1 + DEC_BATCH].reshape(2, 1 + DEC_BATCH, 6, D_MODEL)
    m = jnp.pad(m, ((0, 0), (0, 0), (0, 2), (0, 0)))
    row2 = lambda a: a.reshape(1, -1)

    w_in = jnp.pad(bf(ab_w_in[0]), ((0, 0), (0, AB_PAD - AB_IN_DIM)))
    qkv, z, xbc, dt = _inproj(x, row2(norm_mix_g[0]), m[0], w_in)
    cos_t, sin_t = _rope_tables()
    qg = jnp.tile(attn_q_g[0], ATTN_HEADS).reshape(1, -1)
    kg = jnp.tile(attn_k_g[0], ATTN_KV_HEADS).reshape(1, -1)
    qn, kn = _qkprep(qkv, qg, kg, cos_t, sin_t)
    ck = cache_attn_k[:, 0].reshape(DEC_BATCH, PAST_LEN, ATTN_KV_DIM)
    cvv = cache_attn_v[:, 0].reshape(DEC_BATCH, PAST_LEN, ATTN_KV_DIM)
    attn_c, attn_s = _attention(qn, kn, qkv, ck, cvv)
    attn = jnp.concatenate([attn_c, attn_s], axis=0)

    dtb = jnp.pad(ssd_dt_bias[0].reshape(1, -1), ((0, 0), (0, DT_PAD - 2 * SSD_HEADS)))
    a_row = jnp.pad((-jnp.exp(ssd_a_log[0])).reshape(1, -1), ((0, 0), (0, DT_PAD - 2 * SSD_HEADS)))
    d_row = jnp.repeat(ssd_d[0], SSD_HEAD_DIM).reshape(1, -1)
    ssd_args = (ssd_conv_w[0], row2(ssd_conv_b[0]), dtb, a_row, d_row, row2(ssd_norm_g[0]))
    zero_st = jnp.zeros((BATCH, SSD_HEADS, SSD_HEAD_DIM, SSD_STATE), F32)
    y_c, hf_c, hb_c = _ssd(z, xbc, dt, *ssd_args, zero_st, zero_st, seq=SEQ, nb=BATCH, row0=0)
    y_s, _, _ = _ssd(z, xbc, dt, *ssd_args, state_ssd_fwd[:, 0], state_ssd_bwd[:, 0],
                     seq=DEC_SEQ, nb=DEC_BATCH, row0=R_CTX)
    y_ssd = jnp.concatenate([y_c, y_s], axis=0)
    x = _mixres(x, m[0], attn, y_ssd, bf(ab_w_out[0]))
    x = _ffn(x, row2(norm_ffn_g[0]), m[0], bf(ffn_w_gate[0]), bf(ffn_w_up[0]), bf(ffn_w_down[0]),
             row2(final_norm_g), final=False)

    w1 = bf(jnp.concatenate([rwkv_w1[0, 0], rwkv_w1[0, 1]], axis=1))
    a1 = bf(jnp.concatenate([rwkv_a1[0, 0], rwkv_a1[0, 1]], axis=1))
    zpad = lambda w: bf(jnp.stack([jnp.concatenate([w[0], jnp.zeros_like(w[1])], axis=0),
                                   jnp.concatenate([jnp.zeros_like(w[0]), w[1]], axis=0)]))
    hs = _head_mask()
    ops, gg, bon = _rwkv_pre(x, row2(norm_mix_g[1]), m[1], rwkv_mu[0], bf(rwkv_w_r[0]), bf(rwkv_w_k[0]),
                             bf(rwkv_w_v[0]), w1, zpad(rwkv_w2[0]), a1, zpad(rwkv_a2[0]), bf(rwkv_g1[0]),
                             bf(rwkv_g2[0]), rwkv_w0[0], rwkv_a0[0], row2(rwkv_k_k[0]), row2(rwkv_k_a[0]),
                             rwkv_r_k[0].reshape(1, -1), hs)
    y_c, st_c = _scan_ctx(ops)
    y_s = _scan_smp(ops, jnp.stack([state_rwkv_fwd[:, 0], state_rwkv_bwd[:, 0]]))
    x = _rwkv_post(x, m[1], y_c, y_s, bon, gg, row2(rwkv_ln_g[0]), row2(rwkv_ln_b[0]), bf(rwkv_w_o[0]), hs)
    x = _ffn(x, row2(norm_ffn_g[1]), m[1], bf(ffn_w_gate[1]), bf(ffn_w_up[1]), bf(ffn_w_down[1]),
             row2(final_norm_g), final=True)

    y_prompt = x[:R_CTX].reshape(BATCH, SEQ, D_MODEL)
    y_sample = x[R_CTX:].reshape(DEC_BATCH, DEC_SEQ, D_MODEL)
    new_k = kn[:R_CTX].reshape(BATCH, 1, SEQ, ATTN_KV_HEADS, HEAD_DIM)
    new_v = qkv[:R_CTX, ATTN_Q_DIM + ATTN_KV_DIM:].reshape(BATCH, 1, SEQ, ATTN_KV_HEADS, HEAD_DIM)
    return (y_prompt, y_sample, new_k, new_v, hf_c[:, None], hb_c[:, None], st_c[0][:, None], st_c[1][:, None])
```

```python
import functools
import math

import jax
import jax.numpy as jnp
from jax import lax
from jax.experimental import pallas as pl
from jax.experimental.pallas import tpu as pltpu

F32 = jnp.float32
BF16 = jnp.bfloat16

D_MODEL = 1024
BATCH = 16
SEQ = 256
DEC_BATCH = 2
DEC_SEQ = 1024
PAST_LEN = 256
GRID_W = 64
ATTN_HEADS = 8
ATTN_KV_HEADS = 2
HEAD_DIM = 64
ROPE_THETA = 10000.0
ATTN_Q_DIM = ATTN_HEADS * HEAD_DIM
ATTN_KV_DIM = ATTN_KV_HEADS * HEAD_DIM
SSD_HEADS = 8
SSD_HEAD_DIM = 64
SSD_D_INNER = SSD_HEADS * SSD_HEAD_DIM
SSD_GROUPS = 2
SSD_STATE = 64
SSD_CONV_K = 5
SSD_CHUNK = 128
SSD_CONV_DIM = SSD_D_INNER + 2 * SSD_GROUPS * SSD_STATE
AB_IN_DIM = ATTN_Q_DIM + 2 * ATTN_KV_DIM + SSD_D_INNER + SSD_CONV_DIM + 2 * SSD_HEADS
RWKV_HEAD_DIM = 64
RWKV_HEADS = D_MODEL // RWKV_HEAD_DIM
FFN_DIM = (((8 * D_MODEL + 2) // 3 + 255) // 256) * 256
RMS_EPS = 1e-6
GN_EPS = 64e-5
L2_EPS = 1e-12

R_CTX = BATCH * SEQ
R_SMP = DEC_BATCH * DEC_SEQ
R_ALL = R_CTX + R_SMP
LANES = 128
QKV_DIM = ATTN_Q_DIM + 2 * ATTN_KV_DIM
DT_PAD = LANES
AB_PAD = QKV_DIM + SSD_D_INNER + SSD_CONV_DIM + DT_PAD
VMEM_LIMIT = 56 * 1024 * 1024


def _cparams(sem):
    return pltpu.CompilerParams(dimension_semantics=sem, vmem_limit_bytes=VMEM_LIMIT)


def _mod_index(i, tm):
    n_ctx = R_CTX // tm
    per = DEC_SEQ // tm
    return jnp.where(i < n_ctx, 0, 1 + jnp.maximum(i - n_ctx, 0) // per)


def _sigmoid(x):
    return 1.0 / (1.0 + jnp.exp(-x))


def _silu(x):
    return x * _sigmoid(x)


def _softplus(x):
    return jnp.maximum(x, 0.0) + jnp.log1p(jnp.exp(-jnp.abs(x)))


def _rms(x, g):
    return x * lax.rsqrt(jnp.mean(x * x, axis=-1, keepdims=True) + RMS_EPS) * g


def _head_allsum(x, hs):
    parts = [jnp.dot(x[:, j * LANES:(j + 1) * LANES], hs, precision=lax.Precision.HIGHEST,
                     preferred_element_type=F32) for j in range(x.shape[-1] // LANES)]
    return parts[0] if len(parts) == 1 else jnp.concatenate(parts, axis=1)


def _head_mask():
    lane = jnp.arange(LANES)
    return (lane[:, None] // HEAD_DIM == lane[None, :] // HEAD_DIM).astype(F32)


def _dot(a, b):
    return jnp.dot(a, b, preferred_element_type=F32)


ADA_TN = 1536


def _ada_kernel(c_ref, w_ref, b_ref, o_ref):
    s = _silu(c_ref[...]).astype(BF16)
    o_ref[0] = _dot(s, w_ref[0].astype(BF16)) + b_ref[0]


def _ada(cv, mod_w, mod_b):
    depth = mod_w.shape[0]
    n = mod_w.shape[2]
    return pl.pallas_call(
        _ada_kernel,
        out_shape=jax.ShapeDtypeStruct((depth, 8, n), F32),
        grid=(depth, n // ADA_TN),
        in_specs=[
            pl.BlockSpec((8, D_MODEL), lambda l, j: (0, 0)),
            pl.BlockSpec((1, D_MODEL, ADA_TN), lambda l, j: (l, 0, j)),
            pl.BlockSpec((1, 1, ADA_TN), lambda l, j: (l, 0, j)),
        ],
        out_specs=pl.BlockSpec((1, 8, ADA_TN), lambda l, j: (l, 0, j)),
        compiler_params=_cparams(("arbitrary", "arbitrary")),
        name="ada",
    )(cv, mod_w, mod_b.reshape(depth, 1, n))


INPROJ_TM = 512


def _inproj_kernel(x_ref, g_ref, mod_ref, w_ref, qkv_ref, z_ref, xbc_ref, dt_ref):
    m = mod_ref[0]
    h = _rms(x_ref[...], g_ref[...]) * (1 + m[1:2]) + m[0:1]
    p = _dot(h.astype(BF16), w_ref[...])
    qkv_ref[...] = p[:, 0:QKV_DIM]
    z_ref[...] = p[:, QKV_DIM:QKV_DIM + SSD_D_INNER]
    xbc_ref[...] = p[:, QKV_DIM + SSD_D_INNER:QKV_DIM + SSD_D_INNER + SSD_CONV_DIM]
    dt_ref[...] = p[:, QKV_DIM + SSD_D_INNER + SSD_CONV_DIM:AB_PAD]


def _inproj(x, g, mod, w_pad):
    tm = INPROJ_TM
    row = lambda i: (i, 0)
    return pl.pallas_call(
        _inproj_kernel,
        out_shape=(
            jax.ShapeDtypeStruct((R_ALL, QKV_DIM), F32),
            jax.ShapeDtypeStruct((R_ALL, SSD_D_INNER), F32),
            jax.ShapeDtypeStruct((R_ALL, SSD_CONV_DIM), F32),
            jax.ShapeDtypeStruct((R_ALL, DT_PAD), F32),
        ),
        grid=(R_ALL // tm,),
        in_specs=[
            pl.BlockSpec((tm, D_MODEL), row),
            pl.BlockSpec((1, D_MODEL), lambda i: (0, 0)),
            pl.BlockSpec((1, 8, D_MODEL), lambda i: (_mod_index(i, tm), 0, 0)),
            pl.BlockSpec((D_MODEL, AB_PAD), lambda i: (0, 0)),
        ],
        out_specs=(
            pl.BlockSpec((tm, QKV_DIM), row),
            pl.BlockSpec((tm, SSD_D_INNER), row),
            pl.BlockSpec((tm, SSD_CONV_DIM), row),
            pl.BlockSpec((tm, DT_PAD), row),
        ),
        compiler_params=_cparams(("arbitrary",)),
        name="inproj",
    )(x, g, mod, w_pad)


QK_TM = 256


def _qkprep_kernel(qkv_ref, qg_ref, kg_ref, cos_ref, sin_ref, hs_ref, qn_ref, kn_ref):
    cos = cos_ref[...]
    sin = sin_ref[...]
    hs = hs_ref[...]

    def norm_rope(x, g, reps):
        ms = _head_allsum(x * x, hs) * (1.0 / HEAD_DIM)
        y = x * lax.rsqrt(ms + RMS_EPS) * g
        n = y.shape[-1]
        lane = lax.broadcasted_iota(jnp.int32, y.shape, 1)
        half = HEAD_DIM // 2
        swapped = jnp.where((lane & half) == 0, pltpu.roll(y, n - half, axis=1), pltpu.roll(y, half, axis=1))
        c = jnp.concatenate([cos] * reps, axis=1) if reps > 1 else cos
        s = jnp.concatenate([sin] * reps, axis=1) if reps > 1 else sin
        return y * c + swapped * s

    q = qkv_ref[:, 0:ATTN_Q_DIM]
    k = qkv_ref[:, ATTN_Q_DIM:ATTN_Q_DIM + ATTN_KV_DIM]
    qn_ref[...] = norm_rope(q, qg_ref[...], ATTN_Q_DIM // LANES)
    kn_ref[...] = norm_rope(k, kg_ref[...], 1)


def _qkprep(qkv, qg, kg, cos_t, sin_t):
    tm = QK_TM
    n_ctx = R_CTX // tm
    per = DEC_SEQ // tm
    tab = lambda i: (jnp.where(i < n_ctx, 0, 1 + jnp.maximum(i - n_ctx, 0) % per), 0)
    return pl.pallas_call(
        _qkprep_kernel,
        out_shape=(
            jax.ShapeDtypeStruct((R_ALL, ATTN_Q_DIM), F32),
            jax.ShapeDtypeStruct((R_ALL, ATTN_KV_DIM), F32),
        ),
        grid=(R_ALL // tm,),
        in_specs=[
            pl.BlockSpec((tm, QKV_DIM), lambda i: (i, 0)),
            pl.BlockSpec((1, ATTN_Q_DIM), lambda i: (0, 0)),
            pl.BlockSpec((1, ATTN_KV_DIM), lambda i: (0, 0)),
            pl.BlockSpec((tm, LANES), tab),
            pl.BlockSpec((tm, LANES), tab),
            pl.BlockSpec((LANES, LANES), lambda i: (0, 0)),
        ],
        out_specs=(
            pl.BlockSpec((tm, ATTN_Q_DIM), lambda i: (i, 0)),
            pl.BlockSpec((tm, ATTN_KV_DIM), lambda i: (i, 0)),
        ),
        compiler_params=_cparams(("arbitrary",)),
        name="qkprep",
    )(qkv, qg, kg, cos_t, sin_t, _head_mask())


def _attn_core(q, ks, vs):
    tq = q.shape[0]
    grp = ATTN_HEADS // ATTN_KV_HEADS
    scale = HEAD_DIM ** -0.5
    outs = []
    for g in range(ATTN_KV_HEADS):
        sl = slice(g * HEAD_DIM, (g + 1) * HEAD_DIM)
        qs = jnp.concatenate(
            [q[:, (g * grp + j) * HEAD_DIM:(g * grp + j + 1) * HEAD_DIM] for j in range(grp)], axis=0).astype(BF16)
        ss = [lax.dot_general(qs, k[:, sl].astype(BF16), (((1,), (1,)), ((), ())),
                              preferred_element_type=F32) * scale for k in ks]
        m = ss[0].max(axis=-1, keepdims=True)
        for s in ss[1:]:
            m = jnp.maximum(m, s.max(axis=-1, keepdims=True))
        ps = [jnp.exp(s - m) for s in ss]
        l = ps[0].sum(axis=-1, keepdims=True)
        for p in ps[1:]:
            l = l + p.sum(axis=-1, keepdims=True)
        inv = 1.0 / l
        o = None
        for p, v in zip(ps, vs):
            t = _dot((p * inv).astype(BF16), v[:, sl].astype(BF16))
            o = t if o is None else o + t
        outs += [o[j * tq:(j + 1) * tq] for j in range(grp)]
    return jnp.concatenate(outs, axis=1)


def _attn_ctx_kernel(q_ref, k_ref, v_ref, o_ref):
    o_ref[...] = _attn_core(q_ref[...], [k_ref[...]], [v_ref[...]]).astype(BF16)


def _attn_smp_kernel(q_ref, k_ref, v_ref, ck_ref, cv_ref, o_ref):
    ks = [ck_ref[0], k_ref[...]]
    vs = [cv_ref[0], v_ref[...]]
    o_ref[...] = _attn_core(q_ref[...], ks, vs).astype(BF16)


ATTN_TQ = 128


def _attention(qn, kn, qkv, cache_k, cache_v):
    v_col = (ATTN_Q_DIM + ATTN_KV_DIM) // ATTN_KV_DIM
    ctx = pl.pallas_call(
        _attn_ctx_kernel,
        out_shape=jax.ShapeDtypeStruct((R_CTX, ATTN_Q_DIM), BF16),
        grid=(BATCH,),
        in_specs=[
            pl.BlockSpec((SEQ, ATTN_Q_DIM), lambda b: (b, 0)),
            pl.BlockSpec((SEQ, ATTN_KV_DIM), lambda b: (b, 0)),
            pl.BlockSpec((SEQ, ATTN_KV_DIM), lambda b: (b, v_col)),
        ],
        out_specs=pl.BlockSpec((SEQ, ATTN_Q_DIM), lambda b: (b, 0)),
        compiler_params=_cparams(("arbitrary",)),
        name="attn_ctx",
    )(qn, kn, qkv)
    nq = DEC_SEQ // ATTN_TQ
    q0 = R_CTX // ATTN_TQ
    s0 = R_CTX // DEC_SEQ
    smp = pl.pallas_call(
        _attn_smp_kernel,
        out_shape=jax.ShapeDtypeStruct((R_SMP, ATTN_Q_DIM), BF16),
        grid=(DEC_BATCH, nq),
        in_specs=[
            pl.BlockSpec((ATTN_TQ, ATTN_Q_DIM), lambda b, i: (q0 + b * nq + i, 0)),
            pl.BlockSpec((DEC_SEQ, ATTN_KV_DIM), lambda b, i: (s0 + b, 0)),
            pl.BlockSpec((DEC_SEQ, ATTN_KV_DIM), lambda b, i: (s0 + b, v_col)),
            pl.BlockSpec((1, PAST_LEN, ATTN_KV_DIM), lambda b, i: (b, 0, 0)),
            pl.BlockSpec((1, PAST_LEN, ATTN_KV_DIM), lambda b, i: (b, 0, 0)),
        ],
        out_specs=pl.BlockSpec((ATTN_TQ, ATTN_Q_DIM), lambda b, i: (b * nq + i, 0)),
        compiler_params=_cparams(("arbitrary", "arbitrary")),
        name="attn_smp",
    )(qn, kn, qkv, cache_k, cache_v)
    return ctx, smp


CONV_HALO = 8
SSD_NS = 1


def _cumsum_rows(a, reverse):
    n = a.shape[0]
    row = lax.broadcasted_iota(jnp.int32, a.shape, 0)
    s = 1
    while s < n:
        if reverse:
            a = a + jnp.where(row < n - s, pltpu.roll(a, n - s, axis=0), 0.0)
        else:
            a = a + jnp.where(row >= s, pltpu.roll(a, s, axis=0), 0.0)
        s *= 2
    return a


def _ssd_kernel(z_ref, xbc_ref, dt_ref, cw_ref, cb_ref, dtb_ref, a_ref, d_ref, g_ref, h0f_ref, h0b_ref,
                y_ref, hf_ref, hb_ref, pad_sc, xc_sc, dt_sc, y_sc, h_sc, *, seq):
    L = SSD_CHUNK
    nc = seq // L
    pad = SSD_CONV_K // 2
    zeros = jnp.zeros((CONV_HALO, SSD_CONV_DIM), F32)
    for s in range(SSD_NS):
        pad_sc[s, 0:CONV_HALO, :] = zeros
        pad_sc[s, seq + CONV_HALO:seq + 2 * CONV_HALO, :] = zeros
        pad_sc[s, CONV_HALO:seq + CONV_HALO, :] = xbc_ref[s * seq:(s + 1) * seq, :]
        h_sc[s, 0] = h0f_ref[s]
        h_sc[s, 1] = h0b_ref[s]

    def conv_chunk(c, carry):
        for s in range(SSD_NS):
            r0 = pl.multiple_of(c * L, L)
            win = pad_sc[s, pl.ds(r0, L + 2 * CONV_HALO), :]
            acc = cb_ref[...]
            for i in range(SSD_CONV_K):
                acc = acc + win[CONV_HALO - pad + i:CONV_HALO - pad + i + L, :] * cw_ref[i:i + 1, :]
            xc = _silu(acc)
            g0 = pl.multiple_of(s * seq + c * L, L)
            xc_sc[pl.ds(g0, L), :] = xc
            y_sc[pl.ds(g0, L), :] = xc[:, 0:SSD_D_INNER] * d_ref[...]
            dt_sc[pl.ds(g0, L), :] = _softplus(dt_ref[pl.ds(g0, L), :] + dtb_ref[...])
        return carry

    lax.fori_loop(0, nc, conv_chunk, 0)

    rr = lax.broadcasted_iota(jnp.int32, (L, L), 0)
    cc = lax.broadcasted_iota(jnp.int32, (L, L), 1)
    grp = SSD_HEADS // SSD_GROUPS
    P = SSD_HEAD_DIM
    N = SSD_STATE

    def make_chunk(s, dirn):
        mask = (rr >= cc) if dirn == 0 else (rr <= cc)

        def chunk(ci, carry):
            c = ci if dirn == 0 else nc - 1 - ci
            r0 = pl.multiple_of(s * seq + c * L, L)
            xs = xc_sc[pl.ds(r0, L), 0:SSD_D_INNER]
            bm = xc_sc[pl.ds(r0, L), SSD_D_INNER:SSD_D_INNER + SSD_GROUPS * N]
            cm = xc_sc[pl.ds(r0, L), SSD_D_INNER + SSD_GROUPS * N:SSD_CONV_DIM]
            dtc = dt_sc[pl.ds(r0, L), :]
            acs = _cumsum_rows(dtc * a_ref[...], reverse=(dirn == 1))
            acs_t = acs.T
            tot = acs[L - 1:L, :] if dirn == 0 else acs[0:1, :]
            dec_end = jnp.exp(tot - acs)
            eacs = jnp.exp(acs)
            cdec = jnp.exp(tot)
            cb = [lax.dot_general(cm[:, g * N:(g + 1) * N].astype(BF16), bm[:, g * N:(g + 1) * N].astype(BF16),
                                  (((1,), (1,)), ((), ())), preferred_element_type=F32)
                  for g in range(SSD_GROUPS)]
            for h in range(SSD_HEADS):
                g = h // grp
                ln = h + SSD_HEADS * dirn
                seg = jnp.exp(jnp.where(mask, acs[:, ln:ln + 1] - acs_t[ln:ln + 1, :], -jnp.inf))
                sc = (cb[g] * seg).astype(BF16)
                xdt = (xs[:, h * P:(h + 1) * P] * dtc[:, ln:ln + 1]).astype(BF16)
                hin = h_sc[s, dirn, h]
                ce = (cm[:, g * N:(g + 1) * N] * eacs[:, ln:ln + 1]).astype(BF16)
                yh = _dot(sc, xdt) + lax.dot_general(ce, hin.astype(BF16), (((1,), (1,)), ((), ())),
                                                     preferred_element_type=F32)
                bd = (bm[:, g * N:(g + 1) * N] * dec_end[:, ln:ln + 1]).astype(BF16)
                st = lax.dot_general(xdt, bd, (((0,), (0,)), ((), ())), preferred_element_type=F32)
                h_sc[s, dirn, h] = hin * cdec[:, ln:ln + 1] + st
                y_sc[pl.ds(r0, L), h * P:(h + 1) * P] += yh
            return carry

        return chunk

    chunks = [make_chunk(s, dirn) for s in range(SSD_NS) for dirn in range(2)]

    def all_chunks(ci, carry):
        for fn in chunks:
            carry = fn(ci, carry)
        return carry

    lax.fori_loop(0, nc, all_chunks, 0)

    def out_chunk(c, carry):
        r0 = pl.multiple_of(c * L, L)
        y = y_sc[pl.ds(r0, L), :] * _silu(z_ref[pl.ds(r0, L), :])
        y_ref[pl.ds(r0, L), :] = _rms(y, g_ref[...]).astype(BF16)
        return carry

    lax.fori_loop(0, SSD_NS * nc, out_chunk, 0)
    for s in range(SSD_NS):
        hf_ref[s] = h_sc[s, 0]
        hb_ref[s] = h_sc[s, 1]


def _ssd(z, xbc, dt, cw, cb, dtb, a_row, d_row, g, h0f, h0b, *, seq, nb, row0):
    ns = SSD_NS
    rows = ns * seq
    blk0 = row0 // rows
    row = lambda b: (blk0 + b, 0)
    fixed = lambda b: (0, 0)
    st = lambda b: (b, 0, 0, 0)
    st_shape = (nb, SSD_HEADS, SSD_HEAD_DIM, SSD_STATE)
    st_blk = (ns, SSD_HEADS, SSD_HEAD_DIM, SSD_STATE)
    return pl.pallas_call(
        functools.partial(_ssd_kernel, seq=seq),
        out_shape=(
            jax.ShapeDtypeStruct((nb * seq, SSD_D_INNER), BF16),
            jax.ShapeDtypeStruct(st_shape, F32),
            jax.ShapeDtypeStruct(st_shape, F32),
        ),
        grid=(nb // ns,),
        in_specs=[
            pl.BlockSpec((rows, SSD_D_INNER), row),
            pl.BlockSpec((rows, SSD_CONV_DIM), row),
            pl.BlockSpec((rows, DT_PAD), row),
            pl.BlockSpec((SSD_CONV_K, SSD_CONV_DIM), fixed),
            pl.BlockSpec((1, SSD_CONV_DIM), fixed),
            pl.BlockSpec((1, DT_PAD), fixed),
            pl.BlockSpec((1, DT_PAD), fixed),
            pl.BlockSpec((1, SSD_D_INNER), fixed),
            pl.BlockSpec((1, SSD_D_INNER), fixed),
            pl.BlockSpec(st_blk, st),
            pl.BlockSpec(st_blk, st),
        ],
        out_specs=(
            pl.BlockSpec((rows, SSD_D_INNER), lambda b: (b, 0)),
            pl.BlockSpec(st_blk, st),
            pl.BlockSpec(st_blk, st),
        ),
        scratch_shapes=[
            pltpu.VMEM((ns, seq + 2 * CONV_HALO, SSD_CONV_DIM), F32),
            pltpu.VMEM((rows, SSD_CONV_DIM), F32),
            pltpu.VMEM((rows, DT_PAD), F32),
            pltpu.VMEM((rows, SSD_D_INNER), F32),
            pltpu.VMEM((ns, 2, SSD_HEADS, SSD_HEAD_DIM, SSD_STATE), F32),
        ],
        compiler_params=_cparams(("arbitrary",)),
        name=f"ssd_{seq}",
    )(z, xbc, dt, cw, cb, dtb, a_row, d_row, g, h0f, h0b)


RES_TM = 512


def _mixres_kernel(x_ref, mod_ref, a1_ref, a2_ref, w_ref, o_ref):
    k1 = a1_ref.shape[1]
    out = _dot(a1_ref[...], w_ref[0:k1, :]) + _dot(a2_ref[...], w_ref[k1:, :])
    o_ref[...] = x_ref[...] + mod_ref[0][2:3] * out


def _mixres(x, mod, a1, a2, w):
    tm = RES_TM
    row = lambda i: (i, 0)
    return pl.pallas_call(
        _mixres_kernel,
        out_shape=jax.ShapeDtypeStruct((R_ALL, D_MODEL), F32),
        grid=(R_ALL // tm,),
        in_specs=[
            pl.BlockSpec((tm, D_MODEL), row),
            pl.BlockSpec((1, 8, D_MODEL), lambda i: (_mod_index(i, tm), 0, 0)),
            pl.BlockSpec((tm, a1.shape[1]), row),
            pl.BlockSpec((tm, a2.shape[1]), row),
            pl.BlockSpec(w.shape, lambda i: (0, 0)),
        ],
        out_specs=pl.BlockSpec((tm, D_MODEL), row),
        compiler_params=_cparams(("arbitrary",)),
        name="mixres",
    )(x, mod, a1, a2, w)


FFN_TM = 1024
FFN_TF = 256


def _ffn_kernel(x_ref, g_ref, mod_ref, wg_ref, wu_ref, wd_ref, fg_ref, o_ref, h_sc, acc_sc, *, final):
    j = pl.program_id(1)

    @pl.when(j == 0)
    def _():
        m = mod_ref[0]
        h = _rms(x_ref[...], g_ref[...]) * (1 + m[4:5]) + m[3:4]
        h_sc[...] = h.astype(BF16)
        acc_sc[...] = jnp.zeros_like(acc_sc)

    h = h_sc[...]
    hid = _silu(_dot(h, wg_ref[...])) * _dot(h, wu_ref[...])
    acc_sc[...] += _dot(hid.astype(BF16), wd_ref[...])

    @pl.when(j == pl.num_programs(1) - 1)
    def _():
        y = x_ref[...] + mod_ref[0][5:6] * acc_sc[...]
        if final:
            y = _rms(y, fg_ref[...])
        o_ref[...] = y


def _ffn(x, g, mod, wg, wu, wd, fg, *, final):
    tm, tf = FFN_TM, FFN_TF
    row = lambda i, j: (i, 0)
    return pl.pallas_call(
        functools.partial(_ffn_kernel, final=final),
        out_shape=jax.ShapeDtypeStruct((R_ALL, D_MODEL), F32),
        grid=(R_ALL // tm, FFN_DIM // tf),
        in_specs=[
            pl.BlockSpec((tm, D_MODEL), row),
            pl.BlockSpec((1, D_MODEL), lambda i, j: (0, 0)),
            pl.BlockSpec((1, 8, D_MODEL), lambda i, j: (_mod_index(i, tm), 0, 0)),
            pl.BlockSpec((D_MODEL, tf), lambda i, j: (0, j)),
            pl.BlockSpec((D_MODEL, tf), lambda i, j: (0, j)),
            pl.BlockSpec((tf, D_MODEL), lambda i, j: (j, 0)),
            pl.BlockSpec((1, D_MODEL), lambda i, j: (0, 0)),
        ],
        out_specs=pl.BlockSpec((tm, D_MODEL), row),
        scratch_shapes=[pltpu.VMEM((tm, D_MODEL), BF16), pltpu.VMEM((tm, D_MODEL), F32)],
        compiler_params=_cparams(("arbitrary", "arbitrary")),
        name="ffn_final" if final else "ffn",
    )(x, g, mod, wg, wu, wd, fg)


RW_TM = 256
HALO = 8
N_OPS = 9
DECAY_SCALE = float(math.exp(-0.5))


def _rwkv_pre_kernel(x_ref, xp_ref, xn_ref, g_ref, mod_ref, mu_ref, wr_ref, wk_ref, wv_ref, w1_ref, w2_ref,
                     a1_ref, a2_ref, g1_ref, g2_ref, w0_ref, a0_ref, kk_ref, ka_ref, rk_ref, hs_ref,
                     ops_o, g_o, bon_o):
    i = pl.program_id(0)
    tm = RW_TM
    n_ctx = R_CTX // tm
    per_c = SEQ // tm
    per_s = DEC_SEQ // tm
    rel = jnp.where(i < n_ctx, i % per_c, jnp.maximum(i - n_ctx, 0) % per_s)
    last = jnp.where(i < n_ctx, per_c - 1, per_s - 1)
    m = mod_ref[0]

    def nm(x):
        return _rms(x, g_ref[...]) * (1 + m[1:2]) + m[0:1]

    h = nm(x_ref[...])
    prev_row = jnp.where(rel == 0, 0.0, nm(xp_ref[...])[HALO - 1:HALO, :])
    next_row = jnp.where(rel == last, 0.0, nm(xn_ref[...])[0:1, :])
    row = lax.broadcasted_iota(jnp.int32, h.shape, 0)
    hp = jnp.where(row == 0, prev_row, pltpu.roll(h, 1, axis=0))
    hn = jnp.where(row == tm - 1, next_row, pltpu.roll(h, tm - 1, axis=0))
    dp = hp - h
    dn = hn - h

    def mix(idx):
        return (h + dp * mu_ref[0, idx:idx + 1, :] + dn * mu_ref[1, idx:idx + 1, :]).astype(BF16)

    r = _dot(mix(0), wr_ref[...])
    k = _dot(mix(2), wk_ref[...])
    v = _dot(mix(3), wv_ref[...])
    lw = jnp.tanh(_dot(mix(1), w1_ref[...])).astype(BF16)
    la = _dot(mix(4), a1_ref[...]).astype(BF16)
    gg = _dot(_sigmoid(_dot(mix(5), g1_ref[...])).astype(BF16), g2_ref[...])

    hs = hs_ref[...]
    kk = k * kk_ref[...]
    kk = kk * lax.rsqrt(_head_allsum(kk * kk, hs) + L2_EPS)
    ops_o[0] = r
    ops_o[1] = kk
    ops_o[2] = v
    g_o[...] = gg
    bsum = None
    for j in range(2):
        wl = w0_ref[j:j + 1, :] + _dot(lw, w2_ref[j])
        ops_o[3 + 3 * j] = jnp.exp(-(DECAY_SCALE * _sigmoid(wl)))
        a = _sigmoid(a0_ref[j:j + 1, :] + _dot(la, a2_ref[j]))
        kd = k * (1 + (a - 1) * ka_ref[...])
        ops_o[4 + 3 * j] = kd
        ops_o[5 + 3 * j] = kk * a
        t = r * kd * rk_ref[...]
        bsum = t if bsum is None else bsum + t
    bon_o[...] = _head_allsum(bsum, hs) * v


def _rwkv_pre(x, g, mod, mu, wr, wk, wv, w1, w2, a1, a2, g1, g2, w0, a0, k_k, k_a, r_k, hs):
    tm = RW_TM
    nblk = R_ALL // HALO
    per = tm // HALO
    row = lambda i: (i, 0)
    fixed2 = lambda i: (0, 0)
    fixed3 = lambda i: (0, 0, 0)
    full = lambda a: pl.BlockSpec(a.shape, fixed2 if a.ndim == 2 else fixed3)
    out = jax.ShapeDtypeStruct((R_ALL, D_MODEL), F32)
    orow = pl.BlockSpec((tm, D_MODEL), row)
    return pl.pallas_call(
        _rwkv_pre_kernel,
        out_shape=(jax.ShapeDtypeStruct((N_OPS, R_ALL, D_MODEL), F32), out, out),
        grid=(R_ALL // tm,),
        in_specs=[
            pl.BlockSpec((tm, D_MODEL), row),
            pl.BlockSpec((HALO, D_MODEL), lambda i: (jnp.maximum(i * per - 1, 0), 0)),
            pl.BlockSpec((HALO, D_MODEL), lambda i: (jnp.minimum((i + 1) * per, nblk - 1), 0)),
            pl.BlockSpec((1, D_MODEL), fixed2),
            pl.BlockSpec((1, 8, D_MODEL), lambda i: (_mod_index(i, tm), 0, 0)),
            full(mu), full(wr), full(wk), full(wv), full(w1), full(w2), full(a1), full(a2), full(g1), full(g2),
            full(w0), full(a0), full(k_k), full(k_a), full(r_k), full(hs),
        ],
        out_specs=(pl.BlockSpec((N_OPS, tm, D_MODEL), lambda i: (0, i, 0)), orow, orow),
        compiler_params=_cparams(("arbitrary",)),
        name="rwkv_pre",
    )(x, x, x, g, mod, mu, wr, wk, wv, w1, w2, a1, a2, g1, g2, w0, a0, k_k, k_a, r_k, hs)


SCAN_TT = 64
SCAN_ACC = 4
NK = RWKV_HEAD_DIM


def _scan_kernel(kt_ref, vt_ref, dk_ref, s0_ref, y_ref, st_ref, s_sc):
    d = pl.program_id(0)
    j = pl.program_id(2)

    @pl.when(j == 0)
    def _():
        s_sc[...] = s0_ref[0]

    def tix(i):
        return jnp.where(d == 0, i, SCAN_TT - 1 - i)

    def project(i):
        t = tix(i)
        parts = [None] * SCAN_ACC
        for k in range(NK):
            term = s_sc[k] * kt_ref[1, t, k:k + 1, :]
            parts[k % SCAN_ACC] = term if parts[k % SCAN_ACC] is None else parts[k % SCAN_ACC] + term
        while len(parts) > 1:
            parts = [parts[a] + parts[a + 1] for a in range(0, len(parts), 2)]
        return parts[0]

    def advance(i, sa, with_next):
        t = tix(i)
        tn = tix(i + 1)
        vv = vt_ref[0, t]
        y = None
        sa_next = None
        for k in range(NK):
            sn = (s_sc[k] * dk_ref[0, t, k:k + 1, :] - sa * dk_ref[2, t, k:k + 1, :]
                  + vv * dk_ref[1, t, k:k + 1, :])
            s_sc[k] = sn
            yk = sn * kt_ref[0, t, k:k + 1, :]
            y = yk if y is None else y + yk
            if with_next:
                ak = sn * kt_ref[1, tn, k:k + 1, :]
                sa_next = ak if sa_next is None else sa_next + ak
        y_ref[0, t] = y
        return sa_next

    sa_last = lax.fori_loop(0, SCAN_TT - 1, lambda i, sa: advance(i, sa, True), project(0))
    advance(SCAN_TT - 1, sa_last, False)

    @pl.when(j == pl.num_programs(2) - 1)
    def _():
        st_ref[0] = s_sc[...]


def _scan(kt, vt, dk, s0, *, kt_blk=0, vt_slab=0, dk_blk=0):
    _, seq, nv, chains = vt.shape
    tt = SCAN_TT
    nt = seq // tt
    tb = lambda d, j: jnp.where(d == 0, j, nt - 1 - j)
    sblk = pl.BlockSpec((1, NK, nv, LANES), lambda d, c, j: (d, 0, 0, c))
    yblk = pl.BlockSpec((1, tt, nv, LANES), lambda d, c, j: (d, tb(d, j), 0, c))
    return pl.pallas_call(
        _scan_kernel,
        out_shape=(jax.ShapeDtypeStruct((2, seq, nv, chains), F32), jax.ShapeDtypeStruct((2, NK, nv, chains), F32)),
        grid=(2, chains // LANES, nt),
        in_specs=[
            pl.BlockSpec((2, tt, NK, LANES), lambda d, c, j: (kt_blk, tb(d, j), 0, c)),
            pl.BlockSpec((1, tt, nv, LANES), lambda d, c, j: (vt_slab, tb(d, j), 0, c)),
            pl.BlockSpec((3, tt, NK, LANES), lambda d, c, j: (dk_blk + d, tb(d, j), 0, c)),
            sblk,
        ],
        out_specs=(yblk, sblk),
        scratch_shapes=[pltpu.VMEM((NK, nv, LANES), F32)],
        compiler_params=_cparams(("arbitrary", "arbitrary", "arbitrary")),
        name=f"rwkv_scan_{seq}",
    )(kt, vt, dk, s0)


POST_TM = 256


def _rwkv_post_kernel(x_ref, mod_ref, yc_ref, ys_ref, bon_ref, g_ref, lng_ref, lnb_ref, wo_ref, hs_ref, o_ref):
    is_ctx = pl.program_id(0) < R_CTX // POST_TM
    y = jnp.where(is_ctx, yc_ref[0] + yc_ref[1], ys_ref[0] + ys_ref[1])
    inv = 1.0 / RWKV_HEAD_DIM
    hs = hs_ref[...]
    d = y - _head_allsum(y, hs) * inv
    var = _head_allsum(d * d, hs) * inv
    yn = d * lax.rsqrt(var + GN_EPS) * lng_ref[...] + lnb_ref[...]
    out = ((yn + bon_ref[...]) * g_ref[...]).astype(BF16)
    o_ref[...] = x_ref[...] + mod_ref[0][2:3] * _dot(out, wo_ref[...])


def _rwkv_post(x, mod, yc, ys, bon, g, lng, lnb, wo, hs):
    tm = POST_TM
    n_ctx = R_CTX // tm
    row = lambda i: (i, 0)
    fixed = lambda i: (0, 0)
    big = pl.BlockSpec((tm, D_MODEL), row)
    ycb = pl.BlockSpec((2, tm, D_MODEL), lambda i: (0, jnp.minimum(i, n_ctx - 1), 0))
    ysb = pl.BlockSpec((2, tm, D_MODEL), lambda i: (0, jnp.maximum(i - n_ctx, 0), 0))
    return pl.pallas_call(
        _rwkv_post_kernel,
        out_shape=jax.ShapeDtypeStruct((R_ALL, D_MODEL), F32),
        grid=(R_ALL // tm,),
        in_specs=[
            big,
            pl.BlockSpec((1, 8, D_MODEL), lambda i: (_mod_index(i, tm), 0, 0)),
            ycb, ysb, big, big,
            pl.BlockSpec((1, D_MODEL), fixed),
            pl.BlockSpec((1, D_MODEL), fixed),
            pl.BlockSpec((D_MODEL, D_MODEL), fixed),
            pl.BlockSpec((LANES, LANES), fixed),
        ],
        out_specs=big,
        compiler_params=_cparams(("arbitrary",)),
        name="rwkv_post",
    )(x, mod, yc, ys, bon, g, lng, lnb, wo, hs)


RL_T = 128
NPAIR = D_MODEL // LANES


def _to_chains_kernel(x_ref, o_ref):
    cols = [jnp.swapaxes(x_ref[0, :, :, p * LANES:(p + 1) * LANES], 0, 1) for p in range(NPAIR)]
    for t in range(RL_T):
        tile = jnp.concatenate([c[t] for c in cols], axis=0).T
        o_ref[0, t, :, 0:LANES] = tile[0:NK]
        o_ref[0, t, :, LANES:2 * LANES] = tile[NK:2 * NK]


def _from_chains_kernel(y_ref, o_ref):
    tiles = []
    for t in range(RL_T):
        m = jnp.concatenate([y_ref[0, t, :, 0:LANES], y_ref[0, t, :, LANES:2 * LANES]], axis=0)
        tiles.append(m.T)
    for p in range(NPAIR):
        blk = jnp.stack([m[p * BATCH:(p + 1) * BATCH] for m in tiles], axis=0)
        o_ref[0, :, :, p * LANES:(p + 1) * LANES] = jnp.swapaxes(blk, 0, 1)


def _to_chains(ops):
    n = ops.shape[0]
    return pl.pallas_call(
        _to_chains_kernel,
        out_shape=jax.ShapeDtypeStruct((n, SEQ, NK, 2 * LANES), F32),
        grid=(n, SEQ // RL_T),
        in_specs=[pl.BlockSpec((1, BATCH, RL_T, D_MODEL), lambda o, j: (o, 0, j, 0))],
        out_specs=pl.BlockSpec((1, RL_T, NK, 2 * LANES), lambda o, j: (o, j, 0, 0)),
        compiler_params=_cparams(("arbitrary", "arbitrary")),
        name="to_chains",
    )(ops.reshape(n, R_ALL // SEQ, SEQ, D_MODEL))


def _from_chains(y):
    n = y.shape[0]
    out = pl.pallas_call(
        _from_chains_kernel,
        out_shape=jax.ShapeDtypeStruct((n, BATCH, SEQ, D_MODEL), F32),
        grid=(n, SEQ // RL_T),
        in_specs=[pl.BlockSpec((1, RL_T, NK, 2 * LANES), lambda o, j: (o, j, 0, 0))],
        out_specs=pl.BlockSpec((1, BATCH, RL_T, D_MODEL), lambda o, j: (o, 0, j, 0)),
        compiler_params=_cparams(("arbitrary", "arbitrary")),
        name="from_chains",
    )(y)
    return out.reshape(n, R_CTX, D_MODEL)


def _scan_ctx(ops):
    ch = _to_chains(ops)
    s0 = jnp.zeros((2, NK, NK, 2 * LANES), F32)
    y, st = _scan(ch, ch, ch, s0, vt_slab=2, dk_blk=1)
    st = st.reshape(2, NK, NK, 2, NPAIR, BATCH).transpose(0, 5, 4, 3, 2, 1)
    return _from_chains(y), st.reshape(2, BATCH, RWKV_HEADS, NK, NK)


SMP_REP = LANES // (NPAIR * DEC_BATCH)
SMP_VQ = SMP_REP // 2
SMP_NV = NK // SMP_VQ
SMP_PB = NPAIR * DEC_BATCH


def _smp_tiles(x_ref):
    rows = jnp.stack([x_ref[0, b, :, p * LANES:(p + 1) * LANES] for p in range(NPAIR) for b in range(DEC_BATCH)],
                     axis=0)
    rows = jnp.swapaxes(rows, 0, 1)
    return [jnp.concatenate([rows[t]] * SMP_REP, axis=0).T for t in range(RL_T)]


def _lane_replica(nrows):
    return lax.broadcasted_iota(jnp.int32, (nrows, LANES), 1) // SMP_PB


def _to_chains_smp_k_kernel(x_ref, o_ref):
    odd = _lane_replica(NK) >= SMP_VQ
    for t, tile in enumerate(_smp_tiles(x_ref)):
        o_ref[0, t] = jnp.where(odd, tile[NK:2 * NK], tile[0:NK])


def _to_chains_smp_v_kernel(x_ref, o_ref):
    rep = _lane_replica(SMP_NV)
    for t, tile in enumerate(_smp_tiles(x_ref)):
        acc = tile[0:SMP_NV]
        for r in range(1, SMP_REP):
            acc = jnp.where(rep == r, tile[r * SMP_NV:(r + 1) * SMP_NV], acc)
        o_ref[0, t] = acc


def _from_chains_smp_kernel(y_ref, o_ref):
    rep = _lane_replica(SMP_NV)
    res = []
    for t in range(RL_T):
        y = y_ref[0, t]
        m = jnp.concatenate([jnp.where(rep == r, y, 0.0) for r in range(SMP_REP)], axis=0).T
        acc = m[0:SMP_PB]
        for r in range(1, SMP_REP):
            acc = acc + m[r * SMP_PB:(r + 1) * SMP_PB]
        res.append(acc)
    out = jnp.swapaxes(jnp.stack(res, axis=0), 0, 1)
    for p in range(NPAIR):
        for b in range(DEC_BATCH):
            o_ref[0, b, :, p * LANES:(p + 1) * LANES] = out[p * DEC_BATCH + b]


def _scan_smp(ops, s0):
    nseq = R_ALL // DEC_SEQ
    blk0 = R_CTX // DEC_SEQ // DEC_BATCH
    ops4 = ops.reshape(N_OPS, nseq, DEC_SEQ, D_MODEL)
    nt = DEC_SEQ // RL_T
    in_blk = (1, DEC_BATCH, RL_T, D_MODEL)
    nk_ops = N_OPS - 1
    kidx = lambda o, j: (jnp.where(o < 6, o + 3, o - 6), blk0, j, 0)
    chk = pl.pallas_call(
        _to_chains_smp_k_kernel,
        out_shape=jax.ShapeDtypeStruct((nk_ops, DEC_SEQ, NK, LANES), F32),
        grid=(nk_ops, nt),
        in_specs=[pl.BlockSpec(in_blk, kidx)],
        out_specs=pl.BlockSpec((1, RL_T, NK, LANES), lambda o, j: (o, j, 0, 0)),
        compiler_params=_cparams(("arbitrary", "arbitrary")),
        name="to_chains_smp_k",
    )(ops4)
    chv = pl.pallas_call(
        _to_chains_smp_v_kernel,
        out_shape=jax.ShapeDtypeStruct((1, DEC_SEQ, SMP_NV, LANES), F32),
        grid=(1, nt),
        in_specs=[pl.BlockSpec(in_blk, lambda o, j: (2, blk0, j, 0))],
        out_specs=pl.BlockSpec((1, RL_T, SMP_NV, LANES), lambda o, j: (0, j, 0, 0)),
        compiler_params=_cparams(("arbitrary", "arbitrary")),
        name="to_chains_smp_v",
    )(ops4)
    s0 = s0.reshape(2, DEC_BATCH, NPAIR, 2, SMP_VQ, SMP_NV, NK).transpose(0, 6, 5, 3, 4, 2, 1)
    y, _ = _scan(chk, chv, chk, s0.reshape(2, NK, SMP_NV, LANES), kt_blk=3, vt_slab=0, dk_blk=0)
    out = pl.pallas_call(
        _from_chains_smp_kernel,
        out_shape=jax.ShapeDtypeStruct((2, DEC_BATCH, DEC_SEQ, D_MODEL), F32),
        grid=(2, nt),
        in_specs=[pl.BlockSpec((1, RL_T, SMP_NV, LANES), lambda o, j: (o, j, 0, 0))],
        out_specs=pl.BlockSpec((1, DEC_BATCH, RL_T, D_MODEL), lambda o, j: (o, 0, j, 0)),
        compiler_params=_cparams(("arbitrary", "arbitrary")),
        name="from_chains_smp",
    )(y)
    return out.reshape(2, R_SMP, D_MODEL)


def _rope_tables():
    rows = DEC_SEQ // GRID_W
    row = jnp.repeat(jnp.arange(rows, dtype=F32), GRID_W)
    col = jnp.tile(jnp.arange(GRID_W, dtype=F32), rows)
    n_freq = HEAD_DIM // 4
    inv_freq = ROPE_THETA ** (-jnp.arange(n_freq, dtype=F32) / n_freq)
    ang = jnp.concatenate([row[:, None] * inv_freq, col[:, None] * inv_freq], axis=-1)
    cos, sin = jnp.cos(ang), jnp.sin(ang)
    reps = LANES // HEAD_DIM
    cos_t = jnp.tile(jnp.concatenate([cos, cos], axis=-1), (1, reps))
    sin_t = jnp.tile(jnp.concatenate([-sin, sin], axis=-1), (1, reps))
    ident_c = jnp.ones((QK_TM, LANES), F32)
    ident_s = jnp.zeros((QK_TM, LANES), F32)
    return jnp.concatenate([ident_c, cos_t], axis=0), jnp.concatenate([ident_s, sin_t], axis=0)


def kernel(x_prompt, x_sample, cache_attn_k, cache_attn_v, state_ssd_fwd, state_ssd_bwd, state_rwkv_fwd, state_rwkv_bwd, c, c_ctx, mod_w, mod_b, norm_mix_g, norm_ffn_g, ffn_w_gate, ffn_w_up, ffn_w_down, ab_w_in, ab_w_out, attn_q_g, attn_k_g, ssd_conv_w, ssd_conv_b, ssd_dt_bias, ssd_a_log, ssd_d, ssd_norm_g, rwkv_mu, rwkv_w_r, rwkv_w_k, rwkv_w_v, rwkv_w0, rwkv_w1, rwkv_w2, rwkv_a0, rwkv_a1, rwkv_a2, rwkv_g1, rwkv_g2, rwkv_k_k, rwkv_k_a, rwkv_r_k, rwkv_ln_g, rwkv_ln_b, rwkv_w_o, final_norm_g):
    bf = lambda a: a.astype(BF16)
    x = jnp.concatenate([x_prompt.reshape(R_CTX, D_MODEL), x_sample.reshape(R_SMP, D_MODEL)], axis=0)

    cv = jnp.concatenate([c_ctx[None], c, jnp.zeros((8 - 1 - DEC_BATCH, D_MODEL), F32)], axis=0)
    m = _ada(cv, mod_w, mod_b)
    m = m[:, :1 + DEC_BATCH].reshape(2, 1 + DEC_BATCH, 6, D_MODEL)
    m = jnp.pad(m, ((0, 0), (0, 0), (0, 2), (0, 0)))
    row2 = lambda a: a.reshape(1, -1)

    w_in = jnp.pad(bf(ab_w_in[0]), ((0, 0), (0, AB_PAD - AB_IN_DIM)))
    qkv, z, xbc, dt = _inproj(x, row2(norm_mix_g[0]), m[0], w_in)
    cos_t, sin_t = _rope_tables()
    qg = jnp.tile(attn_q_g[0], ATTN_HEADS).reshape(1, -1)
    kg = jnp.tile(attn_k_g[0], ATTN_KV_HEADS).reshape(1, -1)
    qn, kn = _qkprep(qkv, qg, kg, cos_t, sin_t)
    ck = cache_attn_k[:, 0].reshape(DEC_BATCH, PAST_LEN, ATTN_KV_DIM)
    cvv = cache_attn_v[:, 0].reshape(DEC_BATCH, PAST_LEN, ATTN_KV_DIM)
    attn_c, attn_s = _attention(qn, kn, qkv, ck, cvv)
    attn = jnp.concatenate([attn_c, attn_s], axis=0)

    dtb = jnp.pad(ssd_dt_bias[0].reshape(1, -1), ((0, 0), (0, DT_PAD - 2 * SSD_HEADS)))
    a_row = jnp.pad((-jnp.exp(ssd_a_log[0])).reshape(1, -1), ((0, 0), (0, DT_PAD - 2 * SSD_HEADS)))
    d_row = jnp.repeat(ssd_d[0], SSD_HEAD_DIM).reshape(1, -1)
    ssd_args = (ssd_conv_w[0], row2(ssd_conv_b[0]), dtb, a_row, d_row, row2(ssd_norm_g[0]))
    zero_st = jnp.zeros((BATCH, SSD_HEADS, SSD_HEAD_DIM, SSD_STATE), F32)
    y_c, hf_c, hb_c = _ssd(z, xbc, dt, *ssd_args, zero_st, zero_st, seq=SEQ, nb=BATCH, row0=0)
    y_s, _, _ = _ssd(z, xbc, dt, *ssd_args, state_ssd_fwd[:, 0], state_ssd_bwd[:, 0],
                     seq=DEC_SEQ, nb=DEC_BATCH, row0=R_CTX)
    y_ssd = jnp.concatenate([y_c, y_s], axis=0)
    x = _mixres(x, m[0], attn, y_ssd, bf(ab_w_out[0]))
    x = _ffn(x, row2(norm_ffn_g[0]), m[0], bf(ffn_w_gate[0]), bf(ffn_w_up[0]), bf(ffn_w_down[0]),
             row2(final_norm_g), final=False)

    w1 = bf(jnp.concatenate([rwkv_w1[0, 0], rwkv_w1[0, 1]], axis=1))
    a1 = bf(jnp.concatenate([rwkv_a1[0, 0], rwkv_a1[0, 1]], axis=1))
    zpad = lambda w: bf(jnp.stack([jnp.concatenate([w[0], jnp.zeros_like(w[1])], axis=0),
                                   jnp.concatenate([jnp.zeros_like(w[0]), w[1]], axis=0)]))
    hs = _head_mask()
    ops, gg, bon = _rwkv_pre(x, row2(norm_mix_g[1]), m[1], rwkv_mu[0], bf(rwkv_w_r[0]), bf(rwkv_w_k[0]),
                             bf(rwkv_w_v[0]), w1, zpad(rwkv_w2[0]), a1, zpad(rwkv_a2[0]), bf(rwkv_g1[0]),
                             bf(rwkv_g2[0]), rwkv_w0[0], rwkv_a0[0], row2(rwkv_k_k[0]), row2(rwkv_k_a[0]),
                             rwkv_r_k[0].reshape(1, -1), hs)
    y_c, st_c = _scan_ctx(ops)
    y_s = _scan_smp(ops, jnp.stack([state_rwkv_fwd[:, 0], state_rwkv_bwd[:, 0]]))
    x = _rwkv_post(x, m[1], y_c, y_s, bon, gg, row2(rwkv_ln_g[0]), row2(rwkv_ln_b[0]), bf(rwkv_w_o[0]), hs)
    x = _ffn(x, row2(norm_ffn_g[1]), m[1], bf(ffn_w_gate[1]), bf(ffn_w_up[1]), bf(ffn_w_down[1]),
             row2(final_norm_g), final=True)

    y_prompt = x[:R_CTX].reshape(BATCH, SEQ, D_MODEL)
    y_sample = x[R_CTX:].reshape(DEC_BATCH, DEC_SEQ, D_MODEL)
    new_k = kn[:R_CTX].reshape(BATCH, 1, SEQ, ATTN_KV_HEADS, HEAD_DIM)
    new_v = qkv[:R_CTX, ATTN_Q_DIM + ATTN_KV_DIM:].reshape(BATCH, 1, SEQ, ATTN_KV_HEADS, HEAD_DIM)
    return (y_prompt, y_sample, new_k, new_v, hf_c[:, None], hb_c[:, None], st_c[0][:, None], st_c[1][:, None])
```
